```python
import math
import jax, jax.numpy as jnp
from jax import lax
import numpy as np

D_MODEL = 2048
BATCH = 8
SEQ = 8192
DEPTH = 4

C_A = 512
H_B = 16
HEAD_DIM = 64
C_B = H_B * HEAD_DIM
C_C = 512
POOL_WINDOWS = (2, 4, 8, 16)
N_POOL_GROUPS = len(POOL_WINDOWS)
C_G = C_C // N_POOL_GROUPS
MIX_WIDTH = C_A + C_B + C_C
IN_WIDTH = 2 * C_A + 3 * C_B + C_C

CONV_WIDTH = 31
CONV_HALF = CONV_WIDTH // 2

DILATED_PATTERNS = ((128, 1), (512, 4), (2048, 16))
ATTN_BLOCK = 64
ROT_DIM = HEAD_DIM // 4
ROPE_THETA = 500000.0

FFN_HIDDEN = int(math.ceil((8 * D_MODEL / 3) / 256) * 256)
EPS = 1e-6
NEG = -1e30

kernel_name = "hybrid_conv_dilatedattn_pool_encoder"


def rms_normalize(t):
    tf = t.astype(jnp.float32)
    return tf * lax.rsqrt(jnp.mean(tf * tf, axis=-1, keepdims=True) + EPS)


def rmsnorm(t, g):
    return (rms_normalize(t) * g.astype(jnp.float32)).astype(t.dtype)


def rope_tables(S):
    pos = jnp.arange(S, dtype=jnp.float32)
    inv = ROPE_THETA ** (-jnp.arange(0, ROT_DIM, 2, dtype=jnp.float32) / ROT_DIM)
    ang = pos[:, None] * inv[None, :]
    return jnp.cos(ang), jnp.sin(ang)


def apply_partial_rope(t, cos, sin):
    half = ROT_DIM // 2
    t1 = t[..., :half]
    t2 = t[..., half:ROT_DIM]
    c = cos[None, :, None, :]
    s = sin[None, :, None, :]
    return jnp.concatenate([t1 * c - t2 * s, t2 * c + t1 * s, t[..., ROT_DIM:]], axis=-1)


def conformer_conv(u, w, b, ln_g, ln_b):
    a, gate = jnp.split(u, 2, axis=-1)
    h = a * jax.nn.sigmoid(gate)
    h = lax.conv_general_dilated(
        h, w[:, None, :], window_strides=(1,),
        padding=[(CONV_HALF, CONV_HALF)],
        dimension_numbers=('NWC', 'WIO', 'NWC'),
        feature_group_count=C_A) + b
    hf = h.astype(jnp.float32)
    mu = jnp.mean(hf, axis=-1, keepdims=True)
    var = jnp.mean(jnp.square(hf - mu), axis=-1, keepdims=True)
    hf = (hf - mu) * lax.rsqrt(var + EPS) * ln_g.astype(jnp.float32) + ln_b.astype(jnp.float32)
    return jax.nn.silu(hf).astype(u.dtype)


def banded_dilated_stats(q, k, v, dilation, half):
    B, S, H, E = q.shape
    L = S // dilation
    nb = -(-L // ATTN_BLOCK)
    Lp = nb * ATTN_BLOCK
    qd = q.reshape(B, L, dilation, H, E)
    kd = k.reshape(B, L, dilation, H, E)
    vd = v.reshape(B, L, dilation, H, E)
    pad_q = ((0, 0), (0, Lp - L), (0, 0), (0, 0), (0, 0))
    pad_kv = ((0, 0), (ATTN_BLOCK, Lp - L + ATTN_BLOCK), (0, 0), (0, 0), (0, 0))
    qb = jnp.pad(qd, pad_q).reshape(B, nb, ATTN_BLOCK, dilation, H, E)
    kb = jnp.pad(kd, pad_kv).reshape(B, nb + 2, ATTN_BLOCK, dilation, H, E)
    vb = jnp.pad(vd, pad_kv).reshape(B, nb + 2, ATTN_BLOCK, dilation, H, E)
    kwin = jnp.concatenate([kb[:, :-2], kb[:, 1:-1], kb[:, 2:]], axis=2)
    vwin = jnp.concatenate([vb[:, :-2], vb[:, 1:-1], vb[:, 2:]], axis=2)
    blk = jnp.arange(nb)[:, None]
    jq = blk * ATTN_BLOCK + jnp.arange(ATTN_BLOCK)[None, :]
    jk = (blk - 1) * ATTN_BLOCK + jnp.arange(3 * ATTN_BLOCK)[None, :]
    mask = ((jnp.abs(jk[:, None, :] - jq[:, :, None]) <= half)
            & (jk[:, None, :] >= 0) & (jk[:, None, :] < L))
    scores = jnp.einsum('bnqrhe,bnkrhe->bnrhqk', qb, kwin)
    maskb = mask[None, :, None, None, :, :]
    scores = jnp.where(maskb, scores, NEG)
    m = jnp.max(scores, axis=-1)
    p = jnp.where(maskb, jnp.exp(scores - m[..., None]), 0.0)
    s = jnp.sum(p, axis=-1)
    o = jnp.einsum('bnrhqk,bnkrhe->bnqrhe', p, vwin)
    m = jnp.transpose(m, (0, 1, 4, 2, 3)).reshape(B, Lp, dilation, H)[:, :L].reshape(B, S, H)
    s = jnp.transpose(s, (0, 1, 4, 2, 3)).reshape(B, Lp, dilation, H)[:, :L].reshape(B, S, H)
    o = o.reshape(B, Lp, dilation, H, E)[:, :L].reshape(B, S, H, E)
    return m, s, o


def dilated_attention(qkv, cos, sin):
    B, S, _ = qkv.shape
    q, k, v = jnp.split(qkv.astype(jnp.float32), 3, axis=-1)
    q = apply_partial_rope(q.reshape(B, S, H_B, HEAD_DIM), cos, sin) * (HEAD_DIM ** -0.5)
    k = apply_partial_rope(k.reshape(B, S, H_B, HEAD_DIM), cos, sin)
    v = v.reshape(B, S, H_B, HEAD_DIM)
    ms, ss, os_ = [], [], []
    for window, dilation in DILATED_PATTERNS:
        m, s, o = banded_dilated_stats(q, k, v, dilation, window // (2 * dilation))
        ms.append(m); ss.append(s); os_.append(o)
    m_all = jnp.stack(ms)
    s_all = jnp.stack(ss)
    o_all = jnp.stack(os_)
    wgt = jnp.exp(m_all - jnp.max(m_all, axis=0, keepdims=True))
    num = jnp.sum(wgt[..., None] * o_all, axis=0)
    den = jnp.sum(wgt * s_all, axis=0)
    out = num / den[..., None]
    return out.reshape(B, S, C_B).astype(qkv.dtype)


def pool_mixer(u, w, scale):
    B, S, _ = u.shape
    uf = u.astype(jnp.float32)
    cs = jnp.concatenate([jnp.zeros((B, 1, C_C), jnp.float32), jnp.cumsum(uf, axis=1)], axis=1)
    pos = jnp.arange(S)
    outs = []
    for gi, win in enumerate(POOL_WINDOWS):
        seg = cs[..., gi * C_G:(gi + 1) * C_G]
        lo = jnp.clip(pos - win // 2, 0, S)
        hi = jnp.clip(pos + win - win // 2, 0, S)
        mean = (seg[:, hi] - seg[:, lo]) / (hi - lo).astype(jnp.float32)[None, :, None]
        outs.append(mean - uf[..., gi * C_G:(gi + 1) * C_G])
    pooled = jnp.stack(outs, axis=2)
    mixed = jnp.einsum('bsgc,gcd->bsgd', pooled, w.astype(jnp.float32))
    return (mixed.reshape(B, S, C_C) * scale.astype(jnp.float32)).astype(u.dtype)


def _fwd_setup_inputs(seed: int = 0) -> dict:
    key = jax.random.key(seed)
    ks = jax.random.split(key, 20)
    f32 = jnp.float32

    def nrm(k, shape, scale):
        return jax.random.normal(k, shape, f32) * scale

    def gain(k, shape):
        return 1.0 + 0.05 * jax.random.normal(k, shape, f32)

    return {
        "x": jax.random.normal(ks[0], (BATCH, SEQ, D_MODEL), f32),
        "w_in": nrm(ks[1], (DEPTH, D_MODEL, IN_WIDTH), D_MODEL ** -0.5),
        "conv_w": nrm(ks[2], (DEPTH, CONV_WIDTH, C_A), CONV_WIDTH ** -0.5),
        "conv_b": nrm(ks[3], (DEPTH, C_A), 0.02),
        "conv_ln_g": gain(ks[4], (DEPTH, C_A)),
        "conv_ln_b": nrm(ks[5], (DEPTH, C_A), 0.02),
        "pool_w": nrm(ks[6], (DEPTH, N_POOL_GROUPS, C_G, C_G), C_G ** -0.5),
        "pool_scale": gain(ks[7], (DEPTH, C_C)),
        "g_mix": gain(ks[8], (DEPTH, MIX_WIDTH)),
        "w_out": nrm(ks[9], (DEPTH, MIX_WIDTH, D_MODEL), MIX_WIDTH ** -0.5),
        "g_pre_mix": gain(ks[10], (DEPTH, D_MODEL)),
        "g_post_mix": gain(ks[11], (DEPTH, D_MODEL)),
        "g_pre_ffn": gain(ks[12], (DEPTH, D_MODEL)),
        "g_post_ffn": gain(ks[13], (DEPTH, D_MODEL)),
        "w_gate": nrm(ks[14], (DEPTH, D_MODEL, FFN_HIDDEN), D_MODEL ** -0.5),
        "w_up": nrm(ks[15], (DEPTH, D_MODEL, FFN_HIDDEN), D_MODEL ** -0.5),
        "w_down": nrm(ks[16], (DEPTH, FFN_HIDDEN, D_MODEL), FFN_HIDDEN ** -0.5),
    }


def _fwd_reference(x, w_in, conv_w, conv_b, conv_ln_g, conv_ln_b, pool_w, pool_scale, g_mix,
              w_out, g_pre_mix, g_post_mix, g_pre_ffn, g_post_ffn, w_gate, w_up, w_down):
    S = x.shape[1]
    cos, sin = rope_tables(S)
    a_end = 2 * C_A
    b_end = a_end + 3 * C_B
    for l in range(DEPTH):
        h = rmsnorm(x, g_pre_mix[l])
        proj = h @ w_in[l]
        y_a = conformer_conv(proj[..., :a_end], conv_w[l], conv_b[l], conv_ln_g[l], conv_ln_b[l])
        y_b = dilated_attention(proj[..., a_end:b_end], cos, sin)
        y_c = pool_mixer(proj[..., b_end:], pool_w[l], pool_scale[l])
        y = jnp.concatenate([rms_normalize(y_a), rms_normalize(y_b), rms_normalize(y_c)], axis=-1)
        y = (y * g_mix[l].astype(jnp.float32)).astype(x.dtype)
        x = x + rmsnorm(y @ w_out[l], g_post_mix[l])
        h = rmsnorm(x, g_pre_ffn[l])
        f = (jax.nn.silu(h @ w_gate[l]) * (h @ w_up[l])) @ w_down[l]
        x = x + rmsnorm(f, g_post_ffn[l])
    return x


import jax as _jax
import jax.numpy as _jnp

TWIN_FORMAT = 'train_step'
FWD_PARAMS = ['x', 'w_in', 'conv_w', 'conv_b', 'conv_ln_g', 'conv_ln_b', 'pool_w', 'pool_scale', 'g_mix', 'w_out', 'g_pre_mix', 'g_post_mix', 'g_pre_ffn', 'g_post_ffn', 'w_gate', 'w_up', 'w_down']
TWIN_WEIGHTS = ['w_in', 'conv_w', 'conv_b', 'conv_ln_g', 'conv_ln_b', 'pool_w', 'pool_scale', 'g_mix', 'w_out', 'g_pre_mix', 'g_post_mix', 'g_pre_ffn', 'g_post_ffn', 'w_gate', 'w_up', 'w_down']
TWIN_DIFF_INPUT = 'x'
TWIN_INPUTS = ['x', 'w_in', 'conv_w', 'conv_b', 'conv_ln_g', 'conv_ln_b', 'pool_w', 'pool_scale', 'g_mix', 'w_out', 'g_pre_mix', 'g_post_mix', 'g_pre_ffn', 'g_post_ffn', 'w_gate', 'w_up', 'w_down', 'loss_target', 'm_w_in', 'm_conv_w', 'm_conv_b', 'm_conv_ln_g', 'm_conv_ln_b', 'm_pool_w', 'm_pool_scale', 'm_g_mix', 'm_w_out', 'm_g_pre_mix', 'm_g_post_mix', 'm_g_pre_ffn', 'm_g_post_ffn', 'm_w_gate', 'm_w_up', 'm_w_down', 'v_w_in', 'v_conv_w', 'v_conv_b', 'v_conv_ln_g', 'v_conv_ln_b', 'v_pool_w', 'v_pool_scale', 'v_g_mix', 'v_w_out', 'v_g_pre_mix', 'v_g_post_mix', 'v_g_pre_ffn', 'v_g_post_ffn', 'v_w_gate', 'v_w_up', 'v_w_down']
TWIN_OUTPUTS = ['loss', 'grad_x', 'grad_w_in', 'grad_conv_w', 'grad_conv_b', 'grad_conv_ln_g', 'grad_conv_ln_b', 'grad_pool_w', 'grad_pool_scale', 'grad_g_mix', 'grad_w_out', 'grad_g_pre_mix', 'grad_g_post_mix', 'grad_g_pre_ffn', 'grad_g_post_ffn', 'grad_w_gate', 'grad_w_up', 'grad_w_down', 'delta_w_in', 'delta_conv_w', 'delta_conv_b', 'delta_conv_ln_g', 'delta_conv_ln_b', 'delta_pool_w', 'delta_pool_scale', 'delta_g_mix', 'delta_w_out', 'delta_g_pre_mix', 'delta_g_post_mix', 'delta_g_pre_ffn', 'delta_g_post_ffn', 'delta_w_gate', 'delta_w_up', 'delta_w_down', 'new_m_w_in', 'new_m_conv_w', 'new_m_conv_b', 'new_m_conv_ln_g', 'new_m_conv_ln_b', 'new_m_pool_w', 'new_m_pool_scale', 'new_m_g_mix', 'new_m_w_out', 'new_m_g_pre_mix', 'new_m_g_post_mix', 'new_m_g_pre_ffn', 'new_m_g_post_ffn', 'new_m_w_gate', 'new_m_w_up', 'new_m_w_down', 'new_v_w_in', 'new_v_conv_w', 'new_v_conv_b', 'new_v_conv_ln_g', 'new_v_conv_ln_b', 'new_v_pool_w', 'new_v_pool_scale', 'new_v_g_mix', 'new_v_w_out', 'new_v_g_pre_mix', 'new_v_g_post_mix', 'new_v_g_pre_ffn', 'new_v_g_post_ffn', 'new_v_w_gate', 'new_v_w_up', 'new_v_w_down']
TWIN_LEAF_KINDS = {'loss': 'loss', 'grad_x': 'grad_x', 'grad_w_in': 'grad_w', 'grad_conv_w': 'grad_w', 'grad_conv_b': 'grad_w', 'grad_conv_ln_g': 'grad_w', 'grad_conv_ln_b': 'grad_w', 'grad_pool_w': 'grad_w', 'grad_pool_scale': 'grad_w', 'grad_g_mix': 'grad_w', 'grad_w_out': 'grad_w', 'grad_g_pre_mix': 'grad_w', 'grad_g_post_mix': 'grad_w', 'grad_g_pre_ffn': 'grad_w', 'grad_g_post_ffn': 'grad_w', 'grad_w_gate': 'grad_w', 'grad_w_up': 'grad_w', 'grad_w_down': 'grad_w', 'delta_w_in': 'delta_w', 'delta_conv_w': 'delta_w', 'delta_conv_b': 'delta_w', 'delta_conv_ln_g': 'delta_w', 'delta_conv_ln_b': 'delta_w', 'delta_pool_w': 'delta_w', 'delta_pool_scale': 'delta_w', 'delta_g_mix': 'delta_w', 'delta_w_out': 'delta_w', 'delta_g_pre_mix': 'delta_w', 'delta_g_post_mix': 'delta_w', 'delta_g_pre_ffn': 'delta_w', 'delta_g_post_ffn': 'delta_w', 'delta_w_gate': 'delta_w', 'delta_w_up': 'delta_w', 'delta_w_down': 'delta_w', 'new_m_w_in': 'new_m', 'new_m_conv_w': 'new_m', 'new_m_conv_b': 'new_m', 'new_m_conv_ln_g': 'new_m', 'new_m_conv_ln_b': 'new_m', 'new_m_pool_w': 'new_m', 'new_m_pool_scale': 'new_m', 'new_m_g_mix': 'new_m', 'new_m_w_out': 'new_m', 'new_m_g_pre_mix': 'new_m', 'new_m_g_post_mix': 'new_m', 'new_m_g_pre_ffn': 'new_m', 'new_m_g_post_ffn': 'new_m', 'new_m_w_gate': 'new_m', 'new_m_w_up': 'new_m', 'new_m_w_down': 'new_m', 'new_v_w_in': 'new_v', 'new_v_conv_w': 'new_v', 'new_v_conv_b': 'new_v', 'new_v_conv_ln_g': 'new_v', 'new_v_conv_ln_b': 'new_v', 'new_v_pool_w': 'new_v', 'new_v_pool_scale': 'new_v', 'new_v_g_mix': 'new_v', 'new_v_w_out': 'new_v', 'new_v_g_pre_mix': 'new_v', 'new_v_g_post_mix': 'new_v', 'new_v_g_pre_ffn': 'new_v', 'new_v_g_post_ffn': 'new_v', 'new_v_w_gate': 'new_v', 'new_v_w_up': 'new_v', 'new_v_w_down': 'new_v'}


def _forward(args):
    return _fwd_reference(*[args[k] for k in FWD_PARAMS])


def _output_shape():
    def fwd():
        inp = _fwd_setup_inputs(0)
        return _fwd_reference(*[inp[k] for k in FWD_PARAMS])
    out = _jax.eval_shape(fwd)
    return out.shape, out.dtype

N_MICROBATCH = 1
ADAM_LR = 0.001
ADAM_B1 = 0.9
ADAM_B2 = 0.999
ADAM_EPS = 1e-08
ADAM_WD = 0.01
ADAM_STEP = 10
PER_EXAMPLE_BATCH_AXIS = {'x': 0, 'loss_target': 0}
SHARED_INPUTS = []
_WEIGHT_DTYPES = {'w_in': _jnp.float32, 'conv_w': _jnp.float32, 'conv_b': _jnp.float32, 'conv_ln_g': _jnp.float32, 'conv_ln_b': _jnp.float32, 'pool_w': _jnp.float32, 'pool_scale': _jnp.float32, 'g_mix': _jnp.float32, 'w_out': _jnp.float32, 'g_pre_mix': _jnp.float32, 'g_post_mix': _jnp.float32, 'g_pre_ffn': _jnp.float32, 'g_post_ffn': _jnp.float32, 'w_gate': _jnp.float32, 'w_up': _jnp.float32, 'w_down': _jnp.float32}
MOMENT_SCALE = {'w_in': 7.372568e+00, 'conv_w': 6.089476e+00, 'conv_b': 7.152287e+01, 'conv_ln_g': 2.710748e+01, 'conv_ln_b': 3.812543e+01, 'pool_w': 1.490072e+00, 'pool_scale': 1.643142e+00, 'g_mix': 1.557024e+01, 'w_out': 1.570124e+01, 'g_pre_mix': 1.070702e+01, 'g_post_mix': 3.667153e+01, 'g_pre_ffn': 3.900162e+00, 'g_post_ffn': 3.213633e+01, 'w_gate': 1.084041e+00, 'w_up': 2.129229e+00, 'w_down': 3.506233e+00}


def _to_microbatches(a, axis):
    t = _jnp.moveaxis(a, axis, 0)
    t = t.reshape((N_MICROBATCH, t.shape[0] // N_MICROBATCH) + t.shape[1:])
    return _jnp.moveaxis(t, 1, axis + 1)


def setup_inputs(seed: int = 0) -> dict:
    inp = _fwd_setup_inputs(seed)
    key = _jax.random.fold_in(_jax.random.key(seed), 7919)
    shape, _ = _output_shape()
    out = dict(inp)
    out["loss_target"] = _jax.random.normal(_jax.random.fold_in(key, 0), shape, _jnp.float32)
    for i, name in enumerate(TWIN_WEIGHTS):
        w = inp[name].astype(_jnp.float32)
        if MOMENT_SCALE is None:
            s = _jnp.sqrt(_jnp.mean(_jnp.square(w)) + 1e-30)
        else:
            s = MOMENT_SCALE[name]
        km, kv = _jax.random.split(_jax.random.fold_in(key, i + 1))
        out[name] = w
        out["m_" + name] = s * _jax.random.normal(km, w.shape, _jnp.float32)
        out["v_" + name] = (s * s) * _jax.random.uniform(kv, w.shape, _jnp.float32, 0.5, 1.5)
    if N_MICROBATCH > 1:
        for name, axis in PER_EXAMPLE_BATCH_AXIS.items():
            out[name] = _to_microbatches(out[name], axis)
    return {'x': out['x'], 'w_in': out['w_in'], 'conv_w': out['conv_w'], 'conv_b': out['conv_b'], 'conv_ln_g': out['conv_ln_g'], 'conv_ln_b': out['conv_ln_b'], 'pool_w': out['pool_w'], 'pool_scale': out['pool_scale'], 'g_mix': out['g_mix'], 'w_out': out['w_out'], 'g_pre_mix': out['g_pre_mix'], 'g_post_mix': out['g_post_mix'], 'g_pre_ffn': out['g_pre_ffn'], 'g_post_ffn': out['g_post_ffn'], 'w_gate': out['w_gate'], 'w_up': out['w_up'], 'w_down': out['w_down'], 'loss_target': out['loss_target'], 'm_w_in': out['m_w_in'], 'm_conv_w': out['m_conv_w'], 'm_conv_b': out['m_conv_b'], 'm_conv_ln_g': out['m_conv_ln_g'], 'm_conv_ln_b': out['m_conv_ln_b'], 'm_pool_w': out['m_pool_w'], 'm_pool_scale': out['m_pool_scale'], 'm_g_mix': out['m_g_mix'], 'm_w_out': out['m_w_out'], 'm_g_pre_mix': out['m_g_pre_mix'], 'm_g_post_mix': out['m_g_post_mix'], 'm_g_pre_ffn': out['m_g_pre_ffn'], 'm_g_post_ffn': out['m_g_post_ffn'], 'm_w_gate': out['m_w_gate'], 'm_w_up': out['m_w_up'], 'm_w_down': out['m_w_down'], 'v_w_in': out['v_w_in'], 'v_conv_w': out['v_conv_w'], 'v_conv_b': out['v_conv_b'], 'v_conv_ln_g': out['v_conv_ln_g'], 'v_conv_ln_b': out['v_conv_ln_b'], 'v_pool_w': out['v_pool_w'], 'v_pool_scale': out['v_pool_scale'], 'v_g_mix': out['v_g_mix'], 'v_w_out': out['v_w_out'], 'v_g_pre_mix': out['v_g_pre_mix'], 'v_g_post_mix': out['v_g_post_mix'], 'v_g_pre_ffn': out['v_g_pre_ffn'], 'v_g_post_ffn': out['v_g_post_ffn'], 'v_w_gate': out['v_w_gate'], 'v_w_up': out['v_w_up'], 'v_w_down': out['v_w_down']}


def _loss(weights, diff, rest, loss_target):
    with _jax.named_scope("forward"):
        args = {**rest, TWIN_DIFF_INPUT: diff, **{k: w.astype(_WEIGHT_DTYPES[k]) for k, w in weights.items()}}
        y = _forward(args)
    with _jax.named_scope("loss_head"):
        err = _jnp.square(y.astype(_jnp.float32) - loss_target)
        return 0.5 * _jnp.sum(_jnp.mean(err, axis=-1)) if err.ndim else 0.5 * err


def _adamw(w, g, m, v):
    m = ADAM_B1 * m + (1.0 - ADAM_B1) * g
    v = ADAM_B2 * v + (1.0 - ADAM_B2) * _jnp.square(g)
    m_hat = m / (1.0 - ADAM_B1 ** ADAM_STEP)
    v_hat = v / (1.0 - ADAM_B2 ** ADAM_STEP)
    delta = -ADAM_LR * (m_hat / (_jnp.sqrt(v_hat) + ADAM_EPS) + ADAM_WD * w)
    return delta, m, v


def reference(x, w_in, conv_w, conv_b, conv_ln_g, conv_ln_b, pool_w, pool_scale, g_mix, w_out, g_pre_mix, g_post_mix, g_pre_ffn, g_post_ffn, w_gate, w_up, w_down, loss_target, m_w_in, m_conv_w, m_conv_b, m_conv_ln_g, m_conv_ln_b, m_pool_w, m_pool_scale, m_g_mix, m_w_out, m_g_pre_mix, m_g_post_mix, m_g_pre_ffn, m_g_post_ffn, m_w_gate, m_w_up, m_w_down, v_w_in, v_conv_w, v_conv_b, v_conv_ln_g, v_conv_ln_b, v_pool_w, v_pool_scale, v_g_mix, v_w_out, v_g_pre_mix, v_g_post_mix, v_g_pre_ffn, v_g_post_ffn, v_w_gate, v_w_up, v_w_down):
    given = dict(x=x, w_in=w_in, conv_w=conv_w, conv_b=conv_b, conv_ln_g=conv_ln_g, conv_ln_b=conv_ln_b, pool_w=pool_w, pool_scale=pool_scale, g_mix=g_mix, w_out=w_out, g_pre_mix=g_pre_mix, g_post_mix=g_post_mix, g_pre_ffn=g_pre_ffn, g_post_ffn=g_post_ffn, w_gate=w_gate, w_up=w_up, w_down=w_down, loss_target=loss_target, m_w_in=m_w_in, m_conv_w=m_conv_w, m_conv_b=m_conv_b, m_conv_ln_g=m_conv_ln_g, m_conv_ln_b=m_conv_ln_b, m_pool_w=m_pool_w, m_pool_scale=m_pool_scale, m_g_mix=m_g_mix, m_w_out=m_w_out, m_g_pre_mix=m_g_pre_mix, m_g_post_mix=m_g_post_mix, m_g_pre_ffn=m_g_pre_ffn, m_g_post_ffn=m_g_post_ffn, m_w_gate=m_w_gate, m_w_up=m_w_up, m_w_down=m_w_down, v_w_in=v_w_in, v_conv_w=v_conv_w, v_conv_b=v_conv_b, v_conv_ln_g=v_conv_ln_g, v_conv_ln_b=v_conv_ln_b, v_pool_w=v_pool_w, v_pool_scale=v_pool_scale, v_g_mix=v_g_mix, v_w_out=v_w_out, v_g_pre_mix=v_g_pre_mix, v_g_post_mix=v_g_post_mix, v_g_pre_ffn=v_g_pre_ffn, v_g_post_ffn=v_g_post_ffn, v_w_gate=v_w_gate, v_w_up=v_w_up, v_w_down=v_w_down)
    weights = {n: given[n] for n in TWIN_WEIGHTS}
    shared = {n: given[n] for n in SHARED_INPUTS}
    per_example = {n: given[n] for n in ['x']}
    grad_fn = _jax.value_and_grad(_loss, argnums=(0, 1))

    def one_microbatch(ex, loss_target):
        ex = dict(ex)
        diff = ex.pop(TWIN_DIFF_INPUT)
        return grad_fn(weights, diff, {**shared, **ex}, loss_target)

    if N_MICROBATCH == 1:
        loss, (grad_w, grad_x) = one_microbatch(per_example, given["loss_target"])
    else:
        def body(carry, xs):
            loss_sum, grad_sum = carry
            l_k, (gw_k, gx_k) = one_microbatch(xs[0], xs[1])
            with _jax.named_scope("update"):
                return (loss_sum + l_k, _jax.tree.map(_jnp.add, grad_sum, gw_k)), gx_k

        init = (_jnp.zeros((), _jnp.float32), _jax.tree.map(_jnp.zeros_like, weights))
        (loss, grad_w), grad_x = _jax.lax.scan(body, init, (per_example, given["loss_target"]))
    with _jax.named_scope("update"):
        delta_w, new_m, new_v = {}, {}, {}
        for n in TWIN_WEIGHTS:
            delta_w[n], new_m[n], new_v[n] = _adamw(weights[n], grad_w[n], given["m_" + n], given["v_" + n])
    return (loss, grad_x, *[grad_w[n] for n in TWIN_WEIGHTS], *[delta_w[n] for n in TWIN_WEIGHTS],
            *[new_m[n] for n in TWIN_WEIGHTS], *[new_v[n] for n in TWIN_WEIGHTS])
```

```python
import functools
import math

import jax
import jax.numpy as jnp
from jax import lax
from jax.experimental import pallas as pl
from jax.experimental.pallas import tpu as pltpu

F32 = jnp.float32
BF16 = jnp.bfloat16

N_DEV = 8
DEPTH = 4
EPS = 1e-6
NEG = -1e30

C_A = 512
N_HEADS = 16
HEAD_DIM = 64
C_B = N_HEADS * HEAD_DIM
C_C = 512
POOL_WINDOWS = (2, 4, 8, 16)
C_G = C_C // len(POOL_WINDOWS)
CONV_WIDTH = 31
CONV_HALF = CONV_WIDTH // 2
DILATIONS = (1, 4, 16)
ATTN_HALF = 64
ROT_DIM = HEAD_DIM // 4
ROPE_THETA = 500000.0

ADAM_LR = 0.001
ADAM_B1 = 0.9
ADAM_B2 = 0.999
ADAM_EPS = 1e-08
ADAM_WD = 0.01
ADAM_STEP = 10

LANES = 128
VMEM_LIMIT = 56 * 1024 * 1024
ROW_TILE = 256
SEQ_TILE = 256
ATTN_BLOCK = 128
MESH = pl.DeviceIdType.MESH


def _params(*sem):
    return pltpu.CompilerParams(dimension_semantics=sem, vmem_limit_bytes=VMEM_LIMIT)


def _tile(n, target):
    if n <= target:
        return n
    t = (target // LANES) * LANES
    while t >= LANES:
        if n % t == 0:
            return t
        t -= LANES
    return n


def _rows(n, target):
    t = min(n, target)
    while n % t:
        t //= 2
    return t


def matmul(a, b, mode, out_dtype, name, tm=1024, tn=512, tk=2048):
    if mode == "nn":
        (M, K), (_, N) = a.shape, b.shape
    elif mode == "nt":
        (M, K), (N, _) = a.shape, b.shape
    else:
        (K, M), (_, N) = a.shape, b.shape
    tm, tn, tk = _tile(M, tm), _tile(N, tn), _tile(K, tk)
    nk = K // tk
    dims = {"nn": (((1,), (0,)), ((), ())), "nt": (((1,), (1,)), ((), ())), "tn": (((0,), (0,)), ((), ()))}[mode]

    def body(a_ref, b_ref, o_ref, *acc):
        p = lax.dot_general(a_ref[...], b_ref[...], dims, preferred_element_type=F32)
        if nk == 1:
            o_ref[...] = p.astype(o_ref.dtype)
            return
        acc_ref, = acc
        k = pl.program_id(2)

        @pl.when(k == 0)
        def _():
            acc_ref[...] = p

        @pl.when(k > 0)
        def _():
            acc_ref[...] += p

        @pl.when(k == nk - 1)
        def _():
            o_ref[...] = acc_ref[...].astype(o_ref.dtype)

    if mode == "nn":
        a_spec = pl.BlockSpec((tm, tk), lambda i, j, k: (i, k))
        b_spec = pl.BlockSpec((tk, tn), lambda i, j, k: (k, j))
    elif mode == "nt":
        a_spec = pl.BlockSpec((tm, tk), lambda i, j, k: (i, k))
        b_spec = pl.BlockSpec((tn, tk), lambda i, j, k: (j, k))
    else:
        a_spec = pl.BlockSpec((tk, tm), lambda i, j, k: (k, i))
        b_spec = pl.BlockSpec((tk, tn), lambda i, j, k: (k, j))
    return pl.pallas_call(
        body, name=name, grid=(M // tm, N // tn, nk),
        in_specs=[a_spec, b_spec], out_specs=pl.BlockSpec((tm, tn), lambda i, j, k: (i, j)),
        out_shape=jax.ShapeDtypeStruct((M, N), out_dtype),
        scratch_shapes=[pltpu.VMEM((tm, tn), F32)] if nk > 1 else [],
        compiler_params=_params("parallel", "parallel", "arbitrary"),
    )(a, b)


def _rms(t):
    return t * lax.rsqrt(jnp.mean(t * t, axis=-1, keepdims=True) + EPS)


def _rms_bwd(t, dn):
    r = lax.rsqrt(jnp.mean(t * t, axis=-1, keepdims=True) + EPS)
    n = t * r
    return r * (dn - n * jnp.mean(dn * n, axis=-1, keepdims=True)), n


def _row_spec(tr, d):
    return pl.BlockSpec((tr, d), lambda i: (i, 0))


def _vec_spec(d):
    return pl.BlockSpec((1, d), lambda i: (0, 0))


def rms_fwd(x, g, name):
    S, D = x.shape
    tr = _rows(S, ROW_TILE)

    def body(x_ref, g_ref, o_ref):
        o_ref[...] = (_rms(x_ref[...].astype(F32)) * g_ref[...]).astype(o_ref.dtype)

    return pl.pallas_call(
        body, name=name, grid=(S // tr,), in_specs=[_row_spec(tr, D), _vec_spec(D)], out_specs=_row_spec(tr, D),
        out_shape=jax.ShapeDtypeStruct((S, D), BF16), compiler_params=_params("parallel"),
    )(x, g)


def norm_residual(z, x, g, name):
    S, D = x.shape
    tr = _rows(S, ROW_TILE)

    def body(z_ref, x_ref, g_ref, o_ref):
        o_ref[...] = x_ref[...] + _rms(z_ref[...].astype(F32)) * g_ref[...]

    return pl.pallas_call(
        body, name=name, grid=(S // tr,), in_specs=[_row_spec(tr, D), _row_spec(tr, D), _vec_spec(D)],
        out_specs=_row_spec(tr, D), out_shape=jax.ShapeDtypeStruct((S, D), F32), compiler_params=_params("parallel"),
    )(z, x, g)


def rms_bwd(t, dy, g, res, out_dtype, name):
    S, D = t.shape
    tr = _rows(S, ROW_TILE)
    has_res = res is not None

    def body(t_ref, dy_ref, g_ref, *rest):
        if has_res:
            res_ref, dt_ref, dg_ref = rest
        else:
            dt_ref, dg_ref = rest
        dyv = dy_ref[...].astype(F32)
        dt, n = _rms_bwd(t_ref[...].astype(F32), dyv * g_ref[...])
        if has_res:
            dt = dt + res_ref[...]
        dt_ref[...] = dt.astype(dt_ref.dtype)

        @pl.when(pl.program_id(0) == 0)
        def _():
            dg_ref[...] = jnp.zeros_like(dg_ref)

        dg_ref[...] += jnp.sum(dyv * n, axis=0, keepdims=True)

    ins = [t, dy, g] + ([res] if has_res else [])
    specs = [_row_spec(tr, D), _row_spec(tr, D), _vec_spec(D)] + ([_row_spec(tr, D)] if has_res else [])
    return pl.pallas_call(
        body, name=name, grid=(S // tr,), in_specs=specs, out_specs=[_row_spec(tr, D), _vec_spec(D)],
        out_shape=[jax.ShapeDtypeStruct((S, D), out_dtype), jax.ShapeDtypeStruct((1, D), F32)],
        compiler_params=_params("arbitrary"),
    )(*ins)


def swiglu_fwd(gu, name):
    S, F2 = gu.shape
    F = F2 // 2
    tr = _rows(S, ROW_TILE)

    def body(g_ref, u_ref, o_ref):
        g = g_ref[...].astype(F32)
        o_ref[...] = (g * jax.nn.sigmoid(g) * u_ref[...].astype(F32)).astype(o_ref.dtype)

    return pl.pallas_call(
        body, name=name, grid=(S // tr,),
        in_specs=[pl.BlockSpec((tr, F), lambda i: (i, 0)), pl.BlockSpec((tr, F), lambda i: (i, 1))],
        out_specs=_row_spec(tr, F), out_shape=jax.ShapeDtypeStruct((S, F), BF16), compiler_params=_params("parallel"),
    )(gu, gu)


def swiglu_bwd(gu, da, name):
    S, F2 = gu.shape
    F = F2 // 2
    tr = _rows(S, ROW_TILE)

    def body(g_ref, u_ref, da_ref, o_ref):
        g = g_ref[...].astype(F32)
        u = u_ref[...].astype(F32)
        dav = da_ref[...].astype(F32)
        sig = jax.nn.sigmoid(g)
        o_ref[:, :F] = (dav * u * (sig * (1.0 + g * (1.0 - sig)))).astype(o_ref.dtype)
        o_ref[:, F:] = (dav * (g * sig)).astype(o_ref.dtype)

    return pl.pallas_call(
        body, name=name, grid=(S // tr,),
        in_specs=[pl.BlockSpec((tr, F), lambda i: (i, 0)), pl.BlockSpec((tr, F), lambda i: (i, 1)), _row_spec(tr, F)],
        out_specs=_row_spec(tr, F2), out_shape=jax.ShapeDtypeStruct((S, F2), BF16), compiler_params=_params("parallel"),
    )(gu, gu, da)


def loss_head(y, target, name):
    S, D = y.shape
    tr = _rows(S, ROW_TILE)

    def body(y_ref, t_ref, dy_ref, sq_ref):
        e = y_ref[...] - t_ref[...]
        dy_ref[...] = e * (1.0 / D)

        @pl.when(pl.program_id(0) == 0)
        def _():
            sq_ref[...] = jnp.zeros_like(sq_ref)

        sq_ref[...] += jnp.sum(e * e, axis=0, keepdims=True)

    return pl.pallas_call(
        body, name=name, grid=(S // tr,), in_specs=[_row_spec(tr, D), _row_spec(tr, D)],
        out_specs=[_row_spec(tr, D), _vec_spec(D)],
        out_shape=[jax.ShapeDtypeStruct((S, D), F32), jax.ShapeDtypeStruct((1, D), F32)],
        compiler_params=_params("arbitrary"),
    )(y, target)


def _halo_specs(bs, halo, width, col, n_rows):
    per = bs // halo
    last = n_rows // halo - 1
    cur = pl.BlockSpec((bs, width), lambda i: (i, col))
    prev = pl.BlockSpec((halo, width), lambda i: (jnp.maximum(i * per - 1, 0), col))
    nxt = pl.BlockSpec((halo, width), lambda i: (jnp.minimum((i + 1) * per, last), col))
    return prev, cur, nxt


def _glu(a, g):
    return a.astype(F32) * jax.nn.sigmoid(g.astype(F32))


def _layernorm_silu(c, lg, lb):
    mu = jnp.mean(c, axis=-1, keepdims=True)
    cc = c - mu
    rstd = lax.rsqrt(jnp.mean(cc * cc, axis=-1, keepdims=True) + EPS)
    xh = cc * rstd
    ln = xh * lg + lb
    sig = jax.nn.sigmoid(ln)
    return xh, rstd, ln, sig


def conv_fwd(proj, cw, cb, lg, lb, gm, name):
    S = proj.shape[0]
    bs = _rows(S, SEQ_TILE)
    nb = S // bs
    H = 16

    def body(ap, ac, an, gp, gc, gn, cw_ref, cb_ref, lg_ref, lb_ref, gm_ref, y_ref, win_ref):
        i = pl.program_id(0)
        win_ref[0:H, :] = jnp.where(i > 0, _glu(ap[...], gp[...]), 0.0)
        win_ref[H:H + bs, :] = _glu(ac[...], gc[...])
        win_ref[H + bs:, :] = jnp.where(i < nb - 1, _glu(an[...], gn[...]), 0.0)
        c = jnp.zeros((bs, C_A), F32) + cb_ref[...]
        for t in range(CONV_WIDTH):
            c = c + cw_ref[t:t + 1, :] * win_ref[pl.ds(H - CONV_HALF + t, bs), :]
        _, _, ln, sig = _layernorm_silu(c, lg_ref[...], lb_ref[...])
        y_ref[...] = (_rms(ln * sig) * gm_ref[...]).astype(y_ref.dtype)

    a_specs = _halo_specs(bs, H, C_A, 0, S)
    g_specs = _halo_specs(bs, H, C_A, 1, S)
    vec = _vec_spec(C_A)
    return pl.pallas_call(
        body, name=name, grid=(nb,),
        in_specs=[*a_specs, *g_specs, pl.BlockSpec((32, C_A), lambda i: (0, 0)), vec, vec, vec, vec],
        out_specs=_row_spec(bs, C_A), out_shape=jax.ShapeDtypeStruct((S, C_A), BF16),
        scratch_shapes=[pltpu.VMEM((bs + 2 * H, C_A), F32)], compiler_params=_params("parallel"),
    )(proj, proj, proj, proj, proj, proj, cw, cb, lg, lb, gm)


def conv_bwd(proj, dy, cw, cb, lg, lb, gm, name):
    S = proj.shape[0]
    bs = _rows(S, SEQ_TILE)
    nb = S // bs
    H = 32
    HC = 16
    bc = bs + 2 * HC

    def body(ap, ac, an, gp, gc, gn, dp, dc_, dn, cw_ref, cb_ref, lg_ref, lb_ref, gm_ref,
             dproj_ref, dcw_ref, dcb_ref, dlg_ref, dlb_ref, dgm_ref, win_ref, dcs_ref):
        i = pl.program_id(0)
        win_ref[0:H, :] = jnp.where(i > 0, _glu(ap[...], gp[...]), 0.0)
        win_ref[H:H + bs, :] = _glu(ac[...], gc[...])
        win_ref[H + bs:, :] = jnp.where(i < nb - 1, _glu(an[...], gn[...]), 0.0)
        c = jnp.zeros((bc, C_A), F32) + cb_ref[...]
        for t in range(CONV_WIDTH):
            c = c + cw_ref[t:t + 1, :] * win_ref[pl.ds(H - HC - CONV_HALF + t, bc), :]
        xh, rstd, ln, sig = _layernorm_silu(c, lg_ref[...], lb_ref[...])
        ya = ln * sig
        dyv = jnp.concatenate([dp[...], dc_[...], dn[...]], axis=0).astype(F32)
        dya, n = _rms_bwd(ya, dyv * gm_ref[...])
        dln = dya * (sig * (1.0 + ln * (1.0 - sig)))
        dxh = dln * lg_ref[...]
        dcv = rstd * (dxh - jnp.mean(dxh, axis=-1, keepdims=True) - xh * jnp.mean(dxh * xh, axis=-1, keepdims=True))
        pos = i * bs - HC + lax.broadcasted_iota(jnp.int32, (bc, 1), 0)
        dcv = jnp.where((pos >= 0) & (pos < S), dcv, 0.0)
        dcs_ref[...] = dcv

        @pl.when(i == 0)
        def _():
            for r in (dcw_ref, dcb_ref, dlg_ref, dlb_ref, dgm_ref):
                r[...] = jnp.zeros_like(r)

        mid = slice(HC, HC + bs)
        dcb_ref[...] += jnp.sum(dcv[mid], axis=0, keepdims=True)
        dlg_ref[...] += jnp.sum((dln * xh)[mid], axis=0, keepdims=True)
        dlb_ref[...] += jnp.sum(dln[mid], axis=0, keepdims=True)
        dgm_ref[...] += jnp.sum((dyv * n)[mid], axis=0, keepdims=True)
        dh = jnp.zeros((bs, C_A), F32)
        dcm = dcv[mid]
        for t in range(CONV_WIDTH):
            dh = dh + cw_ref[t:t + 1, :] * dcs_ref[pl.ds(HC + CONV_HALF - t, bs), :]
            dcw_ref[t:t + 1, :] += jnp.sum(dcm * win_ref[pl.ds(H - CONV_HALF + t, bs), :], axis=0, keepdims=True)
        a = ac[...].astype(F32)
        sg = jax.nn.sigmoid(gc[...].astype(F32))
        dproj_ref[:, :C_A] = (dh * sg).astype(dproj_ref.dtype)
        dproj_ref[:, C_A:] = (dh * a * sg * (1.0 - sg)).astype(dproj_ref.dtype)

    a_specs = _halo_specs(bs, H, C_A, 0, S)
    g_specs = _halo_specs(bs, H, C_A, 1, S)
    d_specs = _halo_specs(bs, HC, C_A, 0, S)
    vec = _vec_spec(C_A)
    full = pl.BlockSpec((32, C_A), lambda i: (0, 0))
    vshape = jax.ShapeDtypeStruct((1, C_A), F32)
    return pl.pallas_call(
        body, name=name, grid=(nb,),
        in_specs=[*a_specs, *g_specs, *d_specs, full, vec, vec, vec, vec],
        out_specs=[_row_spec(bs, 2 * C_A), full, vec, vec, vec, vec],
        out_shape=[jax.ShapeDtypeStruct((S, 2 * C_A), BF16), jax.ShapeDtypeStruct((32, C_A), F32),
                   vshape, vshape, vshape, vshape],
        scratch_shapes=[pltpu.VMEM((bs + 2 * H, C_A), F32), pltpu.VMEM((bc, C_A), F32)],
        compiler_params=_params("arbitrary"),
    )(proj, proj, proj, proj, proj, proj, dy, dy, dy, cw, cb, lg, lb, gm)


POOL_HALO = 16


def _shift(x, k):
    n = x.shape[0]
    return pltpu.roll(x, (-k) % n, axis=0)


def _pool_means(u, pos, S):
    w2 = _shift(u, -1) + u
    w4 = _shift(w2, -1) + _shift(w2, 1)
    w8 = _shift(w4, -2) + _shift(w4, 2)
    w16 = _shift(w8, -4) + _shift(w8, 4)
    sums = (w2, w4, w8, w16)
    lane = lax.broadcasted_iota(jnp.int32, (1, C_C), 1)
    total = jnp.zeros_like(u)
    inv = jnp.zeros_like(u)
    for gi, win in enumerate(POOL_WINDOWS):
        cnt = jnp.minimum(pos + (win - win // 2), S) - jnp.maximum(pos - win // 2, 0)
        icnt = 1.0 / jnp.maximum(cnt, 1).astype(F32)
        sel = (lane >= gi * C_G) & (lane < (gi + 1) * C_G)
        total = jnp.where(sel, sums[gi], total)
        inv = jnp.where(sel, icnt, inv)
    return total * inv - u, inv


def _pool_adjoint(e):
    v2 = e + _shift(e, 1)
    v4 = _shift(v2, -1) + _shift(v2, 1)
    v8 = _shift(v4, -2) + _shift(v4, 2)
    v16 = _shift(v8, -4) + _shift(v8, 4)
    sums = (v2, v4, v8, v16)
    lane = lax.broadcasted_iota(jnp.int32, (1, C_C), 1)
    out = jnp.zeros_like(e)
    for gi in range(len(POOL_WINDOWS)):
        sel = (lane >= gi * C_G) & (lane < (gi + 1) * C_G)
        out = jnp.where(sel, sums[gi], out)
    return out


def _pool_window(up, uc, un, i, nb, bs, S):
    H = POOL_HALO
    u = jnp.concatenate([jnp.where(i > 0, up[...].astype(F32), 0.0), uc[...].astype(F32),
                         jnp.where(i < nb - 1, un[...].astype(F32), 0.0)], axis=0)
    pos = i * bs - H + lax.broadcasted_iota(jnp.int32, (bs + 2 * H, 1), 0)
    return u, pos


def _pool_mix(pooled, pw_ref):
    outs = []
    for gi in range(len(POOL_WINDOWS)):
        outs.append(jnp.dot(pooled[:, gi * C_G:(gi + 1) * C_G].astype(BF16), pw_ref[gi].astype(BF16),
                            preferred_element_type=F32))
    return jnp.concatenate(outs, axis=1)


def pool_fwd(proj, pw, ps, gm, name):
    S, width = proj.shape
    col = width // C_C - 1
    bs = _rows(S, SEQ_TILE)
    nb = S // bs
    H = POOL_HALO

    def body(up, uc, un, pw_ref, ps_ref, gm_ref, y_ref):
        i = pl.program_id(0)
        u, pos = _pool_window(up, uc, un, i, nb, bs, S)
        pooled, _ = _pool_means(u, pos, S)
        mixed = _pool_mix(pooled[H:H + bs], pw_ref)
        y_ref[...] = (_rms(mixed * ps_ref[...]) * gm_ref[...]).astype(y_ref.dtype)

    vec = _vec_spec(C_C)
    return pl.pallas_call(
        body, name=name, grid=(nb,),
        in_specs=[*_halo_specs(bs, H, C_C, col, S), pl.BlockSpec((4, C_G, C_G), lambda i: (0, 0, 0)), vec, vec],
        out_specs=_row_spec(bs, C_C), out_shape=jax.ShapeDtypeStruct((S, C_C), BF16),
        compiler_params=_params("parallel"),
    )(proj, proj, proj, pw, ps, gm)


def pool_bwd(proj, dy, pw, ps, gm, name):
    S, width = proj.shape
    col = width // C_C - 1
    dcol = dy.shape[1] // C_C - 1
    bs = _rows(S, SEQ_TILE)
    nb = S // bs
    H = POOL_HALO
    W = bs + 2 * H

    def body(up, uc, un, dp, dc_, dn, pw_ref, ps_ref, gm_ref, du_ref, dpw_ref, dps_ref, dgm_ref):
        i = pl.program_id(0)
        u, pos = _pool_window(up, uc, un, i, nb, bs, S)
        pooled, inv = _pool_means(u, pos, S)
        mixed = _pool_mix(pooled, pw_ref)
        dyv = jnp.concatenate([dp[...], dc_[...], dn[...]], axis=0).astype(F32)
        dyc, n = _rms_bwd(mixed * ps_ref[...], dyv * gm_ref[...])
        dmixed = dyc * ps_ref[...]
        dmb = dmixed.astype(BF16)
        dpooled = jnp.concatenate(
            [lax.dot_general(dmb[:, gi * C_G:(gi + 1) * C_G], pw_ref[gi].astype(BF16), (((1,), (1,)), ((), ())),
                             preferred_element_type=F32) for gi in range(len(POOL_WINDOWS))], axis=1)
        dpooled = jnp.where((pos >= 0) & (pos < S), dpooled, 0.0)
        du = _pool_adjoint(dpooled * inv) - dpooled
        du_ref[...] = du[H:H + bs].astype(du_ref.dtype)

        @pl.when(i == 0)
        def _():
            for r in (dpw_ref, dps_ref, dgm_ref):
                r[...] = jnp.zeros_like(r)

        mid = slice(H, H + bs)
        dps_ref[...] += jnp.sum((dyc * mixed)[mid], axis=0, keepdims=True)
        dgm_ref[...] += jnp.sum((dyv * n)[mid], axis=0, keepdims=True)
        pb = pooled[mid].astype(BF16)
        for gi in range(len(POOL_WINDOWS)):
            sl = slice(gi * C_G, (gi + 1) * C_G)
            dpw_ref[gi] += lax.dot_general(pb[:, sl], dmb[mid][:, sl], (((0,), (0,)), ((), ())),
                                           preferred_element_type=F32)

    vec = _vec_spec(C_C)
    full = pl.BlockSpec((4, C_G, C_G), lambda i: (0, 0, 0))
    vshape = jax.ShapeDtypeStruct((1, C_C), F32)
    return pl.pallas_call(
        body, name=name, grid=(nb,),
        in_specs=[*_halo_specs(bs, H, C_C, col, S), *_halo_specs(bs, H, C_C, dcol, S), full, vec, vec],
        out_specs=[_row_spec(bs, C_C), full, vec, vec],
        out_shape=[jax.ShapeDtypeStruct((S, C_C), BF16), jax.ShapeDtypeStruct((4, C_G, C_G), F32), vshape, vshape],
        compiler_params=_params("arbitrary"),
    )(proj, proj, proj, dy, dy, dy, pw, ps, gm)


def rope_tables(S):
    pos = jnp.arange(S, dtype=F32)
    inv = ROPE_THETA ** (-jnp.arange(0, ROT_DIM, 2, dtype=F32) / ROT_DIM)
    ang = pos[:, None] * inv[None, :]
    half = ROT_DIM // 2
    cos, sin = jnp.cos(ang), jnp.sin(ang)
    zeros = jnp.zeros((S, half), F32)
    rest = jnp.zeros((S, HEAD_DIM - ROT_DIM), F32)
    per_head = (jnp.concatenate([cos, cos, rest + 1.0], axis=1), jnp.concatenate([-sin, zeros, rest], axis=1),
                jnp.concatenate([zeros, sin, rest], axis=1))
    return tuple(jnp.tile(t, (1, LANES // HEAD_DIM)) for t in per_head)


def _rotate(t, c, s1, s2, sign):
    half = ROT_DIM // 2
    return t * c + sign * (pltpu.roll(t, LANES - half, axis=1) * s1 + pltpu.roll(t, half, axis=1) * s2)


def rope_fwd(proj, tables, name):
    S = proj.shape[0]
    tr = _rows(S, ROW_TILE)
    qcol = 2 * C_A // C_B

    def body(q_ref, k_ref, v_ref, c_ref, s1_ref, s2_ref, qo_ref, ko_ref, vo_ref):
        c, s1, s2 = c_ref[...], s1_ref[...], s2_ref[...]
        for p in range(C_B // LANES):
            sl = slice(p * LANES, (p + 1) * LANES)
            qo_ref[:, sl] = (_rotate(q_ref[:, sl].astype(F32), c, s1, s2, 1.0) * HEAD_DIM ** -0.5).astype(BF16)
            ko_ref[:, sl] = _rotate(k_ref[:, sl].astype(F32), c, s1, s2, 1.0).astype(BF16)
        vo_ref[...] = v_ref[...]

    tab = _row_spec(tr, LANES)
    return pl.pallas_call(
        body, name=name, grid=(S // tr,),
        in_specs=[pl.BlockSpec((tr, C_B), lambda i, col=qcol + n: (i, col)) for n in range(3)] + [tab, tab, tab],
        out_specs=[_row_spec(tr, C_B)] * 3,
        out_shape=[jax.ShapeDtypeStruct((S, C_B), BF16)] * 3, compiler_params=_params("parallel"),
    )(proj, proj, proj, *tables)


def rope_bwd(dqs, dks, dvs, tables, name):
    S = dqs[0].shape[0]
    tr = _rows(S, ROW_TILE)
    n = len(dqs)

    def body(*refs):
        dq_refs, dk_refs, dv_refs = refs[:n], refs[n:2 * n], refs[2 * n:3 * n]
        c_ref, s1_ref, s2_ref, o_ref = refs[3 * n:]
        c, s1, s2 = c_ref[...], s1_ref[...], s2_ref[...]
        for p in range(C_B // LANES):
            sl = slice(p * LANES, (p + 1) * LANES)
            dq = sum(r[:, sl].astype(F32) for r in dq_refs)
            dk = sum(r[:, sl].astype(F32) for r in dk_refs)
            dv = sum(r[:, sl].astype(F32) for r in dv_refs)
            o_ref[:, p * LANES:(p + 1) * LANES] = (_rotate(dq, c, s1, s2, -1.0) * HEAD_DIM ** -0.5).astype(BF16)
            o_ref[:, C_B + p * LANES:C_B + (p + 1) * LANES] = _rotate(dk, c, s1, s2, -1.0).astype(BF16)
            o_ref[:, 2 * C_B + p * LANES:2 * C_B + (p + 1) * LANES] = dv.astype(BF16)

    tab = _row_spec(tr, LANES)
    return pl.pallas_call(
        body, name=name, grid=(S // tr,),
        in_specs=[_row_spec(tr, C_B)] * (3 * n) + [tab, tab, tab],
        out_specs=_row_spec(tr, 3 * C_B), out_shape=jax.ShapeDtypeStruct((S, 3 * C_B), BF16),
        compiler_params=_params("parallel"),
    )(*dqs, *dks, *dvs, *tables)


def _attn_specs(bq, width, L):
    per = bq // ATTN_HALF
    last = L // ATTN_HALF - 1
    cur = pl.BlockSpec((None, bq, width), lambda r, j: (r, j, 0))
    prev = pl.BlockSpec((None, ATTN_HALF, width), lambda r, j: (r, jnp.maximum(j * per - 1, 0), 0))
    nxt = pl.BlockSpec((None, ATTN_HALF, width), lambda r, j: (r, jnp.minimum((j + 1) * per, last), 0))
    return prev, cur, nxt


def _window(refs, sl):
    return jnp.concatenate([r[:, sl] for r in refs], axis=0)


def _band_mask(j, bq, L, rows_are_window):
    bw = bq + 2 * ATTN_HALF
    if rows_are_window:
        rp = j * bq - ATTN_HALF + lax.broadcasted_iota(jnp.int32, (bw, 1), 0)
        cp = j * bq + lax.broadcasted_iota(jnp.int32, (1, bq), 1)
        return (jnp.abs(rp - cp) <= ATTN_HALF) & (rp >= 0) & (rp < L)
    rp = j * bq + lax.broadcasted_iota(jnp.int32, (bq, 1), 0)
    cp = j * bq - ATTN_HALF + lax.broadcasted_iota(jnp.int32, (1, bw), 1)
    return (jnp.abs(rp - cp) <= ATTN_HALF) & (cp >= 0) & (cp < L)


def _head_col(stats, h):
    lane = lax.broadcasted_iota(jnp.int32, (1, LANES), 1)
    return jnp.sum(jnp.where(lane == h, stats, 0.0), axis=1, keepdims=True)


_NT = (((1,), (1,)), ((), ()))
_TN = (((0,), (0,)), ((), ()))


def attn_fwd_pattern(qd, kd, vd, name):
    d, L, _ = qd.shape
    bq = _rows(L, ATTN_BLOCK)

    def body(q_ref, kp, kc, kn, vp, vc, vn, o_ref, lse_ref):
        j = pl.program_id(1)
        mask = _band_mask(j, bq, L, False)
        lane = lax.broadcasted_iota(jnp.int32, (1, LANES), 1)
        first = lane < HEAD_DIM
        lse = jnp.zeros((bq, LANES), F32)
        for p in range(C_B // LANES):
            sl = slice(p * LANES, (p + 1) * LANES)
            q = q_ref[:, sl]
            kw = _window((kp, kc, kn), sl)
            vw = _window((vp, vc, vn), sl)
            outs = []
            for hh in range(2):
                qh = jnp.where(first == (hh == 0), q, jnp.zeros_like(q))
                s = jnp.where(mask, lax.dot_general(qh, kw, _NT, preferred_element_type=F32), NEG)
                m = jnp.max(s, axis=1, keepdims=True)
                e = jnp.exp(s - m)
                l = jnp.sum(e, axis=1, keepdims=True)
                outs.append(jnp.dot(e.astype(BF16), vw, preferred_element_type=F32) * (1.0 / l))
                lse = jnp.where(lane == 2 * p + hh, m + jnp.log(l), lse)
            o_ref[:, sl] = jnp.where(first, outs[0], outs[1]).astype(o_ref.dtype)
        lse_ref[...] = lse

    kv = _attn_specs(bq, C_B, L)
    return pl.pallas_call(
        body, name=name, grid=(d, L // bq), in_specs=[kv[1], *kv, *kv],
        out_specs=[kv[1], pl.BlockSpec((None, bq, LANES), lambda r, j: (r, j, 0))],
        out_shape=[jax.ShapeDtypeStruct((d, L, C_B), BF16), jax.ShapeDtypeStruct((d, L, LANES), F32)],
        compiler_params=_params("parallel", "parallel"),
    )(qd, kd, kd, kd, vd, vd, vd)


def attn_combine(os_, lses, gm, name):
    S = os_[0].shape[0]
    tr = _rows(S, ROW_TILE)
    n = len(os_)

    def body(*refs):
        o_refs, l_refs = refs[:n], refs[n:2 * n]
        gm_ref, y_ref, out_ref, lse_ref = refs[2 * n:]
        ls = [r[...] for r in l_refs]
        mx = functools.reduce(jnp.maximum, ls)
        ws = [jnp.exp(l - mx) for l in ls]
        den = sum(ws)
        lse_ref[...] = mx + jnp.log(den)
        wn = [w / den for w in ws]
        lane = lax.broadcasted_iota(jnp.int32, (1, LANES), 1)
        first = lane < HEAD_DIM
        blocks = []
        for p in range(C_B // LANES):
            sl = slice(p * LANES, (p + 1) * LANES)
            acc = jnp.zeros((tr, LANES), F32)
            for w, o_ref in zip(wn, o_refs):
                acc = acc + jnp.where(first, _head_col(w, 2 * p), _head_col(w, 2 * p + 1)) * o_ref[:, sl].astype(F32)
            blocks.append(acc)
        out = jnp.concatenate(blocks, axis=1)
        out_ref[...] = out.astype(out_ref.dtype)
        y_ref[...] = (_rms(out) * gm_ref[...]).astype(y_ref.dtype)

    st = _row_spec(tr, LANES)
    return pl.pallas_call(
        body, name=name, grid=(S // tr,),
        in_specs=[_row_spec(tr, C_B)] * n + [st] * n + [_vec_spec(C_B)],
        out_specs=[_row_spec(tr, C_B), _row_spec(tr, C_B), st],
        out_shape=[jax.ShapeDtypeStruct((S, C_B), BF16), jax.ShapeDtypeStruct((S, C_B), BF16),
                   jax.ShapeDtypeStruct((S, LANES), F32)],
        compiler_params=_params("parallel"),
    )(*os_, *lses, gm)


def attn_out_bwd(out, dy, gm, name):
    S = out.shape[0]
    tr = _rows(S, ROW_TILE)
    dcol = C_A // C_B

    def body(o_ref, dy1, dy2, g_ref, do_ref, delta_ref, dg_ref):
        o = o_ref[...].astype(F32)
        dyv = jnp.concatenate([dy1[...], dy2[...]], axis=1).astype(F32)
        do, n = _rms_bwd(o, dyv * g_ref[...])
        dob = do.astype(BF16)
        do_ref[...] = dob
        prod = dob.astype(F32) * o
        lane = lax.broadcasted_iota(jnp.int32, (1, LANES), 1)
        first = lane < HEAD_DIM
        delta = jnp.zeros((tr, LANES), F32)
        for p in range(C_B // LANES):
            blk = prod[:, p * LANES:(p + 1) * LANES]
            delta = jnp.where(lane == 2 * p, jnp.sum(jnp.where(first, blk, 0.0), axis=1, keepdims=True), delta)
            delta = jnp.where(lane == 2 * p + 1, jnp.sum(jnp.where(first, 0.0, blk), axis=1, keepdims=True), delta)
        delta_ref[...] = delta

        @pl.when(pl.program_id(0) == 0)
        def _():
            dg_ref[...] = jnp.zeros_like(dg_ref)

        dg_ref[...] += jnp.sum(dyv * n, axis=0, keepdims=True)

    del dcol
    return pl.pallas_call(
        body, name=name, grid=(S // tr,),
        in_specs=[_row_spec(tr, C_B), pl.BlockSpec((tr, C_A), lambda i: (i, 1)), pl.BlockSpec((tr, C_A), lambda i: (i, 2)),
                  _vec_spec(C_B)],
        out_specs=[_row_spec(tr, C_B), _row_spec(tr, LANES), _vec_spec(C_B)],
        out_shape=[jax.ShapeDtypeStruct((S, C_B), BF16), jax.ShapeDtypeStruct((S, LANES), F32),
                   jax.ShapeDtypeStruct((1, C_B), F32)],
        compiler_params=_params("arbitrary"),
    )(out, dy, dy, gm)


def attn_bwd_pattern(qd, kd, vd, dod, lsed, deltad, name):
    d, L, _ = qd.shape
    bq = _rows(L, ATTN_BLOCK)

    def body(qp, qc, qn, kp, kc, kn, vp, vc, vn, dp, dc_, dn, lp, lc, ln_, tp, tc, tn, dq_ref, dk_ref, dv_ref):
        j = pl.program_id(1)
        mask_a = _band_mask(j, bq, L, False)
        mask_b = _band_mask(j, bq, L, True)
        lane = lax.broadcasted_iota(jnp.int32, (1, LANES), 1)
        first = lane < HEAD_DIM
        lse_c, delta_c = lc[...], tc[...]
        lse_w = jnp.concatenate([lp[...], lse_c, ln_[...]], axis=0)
        delta_w = jnp.concatenate([tp[...], delta_c, tn[...]], axis=0)
        for p in range(C_B // LANES):
            sl = slice(p * LANES, (p + 1) * LANES)
            q, k, v, do = qc[:, sl], kc[:, sl], vc[:, sl], dc_[:, sl]
            qw, kw, vw, dow = _window((qp, qc, qn), sl), _window((kp, kc, kn), sl), _window((vp, vc, vn), sl), \
                _window((dp, dc_, dn), sl)
            dqs, dks, dvs = [], [], []
            for hh in range(2):
                h = 2 * p + hh
                sel = first == (hh == 0)
                zero = jnp.zeros_like(q)
                zero_w = jnp.zeros_like(qw)
                s = lax.dot_general(jnp.where(sel, q, zero), kw, _NT, preferred_element_type=F32)
                pr = jnp.where(mask_a, jnp.exp(s - _head_col(lse_c, h)), 0.0)
                dpr = lax.dot_general(jnp.where(sel, do, zero), vw, _NT, preferred_element_type=F32)
                ds = (pr * (dpr - _head_col(delta_c, h))).astype(BF16)
                dqs.append(jnp.dot(ds, kw, preferred_element_type=F32))
                s2 = lax.dot_general(jnp.where(sel, qw, zero_w), k, _NT, preferred_element_type=F32)
                pr2 = jnp.where(mask_b, jnp.exp(s2 - _head_col(lse_w, h)), 0.0)
                dvs.append(lax.dot_general(pr2.astype(BF16), dow, _TN, preferred_element_type=F32))
                dpr2 = lax.dot_general(jnp.where(sel, dow, zero_w), v, _NT, preferred_element_type=F32)
                ds2 = (pr2 * (dpr2 - _head_col(delta_w, h))).astype(BF16)
                dks.append(lax.dot_general(ds2, qw, _TN, preferred_element_type=F32))
            dq_ref[:, sl] = jnp.where(first, dqs[0], dqs[1]).astype(dq_ref.dtype)
            dk_ref[:, sl] = jnp.where(first, dks[0], dks[1]).astype(dk_ref.dtype)
            dv_ref[:, sl] = jnp.where(first, dvs[0], dvs[1]).astype(dv_ref.dtype)

    wide = _attn_specs(bq, C_B, L)
    stat = _attn_specs(bq, LANES, L)
    shape = jax.ShapeDtypeStruct((d, L, C_B), BF16)
    return pl.pallas_call(
        body, name=name, grid=(d, L // bq), in_specs=[*wide, *wide, *wide, *wide, *stat, *stat],
        out_specs=[wide[1]] * 3, out_shape=[shape] * 3, compiler_params=_params("parallel", "parallel"),
    )(qd, qd, qd, kd, kd, kd, vd, vd, vd, dod, dod, dod, lsed, lsed, lsed, deltad, deltad, deltad)


def _dilate(a, d):
    S, C = a.shape
    return a.reshape(S // d, d, C).transpose(1, 0, 2)


def _undilate(a):
    d, L, C = a.shape
    return a.transpose(1, 0, 2).reshape(d * L, C)


def adamw(parts, w, m, v, name):
    n, R, C = parts.shape
    tr = _rows(R, ROW_TILE)

    def body(p_ref, w_ref, m_ref, v_ref, g_ref, d_ref, nm_ref, nv_ref):
        g = p_ref[0].astype(F32)
        for k in range(1, n):
            g = g + p_ref[k].astype(F32)
        mm = ADAM_B1 * m_ref[...] + (1.0 - ADAM_B1) * g
        vv = ADAM_B2 * v_ref[...] + (1.0 - ADAM_B2) * jnp.square(g)
        m_hat = mm / (1.0 - ADAM_B1 ** ADAM_STEP)
        v_hat = vv / (1.0 - ADAM_B2 ** ADAM_STEP)
        g_ref[...] = g
        d_ref[...] = -ADAM_LR * (m_hat / (jnp.sqrt(v_hat) + ADAM_EPS) + ADAM_WD * w_ref[...])
        nm_ref[...] = mm
        nv_ref[...] = vv

    spec = _row_spec(tr, C)
    shape = jax.ShapeDtypeStruct((R, C), F32)
    return pl.pallas_call(
        body, name=name, grid=(R // tr,),
        in_specs=[pl.BlockSpec((n, tr, C), lambda i: (0, i, 0)), spec, spec, spec],
        out_specs=[spec] * 4, out_shape=[shape] * 4, compiler_params=_params("parallel"),
    )(parts, w, m, v)


_ANY = pl.BlockSpec(memory_space=pl.ANY)


def _place():
    return lax.axis_index("x"), lax.axis_index("y"), lax.axis_index("c")


def _index(px, py, pc):
    return 4 * px + 2 * py + pc


def all_gather(shards, name):
    T = len(shards)

    def body(*refs):
        ins, outs = refs[:T], refs[T:2 * T]
        send_sems, recv_sems, local_sems = refs[2 * T:]
        x, y, c = _place()
        me, sibling = (x, y, c), (x, y, 1 - c)
        chips = [(1 - x, y), (x, 1 - y), (1 - x, 1 - y)]

        def copy(t, k, block, to, src=None):
            rows = outs[t].at[_index(*block)]
            return pltpu.make_async_remote_copy(
                src_ref=rows if src is None else src, dst_ref=rows, send_sem=send_sems.at[t, k],
                recv_sem=recv_sems.at[t, k], device_id=to, device_id_type=MESH)

        mine = [pltpu.make_async_copy(ins[t], outs[t].at[_index(*me)], local_sems.at[t]) for t in range(T)]
        for cp in mine:
            cp.start()
        first = []
        for t in range(T):
            first.append(copy(t, 0, me, sibling, src=ins[t]))
            first += [copy(t, 1 + j, me, (*chip, c), src=ins[t]) for j, chip in enumerate(chips)]
        for cp in first:
            cp.start()
        passed = []
        for j, chip in enumerate(chips):
            for t in range(T):
                copy(t, 1 + j, (*chip, c), me).wait_recv()
                fwd = copy(t, 4 + j, (*chip, c), sibling)
                fwd.start()
                passed.append(fwd)
        for t in range(T):
            copy(t, 0, sibling, me).wait_recv()
            for j, chip in enumerate(chips):
                copy(t, 4 + j, (*chip, 1 - c), me).wait_recv()
        for cp in first + passed:
            cp.wait_send()
        for cp in mine:
            cp.wait()

    return pl.pallas_call(
        body, name=name, in_specs=[_ANY] * T, out_specs=[_ANY] * T,
        out_shape=[jax.ShapeDtypeStruct((N_DEV, *s.shape), s.dtype) for s in shards],
        scratch_shapes=[pltpu.SemaphoreType.DMA((T, 7)), pltpu.SemaphoreType.DMA((T, 7)), pltpu.SemaphoreType.DMA((T,))],
    )(*shards)


def exchange(parts, name):
    T = len(parts)

    def body(*refs):
        ins, outs = refs[:T], refs[T:2 * T]
        send_sems, recv_sems, local_sems = refs[2 * T:]
        x, y, c = _place()
        me = _index(x, y, c)
        mine = [pltpu.make_async_copy(ins[t].at[me], outs[t].at[me], local_sems.at[t]) for t in range(T)]
        for cp in mine:
            cp.start()
        copies = []
        for k in range(1, N_DEV):
            peer = ((x + (k >> 2)) % 2, (y + ((k >> 1) & 1)) % 2, (c + (k & 1)) % 2)
            there = _index(*peer)
            for t in range(T):
                copies.append(pltpu.make_async_remote_copy(
                    src_ref=ins[t].at[there], dst_ref=outs[t].at[me], send_sem=send_sems.at[t, k - 1],
                    recv_sem=recv_sems.at[t, k - 1], device_id=peer, device_id_type=MESH))
        for cp in copies:
            cp.start()
        for cp in copies:
            cp.wait()
        for cp in mine:
            cp.wait()

    return pl.pallas_call(
        body, name=name, in_specs=[_ANY] * T, out_specs=[_ANY] * T,
        out_shape=[jax.ShapeDtypeStruct(p.shape, p.dtype) for p in parts],
        scratch_shapes=[pltpu.SemaphoreType.DMA((T, 7)), pltpu.SemaphoreType.DMA((T, 7)), pltpu.SemaphoreType.DMA((T,))],
    )(*parts)


def _row(v):
    return v.reshape(1, -1)


def layer_forward(x, w, p, tables, tag):
    gm = p["g_mix"]
    h1 = rms_fwd(x, _row(p["g_pre_mix"]), f"rms_pre_mix{tag}")
    proj = matmul(h1, w["win"], "nn", BF16, f"proj{tag}")
    ya = conv_fwd(proj, p["cw"], _row(p["conv_b"]), _row(p["conv_ln_g"]), _row(p["conv_ln_b"]), _row(gm[:C_A]),
                  f"conv_fwd{tag}")
    q, k, v = rope_fwd(proj, tables, f"rope_fwd{tag}")
    dil = [tuple(_dilate(a, d) for a in (q, k, v)) for d in DILATIONS]
    os_, lses = [], []
    for d, (qd, kd, vd) in zip(DILATIONS, dil):
        o, lse = attn_fwd_pattern(qd, kd, vd, f"attn_fwd_d{d}{tag}")
        os_.append(_undilate(o))
        lses.append(_undilate(lse))
    yb, out, lse = attn_combine(os_, lses, _row(gm[C_A:C_A + C_B]), f"attn_combine{tag}")
    yc = pool_fwd(proj, p["pool_w"], _row(p["pool_scale"]), _row(gm[C_A + C_B:]), f"pool_fwd{tag}")
    y = jnp.concatenate([ya, yb, yc], axis=1)
    z = matmul(y, w["wout"], "nn", BF16, f"mix_out{tag}")
    x2 = norm_residual(z, x, _row(p["g_post_mix"]), f"res_mix{tag}")
    h2 = rms_fwd(x2, _row(p["g_pre_ffn"]), f"rms_pre_ffn{tag}")
    gu = matmul(h2, w["wgu"], "nn", BF16, f"ffn_in{tag}")
    a = swiglu_fwd(gu, f"swiglu_fwd{tag}")
    f = matmul(a, w["wd"], "nn", BF16, f"ffn_out{tag}", tk=2816)
    x3 = norm_residual(f, x2, _row(p["g_post_ffn"]), f"res_ffn{tag}")
    saved = dict(x=x, h1=h1, proj=proj, dil=dil, out=out, lse=lse, y=y, z=z, x2=x2, h2=h2, gu=gu, a=a, f=f)
    return x3, saved


def layer_backward(dx3, s, w, p, tables, tag):
    gm = p["g_mix"]
    df, dg_post_ffn = rms_bwd(s["f"], dx3, _row(p["g_post_ffn"]), None, BF16, f"rms_bwd_post_ffn{tag}")
    da = matmul(df, w["wd"], "nt", BF16, f"d_ffn_act{tag}")
    dwd = matmul(s["a"], df, "tn", F32, f"dw_down{tag}", tm=1408, tn=1024, tk=512)
    dgu = swiglu_bwd(s["gu"], da, f"swiglu_bwd{tag}")
    dh2 = matmul(dgu, w["wgu"], "nt", BF16, f"d_ffn_in{tag}", tk=2816)
    dwgu = matmul(s["h2"], dgu, "tn", F32, f"dw_gate_up{tag}", tm=1024, tn=1408, tk=512)
    dx2, dg_pre_ffn = rms_bwd(s["x2"], dh2, _row(p["g_pre_ffn"]), dx3, F32, f"rms_bwd_pre_ffn{tag}")
    dz, dg_post_mix = rms_bwd(s["z"], dx2, _row(p["g_post_mix"]), None, BF16, f"rms_bwd_post_mix{tag}")
    dy = matmul(dz, w["wout"], "nt", BF16, f"d_mix{tag}")
    dwout = matmul(s["y"], dz, "tn", F32, f"dw_out{tag}", tm=1024, tn=1024, tk=512)
    dconv, dcw, dcb, dlg, dlb, dgm_a = conv_bwd(
        s["proj"], dy, p["cw"], _row(p["conv_b"]), _row(p["conv_ln_g"]), _row(p["conv_ln_b"]), _row(gm[:C_A]),
        f"conv_bwd{tag}")
    do, delta, dgm_b = attn_out_bwd(s["out"], dy, _row(gm[C_A:C_A + C_B]), f"attn_out_bwd{tag}")
    dqs, dks, dvs = [], [], []
    for d, (qd, kd, vd) in zip(DILATIONS, s["dil"]):
        dq, dk, dv = attn_bwd_pattern(qd, kd, vd, _dilate(do, d), _dilate(s["lse"], d), _dilate(delta, d),
                                      f"attn_bwd_d{d}{tag}")
        dqs.append(_undilate(dq))
        dks.append(_undilate(dk))
        dvs.append(_undilate(dv))
    dqkv = rope_bwd(dqs, dks, dvs, tables, f"rope_bwd{tag}")
    du, dpw, dps, dgm_c = pool_bwd(s["proj"], dy, p["pool_w"], _row(p["pool_scale"]), _row(gm[C_A + C_B:]),
                                   f"pool_bwd{tag}")
    dproj = jnp.concatenate([dconv, dqkv, du], axis=1)
    dh1 = matmul(dproj, w["win"], "nt", BF16, f"d_proj{tag}", tk=2304)
    dwin = matmul(s["h1"], dproj, "tn", F32, f"dw_in{tag}", tm=1024, tn=1152, tk=512)
    dx, dg_pre_mix = rms_bwd(s["x"], dh1, _row(p["g_pre_mix"]), dx2, F32, f"rms_bwd_pre_mix{tag}")
    grads = dict(
        win=dwin, wout=dwout, wgu=dwgu, wd=dwd,
        conv_w=dcw[:CONV_WIDTH], conv_b=dcb[0], conv_ln_g=dlg[0], conv_ln_b=dlb[0], pool_w=dpw, pool_scale=dps[0],
        g_mix=jnp.concatenate([dgm_a[0], dgm_b[0], dgm_c[0]]), g_pre_mix=dg_pre_mix[0], g_post_mix=dg_post_mix[0],
        g_pre_ffn=dg_pre_ffn[0], g_post_ffn=dg_post_ffn[0])
    return dx, grads


WEIGHTS = ["w_in", "conv_w", "conv_b", "conv_ln_g", "conv_ln_b", "pool_w", "pool_scale", "g_mix", "w_out", "g_pre_mix",
           "g_post_mix", "g_pre_ffn", "g_post_ffn", "w_gate", "w_up", "w_down"]
BIG = ["w_in", "w_out", "w_gate", "w_up", "w_down"]
REPLICATED = ["conv_b", "conv_ln_g", "conv_ln_b", "pool_w", "pool_scale", "g_mix", "g_pre_mix", "g_post_mix",
              "g_pre_ffn", "g_post_ffn"]
PACK_ROWS = 256


def adamw_layer(parts, w, m, v, layer, prev, name):
    n, R, C = parts.shape
    tr = _rows(R, ROW_TILE)

    def body(p_ref, w_ref, m_ref, v_ref, *rest):
        g_ref, d_ref, nm_ref, nv_ref = rest[-4:]
        g = p_ref[0].astype(F32)
        for k in range(1, n):
            g = g + p_ref[k].astype(F32)
        mm = ADAM_B1 * m_ref[...] + (1.0 - ADAM_B1) * g
        vv = ADAM_B2 * v_ref[...] + (1.0 - ADAM_B2) * jnp.square(g)
        m_hat = mm / (1.0 - ADAM_B1 ** ADAM_STEP)
        v_hat = vv / (1.0 - ADAM_B2 ** ADAM_STEP)
        g_ref[...] = g
        d_ref[...] = -ADAM_LR * (m_hat / (jnp.sqrt(v_hat) + ADAM_EPS) + ADAM_WD * w_ref[...])
        nm_ref[...] = mm
        nv_ref[...] = vv

    spec = pl.BlockSpec((None, tr, C), lambda i: (layer, i, 0))
    shape = jax.ShapeDtypeStruct(w.shape, F32)
    prev = list(prev) if prev is not None else []
    return pl.pallas_call(
        body, name=name, grid=(R // tr,),
        in_specs=[pl.BlockSpec((n, tr, C), lambda i: (0, i, 0)), spec, spec, spec] + [_ANY] * len(prev),
        out_specs=[spec] * 4, out_shape=[shape] * 4,
        input_output_aliases={4 + k: k for k in range(len(prev))}, compiler_params=_params("parallel"),
    )(parts, w, m, v, *prev)


def _to_blocks(g, by_columns):
    if by_columns:
        rows = g.shape[0]
        return g.reshape(rows, N_DEV, -1).transpose(1, 0, 2).astype(BF16)
    return g.reshape(N_DEV, -1, g.shape[1]).astype(BF16)


def _from_blocks(b, by_columns):
    if by_columns:
        return b.transpose(1, 0, 2).reshape(b.shape[1], -1)
    return b.reshape(-1, b.shape[2])


def _pack(arrays):
    flat = jnp.concatenate([a.reshape(-1).astype(F32) for a in arrays])
    unit = PACK_ROWS * LANES
    padded = -(-flat.shape[0] // unit) * unit
    return jnp.pad(flat, (0, padded - flat.shape[0])).reshape(-1, LANES)


def _unpack(packed, like):
    flat = packed.reshape(-1)
    out, at = [], 0
    for a in like:
        out.append(flat[at:at + a.size].reshape(a.shape))
        at += a.size
    return out


def kernel(x, w_in, conv_w, conv_b, conv_ln_g, conv_ln_b, pool_w, pool_scale, g_mix, w_out, g_pre_mix, g_post_mix, g_pre_ffn, g_post_ffn, w_gate, w_up, w_down, loss_target, m_w_in, m_conv_w, m_conv_b, m_conv_ln_g, m_conv_ln_b, m_pool_w, m_pool_scale, m_g_mix, m_w_out, m_g_pre_mix, m_g_post_mix, m_g_pre_ffn, m_g_post_ffn, m_w_gate, m_w_up, m_w_down, v_w_in, v_conv_w, v_conv_b, v_conv_ln_g, v_conv_ln_b, v_pool_w, v_pool_scale, v_g_mix, v_w_out, v_g_pre_mix, v_g_post_mix, v_g_pre_ffn, v_g_post_ffn, v_w_gate, v_w_up, v_w_down):
    w = dict(w_in=w_in, conv_w=conv_w, conv_b=conv_b, conv_ln_g=conv_ln_g, conv_ln_b=conv_ln_b, pool_w=pool_w,
             pool_scale=pool_scale, g_mix=g_mix, w_out=w_out, g_pre_mix=g_pre_mix, g_post_mix=g_post_mix,
             g_pre_ffn=g_pre_ffn, g_post_ffn=g_post_ffn, w_gate=w_gate, w_up=w_up, w_down=w_down)
    m = dict(w_in=m_w_in, conv_w=m_conv_w, conv_b=m_conv_b, conv_ln_g=m_conv_ln_g, conv_ln_b=m_conv_ln_b,
             pool_w=m_pool_w, pool_scale=m_pool_scale, g_mix=m_g_mix, w_out=m_w_out, g_pre_mix=m_g_pre_mix,
             g_post_mix=m_g_post_mix, g_pre_ffn=m_g_pre_ffn, g_post_ffn=m_g_post_ffn, w_gate=m_w_gate, w_up=m_w_up,
             w_down=m_w_down)
    v = dict(w_in=v_w_in, conv_w=v_conv_w, conv_b=v_conv_b, conv_ln_g=v_conv_ln_g, conv_ln_b=v_conv_ln_b,
             pool_w=v_pool_w, pool_scale=v_pool_scale, g_mix=v_g_mix, w_out=v_w_out, g_pre_mix=v_g_pre_mix,
             g_post_mix=v_g_post_mix, g_pre_ffn=v_g_pre_ffn, g_post_ffn=v_g_post_ffn, w_gate=v_w_gate, w_up=v_w_up,
             w_down=v_w_down)
    depth = w_in.shape[0]
    xs, target = x[0], loss_target[0]
    S, D = xs.shape
    tables = rope_tables(S)
    by_columns = dict(w_in=True, w_out=False, w_gate=True, w_up=True, w_down=False)

    cw_shard = jnp.pad(conv_w, ((0, 0), (0, 1), (0, 0)))
    cw_all = all_gather([cw_shard.reshape(-1, LANES)], "gather_conv_w")[0]
    cw_full = cw_all.reshape(N_DEV, depth, 32, -1).transpose(1, 2, 0, 3).reshape(depth, 32, C_A)

    gathered, small, saved = [], [], []
    h = xs
    for l in range(depth):
        blocks = all_gather([w[n][l].astype(BF16) for n in BIG], "gather_weights")
        full = {n: _from_blocks(b, by_columns[n]) for n, b in zip(BIG, blocks)}
        wl = dict(win=full["w_in"], wout=full["w_out"], wd=full["w_down"],
                  wgu=jnp.concatenate([full["w_gate"], full["w_up"]], axis=1))
        pl_ = {n: w[n][l] for n in REPLICATED}
        pl_["cw"] = cw_full[l]
        h, s = layer_forward(h, wl, pl_, tables, "")
        gathered.append(wl)
        small.append(pl_)
        saved.append(s)

    dh, sq = loss_head(h, target, "loss_head")
    loss = lax.psum(0.5 * jnp.sum(sq) / D, ("x", "y", "c"))

    big_out = {n: None for n in BIG}
    small_grads = [None] * depth
    for l in reversed(range(depth)):
        dh, g = layer_backward(dh, saved[l], gathered[l], small[l], tables, "")
        F = g["wgu"].shape[1] // 2
        full = dict(w_in=g["win"], w_out=g["wout"], w_gate=g["wgu"][:, :F], w_up=g["wgu"][:, F:], w_down=g["wd"])
        parts = exchange([_to_blocks(full[n], by_columns[n]) for n in BIG], "exchange_grads")
        for n, p in zip(BIG, parts):
            big_out[n] = adamw_layer(p, w[n], m[n], v[n], l, big_out[n], f"adamw_{n}_layer{l}")
        small_grads[l] = g

    names = REPLICATED + ["conv_w"]
    stacked = [jnp.stack([small_grads[l][n] for l in range(depth)]) for n in names]
    partial = all_gather([_pack(stacked)], "gather_small_grads")[0]
    zeros = jnp.zeros_like(stacked[-1])
    packed = adamw(partial, _pack([w[n] for n in REPLICATED] + [zeros]), _pack([m[n] for n in REPLICATED] + [zeros]),
                   _pack([v[n] for n in REPLICATED] + [zeros + 1.0]), "adamw_replicated")
    small_out = [_unpack(o, stacked) for o in packed]
    out = {n: tuple(o[i] for o in small_out) for i, n in enumerate(REPLICATED)}
    width = conv_w.shape[2]
    g_cw = lax.dynamic_slice_in_dim(small_out[0][-1], _index(*_place()) * width, width, axis=2)
    cw_res = adamw(g_cw.reshape(1, -1, LANES), conv_w.reshape(-1, LANES), m["conv_w"].reshape(-1, LANES),
                   v["conv_w"].reshape(-1, LANES), "adamw_conv_w")
    out["conv_w"] = tuple(o.reshape(conv_w.shape) for o in cw_res)
    for n in BIG:
        out[n] = tuple(big_out[n])
    results = [loss, dh[None]]
    for k in range(4):
        results += [out[n][k] for n in WEIGHTS]
    return tuple(results)
```

```python
import functools
import math

import jax
import jax.numpy as jnp
from jax import lax
from jax.experimental import pallas as pl
from jax.experimental.pallas import tpu as pltpu

F32 = jnp.float32
BF16 = jnp.bfloat16

N_DEV = 8
DEPTH = 4
EPS = 1e-6
NEG = -1e30

C_A = 512
N_HEADS = 16
HEAD_DIM = 64
C_B = N_HEADS * HEAD_DIM
C_C = 512
POOL_WINDOWS = (2, 4, 8, 16)
C_G = C_C // len(POOL_WINDOWS)
CONV_WIDTH = 31
CONV_HALF = CONV_WIDTH // 2
DILATIONS = (1, 4, 16)
ATTN_HALF = 64
ROT_DIM = HEAD_DIM // 4
ROPE_THETA = 500000.0

ADAM_LR = 0.001
ADAM_B1 = 0.9
ADAM_B2 = 0.999
ADAM_EPS = 1e-08
ADAM_WD = 0.01
ADAM_STEP = 10

LANES = 128
VMEM_LIMIT = 56 * 1024 * 1024
ROW_TILE = 256
SEQ_TILE = 256
ATTN_BLOCK = 128
MESH = pl.DeviceIdType.MESH


def _params(*sem):
    return pltpu.CompilerParams(dimension_semantics=sem, vmem_limit_bytes=VMEM_LIMIT)


def _tile(n, target):
    if n <= target:
        return n
    t = (target // LANES) * LANES
    while t >= LANES:
        if n % t == 0:
            return t
        t -= LANES
    return n


def _rows(n, target):
    t = min(n, target)
    while n % t:
        t //= 2
    return t


def matmul(a, b, mode, out_dtype, name, tm=1024, tn=512, tk=2048, comm=None):
    if mode == "nn":
        (M, K), (_, N) = a.shape, b.shape
    elif mode == "nt":
        (M, K), (N, _) = a.shape, b.shape
    else:
        (K, M), (_, N) = a.shape, b.shape
    tm, tn, tk = _tile(M, tm), _tile(N, tn), _tile(K, tk)
    nm, nn, nk = M // tm, N // tn, K // tk
    dims = {"nn": (((1,), (0,)), ((), ())), "nt": (((1,), (1,)), ((), ())), "tn": (((0,), (0,)), ((), ()))}[mode]
    nc = comm.n if comm is not None else 0

    def body(*refs):
        a_ref, b_ref = refs[:2]
        c_ins, o_ref, c_outs = refs[2:2 + nc], refs[2 + nc], refs[3 + nc:3 + 2 * nc]
        scratch = refs[3 + 2 * nc:]
        acc, sems = (scratch[:1], scratch[1:]) if nk > 1 else ((), scratch)
        i, j, k = pl.program_id(0), pl.program_id(1), pl.program_id(2)
        if comm is not None:
            @pl.when((i == 0) & (j == 0) & (k == 0))
            def _():
                comm.start(c_ins, c_outs, sems)

        p = lax.dot_general(a_ref[...], b_ref[...], dims, preferred_element_type=F32)
        if nk == 1:
            o_ref[...] = p.astype(o_ref.dtype)
        else:
            acc_ref, = acc

            @pl.when(k == 0)
            def _():
                acc_ref[...] = p

            @pl.when(k > 0)
            def _():
                acc_ref[...] += p

            @pl.when(k == nk - 1)
            def _():
                o_ref[...] = acc_ref[...].astype(o_ref.dtype)

        if comm is not None:
            @pl.when((i == nm - 1) & (j == nn - 1) & (k == nk - 1))
            def _():
                comm.finish(c_ins, c_outs, sems)

    if mode == "nn":
        a_spec = pl.BlockSpec((tm, tk), lambda i, j, k: (i, k))
        b_spec = pl.BlockSpec((tk, tn), lambda i, j, k: (k, j))
    elif mode == "nt":
        a_spec = pl.BlockSpec((tm, tk), lambda i, j, k: (i, k))
        b_spec = pl.BlockSpec((tn, tk), lambda i, j, k: (j, k))
    else:
        a_spec = pl.BlockSpec((tk, tm), lambda i, j, k: (k, i))
        b_spec = pl.BlockSpec((tk, tn), lambda i, j, k: (k, j))
    o_spec = pl.BlockSpec((tm, tn), lambda i, j, k: (i, j))
    o_shape = jax.ShapeDtypeStruct((M, N), out_dtype)
    acc_shape = [pltpu.VMEM((tm, tn), F32)] if nk > 1 else []
    if comm is None:
        return pl.pallas_call(
            body, name=name, grid=(nm, nn, nk), in_specs=[a_spec, b_spec], out_specs=o_spec, out_shape=o_shape,
            scratch_shapes=acc_shape, compiler_params=_params("parallel", "parallel", "arbitrary"),
        )(a, b)
    res = pl.pallas_call(
        body, name=name, grid=(nm, nn, nk), in_specs=[a_spec, b_spec] + [_ANY] * nc,
        out_specs=[o_spec] + [_ANY] * nc, out_shape=[o_shape] + comm.out_shape,
        scratch_shapes=acc_shape + comm.sems, compiler_params=_params("arbitrary", "arbitrary", "arbitrary"),
    )(a, b, *comm.ins)
    return res[0], res[1:]


def _rms(t):
    return t * lax.rsqrt(jnp.mean(t * t, axis=-1, keepdims=True) + EPS)


def _rms_bwd(t, dn):
    r = lax.rsqrt(jnp.mean(t * t, axis=-1, keepdims=True) + EPS)
    n = t * r
    return r * (dn - n * jnp.mean(dn * n, axis=-1, keepdims=True)), n


def _row_spec(tr, d):
    return pl.BlockSpec((tr, d), lambda i: (i, 0))


def _vec_spec(d):
    return pl.BlockSpec((1, d), lambda i: (0, 0))


def _col_spec(d, tr):
    return pl.BlockSpec((d, tr), lambda i: (0, i))


def rms_fwd(x, g, name):
    S, D = x.shape
    tr = _rows(S, ROW_TILE)

    def body(x_ref, g_ref, o_ref, ot_ref):
        h = _rms(x_ref[...].astype(F32)) * g_ref[...]
        o_ref[...] = h.astype(o_ref.dtype)
        ot_ref[...] = h.T.astype(ot_ref.dtype)

    return pl.pallas_call(
        body, name=name, grid=(S // tr,), in_specs=[_row_spec(tr, D), _vec_spec(D)],
        out_specs=[_row_spec(tr, D), _col_spec(D, tr)],
        out_shape=[jax.ShapeDtypeStruct((S, D), BF16), jax.ShapeDtypeStruct((D, S), BF16)],
        compiler_params=_params("parallel"),
    )(x, g)


def norm_residual(z, x, g, name):
    S, D = x.shape
    tr = _rows(S, ROW_TILE)

    def body(z_ref, x_ref, g_ref, o_ref):
        o_ref[...] = x_ref[...] + _rms(z_ref[...].astype(F32)) * g_ref[...]

    return pl.pallas_call(
        body, name=name, grid=(S // tr,), in_specs=[_row_spec(tr, D), _row_spec(tr, D), _vec_spec(D)],
        out_specs=_row_spec(tr, D), out_shape=jax.ShapeDtypeStruct((S, D), F32), compiler_params=_params("parallel"),
    )(z, x, g)


def rms_bwd(t, dy, g, res, out_dtype, name):
    S, D = t.shape
    tr = _rows(S, ROW_TILE)
    has_res = res is not None

    def body(t_ref, dy_ref, g_ref, *rest):
        if has_res:
            res_ref, dt_ref, dg_ref = rest
        else:
            dt_ref, dg_ref = rest
        dyv = dy_ref[...].astype(F32)
        dt, n = _rms_bwd(t_ref[...].astype(F32), dyv * g_ref[...])
        if has_res:
            dt = dt + res_ref[...]
        dt_ref[...] = dt.astype(dt_ref.dtype)

        @pl.when(pl.program_id(0) == 0)
        def _():
            dg_ref[...] = jnp.zeros_like(dg_ref)

        dg_ref[...] += jnp.sum(dyv * n, axis=0, keepdims=True)

    ins = [t, dy, g] + ([res] if has_res else [])
    specs = [_row_spec(tr, D), _row_spec(tr, D), _vec_spec(D)] + ([_row_spec(tr, D)] if has_res else [])
    return pl.pallas_call(
        body, name=name, grid=(S // tr,), in_specs=specs, out_specs=[_row_spec(tr, D), _vec_spec(D)],
        out_shape=[jax.ShapeDtypeStruct((S, D), out_dtype), jax.ShapeDtypeStruct((1, D), F32)],
        compiler_params=_params("arbitrary"),
    )(*ins)


def swiglu_fwd(gu, name):
    S, F2 = gu.shape
    F = F2 // 2
    tr = _rows(S, ROW_TILE)

    def body(g_ref, u_ref, o_ref, ot_ref):
        g = g_ref[...].astype(F32)
        a = g * jax.nn.sigmoid(g) * u_ref[...].astype(F32)
        o_ref[...] = a.astype(o_ref.dtype)
        ot_ref[...] = a.T.astype(ot_ref.dtype)

    return pl.pallas_call(
        body, name=name, grid=(S // tr,),
        in_specs=[pl.BlockSpec((tr, F), lambda i: (i, 0)), pl.BlockSpec((tr, F), lambda i: (i, 1))],
        out_specs=[_row_spec(tr, F), _col_spec(F, tr)],
        out_shape=[jax.ShapeDtypeStruct((S, F), BF16), jax.ShapeDtypeStruct((F, S), BF16)],
        compiler_params=_params("parallel"),
    )(gu, gu)


def swiglu_bwd(gu, da, name):
    S, F2 = gu.shape
    F = F2 // 2
    tr = _rows(S, ROW_TILE)

    def body(g_ref, u_ref, da_ref, o_ref):
        g = g_ref[...].astype(F32)
        u = u_ref[...].astype(F32)
        dav = da_ref[...].astype(F32)
        sig = jax.nn.sigmoid(g)
        o_ref[:, :F] = (dav * u * (sig * (1.0 + g * (1.0 - sig)))).astype(o_ref.dtype)
        o_ref[:, F:] = (dav * (g * sig)).astype(o_ref.dtype)

    return pl.pallas_call(
        body, name=name, grid=(S // tr,),
        in_specs=[pl.BlockSpec((tr, F), lambda i: (i, 0)), pl.BlockSpec((tr, F), lambda i: (i, 1)), _row_spec(tr, F)],
        out_specs=_row_spec(tr, F2), out_shape=jax.ShapeDtypeStruct((S, F2), BF16), compiler_params=_params("parallel"),
    )(gu, gu, da)


def loss_head(y, target, name):
    S, D = y.shape
    tr = _rows(S, ROW_TILE)

    def body(y_ref, t_ref, dy_ref, sq_ref):
        e = y_ref[...] - t_ref[...]
        dy_ref[...] = e * (1.0 / D)

        @pl.when(pl.program_id(0) == 0)
        def _():
            sq_ref[...] = jnp.zeros_like(sq_ref)

        sq_ref[...] += jnp.sum(e * e, axis=0, keepdims=True)

    return pl.pallas_call(
        body, name=name, grid=(S // tr,), in_specs=[_row_spec(tr, D), _row_spec(tr, D)],
        out_specs=[_row_spec(tr, D), _vec_spec(D)],
        out_shape=[jax.ShapeDtypeStruct((S, D), F32), jax.ShapeDtypeStruct((1, D), F32)],
        compiler_params=_params("arbitrary"),
    )(y, target)


def _halo_specs(bs, halo, width, col, n_rows):
    per = bs // halo
    last = n_rows // halo - 1
    cur = pl.BlockSpec((bs, width), lambda i: (i, col))
    prev = pl.BlockSpec((halo, width), lambda i: (jnp.maximum(i * per - 1, 0), col))
    nxt = pl.BlockSpec((halo, width), lambda i: (jnp.minimum((i + 1) * per, last), col))
    return prev, cur, nxt


def _glu(a, g):
    return a.astype(F32) * jax.nn.sigmoid(g.astype(F32))


def _layernorm_silu(c, lg, lb):
    mu = jnp.mean(c, axis=-1, keepdims=True)
    cc = c - mu
    rstd = lax.rsqrt(jnp.mean(cc * cc, axis=-1, keepdims=True) + EPS)
    xh = cc * rstd
    ln = xh * lg + lb
    sig = jax.nn.sigmoid(ln)
    return xh, rstd, ln, sig


def conv_fwd(proj, cw, cb, lg, lb, gm, name):
    S = proj.shape[0]
    bs = _rows(S, SEQ_TILE)
    nb = S // bs
    H = 16

    def body(ap, ac, an, gp, gc, gn, cw_ref, cb_ref, lg_ref, lb_ref, gm_ref, y_ref, win_ref):
        i = pl.program_id(0)
        win_ref[0:H, :] = jnp.where(i > 0, _glu(ap[...], gp[...]), 0.0)
        win_ref[H:H + bs, :] = _glu(ac[...], gc[...])
        win_ref[H + bs:, :] = jnp.where(i < nb - 1, _glu(an[...], gn[...]), 0.0)
        c = jnp.zeros((bs, C_A), F32) + cb_ref[...]
        for t in range(CONV_WIDTH):
            c = c + cw_ref[t:t + 1, :] * win_ref[pl.ds(H - CONV_HALF + t, bs), :]
        _, _, ln, sig = _layernorm_silu(c, lg_ref[...], lb_ref[...])
        y_ref[...] = (_rms(ln * sig) * gm_ref[...]).astype(y_ref.dtype)

    a_specs = _halo_specs(bs, H, C_A, 0, S)
    g_specs = _halo_specs(bs, H, C_A, 1, S)
    vec = _vec_spec(C_A)
    return pl.pallas_call(
        body, name=name, grid=(nb,),
        in_specs=[*a_specs, *g_specs, pl.BlockSpec((32, C_A), lambda i: (0, 0)), vec, vec, vec, vec],
        out_specs=_row_spec(bs, C_A), out_shape=jax.ShapeDtypeStruct((S, C_A), BF16),
        scratch_shapes=[pltpu.VMEM((bs + 2 * H, C_A), F32)], compiler_params=_params("parallel"),
    )(proj, proj, proj, proj, proj, proj, cw, cb, lg, lb, gm)


def conv_bwd(proj, dy, cw, cb, lg, lb, gm, name):
    S = proj.shape[0]
    bs = _rows(S, SEQ_TILE)
    nb = S // bs
    H = 32
    HC = 16
    bc = bs + 2 * HC

    def body(ap, ac, an, gp, gc, gn, dp, dc_, dn, cw_ref, cb_ref, lg_ref, lb_ref, gm_ref,
             dproj_ref, dcw_ref, dcb_ref, dlg_ref, dlb_ref, dgm_ref, win_ref, dcs_ref):
        i = pl.program_id(0)
        win_ref[0:H, :] = jnp.where(i > 0, _glu(ap[...], gp[...]), 0.0)
        win_ref[H:H + bs, :] = _glu(ac[...], gc[...])
        win_ref[H + bs:, :] = jnp.where(i < nb - 1, _glu(an[...], gn[...]), 0.0)
        c = jnp.zeros((bc, C_A), F32) + cb_ref[...]
        for t in range(CONV_WIDTH):
            c = c + cw_ref[t:t + 1, :] * win_ref[pl.ds(H - HC - CONV_HALF + t, bc), :]
        xh, rstd, ln, sig = _layernorm_silu(c, lg_ref[...], lb_ref[...])
        ya = ln * sig
        dyv = jnp.concatenate([dp[...], dc_[...], dn[...]], axis=0).astype(F32)
        dya, n = _rms_bwd(ya, dyv * gm_ref[...])
        dln = dya * (sig * (1.0 + ln * (1.0 - sig)))
        dxh = dln * lg_ref[...]
        dcv = rstd * (dxh - jnp.mean(dxh, axis=-1, keepdims=True) - xh * jnp.mean(dxh * xh, axis=-1, keepdims=True))
        pos = i * bs - HC + lax.broadcasted_iota(jnp.int32, (bc, 1), 0)
        dcv = jnp.where((pos >= 0) & (pos < S), dcv, 0.0)
        dcs_ref[...] = dcv

        @pl.when(i == 0)
        def _():
            for r in (dcw_ref, dcb_ref, dlg_ref, dlb_ref, dgm_ref):
                r[...] = jnp.zeros_like(r)

        mid = slice(HC, HC + bs)
        dcb_ref[...] += jnp.sum(dcv[mid], axis=0, keepdims=True)
        dlg_ref[...] += jnp.sum((dln * xh)[mid], axis=0, keepdims=True)
        dlb_ref[...] += jnp.sum(dln[mid], axis=0, keepdims=True)
        dgm_ref[...] += jnp.sum((dyv * n)[mid], axis=0, keepdims=True)
        dh = jnp.zeros((bs, C_A), F32)
        dcm = dcv[mid]
        for t in range(CONV_WIDTH):
            dh = dh + cw_ref[t:t + 1, :] * dcs_ref[pl.ds(HC + CONV_HALF - t, bs), :]
            dcw_ref[t:t + 1, :] += jnp.sum(dcm * win_ref[pl.ds(H - CONV_HALF + t, bs), :], axis=0, keepdims=True)
        a = ac[...].astype(F32)
        sg = jax.nn.sigmoid(gc[...].astype(F32))
        dproj_ref[:, :C_A] = (dh * sg).astype(dproj_ref.dtype)
        dproj_ref[:, C_A:] = (dh * a * sg * (1.0 - sg)).astype(dproj_ref.dtype)

    a_specs = _halo_specs(bs, H, C_A, 0, S)
    g_specs = _halo_specs(bs, H, C_A, 1, S)
    d_specs = _halo_specs(bs, HC, C_A, 0, S)
    vec = _vec_spec(C_A)
    full = pl.BlockSpec((32, C_A), lambda i: (0, 0))
    vshape = jax.ShapeDtypeStruct((1, C_A), F32)
    return pl.pallas_call(
        body, name=name, grid=(nb,),
        in_specs=[*a_specs, *g_specs, *d_specs, full, vec, vec, vec, vec],
        out_specs=[_row_spec(bs, 2 * C_A), full, vec, vec, vec, vec],
        out_shape=[jax.ShapeDtypeStruct((S, 2 * C_A), BF16), jax.ShapeDtypeStruct((32, C_A), F32),
                   vshape, vshape, vshape, vshape],
        scratch_shapes=[pltpu.VMEM((bs + 2 * H, C_A), F32), pltpu.VMEM((bc, C_A), F32)],
        compiler_params=_params("arbitrary"),
    )(proj, proj, proj, proj, proj, proj, dy, dy, dy, cw, cb, lg, lb, gm)


POOL_HALO = 16


def _shift(x, k):
    n = x.shape[0]
    return pltpu.roll(x, (-k) % n, axis=0)


def _pool_means(u, pos, S):
    w2 = _shift(u, -1) + u
    w4 = _shift(w2, -1) + _shift(w2, 1)
    w8 = _shift(w4, -2) + _shift(w4, 2)
    w16 = _shift(w8, -4) + _shift(w8, 4)
    sums = (w2, w4, w8, w16)
    lane = lax.broadcasted_iota(jnp.int32, (1, C_C), 1)
    total = jnp.zeros_like(u)
    inv = jnp.zeros_like(u)
    for gi, win in enumerate(POOL_WINDOWS):
        cnt = jnp.minimum(pos + (win - win // 2), S) - jnp.maximum(pos - win // 2, 0)
        icnt = 1.0 / jnp.maximum(cnt, 1).astype(F32)
        sel = (lane >= gi * C_G) & (lane < (gi + 1) * C_G)
        total = jnp.where(sel, sums[gi], total)
        inv = jnp.where(sel, icnt, inv)
    return total * inv - u, inv


def _pool_adjoint(e):
    v2 = e + _shift(e, 1)
    v4 = _shift(v2, -1) + _shift(v2, 1)
    v8 = _shift(v4, -2) + _shift(v4, 2)
    v16 = _shift(v8, -4) + _shift(v8, 4)
    sums = (v2, v4, v8, v16)
    lane = lax.broadcasted_iota(jnp.int32, (1, C_C), 1)
    out = jnp.zeros_like(e)
    for gi in range(len(POOL_WINDOWS)):
        sel = (lane >= gi * C_G) & (lane < (gi + 1) * C_G)
        out = jnp.where(sel, sums[gi], out)
    return out


def _pool_window(up, uc, un, i, nb, bs, S):
    H = POOL_HALO
    u = jnp.concatenate([jnp.where(i > 0, up[...].astype(F32), 0.0), uc[...].astype(F32),
                         jnp.where(i < nb - 1, un[...].astype(F32), 0.0)], axis=0)
    pos = i * bs - H + lax.broadcasted_iota(jnp.int32, (bs + 2 * H, 1), 0)
    return u, pos


def _pool_mix(pooled, pw_ref):
    outs = []
    for gi in range(len(POOL_WINDOWS)):
        outs.append(jnp.dot(pooled[:, gi * C_G:(gi + 1) * C_G].astype(BF16), pw_ref[gi].astype(BF16),
                            preferred_element_type=F32))
    return jnp.concatenate(outs, axis=1)


def pool_fwd(proj, pw, ps, gm, name):
    S, width = proj.shape
    col = width // C_C - 1
    bs = _rows(S, SEQ_TILE)
    nb = S // bs
    H = POOL_HALO

    def body(up, uc, un, pw_ref, ps_ref, gm_ref, y_ref):
        i = pl.program_id(0)
        u, pos = _pool_window(up, uc, un, i, nb, bs, S)
        pooled, _ = _pool_means(u, pos, S)
        mixed = _pool_mix(pooled[H:H + bs], pw_ref)
        y_ref[...] = (_rms(mixed * ps_ref[...]) * gm_ref[...]).astype(y_ref.dtype)

    vec = _vec_spec(C_C)
    return pl.pallas_call(
        body, name=name, grid=(nb,),
        in_specs=[*_halo_specs(bs, H, C_C, col, S), pl.BlockSpec((4, C_G, C_G), lambda i: (0, 0, 0)), vec, vec],
        out_specs=_row_spec(bs, C_C), out_shape=jax.ShapeDtypeStruct((S, C_C), BF16),
        compiler_params=_params("parallel"),
    )(proj, proj, proj, pw, ps, gm)


def pool_bwd(proj, dy, pw, ps, gm, name):
    S, width = proj.shape
    col = width // C_C - 1
    dcol = dy.shape[1] // C_C - 1
    bs = _rows(S, SEQ_TILE)
    nb = S // bs
    H = POOL_HALO
    W = bs + 2 * H

    def body(up, uc, un, dp, dc_, dn, pw_ref, ps_ref, gm_ref, du_ref, dpw_ref, dps_ref, dgm_ref):
        i = pl.program_id(0)
        u, pos = _pool_window(up, uc, un, i, nb, bs, S)
        pooled, inv = _pool_means(u, pos, S)
        mixed = _pool_mix(pooled, pw_ref)
        dyv = jnp.concatenate([dp[...], dc_[...], dn[...]], axis=0).astype(F32)
        dyc, n = _rms_bwd(mixed * ps_ref[...], dyv * gm_ref[...])
        dmixed = dyc * ps_ref[...]
        dmb = dmixed.astype(BF16)
        dpooled = jnp.concatenate(
            [lax.dot_general(dmb[:, gi * C_G:(gi + 1) * C_G], pw_ref[gi].astype(BF16), (((1,), (1,)), ((), ())),
                             preferred_element_type=F32) for gi in range(len(POOL_WINDOWS))], axis=1)
        dpooled = jnp.where((pos >= 0) & (pos < S), dpooled, 0.0)
        du = _pool_adjoint(dpooled * inv) - dpooled
        du_ref[...] = du[H:H + bs].astype(du_ref.dtype)

        @pl.when(i == 0)
        def _():
            for r in (dpw_ref, dps_ref, dgm_ref):
                r[...] = jnp.zeros_like(r)

        mid = slice(H, H + bs)
        dps_ref[...] += jnp.sum((dyc * mixed)[mid], axis=0, keepdims=True)
        dgm_ref[...] += jnp.sum((dyv * n)[mid], axis=0, keepdims=True)
        pb = pooled[mid].astype(BF16)
        for gi in range(len(POOL_WINDOWS)):
            sl = slice(gi * C_G, (gi + 1) * C_G)
            dpw_ref[gi] += lax.dot_general(pb[:, sl], dmb[mid][:, sl], (((0,), (0,)), ((), ())),
                                           preferred_element_type=F32)

    vec = _vec_spec(C_C)
    full = pl.BlockSpec((4, C_G, C_G), lambda i: (0, 0, 0))
    vshape = jax.ShapeDtypeStruct((1, C_C), F32)
    return pl.pallas_call(
        body, name=name, grid=(nb,),
        in_specs=[*_halo_specs(bs, H, C_C, col, S), *_halo_specs(bs, H, C_C, dcol, S), full, vec, vec],
        out_specs=[_row_spec(bs, C_C), full, vec, vec],
        out_shape=[jax.ShapeDtypeStruct((S, C_C), BF16), jax.ShapeDtypeStruct((4, C_G, C_G), F32), vshape, vshape],
        compiler_params=_params("arbitrary"),
    )(proj, proj, proj, dy, dy, dy, pw, ps, gm)


def rope_tables(S):
    pos = jnp.arange(S, dtype=F32)
    inv = ROPE_THETA ** (-jnp.arange(0, ROT_DIM, 2, dtype=F32) / ROT_DIM)
    ang = pos[:, None] * inv[None, :]
    half = ROT_DIM // 2
    cos, sin = jnp.cos(ang), jnp.sin(ang)
    zeros = jnp.zeros((S, half), F32)
    rest = jnp.zeros((S, HEAD_DIM - ROT_DIM), F32)
    per_head = (jnp.concatenate([cos, cos, rest + 1.0], axis=1), jnp.concatenate([-sin, zeros, rest], axis=1),
                jnp.concatenate([zeros, sin, rest], axis=1))
    return tuple(jnp.tile(t, (1, LANES // HEAD_DIM)) for t in per_head)


def _rotate(t, c, s1, s2, sign):
    half = ROT_DIM // 2
    return t * c + sign * (pltpu.roll(t, LANES - half, axis=1) * s1 + pltpu.roll(t, half, axis=1) * s2)


def rope_fwd(proj, tables, name):
    S = proj.shape[0]
    tr = _rows(S, ROW_TILE)
    qcol = 2 * C_A // C_B

    def body(q_ref, k_ref, v_ref, c_ref, s1_ref, s2_ref, qo_ref, ko_ref, vo_ref):
        c, s1, s2 = c_ref[...], s1_ref[...], s2_ref[...]
        for p in range(C_B // LANES):
            sl = slice(p * LANES, (p + 1) * LANES)
            qo_ref[:, sl] = (_rotate(q_ref[:, sl].astype(F32), c, s1, s2, 1.0) * HEAD_DIM ** -0.5).astype(BF16)
            ko_ref[:, sl] = _rotate(k_ref[:, sl].astype(F32), c, s1, s2, 1.0).astype(BF16)
        vo_ref[...] = v_ref[...]

    tab = _row_spec(tr, LANES)
    return pl.pallas_call(
        body, name=name, grid=(S // tr,),
        in_specs=[pl.BlockSpec((tr, C_B), lambda i, col=qcol + n: (i, col)) for n in range(3)] + [tab, tab, tab],
        out_specs=[_row_spec(tr, C_B)] * 3,
        out_shape=[jax.ShapeDtypeStruct((S, C_B), BF16)] * 3, compiler_params=_params("parallel"),
    )(proj, proj, proj, *tables)


def rope_bwd(dqs, dks, dvs, tables, dconv, du, name):
    S = dqs[0].shape[0]
    tr = _rows(S, ROW_TILE)
    n = len(dqs)
    base = 2 * C_A
    width = base + 3 * C_B + C_C

    def body(*refs):
        dq_refs, dk_refs, dv_refs = refs[:n], refs[n:2 * n], refs[2 * n:3 * n]
        c_ref, s1_ref, s2_ref, dconv_ref, du_ref, o_ref = refs[3 * n:]
        c, s1, s2 = c_ref[...], s1_ref[...], s2_ref[...]
        o_ref[:, :base] = dconv_ref[...]
        o_ref[:, base + 3 * C_B:] = du_ref[...]
        for p in range(C_B // LANES):
            sl = slice(p * LANES, (p + 1) * LANES)
            dq = sum(r[:, sl].astype(F32) for r in dq_refs)
            dk = sum(r[:, sl].astype(F32) for r in dk_refs)
            dv = sum(r[:, sl].astype(F32) for r in dv_refs)
            at = base + p * LANES
            o_ref[:, at:at + LANES] = (_rotate(dq, c, s1, s2, -1.0) * HEAD_DIM ** -0.5).astype(BF16)
            o_ref[:, C_B + at:C_B + at + LANES] = _rotate(dk, c, s1, s2, -1.0).astype(BF16)
            o_ref[:, 2 * C_B + at:2 * C_B + at + LANES] = dv.astype(BF16)

    tab = _row_spec(tr, LANES)
    return pl.pallas_call(
        body, name=name, grid=(S // tr,),
        in_specs=[_row_spec(tr, C_B)] * (3 * n) + [tab, tab, tab, _row_spec(tr, base), _row_spec(tr, C_C)],
        out_specs=_row_spec(tr, width), out_shape=jax.ShapeDtypeStruct((S, width), BF16),
        compiler_params=_params("parallel"),
    )(*dqs, *dks, *dvs, *tables, dconv, du)


def _attn_specs(bq, width, L):
    per = bq // ATTN_HALF
    last = L // ATTN_HALF - 1
    cur = pl.BlockSpec((None, bq, width), lambda r, j: (r, j, 0))
    prev = pl.BlockSpec((None, ATTN_HALF, width), lambda r, j: (r, jnp.maximum(j * per - 1, 0), 0))
    nxt = pl.BlockSpec((None, ATTN_HALF, width), lambda r, j: (r, jnp.minimum((j + 1) * per, last), 0))
    return prev, cur, nxt


def _window(refs, sl):
    return jnp.concatenate([r[:, sl] for r in refs], axis=0)


def _band_mask(j, bq, L, rows_are_window):
    bw = bq + 2 * ATTN_HALF
    if rows_are_window:
        rp = j * bq - ATTN_HALF + lax.broadcasted_iota(jnp.int32, (bw, 1), 0)
        cp = j * bq + lax.broadcasted_iota(jnp.int32, (1, bq), 1)
        return (jnp.abs(rp - cp) <= ATTN_HALF) & (rp >= 0) & (rp < L)
    rp = j * bq + lax.broadcasted_iota(jnp.int32, (bq, 1), 0)
    cp = j * bq - ATTN_HALF + lax.broadcasted_iota(jnp.int32, (1, bw), 1)
    return (jnp.abs(rp - cp) <= ATTN_HALF) & (cp >= 0) & (cp < L)


def _head_col(stats, h):
    lane = lax.broadcasted_iota(jnp.int32, (1, LANES), 1)
    return jnp.sum(jnp.where(lane == h, stats, 0.0), axis=1, keepdims=True)


_NT = (((1,), (1,)), ((), ()))
_TN = (((0,), (0,)), ((), ()))


def attn_fwd_pattern(qd, kd, vd, name):
    d, L, _ = qd.shape
    bq = _rows(L, ATTN_BLOCK)

    def body(q_ref, kp, kc, kn, vp, vc, vn, o_ref, lse_ref):
        j = pl.program_id(1)
        mask = _band_mask(j, bq, L, False)
        lane = lax.broadcasted_iota(jnp.int32, (1, LANES), 1)
        first = lane < HEAD_DIM
        lse = jnp.zeros((bq, LANES), F32)
        for p in range(C_B // LANES):
            sl = slice(p * LANES, (p + 1) * LANES)
            q = q_ref[:, sl]
            kw = _window((kp, kc, kn), sl)
            vw = _window((vp, vc, vn), sl)
            outs = []
            for hh in range(2):
                qh = jnp.where(first == (hh == 0), q, jnp.zeros_like(q))
                s = jnp.where(mask, lax.dot_general(qh, kw, _NT, preferred_element_type=F32), NEG)
                m = jnp.max(s, axis=1, keepdims=True)
                e = jnp.exp(s - m)
                l = jnp.sum(e, axis=1, keepdims=True)
                outs.append(jnp.dot(e.astype(BF16), vw, preferred_element_type=F32) * (1.0 / l))
                lse = jnp.where(lane == 2 * p + hh, m + jnp.log(l), lse)
            o_ref[:, sl] = jnp.where(first, outs[0], outs[1]).astype(o_ref.dtype)
        lse_ref[...] = lse

    kv = _attn_specs(bq, C_B, L)
    return pl.pallas_call(
        body, name=name, grid=(d, L // bq), in_specs=[kv[1], *kv, *kv],
        out_specs=[kv[1], pl.BlockSpec((None, bq, LANES), lambda r, j: (r, j, 0))],
        out_shape=[jax.ShapeDtypeStruct((d, L, C_B), BF16), jax.ShapeDtypeStruct((d, L, LANES), F32)],
        compiler_params=_params("parallel", "parallel"),
    )(qd, kd, kd, kd, vd, vd, vd)


def attn_combine(os_, lses, gm, ya, yc, name):
    S = os_[0].shape[0]
    tr = _rows(S, ROW_TILE)
    n = len(os_)

    def body(*refs):
        o_refs, l_refs = refs[:n], refs[n:2 * n]
        gm_ref, ya_ref, yc_ref, y_ref, out_ref, lse_ref = refs[2 * n:]
        ls = [r[...] for r in l_refs]
        mx = functools.reduce(jnp.maximum, ls)
        ws = [jnp.exp(l - mx) for l in ls]
        den = sum(ws)
        lse_ref[...] = mx + jnp.log(den)
        wn = [w / den for w in ws]
        lane = lax.broadcasted_iota(jnp.int32, (1, LANES), 1)
        first = lane < HEAD_DIM
        blocks = []
        for p in range(C_B // LANES):
            sl = slice(p * LANES, (p + 1) * LANES)
            acc = jnp.zeros((tr, LANES), F32)
            for w, o_ref in zip(wn, o_refs):
                acc = acc + jnp.where(first, _head_col(w, 2 * p), _head_col(w, 2 * p + 1)) * o_ref[:, sl].astype(F32)
            blocks.append(acc)
        out = jnp.concatenate(blocks, axis=1)
        out_ref[...] = out.astype(out_ref.dtype)
        y_ref[:, :C_A] = ya_ref[...]
        y_ref[:, C_A:C_A + C_B] = (_rms(out) * gm_ref[...]).astype(y_ref.dtype)
        y_ref[:, C_A + C_B:] = yc_ref[...]

    st = _row_spec(tr, LANES)
    mix = C_A + C_B + C_C
    return pl.pallas_call(
        body, name=name, grid=(S // tr,),
        in_specs=[_row_spec(tr, C_B)] * n + [st] * n + [_vec_spec(C_B), _row_spec(tr, C_A), _row_spec(tr, C_C)],
        out_specs=[_row_spec(tr, mix), _row_spec(tr, C_B), st],
        out_shape=[jax.ShapeDtypeStruct((S, mix), BF16), jax.ShapeDtypeStruct((S, C_B), BF16),
                   jax.ShapeDtypeStruct((S, LANES), F32)],
        compiler_params=_params("parallel"),
    )(*os_, *lses, gm, ya, yc)


def attn_out_bwd(out, dy, gm, name):
    S = out.shape[0]
    tr = _rows(S, ROW_TILE)
    dcol = C_A // C_B

    def body(o_ref, dy1, dy2, g_ref, do_ref, delta_ref, dg_ref):
        o = o_ref[...].astype(F32)
        dyv = jnp.concatenate([dy1[...], dy2[...]], axis=1).astype(F32)
        do, n = _rms_bwd(o, dyv * g_ref[...])
        dob = do.astype(BF16)
        do_ref[...] = dob
        prod = dob.astype(F32) * o
        lane = lax.broadcasted_iota(jnp.int32, (1, LANES), 1)
        first = lane < HEAD_DIM
        delta = jnp.zeros((tr, LANES), F32)
        for p in range(C_B // LANES):
            blk = prod[:, p * LANES:(p + 1) * LANES]
            delta = jnp.where(lane == 2 * p, jnp.sum(jnp.where(first, blk, 0.0), axis=1, keepdims=True), delta)
            delta = jnp.where(lane == 2 * p + 1, jnp.sum(jnp.where(first, 0.0, blk), axis=1, keepdims=True), delta)
        delta_ref[...] = delta

        @pl.when(pl.program_id(0) == 0)
        def _():
            dg_ref[...] = jnp.zeros_like(dg_ref)

        dg_ref[...] += jnp.sum(dyv * n, axis=0, keepdims=True)

    del dcol
    return pl.pallas_call(
        body, name=name, grid=(S // tr,),
        in_specs=[_row_spec(tr, C_B), pl.BlockSpec((tr, C_A), lambda i: (i, 1)), pl.BlockSpec((tr, C_A), lambda i: (i, 2)),
                  _vec_spec(C_B)],
        out_specs=[_row_spec(tr, C_B), _row_spec(tr, LANES), _vec_spec(C_B)],
        out_shape=[jax.ShapeDtypeStruct((S, C_B), BF16), jax.ShapeDtypeStruct((S, LANES), F32),
                   jax.ShapeDtypeStruct((1, C_B), F32)],
        compiler_params=_params("arbitrary"),
    )(out, dy, dy, gm)


def attn_bwd_pattern(qd, kd, vd, dod, lsed, deltad, name):
    d, L, _ = qd.shape
    bq = _rows(L, ATTN_BLOCK)

    def body(qp, qc, qn, kp, kc, kn, vp, vc, vn, dp, dc_, dn, lp, lc, ln_, tp, tc, tn, dq_ref, dk_ref, dv_ref):
        j = pl.program_id(1)
        mask_a = _band_mask(j, bq, L, False)
        mask_b = _band_mask(j, bq, L, True)
        lane = lax.broadcasted_iota(jnp.int32, (1, LANES), 1)
        first = lane < HEAD_DIM
        lse_c, delta_c = lc[...], tc[...]
        lse_w = jnp.concatenate([lp[...], lse_c, ln_[...]], axis=0)
        delta_w = jnp.concatenate([tp[...], delta_c, tn[...]], axis=0)
        for p in range(C_B // LANES):
            sl = slice(p * LANES, (p + 1) * LANES)
            q, k, v, do = qc[:, sl], kc[:, sl], vc[:, sl], dc_[:, sl]
            qw, kw, vw, dow = _window((qp, qc, qn), sl), _window((kp, kc, kn), sl), _window((vp, vc, vn), sl), \
                _window((dp, dc_, dn), sl)
            dqs, dks, dvs = [], [], []
            for hh in range(2):
                h = 2 * p + hh
                sel = first == (hh == 0)
                zero = jnp.zeros_like(q)
                zero_w = jnp.zeros_like(qw)
                s = lax.dot_general(jnp.where(sel, q, zero), kw, _NT, preferred_element_type=F32)
                pr = jnp.where(mask_a, jnp.exp(s - _head_col(lse_c, h)), 0.0)
                dpr = lax.dot_general(jnp.where(sel, do, zero), vw, _NT, preferred_element_type=F32)
                ds = (pr * (dpr - _head_col(delta_c, h))).astype(BF16)
                dqs.append(jnp.dot(ds, kw, preferred_element_type=F32))
                s2 = lax.dot_general(jnp.where(sel, qw, zero_w), k, _NT, preferred_element_type=F32)
                pr2 = jnp.where(mask_b, jnp.exp(s2 - _head_col(lse_w, h)), 0.0)
                dvs.append(lax.dot_general(pr2.astype(BF16), dow, _TN, preferred_element_type=F32))
                dpr2 = lax.dot_general(jnp.where(sel, dow, zero_w), v, _NT, preferred_element_type=F32)
                ds2 = (pr2 * (dpr2 - _head_col(delta_w, h))).astype(BF16)
                dks.append(lax.dot_general(ds2, qw, _TN, preferred_element_type=F32))
            dq_ref[:, sl] = jnp.where(first, dqs[0], dqs[1]).astype(dq_ref.dtype)
            dk_ref[:, sl] = jnp.where(first, dks[0], dks[1]).astype(dk_ref.dtype)
            dv_ref[:, sl] = jnp.where(first, dvs[0], dvs[1]).astype(dv_ref.dtype)

    wide = _attn_specs(bq, C_B, L)
    stat = _attn_specs(bq, LANES, L)
    shape = jax.ShapeDtypeStruct((d, L, C_B), BF16)
    return pl.pallas_call(
        body, name=name, grid=(d, L // bq), in_specs=[*wide, *wide, *wide, *wide, *stat, *stat],
        out_specs=[wide[1]] * 3, out_shape=[shape] * 3, compiler_params=_params("parallel", "parallel"),
    )(qd, qd, qd, kd, kd, kd, vd, vd, vd, dod, dod, dod, lsed, lsed, lsed, deltad, deltad, deltad)


def _dilate(a, d):
    S, C = a.shape
    return a.reshape(S // d, d, C).transpose(1, 0, 2)


def _undilate(a):
    d, L, C = a.shape
    return a.transpose(1, 0, 2).reshape(d * L, C)


def adamw(parts, w, m, v, name):
    n, R, C = parts.shape
    tr = _rows(R, ROW_TILE)

    def body(p_ref, w_ref, m_ref, v_ref, g_ref, d_ref, nm_ref, nv_ref):
        g = p_ref[0].astype(F32)
        for k in range(1, n):
            g = g + p_ref[k].astype(F32)
        mm = ADAM_B1 * m_ref[...] + (1.0 - ADAM_B1) * g
        vv = ADAM_B2 * v_ref[...] + (1.0 - ADAM_B2) * jnp.square(g)
        m_hat = mm / (1.0 - ADAM_B1 ** ADAM_STEP)
        v_hat = vv / (1.0 - ADAM_B2 ** ADAM_STEP)
        g_ref[...] = g
        d_ref[...] = -ADAM_LR * (m_hat / (jnp.sqrt(v_hat) + ADAM_EPS) + ADAM_WD * w_ref[...])
        nm_ref[...] = mm
        nv_ref[...] = vv

    spec = _row_spec(tr, C)
    shape = jax.ShapeDtypeStruct((R, C), F32)
    return pl.pallas_call(
        body, name=name, grid=(R // tr,),
        in_specs=[pl.BlockSpec((n, tr, C), lambda i: (0, i, 0)), spec, spec, spec],
        out_specs=[spec] * 4, out_shape=[shape] * 4, compiler_params=_params("parallel"),
    )(parts, w, m, v)


_ANY = pl.BlockSpec(memory_space=pl.ANY)


def _place():
    return lax.axis_index("x"), lax.axis_index("y"), lax.axis_index("c")


def _index(px, py, pc):
    return 4 * px + 2 * py + pc


class Gather:
    def __init__(self, shards):
        self.ins = list(shards)
        T = self.n = len(shards)
        self.out_shape = [jax.ShapeDtypeStruct((N_DEV, *s.shape), s.dtype) for s in shards]
        self.sems = [pltpu.SemaphoreType.DMA((T, 7)), pltpu.SemaphoreType.DMA((T, 7)), pltpu.SemaphoreType.DMA((T,))]

    def _plan(self, ins, outs, sems):
        send_sems, recv_sems, local_sems = sems
        x, y, c = _place()
        me, sibling = (x, y, c), (x, y, 1 - c)
        chips = [(1 - x, y), (x, 1 - y), (1 - x, 1 - y)]

        def copy(t, k, block, to, src=None):
            rows = outs[t].at[_index(*block)]
            return pltpu.make_async_remote_copy(
                src_ref=rows if src is None else src, dst_ref=rows, send_sem=send_sems.at[t, k],
                recv_sem=recv_sems.at[t, k], device_id=to, device_id_type=MESH)

        mine = [pltpu.make_async_copy(ins[t], outs[t].at[_index(*me)], local_sems.at[t]) for t in range(self.n)]
        first = []
        for t in range(self.n):
            first.append(copy(t, 0, me, sibling, src=ins[t]))
            first += [copy(t, 1 + j, me, (*chip, c), src=ins[t]) for j, chip in enumerate(chips)]
        return copy, mine, first, me, sibling, chips, c

    def start(self, ins, outs, sems):
        _, mine, first, *_ = self._plan(ins, outs, sems)
        for cp in mine + first:
            cp.start()

    def finish(self, ins, outs, sems):
        copy, mine, first, me, sibling, chips, c = self._plan(ins, outs, sems)
        passed = []
        for j, chip in enumerate(chips):
            for t in range(self.n):
                copy(t, 1 + j, (*chip, c), me).wait_recv()
                fwd = copy(t, 4 + j, (*chip, c), sibling)
                fwd.start()
                passed.append(fwd)
        for t in range(self.n):
            copy(t, 0, sibling, me).wait_recv()
            for j, chip in enumerate(chips):
                copy(t, 4 + j, (*chip, 1 - c), me).wait_recv()
        for cp in first + passed:
            cp.wait_send()
        for cp in mine:
            cp.wait()


class Exchange:
    def __init__(self, parts):
        self.ins = list(parts)
        T = self.n = len(parts)
        self.out_shape = [jax.ShapeDtypeStruct(p.shape, p.dtype) for p in parts]
        self.sems = [pltpu.SemaphoreType.DMA((T, 7)), pltpu.SemaphoreType.DMA((T, 7)), pltpu.SemaphoreType.DMA((T,))]

    def _plan(self, ins, outs, sems):
        send_sems, recv_sems, local_sems = sems
        x, y, c = _place()
        me = _index(x, y, c)
        copies = [pltpu.make_async_copy(ins[t].at[me], outs[t].at[me], local_sems.at[t]) for t in range(self.n)]
        for k in range(1, N_DEV):
            peer = ((x + (k >> 2)) % 2, (y + ((k >> 1) & 1)) % 2, (c + (k & 1)) % 2)
            there = _index(*peer)
            for t in range(self.n):
                copies.append(pltpu.make_async_remote_copy(
                    src_ref=ins[t].at[there], dst_ref=outs[t].at[me], send_sem=send_sems.at[t, k - 1],
                    recv_sem=recv_sems.at[t, k - 1], device_id=peer, device_id_type=MESH))
        return copies

    def start(self, ins, outs, sems):
        for cp in self._plan(ins, outs, sems):
            cp.start()

    def finish(self, ins, outs, sems):
        for cp in self._plan(ins, outs, sems):
            cp.wait()


def communicate(comm, name):
    T = comm.n

    def body(*refs):
        ins, outs, sems = refs[:T], refs[T:2 * T], refs[2 * T:]
        comm.start(ins, outs, sems)
        comm.finish(ins, outs, sems)

    return pl.pallas_call(
        body, name=name, in_specs=[_ANY] * T, out_specs=[_ANY] * T, out_shape=comm.out_shape,
        scratch_shapes=comm.sems,
    )(*comm.ins)


def all_gather(shards, name):
    return communicate(Gather(shards), name)


def exchange(parts, name):
    return communicate(Exchange(parts), name)


def _row(v):
    return v.reshape(1, -1)


def _hosted(res):
    return res if isinstance(res, tuple) else (res, None)


def mix_forward(x, w, p, tables, comm=None):
    gm = p["g_mix"]
    h1, h1t = rms_fwd(x, _row(p["g_pre_mix"]), "rms_pre_mix")
    proj, got = _hosted(matmul(h1, w["win"], "nn", BF16, "proj", comm=comm))
    ya = conv_fwd(proj, p["cw"], _row(p["conv_b"]), _row(p["conv_ln_g"]), _row(p["conv_ln_b"]), _row(gm[:C_A]),
                  "conv_fwd")
    q, k, v = rope_fwd(proj, tables, "rope_fwd")
    dil = [tuple(_dilate(a, d) for a in (q, k, v)) for d in DILATIONS]
    os_, lses = [], []
    for d, (qd, kd, vd) in zip(DILATIONS, dil):
        o, lse = attn_fwd_pattern(qd, kd, vd, f"attn_fwd_d{d}")
        os_.append(_undilate(o))
        lses.append(_undilate(lse))
    yc = pool_fwd(proj, p["pool_w"], _row(p["pool_scale"]), _row(gm[C_A + C_B:]), "pool_fwd")
    y, out, lse = attn_combine(os_, lses, _row(gm[C_A:C_A + C_B]), ya, yc, "attn_combine")
    z = matmul(y, w["wout"], "nn", BF16, "mix_out")
    x2 = norm_residual(z, x, _row(p["g_post_mix"]), "res_mix")
    return x2, dict(x=x, h1t=h1t, proj=proj, dil=dil, out=out, lse=lse, y=y, z=z), got


def ffn_forward(x2, w, p, comm_in=None, comm_out=None):
    h2, h2t = rms_fwd(x2, _row(p["g_pre_ffn"]), "rms_pre_ffn")
    gu, got_in = _hosted(matmul(h2, w["wgu"], "nn", BF16, "ffn_in", comm=comm_in))
    a, at = swiglu_fwd(gu, "swiglu_fwd")
    f, got_out = _hosted(matmul(a, w["wd"], "nn", BF16, "ffn_out", tk=2816, comm=comm_out))
    x3 = norm_residual(f, x2, _row(p["g_post_ffn"]), "res_ffn")
    return x3, dict(x2=x2, h2t=h2t, gu=gu, at=at, f=f), got_in, got_out


def _to_blocks(g, by_columns):
    if by_columns:
        rows = g.shape[0]
        return g.reshape(rows, N_DEV, -1).transpose(1, 0, 2).astype(BF16)
    return g.reshape(N_DEV, -1, g.shape[1]).astype(BF16)


def _from_blocks(b, by_columns):
    if by_columns:
        return b.transpose(1, 0, 2).reshape(b.shape[1], -1)
    return b.reshape(-1, b.shape[2])


def _exchange_of(g, by_columns):
    return Exchange([_to_blocks(g, by_columns)]) if g is not None else None


def ffn_backward(dx3, s, w, p, ride_act=None, ride_down=None):
    df, dg_post_ffn = rms_bwd(s["f"], dx3, _row(p["g_post_ffn"]), None, BF16, "rms_bwd_post_ffn")
    da, got_act = _hosted(matmul(df, w["wd"], "nt", BF16, "d_ffn_act", comm=ride_act))
    dwd, got_down = _hosted(matmul(s["at"], df, "nn", F32, "dw_down", tm=1408, tn=1024, tk=1024, comm=ride_down))
    dgu = swiglu_bwd(s["gu"], da, "swiglu_bwd")
    dh2, got_wd = matmul(dgu, w["wgu"], "nt", BF16, "d_ffn_in", tk=2816, comm=_exchange_of(dwd, False))
    dwgu = matmul(s["h2t"], dgu, "nn", F32, "dw_gate_up", tm=1024, tn=1408, tk=1024)
    dx2, dg_pre_ffn = rms_bwd(s["x2"], dh2, _row(p["g_pre_ffn"]), dx3, F32, "rms_bwd_pre_ffn")
    return dx2, dict(dwgu=dwgu, parts_wd=got_wd[0], got_act=got_act, got_down=got_down,
                     g_pre_ffn=dg_pre_ffn[0], g_post_ffn=dg_post_ffn[0])


def mix_backward(dx2, s, w, p, tables, dwgu):
    gm = p["g_mix"]
    dz, dg_post_mix = rms_bwd(s["z"], dx2, _row(p["g_post_mix"]), None, BF16, "rms_bwd_post_mix")
    dy = matmul(dz, w["wout"], "nt", BF16, "d_mix")
    dwout = matmul(s["y"], dz, "tn", F32, "dw_out", tm=1024, tn=1024, tk=512)
    dconv, dcw, dcb, dlg, dlb, dgm_a = conv_bwd(
        s["proj"], dy, p["cw"], _row(p["conv_b"]), _row(p["conv_ln_g"]), _row(p["conv_ln_b"]), _row(gm[:C_A]),
        "conv_bwd")
    do, delta, dgm_b = attn_out_bwd(s["out"], dy, _row(gm[C_A:C_A + C_B]), "attn_out_bwd")
    dqs, dks, dvs = [], [], []
    for d, (qd, kd, vd) in zip(DILATIONS, s["dil"]):
        dq, dk, dv = attn_bwd_pattern(qd, kd, vd, _dilate(do, d), _dilate(s["lse"], d), _dilate(delta, d),
                                      f"attn_bwd_d{d}")
        dqs.append(_undilate(dq))
        dks.append(_undilate(dk))
        dvs.append(_undilate(dv))
    du, dpw, dps, dgm_c = pool_bwd(s["proj"], dy, p["pool_w"], _row(p["pool_scale"]), _row(gm[C_A + C_B:]),
                                   "pool_bwd")
    dproj = rope_bwd(dqs, dks, dvs, tables, dconv, du, "rope_bwd")
    F = dwgu.shape[1] // 2
    dh1, got_gate = matmul(dproj, w["win"], "nt", BF16, "d_proj", tk=2304, comm=_exchange_of(dwgu[:, :F], True))
    dwin, got_up = matmul(s["h1t"], dproj, "nn", F32, "dw_in", tm=1024, tn=1152, tk=1024,
                          comm=_exchange_of(dwgu[:, F:], True))
    dx, dg_pre_mix = rms_bwd(s["x"], dh1, _row(p["g_pre_mix"]), dx2, F32, "rms_bwd_pre_mix")
    return dx, dict(
        dwin=dwin, dwout=dwout, parts_gate=got_gate[0], parts_up=got_up[0],
        conv_w=dcw[:CONV_WIDTH], conv_b=dcb[0], conv_ln_g=dlg[0], conv_ln_b=dlb[0], pool_w=dpw, pool_scale=dps[0],
        g_mix=jnp.concatenate([dgm_a[0], dgm_b[0], dgm_c[0]]), g_pre_mix=dg_pre_mix[0], g_post_mix=dg_post_mix[0])


WEIGHTS = ["w_in", "conv_w", "conv_b", "conv_ln_g", "conv_ln_b", "pool_w", "pool_scale", "g_mix", "w_out", "g_pre_mix",
           "g_post_mix", "g_pre_ffn", "g_post_ffn", "w_gate", "w_up", "w_down"]
BIG = ["w_in", "w_out", "w_gate", "w_up", "w_down"]
REPLICATED = ["conv_b", "conv_ln_g", "conv_ln_b", "pool_w", "pool_scale", "g_mix", "g_pre_mix", "g_post_mix",
              "g_pre_ffn", "g_post_ffn"]
PACK_ROWS = 256


def adamw_layer(parts, w, m, v, layer, prev, name):
    n, R, C = parts.shape
    tr = _rows(R, ROW_TILE)

    def body(p_ref, w_ref, m_ref, v_ref, *rest):
        g_ref, d_ref, nm_ref, nv_ref = rest[-4:]
        g = p_ref[0].astype(F32)
        for k in range(1, n):
            g = g + p_ref[k].astype(F32)
        mm = ADAM_B1 * m_ref[...] + (1.0 - ADAM_B1) * g
        vv = ADAM_B2 * v_ref[...] + (1.0 - ADAM_B2) * jnp.square(g)
        m_hat = mm / (1.0 - ADAM_B1 ** ADAM_STEP)
        v_hat = vv / (1.0 - ADAM_B2 ** ADAM_STEP)
        g_ref[...] = g
        d_ref[...] = -ADAM_LR * (m_hat / (jnp.sqrt(v_hat) + ADAM_EPS) + ADAM_WD * w_ref[...])
        nm_ref[...] = mm
        nv_ref[...] = vv

    spec = pl.BlockSpec((None, tr, C), lambda i: (layer, i, 0))
    shape = jax.ShapeDtypeStruct(w.shape, F32)
    prev = list(prev) if prev is not None else []
    return pl.pallas_call(
        body, name=name, grid=(R // tr,),
        in_specs=[pl.BlockSpec((n, tr, C), lambda i: (0, i, 0)), spec, spec, spec] + [_ANY] * len(prev),
        out_specs=[spec] * 4, out_shape=[shape] * 4,
        input_output_aliases={4 + k: k for k in range(len(prev))}, compiler_params=_params("parallel"),
    )(parts, w, m, v, *prev)


def _pack(arrays):
    flat = jnp.concatenate([a.reshape(-1).astype(F32) for a in arrays])
    unit = PACK_ROWS * LANES
    padded = -(-flat.shape[0] // unit) * unit
    return jnp.pad(flat, (0, padded - flat.shape[0])).reshape(-1, LANES)


def _unpack(packed, like):
    flat = packed.reshape(-1)
    out, at = [], 0
    for a in like:
        out.append(flat[at:at + a.size].reshape(a.shape))
        at += a.size
    return out


def kernel(x, w_in, conv_w, conv_b, conv_ln_g, conv_ln_b, pool_w, pool_scale, g_mix, w_out, g_pre_mix, g_post_mix, g_pre_ffn, g_post_ffn, w_gate, w_up, w_down, loss_target, m_w_in, m_conv_w, m_conv_b, m_conv_ln_g, m_conv_ln_b, m_pool_w, m_pool_scale, m_g_mix, m_w_out, m_g_pre_mix, m_g_post_mix, m_g_pre_ffn, m_g_post_ffn, m_w_gate, m_w_up, m_w_down, v_w_in, v_conv_w, v_conv_b, v_conv_ln_g, v_conv_ln_b, v_pool_w, v_pool_scale, v_g_mix, v_w_out, v_g_pre_mix, v_g_post_mix, v_g_pre_ffn, v_g_post_ffn, v_w_gate, v_w_up, v_w_down):
    w = dict(w_in=w_in, conv_w=conv_w, conv_b=conv_b, conv_ln_g=conv_ln_g, conv_ln_b=conv_ln_b, pool_w=pool_w,
             pool_scale=pool_scale, g_mix=g_mix, w_out=w_out, g_pre_mix=g_pre_mix, g_post_mix=g_post_mix,
             g_pre_ffn=g_pre_ffn, g_post_ffn=g_post_ffn, w_gate=w_gate, w_up=w_up, w_down=w_down)
    m = dict(w_in=m_w_in, conv_w=m_conv_w, conv_b=m_conv_b, conv_ln_g=m_conv_ln_g, conv_ln_b=m_conv_ln_b,
             pool_w=m_pool_w, pool_scale=m_pool_scale, g_mix=m_g_mix, w_out=m_w_out, g_pre_mix=m_g_pre_mix,
             g_post_mix=m_g_post_mix, g_pre_ffn=m_g_pre_ffn, g_post_ffn=m_g_post_ffn, w_gate=m_w_gate, w_up=m_w_up,
             w_down=m_w_down)
    v = dict(w_in=v_w_in, conv_w=v_conv_w, conv_b=v_conv_b, conv_ln_g=v_conv_ln_g, conv_ln_b=v_conv_ln_b,
             pool_w=v_pool_w, pool_scale=v_pool_scale, g_mix=v_g_mix, w_out=v_w_out, g_pre_mix=v_g_pre_mix,
             g_post_mix=v_g_post_mix, g_pre_ffn=v_g_pre_ffn, g_post_ffn=v_g_post_ffn, w_gate=v_w_gate, w_up=v_w_up,
             w_down=v_w_down)
    depth = w_in.shape[0]
    xs, target = x[0], loss_target[0]
    S, D = xs.shape
    tables = rope_tables(S)

    cw_shard = jnp.pad(conv_w, ((0, 0), (0, 1), (0, 0)))
    cw_all = all_gather([cw_shard.reshape(-1, LANES)], "gather_conv_w")[0]
    cw_full = cw_all.reshape(N_DEV, depth, 32, -1).transpose(1, 2, 0, 3).reshape(depth, 32, C_A)
    small = []
    for l in range(depth):
        small.append({n: w[n][l] for n in REPLICATED})
        small[l]["cw"] = cw_full[l]

    def mix_shards(l):
        return Gather([w["w_in"][l].astype(BF16), w["w_out"][l].astype(BF16)])

    def ffn_shards(l):
        return Gather([w[n][l].astype(BF16) for n in ("w_gate", "w_up", "w_down")])

    def mix_weights(blocks):
        return dict(win=_from_blocks(blocks[0], True), wout=_from_blocks(blocks[1], False))

    def ffn_weights(blocks):
        return dict(wgu=jnp.concatenate([_from_blocks(blocks[0], True), _from_blocks(blocks[1], True)], axis=1),
                    wd=_from_blocks(blocks[2], False))

    w_mix = [None] * depth
    w_ffn = [None] * depth
    saved_mix, saved_ffn = [None] * depth, [None] * depth
    w_mix[0] = mix_weights(communicate(mix_shards(0), "gather_mix_weights"))
    h = xs
    for l in range(depth):
        x2, saved_mix[l], got = mix_forward(h, w_mix[l], small[l], tables, ffn_shards(0) if l == 0 else None)
        if l == 0:
            w_ffn[0] = ffn_weights(got)
        more = l + 1 < depth
        h, saved_ffn[l], got_in, got_out = ffn_forward(
            x2, w_ffn[l], small[l], ffn_shards(l + 1) if more else None, mix_shards(l + 1) if more else None)
        if more:
            w_ffn[l + 1] = ffn_weights(got_in)
            w_mix[l + 1] = mix_weights(got_out)

    dh, sq = loss_head(h, target, "loss_head")
    loss = lax.psum(0.5 * jnp.sum(sq) / D, ("x", "y", "c"))

    big_out = {n: None for n in BIG}

    def update(n, l, parts):
        big_out[n] = adamw_layer(parts, w[n], m[n], v[n], l, big_out[n], f"adamw_{n}_layer{l}")

    small_grads = [None] * depth
    carried = None
    for l in reversed(range(depth)):
        ride_act = _exchange_of(carried[0], True) if carried else None
        ride_down = _exchange_of(carried[1], False) if carried else None
        dx2, gf = ffn_backward(dh, saved_ffn[l], w_ffn[l], small[l], ride_act, ride_down)
        if carried:
            update("w_in", l + 1, gf["got_act"][0])
            update("w_out", l + 1, gf["got_down"][0])
        update("w_down", l, gf["parts_wd"])
        dh, gm_ = mix_backward(dx2, saved_mix[l], w_mix[l], small[l], tables, gf["dwgu"])
        update("w_gate", l, gm_["parts_gate"])
        update("w_up", l, gm_["parts_up"])
        carried = (gm_["dwin"], gm_["dwout"])
        small_grads[l] = {**gf, **gm_}
    last = exchange([_to_blocks(carried[0], True), _to_blocks(carried[1], False)], "exchange_last_grads")
    update("w_in", 0, last[0])
    update("w_out", 0, last[1])

    names = REPLICATED + ["conv_w"]
    stacked = [jnp.stack([small_grads[l][n] for l in range(depth)]) for n in names]
    partial = all_gather([_pack(stacked)], "gather_small_grads")[0]
    zeros = jnp.zeros_like(stacked[-1])
    packed = adamw(partial, _pack([w[n] for n in REPLICATED] + [zeros]), _pack([m[n] for n in REPLICATED] + [zeros]),
                   _pack([v[n] for n in REPLICATED] + [zeros + 1.0]), "adamw_replicated")
    small_out = [_unpack(o, stacked) for o in packed]
    out = {n: tuple(o[i] for o in small_out) for i, n in enumerate(REPLICATED)}
    width = conv_w.shape[2]
    g_cw = lax.dynamic_slice_in_dim(small_out[0][-1], _index(*_place()) * width, width, axis=2)
    cw_res = adamw(g_cw.reshape(1, -1, LANES), conv_w.reshape(-1, LANES), m["conv_w"].reshape(-1, LANES),
                   v["conv_w"].reshape(-1, LANES), "adamw_conv_w")
    out["conv_w"] = tuple(o.reshape(conv_w.shape) for o in cw_res)
    for n in BIG:
        out[n] = tuple(big_out[n])
    results = [loss, dh[None]]
    for k in range(4):
        results += [out[n][k] for n in WEIGHTS]
    return tuple(results)
```

```python
import functools
import math

import jax
import jax.numpy as jnp
from jax import lax
from jax.experimental import pallas as pl
from jax.experimental.pallas import tpu as pltpu

F32 = jnp.float32
BF16 = jnp.bfloat16

N_DEV = 8
DEPTH = 4
EPS = 1e-6
NEG = -1e30

C_A = 512
N_HEADS = 16
HEAD_DIM = 64
C_B = N_HEADS * HEAD_DIM
C_C = 512
POOL_WINDOWS = (2, 4, 8, 16)
C_G = C_C // len(POOL_WINDOWS)
CONV_WIDTH = 31
CONV_HALF = CONV_WIDTH // 2
DILATIONS = (1, 4, 16)
ATTN_HALF = 64
ROT_DIM = HEAD_DIM // 4
ROPE_THETA = 500000.0

ADAM_LR = 0.001
ADAM_B1 = 0.9
ADAM_B2 = 0.999
ADAM_EPS = 1e-08
ADAM_WD = 0.01
ADAM_STEP = 10

LANES = 128
VMEM_LIMIT = 56 * 1024 * 1024
ROW_TILE = 256
SEQ_TILE = 256
ATTN_BLOCK = 128
MESH = pl.DeviceIdType.MESH


def _params(*sem):
    return pltpu.CompilerParams(dimension_semantics=sem, vmem_limit_bytes=VMEM_LIMIT)


def _tile(n, target):
    if n <= target:
        return n
    t = (target // LANES) * LANES
    while t >= LANES:
        if n % t == 0:
            return t
        t -= LANES
    return n


def _rows(n, target):
    t = min(n, target)
    while n % t:
        t //= 2
    return t


def matmul(a, b, mode, out_dtype, name, tm=1024, tn=512, tk=2048, comm=None):
    if mode == "nn":
        (M, K), (_, N) = a.shape, b.shape
    elif mode == "nt":
        (M, K), (N, _) = a.shape, b.shape
    else:
        (K, M), (_, N) = a.shape, b.shape
    tm, tn, tk = _tile(M, tm), _tile(N, tn), _tile(K, tk)
    nm, nn, nk = M // tm, N // tn, K // tk
    dims = {"nn": (((1,), (0,)), ((), ())), "nt": (((1,), (1,)), ((), ())), "tn": (((0,), (0,)), ((), ()))}[mode]
    nc = comm.n if comm is not None else 0

    def body(*refs):
        a_ref, b_ref = refs[:2]
        c_ins, o_ref, c_outs = refs[2:2 + nc], refs[2 + nc], refs[3 + nc:3 + 2 * nc]
        scratch = refs[3 + 2 * nc:]
        acc, sems = (scratch[:1], scratch[1:]) if nk > 1 else ((), scratch)
        i, j, k = pl.program_id(0), pl.program_id(1), pl.program_id(2)
        if comm is not None:
            @pl.when((i == 0) & (j == 0) & (k == 0))
            def _():
                comm.start(c_ins, c_outs, sems)

        p = lax.dot_general(a_ref[...], b_ref[...], dims, preferred_element_type=F32)
        if nk == 1:
            o_ref[...] = p.astype(o_ref.dtype)
        else:
            acc_ref, = acc

            @pl.when(k == 0)
            def _():
                acc_ref[...] = p

            @pl.when(k > 0)
            def _():
                acc_ref[...] += p

            @pl.when(k == nk - 1)
            def _():
                o_ref[...] = acc_ref[...].astype(o_ref.dtype)

        if comm is not None:
            @pl.when((i == nm - 1) & (j == nn - 1) & (k == nk - 1))
            def _():
                comm.finish(c_ins, c_outs, sems)

    if mode == "nn":
        a_spec = pl.BlockSpec((tm, tk), lambda i, j, k: (i, k))
        b_spec = pl.BlockSpec((tk, tn), lambda i, j, k: (k, j))
    elif mode == "nt":
        a_spec = pl.BlockSpec((tm, tk), lambda i, j, k: (i, k))
        b_spec = pl.BlockSpec((tn, tk), lambda i, j, k: (j, k))
    else:
        a_spec = pl.BlockSpec((tk, tm), lambda i, j, k: (k, i))
        b_spec = pl.BlockSpec((tk, tn), lambda i, j, k: (k, j))
    o_spec = pl.BlockSpec((tm, tn), lambda i, j, k: (i, j))
    o_shape = jax.ShapeDtypeStruct((M, N), out_dtype)
    acc_shape = [pltpu.VMEM((tm, tn), F32)] if nk > 1 else []
    if comm is None:
        return pl.pallas_call(
            body, name=name, grid=(nm, nn, nk), in_specs=[a_spec, b_spec], out_specs=o_spec, out_shape=o_shape,
            scratch_shapes=acc_shape, compiler_params=_params("parallel", "parallel", "arbitrary"),
        )(a, b)
    res = pl.pallas_call(
        body, name=name, grid=(nm, nn, nk), in_specs=[a_spec, b_spec] + [_ANY] * nc,
        out_specs=[o_spec] + [_ANY] * nc, out_shape=[o_shape] + comm.out_shape,
        scratch_shapes=acc_shape + comm.sems, compiler_params=_params("arbitrary", "arbitrary", "arbitrary"),
    )(a, b, *comm.ins)
    return res[0], res[1:]


def _rms(t):
    return t * lax.rsqrt(jnp.mean(t * t, axis=-1, keepdims=True) + EPS)


def _rms_bwd(t, dn):
    r = lax.rsqrt(jnp.mean(t * t, axis=-1, keepdims=True) + EPS)
    n = t * r
    return r * (dn - n * jnp.mean(dn * n, axis=-1, keepdims=True)), n


def _row_spec(tr, d):
    return pl.BlockSpec((tr, d), lambda i: (i, 0))


def _vec_spec(d):
    return pl.BlockSpec((1, d), lambda i: (0, 0))


def _col_spec(d, tr):
    return pl.BlockSpec((d, tr), lambda i: (0, i))


def rms_fwd(x, g, name):
    S, D = x.shape
    tr = _rows(S, ROW_TILE)

    def body(x_ref, g_ref, o_ref, ot_ref):
        h = _rms(x_ref[...].astype(F32)) * g_ref[...]
        o_ref[...] = h.astype(o_ref.dtype)
        ot_ref[...] = h.T.astype(ot_ref.dtype)

    return pl.pallas_call(
        body, name=name, grid=(S // tr,), in_specs=[_row_spec(tr, D), _vec_spec(D)],
        out_specs=[_row_spec(tr, D), _col_spec(D, tr)],
        out_shape=[jax.ShapeDtypeStruct((S, D), BF16), jax.ShapeDtypeStruct((D, S), BF16)],
        compiler_params=_params("parallel"),
    )(x, g)


def norm_residual(z, x, g, name):
    S, D = x.shape
    tr = _rows(S, ROW_TILE)

    def body(z_ref, x_ref, g_ref, o_ref):
        o_ref[...] = x_ref[...] + _rms(z_ref[...].astype(F32)) * g_ref[...]

    return pl.pallas_call(
        body, name=name, grid=(S // tr,), in_specs=[_row_spec(tr, D), _row_spec(tr, D), _vec_spec(D)],
        out_specs=_row_spec(tr, D), out_shape=jax.ShapeDtypeStruct((S, D), F32), compiler_params=_params("parallel"),
    )(z, x, g)


def rms_bwd(t, dy, g, res, out_dtype, name):
    S, D = t.shape
    tr = _rows(S, ROW_TILE)
    has_res = res is not None

    def body(t_ref, dy_ref, g_ref, *rest):
        if has_res:
            res_ref, dt_ref, dg_ref = rest
        else:
            dt_ref, dg_ref = rest
        dyv = dy_ref[...].astype(F32)
        dt, n = _rms_bwd(t_ref[...].astype(F32), dyv * g_ref[...])
        if has_res:
            dt = dt + res_ref[...]
        dt_ref[...] = dt.astype(dt_ref.dtype)

        @pl.when(pl.program_id(0) == 0)
        def _():
            dg_ref[...] = jnp.zeros_like(dg_ref)

        dg_ref[...] += jnp.sum(dyv * n, axis=0, keepdims=True)

    ins = [t, dy, g] + ([res] if has_res else [])
    specs = [_row_spec(tr, D), _row_spec(tr, D), _vec_spec(D)] + ([_row_spec(tr, D)] if has_res else [])
    return pl.pallas_call(
        body, name=name, grid=(S // tr,), in_specs=specs, out_specs=[_row_spec(tr, D), _vec_spec(D)],
        out_shape=[jax.ShapeDtypeStruct((S, D), out_dtype), jax.ShapeDtypeStruct((1, D), F32)],
        compiler_params=_params("arbitrary"),
    )(*ins)


def swiglu_fwd(gu, name):
    S, F2 = gu.shape
    F = F2 // 2
    tr = _rows(S, ROW_TILE)

    def body(g_ref, u_ref, o_ref, ot_ref):
        g = g_ref[...].astype(F32)
        a = g * jax.nn.sigmoid(g) * u_ref[...].astype(F32)
        o_ref[...] = a.astype(o_ref.dtype)
        ot_ref[...] = a.T.astype(ot_ref.dtype)

    return pl.pallas_call(
        body, name=name, grid=(S // tr,),
        in_specs=[pl.BlockSpec((tr, F), lambda i: (i, 0)), pl.BlockSpec((tr, F), lambda i: (i, 1))],
        out_specs=[_row_spec(tr, F), _col_spec(F, tr)],
        out_shape=[jax.ShapeDtypeStruct((S, F), BF16), jax.ShapeDtypeStruct((F, S), BF16)],
        compiler_params=_params("parallel"),
    )(gu, gu)


def swiglu_bwd(gu, da, name):
    S, F2 = gu.shape
    F = F2 // 2
    tr = _rows(S, ROW_TILE)

    def body(g_ref, u_ref, da_ref, o_ref):
        g = g_ref[...].astype(F32)
        u = u_ref[...].astype(F32)
        dav = da_ref[...].astype(F32)
        sig = jax.nn.sigmoid(g)
        o_ref[:, :F] = (dav * u * (sig * (1.0 + g * (1.0 - sig)))).astype(o_ref.dtype)
        o_ref[:, F:] = (dav * (g * sig)).astype(o_ref.dtype)

    return pl.pallas_call(
        body, name=name, grid=(S // tr,),
        in_specs=[pl.BlockSpec((tr, F), lambda i: (i, 0)), pl.BlockSpec((tr, F), lambda i: (i, 1)), _row_spec(tr, F)],
        out_specs=_row_spec(tr, F2), out_shape=jax.ShapeDtypeStruct((S, F2), BF16), compiler_params=_params("parallel"),
    )(gu, gu, da)


def loss_head(y, target, name):
    S, D = y.shape
    tr = _rows(S, ROW_TILE)

    def body(y_ref, t_ref, dy_ref, sq_ref):
        e = y_ref[...] - t_ref[...]
        dy_ref[...] = e * (1.0 / D)

        @pl.when(pl.program_id(0) == 0)
        def _():
            sq_ref[...] = jnp.zeros_like(sq_ref)

        sq_ref[...] += jnp.sum(e * e, axis=0, keepdims=True)

    return pl.pallas_call(
        body, name=name, grid=(S // tr,), in_specs=[_row_spec(tr, D), _row_spec(tr, D)],
        out_specs=[_row_spec(tr, D), _vec_spec(D)],
        out_shape=[jax.ShapeDtypeStruct((S, D), F32), jax.ShapeDtypeStruct((1, D), F32)],
        compiler_params=_params("arbitrary"),
    )(y, target)


def _halo_specs(bs, halo, width, col, n_rows):
    per = bs // halo
    last = n_rows // halo - 1
    cur = pl.BlockSpec((bs, width), lambda i: (i, col))
    prev = pl.BlockSpec((halo, width), lambda i: (jnp.maximum(i * per - 1, 0), col))
    nxt = pl.BlockSpec((halo, width), lambda i: (jnp.minimum((i + 1) * per, last), col))
    return prev, cur, nxt


def _glu(a, g):
    return a.astype(F32) * jax.nn.sigmoid(g.astype(F32))


def _layernorm_silu(c, lg, lb):
    mu = jnp.mean(c, axis=-1, keepdims=True)
    cc = c - mu
    rstd = lax.rsqrt(jnp.mean(cc * cc, axis=-1, keepdims=True) + EPS)
    xh = cc * rstd
    ln = xh * lg + lb
    sig = jax.nn.sigmoid(ln)
    return xh, rstd, ln, sig


def conv_fwd(proj, cw, cb, lg, lb, gm, name):
    S = proj.shape[0]
    bs = _rows(S, SEQ_TILE)
    nb = S // bs
    H = 16

    def body(ap, ac, an, gp, gc, gn, cw_ref, cb_ref, lg_ref, lb_ref, gm_ref, y_ref, win_ref):
        i = pl.program_id(0)
        win_ref[0:H, :] = jnp.where(i > 0, _glu(ap[...], gp[...]), 0.0)
        win_ref[H:H + bs, :] = _glu(ac[...], gc[...])
        win_ref[H + bs:, :] = jnp.where(i < nb - 1, _glu(an[...], gn[...]), 0.0)
        c = jnp.zeros((bs, C_A), F32) + cb_ref[...]
        for t in range(CONV_WIDTH):
            c = c + cw_ref[t:t + 1, :] * win_ref[pl.ds(H - CONV_HALF + t, bs), :]
        _, _, ln, sig = _layernorm_silu(c, lg_ref[...], lb_ref[...])
        y_ref[...] = (_rms(ln * sig) * gm_ref[...]).astype(y_ref.dtype)

    a_specs = _halo_specs(bs, H, C_A, 0, S)
    g_specs = _halo_specs(bs, H, C_A, 1, S)
    vec = _vec_spec(C_A)
    return pl.pallas_call(
        body, name=name, grid=(nb,),
        in_specs=[*a_specs, *g_specs, pl.BlockSpec((32, C_A), lambda i: (0, 0)), vec, vec, vec, vec],
        out_specs=_row_spec(bs, C_A), out_shape=jax.ShapeDtypeStruct((S, C_A), BF16),
        scratch_shapes=[pltpu.VMEM((bs + 2 * H, C_A), F32)], compiler_params=_params("parallel"),
    )(proj, proj, proj, proj, proj, proj, cw, cb, lg, lb, gm)


def conv_bwd(proj, dy, cw, cb, lg, lb, gm, name):
    S = proj.shape[0]
    bs = _rows(S, SEQ_TILE)
    nb = S // bs
    H = 32
    HC = 16
    bc = bs + 2 * HC

    def body(ap, ac, an, gp, gc, gn, dp, dc_, dn, cw_ref, cb_ref, lg_ref, lb_ref, gm_ref,
             dproj_ref, dcw_ref, dcb_ref, dlg_ref, dlb_ref, dgm_ref, win_ref, dcs_ref):
        i = pl.program_id(0)
        win_ref[0:H, :] = jnp.where(i > 0, _glu(ap[...], gp[...]), 0.0)
        win_ref[H:H + bs, :] = _glu(ac[...], gc[...])
        win_ref[H + bs:, :] = jnp.where(i < nb - 1, _glu(an[...], gn[...]), 0.0)
        c = jnp.zeros((bc, C_A), F32) + cb_ref[...]
        for t in range(CONV_WIDTH):
            c = c + cw_ref[t:t + 1, :] * win_ref[pl.ds(H - HC - CONV_HALF + t, bc), :]
        xh, rstd, ln, sig = _layernorm_silu(c, lg_ref[...], lb_ref[...])
        ya = ln * sig
        dyv = jnp.concatenate([dp[...], dc_[...], dn[...]], axis=0).astype(F32)
        dya, n = _rms_bwd(ya, dyv * gm_ref[...])
        dln = dya * (sig * (1.0 + ln * (1.0 - sig)))
        dxh = dln * lg_ref[...]
        dcv = rstd * (dxh - jnp.mean(dxh, axis=-1, keepdims=True) - xh * jnp.mean(dxh * xh, axis=-1, keepdims=True))
        pos = i * bs - HC + lax.broadcasted_iota(jnp.int32, (bc, 1), 0)
        dcv = jnp.where((pos >= 0) & (pos < S), dcv, 0.0)
        dcs_ref[...] = dcv

        @pl.when(i == 0)
        def _():
            for r in (dcw_ref, dcb_ref, dlg_ref, dlb_ref, dgm_ref):
                r[...] = jnp.zeros_like(r)

        mid = slice(HC, HC + bs)
        dcb_ref[...] += jnp.sum(dcv[mid], axis=0, keepdims=True)
        dlg_ref[...] += jnp.sum((dln * xh)[mid], axis=0, keepdims=True)
        dlb_ref[...] += jnp.sum(dln[mid], axis=0, keepdims=True)
        dgm_ref[...] += jnp.sum((dyv * n)[mid], axis=0, keepdims=True)
        dh = jnp.zeros((bs, C_A), F32)
        dcm = dcv[mid]
        for t in range(CONV_WIDTH):
            dh = dh + cw_ref[t:t + 1, :] * dcs_ref[pl.ds(HC + CONV_HALF - t, bs), :]
            dcw_ref[t:t + 1, :] += jnp.sum(dcm * win_ref[pl.ds(H - CONV_HALF + t, bs), :], axis=0, keepdims=True)
        a = ac[...].astype(F32)
        sg = jax.nn.sigmoid(gc[...].astype(F32))
        dproj_ref[:, :C_A] = (dh * sg).astype(dproj_ref.dtype)
        dproj_ref[:, C_A:] = (dh * a * sg * (1.0 - sg)).astype(dproj_ref.dtype)

    a_specs = _halo_specs(bs, H, C_A, 0, S)
    g_specs = _halo_specs(bs, H, C_A, 1, S)
    d_specs = _halo_specs(bs, HC, C_A, 0, S)
    vec = _vec_spec(C_A)
    full = pl.BlockSpec((32, C_A), lambda i: (0, 0))
    vshape = jax.ShapeDtypeStruct((1, C_A), F32)
    return pl.pallas_call(
        body, name=name, grid=(nb,),
        in_specs=[*a_specs, *g_specs, *d_specs, full, vec, vec, vec, vec],
        out_specs=[_row_spec(bs, 2 * C_A), full, vec, vec, vec, vec],
        out_shape=[jax.ShapeDtypeStruct((S, 2 * C_A), BF16), jax.ShapeDtypeStruct((32, C_A), F32),
                   vshape, vshape, vshape, vshape],
        scratch_shapes=[pltpu.VMEM((bs + 2 * H, C_A), F32), pltpu.VMEM((bc, C_A), F32)],
        compiler_params=_params("arbitrary"),
    )(proj, proj, proj, proj, proj, proj, dy, dy, dy, cw, cb, lg, lb, gm)


POOL_HALO = 16


def _shift(x, k):
    n = x.shape[0]
    return pltpu.roll(x, (-k) % n, axis=0)


def _pool_means(u, pos, S):
    w2 = _shift(u, -1) + u
    w4 = _shift(w2, -1) + _shift(w2, 1)
    w8 = _shift(w4, -2) + _shift(w4, 2)
    w16 = _shift(w8, -4) + _shift(w8, 4)
    sums = (w2, w4, w8, w16)
    lane = lax.broadcasted_iota(jnp.int32, (1, C_C), 1)
    total = jnp.zeros_like(u)
    inv = jnp.zeros_like(u)
    for gi, win in enumerate(POOL_WINDOWS):
        cnt = jnp.minimum(pos + (win - win // 2), S) - jnp.maximum(pos - win // 2, 0)
        icnt = 1.0 / jnp.maximum(cnt, 1).astype(F32)
        sel = (lane >= gi * C_G) & (lane < (gi + 1) * C_G)
        total = jnp.where(sel, sums[gi], total)
        inv = jnp.where(sel, icnt, inv)
    return total * inv - u, inv


def _pool_adjoint(e):
    v2 = e + _shift(e, 1)
    v4 = _shift(v2, -1) + _shift(v2, 1)
    v8 = _shift(v4, -2) + _shift(v4, 2)
    v16 = _shift(v8, -4) + _shift(v8, 4)
    sums = (v2, v4, v8, v16)
    lane = lax.broadcasted_iota(jnp.int32, (1, C_C), 1)
    out = jnp.zeros_like(e)
    for gi in range(len(POOL_WINDOWS)):
        sel = (lane >= gi * C_G) & (lane < (gi + 1) * C_G)
        out = jnp.where(sel, sums[gi], out)
    return out


def _pool_window(up, uc, un, i, nb, bs, S):
    H = POOL_HALO
    u = jnp.concatenate([jnp.where(i > 0, up[...].astype(F32), 0.0), uc[...].astype(F32),
                         jnp.where(i < nb - 1, un[...].astype(F32), 0.0)], axis=0)
    pos = i * bs - H + lax.broadcasted_iota(jnp.int32, (bs + 2 * H, 1), 0)
    return u, pos


def _pool_mix(pooled, pw_ref):
    outs = []
    for gi in range(len(POOL_WINDOWS)):
        outs.append(jnp.dot(pooled[:, gi * C_G:(gi + 1) * C_G].astype(BF16), pw_ref[gi].astype(BF16),
                            preferred_element_type=F32))
    return jnp.concatenate(outs, axis=1)


def pool_fwd(proj, pw, ps, gm, name):
    S, width = proj.shape
    col = width // C_C - 1
    bs = _rows(S, SEQ_TILE)
    nb = S // bs
    H = POOL_HALO

    def body(up, uc, un, pw_ref, ps_ref, gm_ref, y_ref):
        i = pl.program_id(0)
        u, pos = _pool_window(up, uc, un, i, nb, bs, S)
        pooled, _ = _pool_means(u, pos, S)
        mixed = _pool_mix(pooled[H:H + bs], pw_ref)
        y_ref[...] = (_rms(mixed * ps_ref[...]) * gm_ref[...]).astype(y_ref.dtype)

    vec = _vec_spec(C_C)
    return pl.pallas_call(
        body, name=name, grid=(nb,),
        in_specs=[*_halo_specs(bs, H, C_C, col, S), pl.BlockSpec((4, C_G, C_G), lambda i: (0, 0, 0)), vec, vec],
        out_specs=_row_spec(bs, C_C), out_shape=jax.ShapeDtypeStruct((S, C_C), BF16),
        compiler_params=_params("parallel"),
    )(proj, proj, proj, pw, ps, gm)


def pool_bwd(proj, dy, pw, ps, gm, name):
    S, width = proj.shape
    col = width // C_C - 1
    dcol = dy.shape[1] // C_C - 1
    bs = _rows(S, SEQ_TILE)
    nb = S // bs
    H = POOL_HALO
    W = bs + 2 * H

    def body(up, uc, un, dp, dc_, dn, pw_ref, ps_ref, gm_ref, du_ref, dpw_ref, dps_ref, dgm_ref):
        i = pl.program_id(0)
        u, pos = _pool_window(up, uc, un, i, nb, bs, S)
        pooled, inv = _pool_means(u, pos, S)
        mixed = _pool_mix(pooled, pw_ref)
        dyv = jnp.concatenate([dp[...], dc_[...], dn[...]], axis=0).astype(F32)
        dyc, n = _rms_bwd(mixed * ps_ref[...], dyv * gm_ref[...])
        dmixed = dyc * ps_ref[...]
        dmb = dmixed.astype(BF16)
        dpooled = jnp.concatenate(
            [lax.dot_general(dmb[:, gi * C_G:(gi + 1) * C_G], pw_ref[gi].astype(BF16), (((1,), (1,)), ((), ())),
                             preferred_element_type=F32) for gi in range(len(POOL_WINDOWS))], axis=1)
        dpooled = jnp.where((pos >= 0) & (pos < S), dpooled, 0.0)
        du = _pool_adjoint(dpooled * inv) - dpooled
        du_ref[...] = du[H:H + bs].astype(du_ref.dtype)

        @pl.when(i == 0)
        def _():
            for r in (dpw_ref, dps_ref, dgm_ref):
                r[...] = jnp.zeros_like(r)

        mid = slice(H, H + bs)
        dps_ref[...] += jnp.sum((dyc * mixed)[mid], axis=0, keepdims=True)
        dgm_ref[...] += jnp.sum((dyv * n)[mid], axis=0, keepdims=True)
        pb = pooled[mid].astype(BF16)
        for gi in range(len(POOL_WINDOWS)):
            sl = slice(gi * C_G, (gi + 1) * C_G)
            dpw_ref[gi] += lax.dot_general(pb[:, sl], dmb[mid][:, sl], (((0,), (0,)), ((), ())),
                                           preferred_element_type=F32)

    vec = _vec_spec(C_C)
    full = pl.BlockSpec((4, C_G, C_G), lambda i: (0, 0, 0))
    vshape = jax.ShapeDtypeStruct((1, C_C), F32)
    return pl.pallas_call(
        body, name=name, grid=(nb,),
        in_specs=[*_halo_specs(bs, H, C_C, col, S), *_halo_specs(bs, H, C_C, dcol, S), full, vec, vec],
        out_specs=[_row_spec(bs, C_C), full, vec, vec],
        out_shape=[jax.ShapeDtypeStruct((S, C_C), BF16), jax.ShapeDtypeStruct((4, C_G, C_G), F32), vshape, vshape],
        compiler_params=_params("arbitrary"),
    )(proj, proj, proj, dy, dy, dy, pw, ps, gm)


def rope_tables(S):
    pos = jnp.arange(S, dtype=F32)
    inv = ROPE_THETA ** (-jnp.arange(0, ROT_DIM, 2, dtype=F32) / ROT_DIM)
    ang = pos[:, None] * inv[None, :]
    half = ROT_DIM // 2
    cos, sin = jnp.cos(ang), jnp.sin(ang)
    zeros = jnp.zeros((S, half), F32)
    rest = jnp.zeros((S, HEAD_DIM - ROT_DIM), F32)
    per_head = (jnp.concatenate([cos, cos, rest + 1.0], axis=1), jnp.concatenate([-sin, zeros, rest], axis=1),
                jnp.concatenate([zeros, sin, rest], axis=1))
    return tuple(jnp.tile(t, (1, LANES // HEAD_DIM)) for t in per_head)


def _rotate(t, c, s1, s2, sign):
    half = ROT_DIM // 2
    return t * c + sign * (pltpu.roll(t, LANES - half, axis=1) * s1 + pltpu.roll(t, half, axis=1) * s2)


def rope_fwd(proj, tables, name):
    S = proj.shape[0]
    tr = _rows(S, ROW_TILE)
    qcol = 2 * C_A // C_B
    nd = len(DILATIONS)

    def body(q_ref, k_ref, v_ref, c_ref, s1_ref, s2_ref, *outs):
        c, s1, s2 = c_ref[...], s1_ref[...], s2_ref[...]
        qs, ks = [], []
        for p in range(C_B // LANES):
            sl = slice(p * LANES, (p + 1) * LANES)
            qs.append((_rotate(q_ref[:, sl].astype(F32), c, s1, s2, 1.0) * HEAD_DIM ** -0.5).astype(BF16))
            ks.append(_rotate(k_ref[:, sl].astype(F32), c, s1, s2, 1.0).astype(BF16))
        tensors = (jnp.concatenate(qs, axis=1), jnp.concatenate(ks, axis=1), v_ref[...])
        for n, d in enumerate(DILATIONS):
            for t, x in enumerate(tensors):
                _store_dilated(outs[3 * n + t], x, d)

    tab = _row_spec(tr, LANES)
    res = pl.pallas_call(
        body, name=name, grid=(S // tr,),
        in_specs=[pl.BlockSpec((tr, C_B), lambda i, col=qcol + n: (i, col)) for n in range(3)] + [tab, tab, tab],
        out_specs=[_dilated_spec(tr, d, C_B) for d in DILATIONS for _ in range(3)],
        out_shape=[jax.ShapeDtypeStruct((d, S // d, C_B), BF16) for d in DILATIONS for _ in range(3)],
        compiler_params=_params("parallel"),
    )(proj, proj, proj, *tables)
    return [tuple(res[3 * n:3 * n + 3]) for n in range(nd)]


def rope_bwd(dqs, dks, dvs, tables, dconv, du, name):
    S = dconv.shape[0]
    tr = _rows(S, ROW_TILE)
    n = len(dqs)
    base = 2 * C_A
    width = base + 3 * C_B + C_C

    def body(*refs):
        dq_refs, dk_refs, dv_refs = refs[:n], refs[n:2 * n], refs[2 * n:3 * n]
        c_ref, s1_ref, s2_ref, dconv_ref, du_ref, o_ref = refs[3 * n:]
        c, s1, s2 = c_ref[...], s1_ref[...], s2_ref[...]
        o_ref[:, :base] = dconv_ref[...]
        o_ref[:, base + 3 * C_B:] = du_ref[...]
        dq_all = sum(_load_dilated(r, d).astype(F32) for r, d in zip(dq_refs, DILATIONS))
        dk_all = sum(_load_dilated(r, d).astype(F32) for r, d in zip(dk_refs, DILATIONS))
        dv_all = sum(_load_dilated(r, d).astype(F32) for r, d in zip(dv_refs, DILATIONS))
        for p in range(C_B // LANES):
            sl = slice(p * LANES, (p + 1) * LANES)
            dq, dk, dv = dq_all[:, sl], dk_all[:, sl], dv_all[:, sl]
            at = base + p * LANES
            o_ref[:, at:at + LANES] = (_rotate(dq, c, s1, s2, -1.0) * HEAD_DIM ** -0.5).astype(BF16)
            o_ref[:, C_B + at:C_B + at + LANES] = _rotate(dk, c, s1, s2, -1.0).astype(BF16)
            o_ref[:, 2 * C_B + at:2 * C_B + at + LANES] = dv.astype(BF16)

    tab = _row_spec(tr, LANES)
    return pl.pallas_call(
        body, name=name, grid=(S // tr,),
        in_specs=[_dilated_spec(tr, d, C_B) for _ in range(3) for d in DILATIONS]
        + [tab, tab, tab, _row_spec(tr, base), _row_spec(tr, C_C)],
        out_specs=_row_spec(tr, width), out_shape=jax.ShapeDtypeStruct((S, width), BF16),
        compiler_params=_params("parallel"),
    )(*dqs, *dks, *dvs, *tables, dconv, du)


def _attn_specs(bq, width, L):
    per = bq // ATTN_HALF
    last = L // ATTN_HALF - 1
    cur = pl.BlockSpec((None, bq, width), lambda r, j: (r, j, 0))
    prev = pl.BlockSpec((None, ATTN_HALF, width), lambda r, j: (r, jnp.maximum(j * per - 1, 0), 0))
    nxt = pl.BlockSpec((None, ATTN_HALF, width), lambda r, j: (r, jnp.minimum((j + 1) * per, last), 0))
    return prev, cur, nxt


def _window(refs, sl):
    return jnp.concatenate([r[:, sl] for r in refs], axis=0)


def _band_mask(j, bq, L, rows_are_window):
    bw = bq + 2 * ATTN_HALF
    if rows_are_window:
        rp = j * bq - ATTN_HALF + lax.broadcasted_iota(jnp.int32, (bw, 1), 0)
        cp = j * bq + lax.broadcasted_iota(jnp.int32, (1, bq), 1)
        return (jnp.abs(rp - cp) <= ATTN_HALF) & (rp >= 0) & (rp < L)
    rp = j * bq + lax.broadcasted_iota(jnp.int32, (bq, 1), 0)
    cp = j * bq - ATTN_HALF + lax.broadcasted_iota(jnp.int32, (1, bw), 1)
    return (jnp.abs(rp - cp) <= ATTN_HALF) & (cp >= 0) & (cp < L)


def _head_col(stats, h):
    lane = lax.broadcasted_iota(jnp.int32, (1, LANES), 1)
    return jnp.sum(jnp.where(lane == h, stats, 0.0), axis=1, keepdims=True)


_NT = (((1,), (1,)), ((), ()))
_TN = (((0,), (0,)), ((), ()))


def attn_fwd_pattern(qd, kd, vd, name):
    d, L, _ = qd.shape
    bq = _rows(L, ATTN_BLOCK)

    def body(q_ref, kp, kc, kn, vp, vc, vn, o_ref, lse_ref):
        j = pl.program_id(1)
        mask = _band_mask(j, bq, L, False)
        lane = lax.broadcasted_iota(jnp.int32, (1, LANES), 1)
        first = lane < HEAD_DIM
        lse = jnp.zeros((bq, LANES), F32)
        for p in range(C_B // LANES):
            sl = slice(p * LANES, (p + 1) * LANES)
            q = q_ref[:, sl]
            kw = _window((kp, kc, kn), sl)
            vw = _window((vp, vc, vn), sl)
            outs = []
            for hh in range(2):
                qh = jnp.where(first == (hh == 0), q, jnp.zeros_like(q))
                s = jnp.where(mask, lax.dot_general(qh, kw, _NT, preferred_element_type=F32), NEG)
                m = jnp.max(s, axis=1, keepdims=True)
                e = jnp.exp(s - m)
                l = jnp.sum(e, axis=1, keepdims=True)
                outs.append(jnp.dot(e.astype(BF16), vw, preferred_element_type=F32) * (1.0 / l))
                lse = jnp.where(lane == 2 * p + hh, m + jnp.log(l), lse)
            o_ref[:, sl] = jnp.where(first, outs[0], outs[1]).astype(o_ref.dtype)
        lse_ref[...] = lse

    kv = _attn_specs(bq, C_B, L)
    return pl.pallas_call(
        body, name=name, grid=(d, L // bq), in_specs=[kv[1], *kv, *kv],
        out_specs=[kv[1], pl.BlockSpec((None, bq, LANES), lambda r, j: (r, j, 0))],
        out_shape=[jax.ShapeDtypeStruct((d, L, C_B), BF16), jax.ShapeDtypeStruct((d, L, LANES), F32)],
        compiler_params=_params("parallel", "parallel"),
    )(qd, kd, kd, kd, vd, vd, vd)


def attn_combine(os_, lses, gm, ya, yc, name):
    S = ya.shape[0]
    tr = _rows(S, ROW_TILE)
    n = len(os_)

    def body(*refs):
        o_refs, l_refs = refs[:n], refs[n:2 * n]
        gm_ref, ya_ref, yc_ref, y_ref, out_ref, lse_ref = refs[2 * n:]
        ls = [_load_dilated(r, d) for r, d in zip(l_refs, DILATIONS)]
        os_tok = [_load_dilated(r, d) for r, d in zip(o_refs, DILATIONS)]
        mx = functools.reduce(jnp.maximum, ls)
        ws = [jnp.exp(l - mx) for l in ls]
        den = sum(ws)
        lse_ref[...] = mx + jnp.log(den)
        wn = [w / den for w in ws]
        lane = lax.broadcasted_iota(jnp.int32, (1, LANES), 1)
        first = lane < HEAD_DIM
        blocks = []
        for p in range(C_B // LANES):
            sl = slice(p * LANES, (p + 1) * LANES)
            acc = jnp.zeros((tr, LANES), F32)
            for w, o in zip(wn, os_tok):
                acc = acc + jnp.where(first, _head_col(w, 2 * p), _head_col(w, 2 * p + 1)) * o[:, sl].astype(F32)
            blocks.append(acc)
        out = jnp.concatenate(blocks, axis=1)
        out_ref[...] = out.astype(out_ref.dtype)
        y_ref[:, :C_A] = ya_ref[...]
        y_ref[:, C_A:C_A + C_B] = (_rms(out) * gm_ref[...]).astype(y_ref.dtype)
        y_ref[:, C_A + C_B:] = yc_ref[...]

    st = _row_spec(tr, LANES)
    mix = C_A + C_B + C_C
    return pl.pallas_call(
        body, name=name, grid=(S // tr,),
        in_specs=[_dilated_spec(tr, d, C_B) for d in DILATIONS] + [_dilated_spec(tr, d, LANES) for d in DILATIONS]
        + [_vec_spec(C_B), _row_spec(tr, C_A), _row_spec(tr, C_C)],
        out_specs=[_row_spec(tr, mix), _row_spec(tr, C_B), st],
        out_shape=[jax.ShapeDtypeStruct((S, mix), BF16), jax.ShapeDtypeStruct((S, C_B), BF16),
                   jax.ShapeDtypeStruct((S, LANES), F32)],
        compiler_params=_params("parallel"),
    )(*os_, *lses, gm, ya, yc)


def attn_out_bwd(out, lse, dy, gm, name):
    S = out.shape[0]
    tr = _rows(S, ROW_TILE)
    nd = len(DILATIONS)

    def body(o_ref, lse_ref, dy1, dy2, g_ref, *outs):
        dg_ref = outs[-1]
        o = o_ref[...].astype(F32)
        dyv = jnp.concatenate([dy1[...], dy2[...]], axis=1).astype(F32)
        do, n = _rms_bwd(o, dyv * g_ref[...])
        dob = do.astype(BF16)
        prod = dob.astype(F32) * o
        lane = lax.broadcasted_iota(jnp.int32, (1, LANES), 1)
        first = lane < HEAD_DIM
        delta = jnp.zeros((tr, LANES), F32)
        for p in range(C_B // LANES):
            blk = prod[:, p * LANES:(p + 1) * LANES]
            delta = jnp.where(lane == 2 * p, jnp.sum(jnp.where(first, blk, 0.0), axis=1, keepdims=True), delta)
            delta = jnp.where(lane == 2 * p + 1, jnp.sum(jnp.where(first, 0.0, blk), axis=1, keepdims=True), delta)
        lse_v = lse_ref[...]
        for k, d in enumerate(DILATIONS):
            _store_dilated(outs[3 * k], dob, d)
            _store_dilated(outs[3 * k + 1], lse_v, d)
            _store_dilated(outs[3 * k + 2], delta, d)

        @pl.when(pl.program_id(0) == 0)
        def _():
            dg_ref[...] = jnp.zeros_like(dg_ref)

        dg_ref[...] += jnp.sum(dyv * n, axis=0, keepdims=True)

    widths = (C_B, LANES, LANES)
    dtypes = (BF16, F32, F32)
    res = pl.pallas_call(
        body, name=name, grid=(S // tr,),
        in_specs=[_row_spec(tr, C_B), _row_spec(tr, LANES), pl.BlockSpec((tr, C_A), lambda i: (i, 1)),
                  pl.BlockSpec((tr, C_A), lambda i: (i, 2)), _vec_spec(C_B)],
        out_specs=[_dilated_spec(tr, d, wd) for d in DILATIONS for wd in widths] + [_vec_spec(C_B)],
        out_shape=[jax.ShapeDtypeStruct((d, S // d, wd), dt) for d in DILATIONS for wd, dt in zip(widths, dtypes)]
        + [jax.ShapeDtypeStruct((1, C_B), F32)],
        compiler_params=_params("arbitrary"),
    )(out, lse, dy, dy, gm)
    return [tuple(res[3 * k:3 * k + 3]) for k in range(nd)], res[-1]


def attn_bwd_pattern(qd, kd, vd, dod, lsed, deltad, name):
    d, L, _ = qd.shape
    bq = _rows(L, ATTN_BLOCK)

    nb = L // bq
    bw = bq + 2 * ATTN_HALF
    lo = bq - ATTN_HALF

    def body(qc, dc_, lc, tc, kp, kc, kn, vp, vc, vn, dq_ref, dk_ref, dv_ref, dk_acc, dv_acc):
        j = pl.program_id(1)

        @pl.when(j == 0)
        def _():
            dk_acc[...] = jnp.zeros_like(dk_acc)
            dv_acc[...] = jnp.zeros_like(dv_acc)

        @pl.when(j > 0)
        def _():
            for acc in (dk_acc, dv_acc):
                acc[0:bq, :] = acc[bq:2 * bq, :]
                acc[bq:2 * bq, :] = acc[2 * bq:, :]
                acc[2 * bq:, :] = jnp.zeros((bq, C_B), F32)

        @pl.when(j < nb)
        def _():
            mask = _band_mask(j, bq, L, False)
            lane = lax.broadcasted_iota(jnp.int32, (1, LANES), 1)
            first = lane < HEAD_DIM
            lse_c, delta_c = lc[...], tc[...]
            for p in range(C_B // LANES):
                sl = slice(p * LANES, (p + 1) * LANES)
                q, do = qc[:, sl], dc_[:, sl]
                kw, vw = _window((kp, kc, kn), sl), _window((vp, vc, vn), sl)
                dqs, dks, dvs = [], [], []
                for hh in range(2):
                    h = 2 * p + hh
                    sel = first == (hh == 0)
                    zero = jnp.zeros_like(q)
                    s = lax.dot_general(jnp.where(sel, q, zero), kw, _NT, preferred_element_type=F32)
                    pr = jnp.where(mask, jnp.exp(s - _head_col(lse_c, h)), 0.0)
                    dpr = lax.dot_general(jnp.where(sel, do, zero), vw, _NT, preferred_element_type=F32)
                    ds = (pr * (dpr - _head_col(delta_c, h))).astype(BF16)
                    dqs.append(jnp.dot(ds, kw, preferred_element_type=F32))
                    dks.append(lax.dot_general(ds, q, _TN, preferred_element_type=F32))
                    dvs.append(lax.dot_general(pr.astype(BF16), do, _TN, preferred_element_type=F32))
                dq_ref[:, sl] = jnp.where(first, dqs[0], dqs[1]).astype(dq_ref.dtype)
                dk_acc[lo:lo + bw, sl] += jnp.where(first, dks[0], dks[1])
                dv_acc[lo:lo + bw, sl] += jnp.where(first, dvs[0], dvs[1])

        dk_ref[...] = dk_acc[0:bq, :].astype(dk_ref.dtype)
        dv_ref[...] = dv_acc[0:bq, :].astype(dv_ref.dtype)

    per = bq // ATTN_HALF
    last = L // ATTN_HALF - 1

    def clamp(j):
        return jnp.minimum(j, nb - 1)

    def specs(width):
        cur = pl.BlockSpec((None, bq, width), lambda r, j: (r, clamp(j), 0))
        prev = pl.BlockSpec((None, ATTN_HALF, width), lambda r, j: (r, jnp.maximum(clamp(j) * per - 1, 0), 0))
        nxt = pl.BlockSpec((None, ATTN_HALF, width), lambda r, j: (r, jnp.minimum((clamp(j) + 1) * per, last), 0))
        return prev, cur, nxt

    wide = specs(C_B)
    stat = specs(LANES)[1]
    lagged = pl.BlockSpec((None, bq, C_B), lambda r, j: (r, jnp.maximum(j - 1, 0), 0))
    shape = jax.ShapeDtypeStruct((d, L, C_B), BF16)
    return pl.pallas_call(
        body, name=name, grid=(d, nb + 1), in_specs=[wide[1], wide[1], stat, stat, *wide, *wide],
        out_specs=[wide[1], lagged, lagged], out_shape=[shape] * 3,
        scratch_shapes=[pltpu.VMEM((3 * bq, C_B), F32), pltpu.VMEM((3 * bq, C_B), F32)],
        compiler_params=_params("arbitrary", "arbitrary"),
    )(qd, dod, lsed, deltad, kd, kd, kd, vd, vd, vd)


def _dilated_spec(tr, d, width):
    return pl.BlockSpec((d, tr // d, width), lambda i: (0, i, 0))


def _perm_matrix(n, d, inverse):
    lb = n // d
    row = lax.broadcasted_iota(jnp.int32, (n, n), 0)
    col = lax.broadcasted_iota(jnp.int32, (n, n), 1)
    source = (row % d) * lb + row // d if inverse else (row % lb) * d + row // lb
    return (col == source).astype(BF16)


def _permute(p, x):
    if x.dtype == BF16:
        return jnp.dot(p, x, preferred_element_type=F32).astype(BF16)
    hi = x.astype(BF16)
    rest = x - hi.astype(F32)
    mid = rest.astype(BF16)
    lo = (rest - mid.astype(F32)).astype(BF16)
    return (jnp.dot(p, hi, preferred_element_type=F32) + jnp.dot(p, mid, preferred_element_type=F32)
            + jnp.dot(p, lo, preferred_element_type=F32))


def _store_dilated(ref, x, d):
    n = x.shape[0]
    if d == 1:
        ref[0] = x
        return
    y = _permute(_perm_matrix(n, d, False), x)
    lb = n // d
    for r in range(d):
        ref[r] = y[r * lb:(r + 1) * lb]


def _load_dilated(ref, d):
    if d == 1:
        return ref[0]
    y = jnp.concatenate([ref[r] for r in range(d)], axis=0)
    return _permute(_perm_matrix(y.shape[0], d, True), y)


def adamw(parts, w, m, v, name):
    n, R, C = parts.shape
    tr = _rows(R, ROW_TILE)

    def body(p_ref, w_ref, m_ref, v_ref, g_ref, d_ref, nm_ref, nv_ref):
        g = p_ref[0].astype(F32)
        for k in range(1, n):
            g = g + p_ref[k].astype(F32)
        mm = ADAM_B1 * m_ref[...] + (1.0 - ADAM_B1) * g
        vv = ADAM_B2 * v_ref[...] + (1.0 - ADAM_B2) * jnp.square(g)
        m_hat = mm / (1.0 - ADAM_B1 ** ADAM_STEP)
        v_hat = vv / (1.0 - ADAM_B2 ** ADAM_STEP)
        g_ref[...] = g
        d_ref[...] = -ADAM_LR * (m_hat / (jnp.sqrt(v_hat) + ADAM_EPS) + ADAM_WD * w_ref[...])
        nm_ref[...] = mm
        nv_ref[...] = vv

    spec = _row_spec(tr, C)
    shape = jax.ShapeDtypeStruct((R, C), F32)
    return pl.pallas_call(
        body, name=name, grid=(R // tr,),
        in_specs=[pl.BlockSpec((n, tr, C), lambda i: (0, i, 0)), spec, spec, spec],
        out_specs=[spec] * 4, out_shape=[shape] * 4, compiler_params=_params("parallel"),
    )(parts, w, m, v)


_ANY = pl.BlockSpec(memory_space=pl.ANY)


def _place():
    return lax.axis_index("x"), lax.axis_index("y"), lax.axis_index("c")


def _index(px, py, pc):
    return 4 * px + 2 * py + pc


class Gather:
    def __init__(self, shards):
        self.ins = list(shards)
        T = self.n = len(shards)
        self.out_shape = [jax.ShapeDtypeStruct((N_DEV, *s.shape), s.dtype) for s in shards]
        self.sems = [pltpu.SemaphoreType.DMA((T, 7)), pltpu.SemaphoreType.DMA((T, 7)), pltpu.SemaphoreType.DMA((T,))]

    def _plan(self, ins, outs, sems):
        send_sems, recv_sems, local_sems = sems
        x, y, c = _place()
        me, sibling = (x, y, c), (x, y, 1 - c)
        chips = [(1 - x, y), (x, 1 - y), (1 - x, 1 - y)]

        def copy(t, k, block, to, src=None):
            rows = outs[t].at[_index(*block)]
            return pltpu.make_async_remote_copy(
                src_ref=rows if src is None else src, dst_ref=rows, send_sem=send_sems.at[t, k],
                recv_sem=recv_sems.at[t, k], device_id=to, device_id_type=MESH)

        mine = [pltpu.make_async_copy(ins[t], outs[t].at[_index(*me)], local_sems.at[t]) for t in range(self.n)]
        first = []
        for t in range(self.n):
            first.append(copy(t, 0, me, sibling, src=ins[t]))
            first += [copy(t, 1 + j, me, (*chip, c), src=ins[t]) for j, chip in enumerate(chips)]
        return copy, mine, first, me, sibling, chips, c

    def start(self, ins, outs, sems):
        _, mine, first, *_ = self._plan(ins, outs, sems)
        for cp in mine + first:
            cp.start()

    def finish(self, ins, outs, sems):
        copy, mine, first, me, sibling, chips, c = self._plan(ins, outs, sems)
        passed = []
        for j, chip in enumerate(chips):
            for t in range(self.n):
                copy(t, 1 + j, (*chip, c), me).wait_recv()
                fwd = copy(t, 4 + j, (*chip, c), sibling)
                fwd.start()
                passed.append(fwd)
        for t in range(self.n):
            copy(t, 0, sibling, me).wait_recv()
            for j, chip in enumerate(chips):
                copy(t, 4 + j, (*chip, 1 - c), me).wait_recv()
        for cp in first + passed:
            cp.wait_send()
        for cp in mine:
            cp.wait()


class Exchange:
    def __init__(self, parts):
        self.ins = list(parts)
        T = self.n = len(parts)
        self.out_shape = [jax.ShapeDtypeStruct(p.shape, p.dtype) for p in parts]
        self.sems = [pltpu.SemaphoreType.DMA((T, 7)), pltpu.SemaphoreType.DMA((T, 7)), pltpu.SemaphoreType.DMA((T,))]

    def _plan(self, ins, outs, sems):
        send_sems, recv_sems, local_sems = sems
        x, y, c = _place()
        me = _index(x, y, c)
        copies = [pltpu.make_async_copy(ins[t].at[me], outs[t].at[me], local_sems.at[t]) for t in range(self.n)]
        for k in range(1, N_DEV):
            peer = ((x + (k >> 2)) % 2, (y + ((k >> 1) & 1)) % 2, (c + (k & 1)) % 2)
            there = _index(*peer)
            for t in range(self.n):
                copies.append(pltpu.make_async_remote_copy(
                    src_ref=ins[t].at[there], dst_ref=outs[t].at[me], send_sem=send_sems.at[t, k - 1],
                    recv_sem=recv_sems.at[t, k - 1], device_id=peer, device_id_type=MESH))
        return copies

    def start(self, ins, outs, sems):
        for cp in self._plan(ins, outs, sems):
            cp.start()

    def finish(self, ins, outs, sems):
        for cp in self._plan(ins, outs, sems):
            cp.wait()


def communicate(comm, name):
    T = comm.n

    def body(*refs):
        ins, outs, sems = refs[:T], refs[T:2 * T], refs[2 * T:]
        comm.start(ins, outs, sems)
        comm.finish(ins, outs, sems)

    return pl.pallas_call(
        body, name=name, in_specs=[_ANY] * T, out_specs=[_ANY] * T, out_shape=comm.out_shape,
        scratch_shapes=comm.sems,
    )(*comm.ins)


def all_gather(shards, name):
    return communicate(Gather(shards), name)


def exchange(parts, name):
    return communicate(Exchange(parts), name)


def _row(v):
    return v.reshape(1, -1)


def _hosted(res):
    return res if isinstance(res, tuple) else (res, None)


def mix_forward(x, w, p, tables, comm=None):
    gm = p["g_mix"]
    h1, h1t = rms_fwd(x, _row(p["g_pre_mix"]), "rms_pre_mix")
    proj, got = _hosted(matmul(h1, w["win"], "nn", BF16, "proj", comm=comm))
    ya = conv_fwd(proj, p["cw"], _row(p["conv_b"]), _row(p["conv_ln_g"]), _row(p["conv_ln_b"]), _row(gm[:C_A]),
                  "conv_fwd")
    dil = rope_fwd(proj, tables, "rope_fwd")
    os_, lses = [], []
    for d, (qd, kd, vd) in zip(DILATIONS, dil):
        o, lse = attn_fwd_pattern(qd, kd, vd, f"attn_fwd_d{d}")
        os_.append(o)
        lses.append(lse)
    yc = pool_fwd(proj, p["pool_w"], _row(p["pool_scale"]), _row(gm[C_A + C_B:]), "pool_fwd")
    y, out, lse = attn_combine(os_, lses, _row(gm[C_A:C_A + C_B]), ya, yc, "attn_combine")
    z = matmul(y, w["wout"], "nn", BF16, "mix_out")
    x2 = norm_residual(z, x, _row(p["g_post_mix"]), "res_mix")
    return x2, dict(x=x, h1t=h1t, proj=proj, dil=dil, out=out, lse=lse, y=y, z=z), got


def ffn_forward(x2, w, p, comm_in=None, comm_out=None):
    h2, h2t = rms_fwd(x2, _row(p["g_pre_ffn"]), "rms_pre_ffn")
    gu, got_in = _hosted(matmul(h2, w["wgu"], "nn", BF16, "ffn_in", comm=comm_in))
    a, at = swiglu_fwd(gu, "swiglu_fwd")
    f, got_out = _hosted(matmul(a, w["wd"], "nn", BF16, "ffn_out", tk=2816, comm=comm_out))
    x3 = norm_residual(f, x2, _row(p["g_post_ffn"]), "res_ffn")
    return x3, dict(x2=x2, h2t=h2t, gu=gu, at=at, f=f), got_in, got_out


def _to_blocks(g, by_columns):
    if by_columns:
        rows = g.shape[0]
        return g.reshape(rows, N_DEV, -1).transpose(1, 0, 2).astype(BF16)
    return g.reshape(N_DEV, -1, g.shape[1]).astype(BF16)


def _from_blocks(b, by_columns):
    if by_columns:
        return b.transpose(1, 0, 2).reshape(b.shape[1], -1)
    return b.reshape(-1, b.shape[2])


def _exchange_of(g, by_columns):
    return Exchange([_to_blocks(g, by_columns)]) if g is not None else None


def ffn_backward(dx3, s, w, p, ride_act=None, ride_down=None):
    df, dg_post_ffn = rms_bwd(s["f"], dx3, _row(p["g_post_ffn"]), None, BF16, "rms_bwd_post_ffn")
    da, got_act = _hosted(matmul(df, w["wd"], "nt", BF16, "d_ffn_act", comm=ride_act))
    dwd, got_down = _hosted(matmul(s["at"], df, "nn", F32, "dw_down", tm=1408, tn=1024, tk=1024, comm=ride_down))
    dgu = swiglu_bwd(s["gu"], da, "swiglu_bwd")
    dh2, got_wd = matmul(dgu, w["wgu"], "nt", BF16, "d_ffn_in", tk=2816, comm=_exchange_of(dwd, False))
    dwgu = matmul(s["h2t"], dgu, "nn", F32, "dw_gate_up", tm=1024, tn=1408, tk=1024)
    dx2, dg_pre_ffn = rms_bwd(s["x2"], dh2, _row(p["g_pre_ffn"]), dx3, F32, "rms_bwd_pre_ffn")
    return dx2, dict(dwgu=dwgu, parts_wd=got_wd[0], got_act=got_act, got_down=got_down,
                     g_pre_ffn=dg_pre_ffn[0], g_post_ffn=dg_post_ffn[0])


def mix_backward(dx2, s, w, p, tables, dwgu):
    gm = p["g_mix"]
    dz, dg_post_mix = rms_bwd(s["z"], dx2, _row(p["g_post_mix"]), None, BF16, "rms_bwd_post_mix")
    dy = matmul(dz, w["wout"], "nt", BF16, "d_mix")
    dwout = matmul(s["y"], dz, "tn", F32, "dw_out", tm=1024, tn=1024, tk=512)
    dconv, dcw, dcb, dlg, dlb, dgm_a = conv_bwd(
        s["proj"], dy, p["cw"], _row(p["conv_b"]), _row(p["conv_ln_g"]), _row(p["conv_ln_b"]), _row(gm[:C_A]),
        "conv_bwd")
    stats, dgm_b = attn_out_bwd(s["out"], s["lse"], dy, _row(gm[C_A:C_A + C_B]), "attn_out_bwd")
    dqs, dks, dvs = [], [], []
    for d, (qd, kd, vd), (dod, lsed, deltad) in zip(DILATIONS, s["dil"], stats):
        dq, dk, dv = attn_bwd_pattern(qd, kd, vd, dod, lsed, deltad, f"attn_bwd_d{d}")
        dqs.append(dq)
        dks.append(dk)
        dvs.append(dv)
    du, dpw, dps, dgm_c = pool_bwd(s["proj"], dy, p["pool_w"], _row(p["pool_scale"]), _row(gm[C_A + C_B:]),
                                   "pool_bwd")
    dproj = rope_bwd(dqs, dks, dvs, tables, dconv, du, "rope_bwd")
    F = dwgu.shape[1] // 2
    dh1, got_gate = matmul(dproj, w["win"], "nt", BF16, "d_proj", tk=2304, comm=_exchange_of(dwgu[:, :F], True))
    dwin, got_up = matmul(s["h1t"], dproj, "nn", F32, "dw_in", tm=1024, tn=1152, tk=1024,
                          comm=_exchange_of(dwgu[:, F:], True))
    dx, dg_pre_mix = rms_bwd(s["x"], dh1, _row(p["g_pre_mix"]), dx2, F32, "rms_bwd_pre_mix")
    return dx, dict(
        dwin=dwin, dwout=dwout, parts_gate=got_gate[0], parts_up=got_up[0],
        conv_w=dcw[:CONV_WIDTH], conv_b=dcb[0], conv_ln_g=dlg[0], conv_ln_b=dlb[0], pool_w=dpw, pool_scale=dps[0],
        g_mix=jnp.concatenate([dgm_a[0], dgm_b[0], dgm_c[0]]), g_pre_mix=dg_pre_mix[0], g_post_mix=dg_post_mix[0])


WEIGHTS = ["w_in", "conv_w", "conv_b", "conv_ln_g", "conv_ln_b", "pool_w", "pool_scale", "g_mix", "w_out", "g_pre_mix",
           "g_post_mix", "g_pre_ffn", "g_post_ffn", "w_gate", "w_up", "w_down"]
BIG = ["w_in", "w_out", "w_gate", "w_up", "w_down"]
REPLICATED = ["conv_b", "conv_ln_g", "conv_ln_b", "pool_w", "pool_scale", "g_mix", "g_pre_mix", "g_post_mix",
              "g_pre_ffn", "g_post_ffn"]
PACK_ROWS = 256


def adamw_layer(parts, w, m, v, layer, prev, name):
    n, R, C = parts.shape
    tr = _rows(R, ROW_TILE)

    def body(p_ref, w_ref, m_ref, v_ref, *rest):
        g_ref, d_ref, nm_ref, nv_ref = rest[-4:]
        g = p_ref[0].astype(F32)
        for k in range(1, n):
            g = g + p_ref[k].astype(F32)
        mm = ADAM_B1 * m_ref[...] + (1.0 - ADAM_B1) * g
        vv = ADAM_B2 * v_ref[...] + (1.0 - ADAM_B2) * jnp.square(g)
        m_hat = mm / (1.0 - ADAM_B1 ** ADAM_STEP)
        v_hat = vv / (1.0 - ADAM_B2 ** ADAM_STEP)
        g_ref[...] = g
        d_ref[...] = -ADAM_LR * (m_hat / (jnp.sqrt(v_hat) + ADAM_EPS) + ADAM_WD * w_ref[...])
        nm_ref[...] = mm
        nv_ref[...] = vv

    spec = pl.BlockSpec((None, tr, C), lambda i: (layer, i, 0))
    shape = jax.ShapeDtypeStruct(w.shape, F32)
    prev = list(prev) if prev is not None else []
    return pl.pallas_call(
        body, name=name, grid=(R // tr,),
        in_specs=[pl.BlockSpec((n, tr, C), lambda i: (0, i, 0)), spec, spec, spec] + [_ANY] * len(prev),
        out_specs=[spec] * 4, out_shape=[shape] * 4,
        input_output_aliases={4 + k: k for k in range(len(prev))}, compiler_params=_params("parallel"),
    )(parts, w, m, v, *prev)


def _pack(arrays):
    flat = jnp.concatenate([a.reshape(-1).astype(F32) for a in arrays])
    unit = PACK_ROWS * LANES
    padded = -(-flat.shape[0] // unit) * unit
    return jnp.pad(flat, (0, padded - flat.shape[0])).reshape(-1, LANES)


def _unpack(packed, like):
    flat = packed.reshape(-1)
    out, at = [], 0
    for a in like:
        out.append(flat[at:at + a.size].reshape(a.shape))
        at += a.size
    return out


def kernel(x, w_in, conv_w, conv_b, conv_ln_g, conv_ln_b, pool_w, pool_scale, g_mix, w_out, g_pre_mix, g_post_mix, g_pre_ffn, g_post_ffn, w_gate, w_up, w_down, loss_target, m_w_in, m_conv_w, m_conv_b, m_conv_ln_g, m_conv_ln_b, m_pool_w, m_pool_scale, m_g_mix, m_w_out, m_g_pre_mix, m_g_post_mix, m_g_pre_ffn, m_g_post_ffn, m_w_gate, m_w_up, m_w_down, v_w_in, v_conv_w, v_conv_b, v_conv_ln_g, v_conv_ln_b, v_pool_w, v_pool_scale, v_g_mix, v_w_out, v_g_pre_mix, v_g_post_mix, v_g_pre_ffn, v_g_post_ffn, v_w_gate, v_w_up, v_w_down):
    w = dict(w_in=w_in, conv_w=conv_w, conv_b=conv_b, conv_ln_g=conv_ln_g, conv_ln_b=conv_ln_b, pool_w=pool_w,
             pool_scale=pool_scale, g_mix=g_mix, w_out=w_out, g_pre_mix=g_pre_mix, g_post_mix=g_post_mix,
             g_pre_ffn=g_pre_ffn, g_post_ffn=g_post_ffn, w_gate=w_gate, w_up=w_up, w_down=w_down)
    m = dict(w_in=m_w_in, conv_w=m_conv_w, conv_b=m_conv_b, conv_ln_g=m_conv_ln_g, conv_ln_b=m_conv_ln_b,
             pool_w=m_pool_w, pool_scale=m_pool_scale, g_mix=m_g_mix, w_out=m_w_out, g_pre_mix=m_g_pre_mix,
             g_post_mix=m_g_post_mix, g_pre_ffn=m_g_pre_ffn, g_post_ffn=m_g_post_ffn, w_gate=m_w_gate, w_up=m_w_up,
             w_down=m_w_down)
    v = dict(w_in=v_w_in, conv_w=v_conv_w, conv_b=v_conv_b, conv_ln_g=v_conv_ln_g, conv_ln_b=v_conv_ln_b,
             pool_w=v_pool_w, pool_scale=v_pool_scale, g_mix=v_g_mix, w_out=v_w_out, g_pre_mix=v_g_pre_mix,
             g_post_mix=v_g_post_mix, g_pre_ffn=v_g_pre_ffn, g_post_ffn=v_g_post_ffn, w_gate=v_w_gate, w_up=v_w_up,
             w_down=v_w_down)
    depth = w_in.shape[0]
    xs, target = x[0], loss_target[0]
    S, D = xs.shape
    tables = rope_tables(S)

    cw_shard = jnp.pad(conv_w, ((0, 0), (0, 1), (0, 0)))
    cw_all = all_gather([cw_shard.reshape(-1, LANES)], "gather_conv_w")[0]
    cw_full = cw_all.reshape(N_DEV, depth, 32, -1).transpose(1, 2, 0, 3).reshape(depth, 32, C_A)
    small = []
    for l in range(depth):
        small.append({n: w[n][l] for n in REPLICATED})
        small[l]["cw"] = cw_full[l]

    def mix_shards(l):
        return Gather([w["w_in"][l].astype(BF16), w["w_out"][l].astype(BF16)])

    def ffn_shards(l):
        return Gather([w[n][l].astype(BF16) for n in ("w_gate", "w_up", "w_down")])

    def mix_weights(blocks):
        return dict(win=_from_blocks(blocks[0], True), wout=_from_blocks(blocks[1], False))

    def ffn_weights(blocks):
        return dict(wgu=jnp.concatenate([_from_blocks(blocks[0], True), _from_blocks(blocks[1], True)], axis=1),
                    wd=_from_blocks(blocks[2], False))

    w_mix = [None] * depth
    w_ffn = [None] * depth
    saved_mix, saved_ffn = [None] * depth, [None] * depth
    w_mix[0] = mix_weights(communicate(mix_shards(0), "gather_mix_weights"))
    h = xs
    for l in range(depth):
        x2, saved_mix[l], got = mix_forward(h, w_mix[l], small[l], tables, ffn_shards(0) if l == 0 else None)
        if l == 0:
            w_ffn[0] = ffn_weights(got)
        more = l + 1 < depth
        h, saved_ffn[l], got_in, got_out = ffn_forward(
            x2, w_ffn[l], small[l], ffn_shards(l + 1) if more else None, mix_shards(l + 1) if more else None)
        if more:
            w_ffn[l + 1] = ffn_weights(got_in)
            w_mix[l + 1] = mix_weights(got_out)

    dh, sq = loss_head(h, target, "loss_head")
    loss = lax.psum(0.5 * jnp.sum(sq) / D, ("x", "y", "c"))

    big_out = {n: None for n in BIG}

    def update(n, l, parts):
        big_out[n] = adamw_layer(parts, w[n], m[n], v[n], l, big_out[n], f"adamw_{n}_layer{l}")

    small_grads = [None] * depth
    carried = None
    for l in reversed(range(depth)):
        ride_act = _exchange_of(carried[0], True) if carried else None
        ride_down = _exchange_of(carried[1], False) if carried else None
        dx2, gf = ffn_backward(dh, saved_ffn[l], w_ffn[l], small[l], ride_act, ride_down)
        if carried:
            update("w_in", l + 1, gf["got_act"][0])
            update("w_out", l + 1, gf["got_down"][0])
        update("w_down", l, gf["parts_wd"])
        dh, gm_ = mix_backward(dx2, saved_mix[l], w_mix[l], small[l], tables, gf["dwgu"])
        update("w_gate", l, gm_["parts_gate"])
        update("w_up", l, gm_["parts_up"])
        carried = (gm_["dwin"], gm_["dwout"])
        small_grads[l] = {**gf, **gm_}
    last = exchange([_to_blocks(carried[0], True), _to_blocks(carried[1], False)], "exchange_last_grads")
    update("w_in", 0, last[0])
    update("w_out", 0, last[1])

    names = REPLICATED + ["conv_w"]
    stacked = [jnp.stack([small_grads[l][n] for l in range(depth)]) for n in names]
    partial = all_gather([_pack(stacked)], "gather_small_grads")[0]
    zeros = jnp.zeros_like(stacked[-1])
    packed = adamw(partial, _pack([w[n] for n in REPLICATED] + [zeros]), _pack([m[n] for n in REPLICATED] + [zeros]),
                   _pack([v[n] for n in REPLICATED] + [zeros + 1.0]), "adamw_replicated")
    small_out = [_unpack(o, stacked) for o in packed]
    out = {n: tuple(o[i] for o in small_out) for i, n in enumerate(REPLICATED)}
    width = conv_w.shape[2]
    g_cw = lax.dynamic_slice_in_dim(small_out[0][-1], _index(*_place()) * width, width, axis=2)
    cw_res = adamw(g_cw.reshape(1, -1, LANES), conv_w.reshape(-1, LANES), m["conv_w"].reshape(-1, LANES),
                   v["conv_w"].reshape(-1, LANES), "adamw_conv_w")
    out["conv_w"] = tuple(o.reshape(conv_w.shape) for o in cw_res)
    for n in BIG:
        out[n] = tuple(big_out[n])
    results = [loss, dh[None]]
    for k in range(4):
        results += [out[n][k] for n in WEIGHTS]
    return tuple(results)
```

```python
import functools
import math

import jax
import jax.numpy as jnp
from jax import lax
from jax.experimental import pallas as pl
from jax.experimental.pallas import tpu as pltpu

F32 = jnp.float32
BF16 = jnp.bfloat16

N_DEV = 8
DEPTH = 4
EPS = 1e-6
NEG = -1e30

C_A = 512
N_HEADS = 16
HEAD_DIM = 64
C_B = N_HEADS * HEAD_DIM
C_C = 512
POOL_WINDOWS = (2, 4, 8, 16)
C_G = C_C // len(POOL_WINDOWS)
CONV_WIDTH = 31
CONV_HALF = CONV_WIDTH // 2
DILATIONS = (1, 4, 16)
ATTN_HALF = 64
ROT_DIM = HEAD_DIM // 4
ROPE_THETA = 500000.0

ADAM_LR = 0.001
ADAM_B1 = 0.9
ADAM_B2 = 0.999
ADAM_EPS = 1e-08
ADAM_WD = 0.01
ADAM_STEP = 10

LANES = 128
SUBLANES = 8
VMEM_LIMIT = 56 * 1024 * 1024
ROW_TILE = 256
SEQ_TILE = 256
ATTN_BLOCK = 128
MESH = pl.DeviceIdType.MESH


def _params(*sem):
    return pltpu.CompilerParams(dimension_semantics=sem, vmem_limit_bytes=VMEM_LIMIT)


def _tile(n, target):
    if n <= target:
        return n
    t = (target // LANES) * LANES
    while t >= LANES:
        if n % t == 0:
            return t
        t -= LANES
    return n


def _rows(n, target):
    t = min(n, target)
    while n % t:
        t //= 2
    return t


def matmul(a, b, mode, out_dtype, name, tm=1024, tn=512, tk=2048, comm=None):
    if mode == "nn":
        (M, K), (_, N) = a.shape, b.shape
    elif mode == "nt":
        (M, K), (N, _) = a.shape, b.shape
    else:
        (K, M), (_, N) = a.shape, b.shape
    tm, tn, tk = _tile(M, tm), _tile(N, tn), _tile(K, tk)
    nm, nn, nk = M // tm, N // tn, K // tk
    dims = {"nn": (((1,), (0,)), ((), ())), "nt": (((1,), (1,)), ((), ())), "tn": (((0,), (0,)), ((), ()))}[mode]
    nc = comm.n if comm is not None else 0

    def body(*refs):
        a_ref, b_ref = refs[:2]
        c_ins, o_ref, c_outs = refs[2:2 + nc], refs[2 + nc], refs[3 + nc:3 + 2 * nc]
        scratch = refs[3 + 2 * nc:]
        acc, sems = (scratch[:1], scratch[1:]) if nk > 1 else ((), scratch)
        i, j, k = pl.program_id(0), pl.program_id(1), pl.program_id(2)
        if comm is not None:
            @pl.when((i == 0) & (j == 0) & (k == 0))
            def _():
                comm.start(c_ins, c_outs, sems)

        p = lax.dot_general(a_ref[...], b_ref[...], dims, preferred_element_type=F32)
        if nk == 1:
            o_ref[...] = p.astype(o_ref.dtype)
        else:
            acc_ref, = acc

            @pl.when(k == 0)
            def _():
                acc_ref[...] = p

            @pl.when(k > 0)
            def _():
                acc_ref[...] += p

            @pl.when(k == nk - 1)
            def _():
                o_ref[...] = acc_ref[...].astype(o_ref.dtype)

        if comm is not None:
            @pl.when((i == nm - 1) & (j == nn - 1) & (k == nk - 1))
            def _():
                comm.finish(c_ins, c_outs, sems)

    if mode == "nn":
        a_spec = pl.BlockSpec((tm, tk), lambda i, j, k: (i, k))
        b_spec = pl.BlockSpec((tk, tn), lambda i, j, k: (k, j))
    elif mode == "nt":
        a_spec = pl.BlockSpec((tm, tk), lambda i, j, k: (i, k))
        b_spec = pl.BlockSpec((tn, tk), lambda i, j, k: (j, k))
    else:
        a_spec = pl.BlockSpec((tk, tm), lambda i, j, k: (k, i))
        b_spec = pl.BlockSpec((tk, tn), lambda i, j, k: (k, j))
    o_spec = pl.BlockSpec((tm, tn), lambda i, j, k: (i, j))
    o_shape = jax.ShapeDtypeStruct((M, N), out_dtype)
    acc_shape = [pltpu.VMEM((tm, tn), F32)] if nk > 1 else []
    if comm is None:
        return pl.pallas_call(
            body, name=name, grid=(nm, nn, nk), in_specs=[a_spec, b_spec], out_specs=o_spec, out_shape=o_shape,
            scratch_shapes=acc_shape, compiler_params=_params("parallel", "parallel", "arbitrary"),
        )(a, b)
    res = pl.pallas_call(
        body, name=name, grid=(nm, nn, nk), in_specs=[a_spec, b_spec] + [_ANY] * nc,
        out_specs=[o_spec] + [_ANY] * nc, out_shape=[o_shape] + comm.out_shape,
        scratch_shapes=acc_shape + comm.sems, compiler_params=_params("arbitrary", "arbitrary", "arbitrary"),
    )(a, b, *comm.ins)
    return res[0], res[1:]


def _rms(t):
    return t * lax.rsqrt(jnp.mean(t * t, axis=-1, keepdims=True) + EPS)


def _rms_bwd(t, dn):
    r = lax.rsqrt(jnp.mean(t * t, axis=-1, keepdims=True) + EPS)
    n = t * r
    return r * (dn - n * jnp.mean(dn * n, axis=-1, keepdims=True)), n


def _row_spec(tr, d):
    return pl.BlockSpec((tr, d), lambda i: (i, 0))


def _vec_spec(d):
    return pl.BlockSpec((1, d), lambda i: (0, 0))


def _col_spec(d, tr):
    return pl.BlockSpec((d, tr), lambda i: (0, i))


def rms_fwd(x, g, name):
    S, D = x.shape
    tr = _rows(S, ROW_TILE)

    def body(x_ref, g_ref, o_ref, ot_ref):
        h = _rms(x_ref[...].astype(F32)) * g_ref[...]
        o_ref[...] = h.astype(o_ref.dtype)
        ot_ref[...] = h.T.astype(ot_ref.dtype)

    return pl.pallas_call(
        body, name=name, grid=(S // tr,), in_specs=[_row_spec(tr, D), _vec_spec(D)],
        out_specs=[_row_spec(tr, D), _col_spec(D, tr)],
        out_shape=[jax.ShapeDtypeStruct((S, D), BF16), jax.ShapeDtypeStruct((D, S), BF16)],
        compiler_params=_params("parallel"),
    )(x, g)


def norm_residual(z, x, g, name):
    S, D = x.shape
    tr = _rows(S, ROW_TILE)

    def body(z_ref, x_ref, g_ref, o_ref):
        o_ref[...] = x_ref[...] + _rms(z_ref[...].astype(F32)) * g_ref[...]

    return pl.pallas_call(
        body, name=name, grid=(S // tr,), in_specs=[_row_spec(tr, D), _row_spec(tr, D), _vec_spec(D)],
        out_specs=_row_spec(tr, D), out_shape=jax.ShapeDtypeStruct((S, D), F32), compiler_params=_params("parallel"),
    )(z, x, g)


def rms_bwd(t, dy, g, res, out_dtype, name):
    S, D = t.shape
    tr = _rows(S, ROW_TILE)
    has_res = res is not None

    def body(t_ref, dy_ref, g_ref, *rest):
        if has_res:
            res_ref, dt_ref, dg_ref = rest
        else:
            dt_ref, dg_ref = rest
        dyv = dy_ref[...].astype(F32)
        dt, n = _rms_bwd(t_ref[...].astype(F32), dyv * g_ref[...])
        if has_res:
            dt = dt + res_ref[...]
        dt_ref[...] = dt.astype(dt_ref.dtype)

        @pl.when(pl.program_id(0) == 0)
        def _():
            dg_ref[...] = jnp.zeros_like(dg_ref)

        dg_ref[...] += jnp.sum(dyv * n, axis=0, keepdims=True)

    ins = [t, dy, g] + ([res] if has_res else [])
    specs = [_row_spec(tr, D), _row_spec(tr, D), _vec_spec(D)] + ([_row_spec(tr, D)] if has_res else [])
    return pl.pallas_call(
        body, name=name, grid=(S // tr,), in_specs=specs, out_specs=[_row_spec(tr, D), _vec_spec(D)],
        out_shape=[jax.ShapeDtypeStruct((S, D), out_dtype), jax.ShapeDtypeStruct((1, D), F32)],
        compiler_params=_params("arbitrary"),
    )(*ins)


def swiglu_fwd(gu, name):
    S, F2 = gu.shape
    F = F2 // 2
    tr = _rows(S, ROW_TILE)

    def body(g_ref, u_ref, o_ref, ot_ref):
        g = g_ref[...].astype(F32)
        a = g * jax.nn.sigmoid(g) * u_ref[...].astype(F32)
        o_ref[...] = a.astype(o_ref.dtype)
        ot_ref[...] = a.T.astype(ot_ref.dtype)

    return pl.pallas_call(
        body, name=name, grid=(S // tr,),
        in_specs=[pl.BlockSpec((tr, F), lambda i: (i, 0)), pl.BlockSpec((tr, F), lambda i: (i, 1))],
        out_specs=[_row_spec(tr, F), _col_spec(F, tr)],
        out_shape=[jax.ShapeDtypeStruct((S, F), BF16), jax.ShapeDtypeStruct((F, S), BF16)],
        compiler_params=_params("parallel"),
    )(gu, gu)


def swiglu_bwd(gu, da, name):
    S, F2 = gu.shape
    F = F2 // 2
    tr = _rows(S, ROW_TILE)

    def body(g_ref, u_ref, da_ref, o_ref):
        g = g_ref[...].astype(F32)
        u = u_ref[...].astype(F32)
        dav = da_ref[...].astype(F32)
        sig = jax.nn.sigmoid(g)
        o_ref[:, :F] = (dav * u * (sig * (1.0 + g * (1.0 - sig)))).astype(o_ref.dtype)
        o_ref[:, F:] = (dav * (g * sig)).astype(o_ref.dtype)

    return pl.pallas_call(
        body, name=name, grid=(S // tr,),
        in_specs=[pl.BlockSpec((tr, F), lambda i: (i, 0)), pl.BlockSpec((tr, F), lambda i: (i, 1)), _row_spec(tr, F)],
        out_specs=_row_spec(tr, F2), out_shape=jax.ShapeDtypeStruct((S, F2), BF16), compiler_params=_params("parallel"),
    )(gu, gu, da)


def loss_head(y, target, name):
    S, D = y.shape
    tr = _rows(S, ROW_TILE)

    def body(y_ref, t_ref, dy_ref, sq_ref):
        e = y_ref[...] - t_ref[...]
        dy_ref[...] = e * (1.0 / D)

        @pl.when(pl.program_id(0) == 0)
        def _():
            sq_ref[...] = jnp.zeros_like(sq_ref)

        sq_ref[...] += jnp.sum(e * e, axis=0, keepdims=True)

    return pl.pallas_call(
        body, name=name, grid=(S // tr,), in_specs=[_row_spec(tr, D), _row_spec(tr, D)],
        out_specs=[_row_spec(tr, D), _vec_spec(D)],
        out_shape=[jax.ShapeDtypeStruct((S, D), F32), jax.ShapeDtypeStruct((1, D), F32)],
        compiler_params=_params("arbitrary"),
    )(y, target)


def _halo_specs(bs, halo, width, col, n_rows):
    per = bs // halo
    last = n_rows // halo - 1
    cur = pl.BlockSpec((bs, width), lambda i: (i, col))
    prev = pl.BlockSpec((halo, width), lambda i: (jnp.maximum(i * per - 1, 0), col))
    nxt = pl.BlockSpec((halo, width), lambda i: (jnp.minimum((i + 1) * per, last), col))
    return prev, cur, nxt


TAP_CHUNK = 32


def _build_phases(ref, phases_ref):
    total = ref.shape[0] - SUBLANES
    for b in range(SUBLANES):
        phases_ref[b, 0:total, :] = ref[pl.ds(b, total), :]


def _tap_rows(phases_ref, off, n):
    b = off % SUBLANES
    return phases_ref[b, off - b:off - b + n, :]


def _conv_taps(phases_ref, offsets, n, weights_ref, init, out_ref):
    for r0 in range(0, n, TAP_CHUNK):
        rows = min(TAP_CHUNK, n - r0)
        acc = jnp.zeros((rows, C_A), F32) + init
        for t, off in offsets.items():
            acc = acc + weights_ref[t:t + 1, :] * _tap_rows(phases_ref, off + r0, rows)
        out_ref[r0:r0 + rows, :] = acc


def _glu(a, g):
    return a.astype(F32) * jax.nn.sigmoid(g.astype(F32))


def _layernorm_silu(c, lg, lb):
    mu = jnp.mean(c, axis=-1, keepdims=True)
    cc = c - mu
    rstd = lax.rsqrt(jnp.mean(cc * cc, axis=-1, keepdims=True) + EPS)
    xh = cc * rstd
    ln = xh * lg + lb
    sig = jax.nn.sigmoid(ln)
    return xh, rstd, ln, sig


def conv_fwd(proj, cw, cb, lg, lb, gm, name):
    S = proj.shape[0]
    bs = _rows(S, SEQ_TILE)
    nb = S // bs
    H = 16

    def body(ap, ac, an, gp, gc, gn, cw_ref, cb_ref, lg_ref, lb_ref, gm_ref, y_ref, win_ref, phases_ref, c_ref):
        i = pl.program_id(0)
        win_ref[0:H, :] = jnp.where(i > 0, _glu(ap[...], gp[...]), 0.0)
        win_ref[H:H + bs, :] = _glu(ac[...], gc[...])
        win_ref[H + bs:2 * H + bs, :] = jnp.where(i < nb - 1, _glu(an[...], gn[...]), 0.0)
        win_ref[bs + 2 * H:, :] = jnp.zeros((SUBLANES, C_A), F32)
        _build_phases(win_ref, phases_ref)
        _conv_taps(phases_ref, {t: H - CONV_HALF + t for t in range(CONV_WIDTH)}, bs, cw_ref, cb_ref[...], c_ref)
        _, _, ln, sig = _layernorm_silu(c_ref[...], lg_ref[...], lb_ref[...])
        y_ref[...] = (_rms(ln * sig) * gm_ref[...]).astype(y_ref.dtype)

    a_specs = _halo_specs(bs, H, C_A, 0, S)
    g_specs = _halo_specs(bs, H, C_A, 1, S)
    vec = _vec_spec(C_A)
    return pl.pallas_call(
        body, name=name, grid=(nb,),
        in_specs=[*a_specs, *g_specs, pl.BlockSpec((32, C_A), lambda i: (0, 0)), vec, vec, vec, vec],
        out_specs=_row_spec(bs, C_A), out_shape=jax.ShapeDtypeStruct((S, C_A), BF16),
        scratch_shapes=[pltpu.VMEM((bs + 2 * H + SUBLANES, C_A), F32), pltpu.VMEM((SUBLANES, bs + 2 * H, C_A), F32),
                        pltpu.VMEM((bs, C_A), F32)],
        compiler_params=_params("parallel"),
    )(proj, proj, proj, proj, proj, proj, cw, cb, lg, lb, gm)


def conv_bwd(proj, dy, cw, cb, lg, lb, gm, name):
    S = proj.shape[0]
    bs = _rows(S, SEQ_TILE)
    nb = S // bs
    H = 32
    HC = 16
    bc = bs + 2 * HC

    def body(ap, ac, an, gp, gc, gn, dp, dc_, dn, cw_ref, cb_ref, lg_ref, lb_ref, gm_ref,
             dproj_ref, dcw_ref, dcb_ref, dlg_ref, dlb_ref, dgm_ref, win_ref, dcs_ref, phases_ref, c_ref):
        i = pl.program_id(0)
        win_ref[0:H, :] = jnp.where(i > 0, _glu(ap[...], gp[...]), 0.0)
        win_ref[H:H + bs, :] = _glu(ac[...], gc[...])
        win_ref[H + bs:2 * H + bs, :] = jnp.where(i < nb - 1, _glu(an[...], gn[...]), 0.0)
        win_ref[bs + 2 * H:, :] = jnp.zeros((SUBLANES, C_A), F32)
        _build_phases(win_ref, phases_ref)
        _conv_taps(phases_ref, {t: H - HC - CONV_HALF + t for t in range(CONV_WIDTH)}, bc, cw_ref, cb_ref[...], c_ref)
        xh, rstd, ln, sig = _layernorm_silu(c_ref[...], lg_ref[...], lb_ref[...])
        ya = ln * sig
        dyv = jnp.concatenate([dp[...], dc_[...], dn[...]], axis=0).astype(F32)
        dya, n = _rms_bwd(ya, dyv * gm_ref[...])
        dln = dya * (sig * (1.0 + ln * (1.0 - sig)))
        dxh = dln * lg_ref[...]
        dcv = rstd * (dxh - jnp.mean(dxh, axis=-1, keepdims=True) - xh * jnp.mean(dxh * xh, axis=-1, keepdims=True))
        pos = i * bs - HC + lax.broadcasted_iota(jnp.int32, (bc, 1), 0)
        dcv = jnp.where((pos >= 0) & (pos < S), dcv, 0.0)
        dcs_ref[0:bc, :] = dcv
        dcs_ref[bc:, :] = jnp.zeros((SUBLANES, C_A), F32)

        @pl.when(i == 0)
        def _():
            for r in (dcw_ref, dcb_ref, dlg_ref, dlb_ref, dgm_ref):
                r[...] = jnp.zeros_like(r)

        mid = slice(HC, HC + bs)
        dcb_ref[...] += jnp.sum(dcv[mid], axis=0, keepdims=True)
        dlg_ref[...] += jnp.sum((dln * xh)[mid], axis=0, keepdims=True)
        dlb_ref[...] += jnp.sum(dln[mid], axis=0, keepdims=True)
        dgm_ref[...] += jnp.sum((dyv * n)[mid], axis=0, keepdims=True)
        dcm = dcv[mid]
        for t in range(CONV_WIDTH):
            dcw_ref[t:t + 1, :] += jnp.sum(dcm * _tap_rows(phases_ref, H - CONV_HALF + t, bs), axis=0, keepdims=True)
        _build_phases(dcs_ref, phases_ref)
        _conv_taps(phases_ref, {t: HC + CONV_HALF - t for t in range(CONV_WIDTH)}, bs, cw_ref, 0.0, c_ref)
        dh = c_ref[0:bs, :]
        a = ac[...].astype(F32)
        sg = jax.nn.sigmoid(gc[...].astype(F32))
        dproj_ref[:, :C_A] = (dh * sg).astype(dproj_ref.dtype)
        dproj_ref[:, C_A:] = (dh * a * sg * (1.0 - sg)).astype(dproj_ref.dtype)

    a_specs = _halo_specs(bs, H, C_A, 0, S)
    g_specs = _halo_specs(bs, H, C_A, 1, S)
    d_specs = _halo_specs(bs, HC, C_A, 0, S)
    vec = _vec_spec(C_A)
    full = pl.BlockSpec((32, C_A), lambda i: (0, 0))
    vshape = jax.ShapeDtypeStruct((1, C_A), F32)
    return pl.pallas_call(
        body, name=name, grid=(nb,),
        in_specs=[*a_specs, *g_specs, *d_specs, full, vec, vec, vec, vec],
        out_specs=[_row_spec(bs, 2 * C_A), full, vec, vec, vec, vec],
        out_shape=[jax.ShapeDtypeStruct((S, 2 * C_A), BF16), jax.ShapeDtypeStruct((32, C_A), F32),
                   vshape, vshape, vshape, vshape],
        scratch_shapes=[pltpu.VMEM((bs + 2 * H + SUBLANES, C_A), F32), pltpu.VMEM((bc + SUBLANES, C_A), F32),
                        pltpu.VMEM((SUBLANES, bs + 2 * H, C_A), F32), pltpu.VMEM((bc, C_A), F32)],
        compiler_params=_params("arbitrary"),
    )(proj, proj, proj, proj, proj, proj, dy, dy, dy, cw, cb, lg, lb, gm)


POOL_HALO = 16


def _shift(x, k):
    n = x.shape[0]
    return pltpu.roll(x, (-k) % n, axis=0)


def _pool_means(u, pos, S):
    w2 = _shift(u, -1) + u
    w4 = _shift(w2, -1) + _shift(w2, 1)
    w8 = _shift(w4, -2) + _shift(w4, 2)
    w16 = _shift(w8, -4) + _shift(w8, 4)
    sums = (w2, w4, w8, w16)
    lane = lax.broadcasted_iota(jnp.int32, (1, C_C), 1)
    total = jnp.zeros_like(u)
    inv = jnp.zeros_like(u)
    for gi, win in enumerate(POOL_WINDOWS):
        cnt = jnp.minimum(pos + (win - win // 2), S) - jnp.maximum(pos - win // 2, 0)
        icnt = 1.0 / jnp.maximum(cnt, 1).astype(F32)
        sel = (lane >= gi * C_G) & (lane < (gi + 1) * C_G)
        total = jnp.where(sel, sums[gi], total)
        inv = jnp.where(sel, icnt, inv)
    return total * inv - u, inv


def _pool_adjoint(e):
    v2 = e + _shift(e, 1)
    v4 = _shift(v2, -1) + _shift(v2, 1)
    v8 = _shift(v4, -2) + _shift(v4, 2)
    v16 = _shift(v8, -4) + _shift(v8, 4)
    sums = (v2, v4, v8, v16)
    lane = lax.broadcasted_iota(jnp.int32, (1, C_C), 1)
    out = jnp.zeros_like(e)
    for gi in range(len(POOL_WINDOWS)):
        sel = (lane >= gi * C_G) & (lane < (gi + 1) * C_G)
        out = jnp.where(sel, sums[gi], out)
    return out


def _pool_window(up, uc, un, i, nb, bs, S):
    H = POOL_HALO
    u = jnp.concatenate([jnp.where(i > 0, up[...].astype(F32), 0.0), uc[...].astype(F32),
                         jnp.where(i < nb - 1, un[...].astype(F32), 0.0)], axis=0)
    pos = i * bs - H + lax.broadcasted_iota(jnp.int32, (bs + 2 * H, 1), 0)
    return u, pos


def _pool_mix(pooled, pw_ref):
    outs = []
    for gi in range(len(POOL_WINDOWS)):
        outs.append(jnp.dot(pooled[:, gi * C_G:(gi + 1) * C_G].astype(BF16), pw_ref[gi].astype(BF16),
                            preferred_element_type=F32))
    return jnp.concatenate(outs, axis=1)


def pool_fwd(proj, pw, ps, gm, name):
    S, width = proj.shape
    col = width // C_C - 1
    bs = _rows(S, SEQ_TILE)
    nb = S // bs
    H = POOL_HALO

    def body(up, uc, un, pw_ref, ps_ref, gm_ref, y_ref):
        i = pl.program_id(0)
        u, pos = _pool_window(up, uc, un, i, nb, bs, S)
        pooled, _ = _pool_means(u, pos, S)
        mixed = _pool_mix(pooled[H:H + bs], pw_ref)
        y_ref[...] = (_rms(mixed * ps_ref[...]) * gm_ref[...]).astype(y_ref.dtype)

    vec = _vec_spec(C_C)
    return pl.pallas_call(
        body, name=name, grid=(nb,),
        in_specs=[*_halo_specs(bs, H, C_C, col, S), pl.BlockSpec((4, C_G, C_G), lambda i: (0, 0, 0)), vec, vec],
        out_specs=_row_spec(bs, C_C), out_shape=jax.ShapeDtypeStruct((S, C_C), BF16),
        compiler_params=_params("parallel"),
    )(proj, proj, proj, pw, ps, gm)


def pool_bwd(proj, dy, pw, ps, gm, name):
    S, width = proj.shape
    col = width // C_C - 1
    dcol = dy.shape[1] // C_C - 1
    bs = _rows(S, SEQ_TILE)
    nb = S // bs
    H = POOL_HALO
    W = bs + 2 * H

    def body(up, uc, un, dp, dc_, dn, pw_ref, ps_ref, gm_ref, du_ref, dpw_ref, dps_ref, dgm_ref):
        i = pl.program_id(0)
        u, pos = _pool_window(up, uc, un, i, nb, bs, S)
        pooled, inv = _pool_means(u, pos, S)
        mixed = _pool_mix(pooled, pw_ref)
        dyv = jnp.concatenate([dp[...], dc_[...], dn[...]], axis=0).astype(F32)
        dyc, n = _rms_bwd(mixed * ps_ref[...], dyv * gm_ref[...])
        dmixed = dyc * ps_ref[...]
        dmb = dmixed.astype(BF16)
        dpooled = jnp.concatenate(
            [lax.dot_general(dmb[:, gi * C_G:(gi + 1) * C_G], pw_ref[gi].astype(BF16), (((1,), (1,)), ((), ())),
                             preferred_element_type=F32) for gi in range(len(POOL_WINDOWS))], axis=1)
        dpooled = jnp.where((pos >= 0) & (pos < S), dpooled, 0.0)
        du = _pool_adjoint(dpooled * inv) - dpooled
        du_ref[...] = du[H:H + bs].astype(du_ref.dtype)

        @pl.when(i == 0)
        def _():
            for r in (dpw_ref, dps_ref, dgm_ref):
                r[...] = jnp.zeros_like(r)

        mid = slice(H, H + bs)
        dps_ref[...] += jnp.sum((dyc * mixed)[mid], axis=0, keepdims=True)
        dgm_ref[...] += jnp.sum((dyv * n)[mid], axis=0, keepdims=True)
        pb = pooled[mid].astype(BF16)
        for gi in range(len(POOL_WINDOWS)):
            sl = slice(gi * C_G, (gi + 1) * C_G)
            dpw_ref[gi] += lax.dot_general(pb[:, sl], dmb[mid][:, sl], (((0,), (0,)), ((), ())),
                                           preferred_element_type=F32)

    vec = _vec_spec(C_C)
    full = pl.BlockSpec((4, C_G, C_G), lambda i: (0, 0, 0))
    vshape = jax.ShapeDtypeStruct((1, C_C), F32)
    return pl.pallas_call(
        body, name=name, grid=(nb,),
        in_specs=[*_halo_specs(bs, H, C_C, col, S), *_halo_specs(bs, H, C_C, dcol, S), full, vec, vec],
        out_specs=[_row_spec(bs, C_C), full, vec, vec],
        out_shape=[jax.ShapeDtypeStruct((S, C_C), BF16), jax.ShapeDtypeStruct((4, C_G, C_G), F32), vshape, vshape],
        compiler_params=_params("arbitrary"),
    )(proj, proj, proj, dy, dy, dy, pw, ps, gm)


def rope_tables(S):
    pos = jnp.arange(S, dtype=F32)
    inv = ROPE_THETA ** (-jnp.arange(0, ROT_DIM, 2, dtype=F32) / ROT_DIM)
    ang = pos[:, None] * inv[None, :]
    half = ROT_DIM // 2
    cos, sin = jnp.cos(ang), jnp.sin(ang)
    zeros = jnp.zeros((S, half), F32)
    rest = jnp.zeros((S, HEAD_DIM - ROT_DIM), F32)
    per_head = (jnp.concatenate([cos, cos, rest + 1.0], axis=1), jnp.concatenate([-sin, zeros, rest], axis=1),
                jnp.concatenate([zeros, sin, rest], axis=1))
    return tuple(jnp.tile(t, (1, LANES // HEAD_DIM)) for t in per_head)


def _rotate(t, c, s1, s2, sign):
    half = ROT_DIM // 2
    return t * c + sign * (pltpu.roll(t, LANES - half, axis=1) * s1 + pltpu.roll(t, half, axis=1) * s2)


def rope_fwd(proj, tables, name):
    S = proj.shape[0]
    tr = _rows(S, ROW_TILE)
    qcol = 2 * C_A // C_B
    nd = len(DILATIONS)

    def body(q_ref, k_ref, v_ref, c_ref, s1_ref, s2_ref, *outs):
        c, s1, s2 = c_ref[...], s1_ref[...], s2_ref[...]
        qs, ks = [], []
        for p in range(C_B // LANES):
            sl = slice(p * LANES, (p + 1) * LANES)
            qs.append((_rotate(q_ref[:, sl].astype(F32), c, s1, s2, 1.0) * HEAD_DIM ** -0.5).astype(BF16))
            ks.append(_rotate(k_ref[:, sl].astype(F32), c, s1, s2, 1.0).astype(BF16))
        tensors = (jnp.concatenate(qs, axis=1), jnp.concatenate(ks, axis=1), v_ref[...])
        for n, d in enumerate(DILATIONS):
            for t, x in enumerate(tensors):
                _store_dilated(outs[3 * n + t], x, d)

    tab = _row_spec(tr, LANES)
    res = pl.pallas_call(
        body, name=name, grid=(S // tr,),
        in_specs=[pl.BlockSpec((tr, C_B), lambda i, col=qcol + n: (i, col)) for n in range(3)] + [tab, tab, tab],
        out_specs=[_dilated_spec(tr, d, C_B) for d in DILATIONS for _ in range(3)],
        out_shape=[jax.ShapeDtypeStruct((d, S // d, C_B), BF16) for d in DILATIONS for _ in range(3)],
        compiler_params=_params("parallel"),
    )(proj, proj, proj, *tables)
    return [tuple(res[3 * n:3 * n + 3]) for n in range(nd)]


def rope_bwd(dqs, dks, dvs, tables, dconv, du, name):
    S = dconv.shape[0]
    tr = _rows(S, ROW_TILE)
    n = len(dqs)
    base = 2 * C_A
    width = base + 3 * C_B + C_C

    def body(*refs):
        dq_refs, dk_refs, dv_refs = refs[:n], refs[n:2 * n], refs[2 * n:3 * n]
        c_ref, s1_ref, s2_ref, dconv_ref, du_ref, o_ref = refs[3 * n:]
        c, s1, s2 = c_ref[...], s1_ref[...], s2_ref[...]
        o_ref[:, :base] = dconv_ref[...]
        o_ref[:, base + 3 * C_B:] = du_ref[...]
        dq_all = sum(_load_dilated(r, d).astype(F32) for r, d in zip(dq_refs, DILATIONS))
        dk_all = sum(_load_dilated(r, d).astype(F32) for r, d in zip(dk_refs, DILATIONS))
        dv_all = sum(_load_dilated(r, d).astype(F32) for r, d in zip(dv_refs, DILATIONS))
        for p in range(C_B // LANES):
            sl = slice(p * LANES, (p + 1) * LANES)
            dq, dk, dv = dq_all[:, sl], dk_all[:, sl], dv_all[:, sl]
            at = base + p * LANES
            o_ref[:, at:at + LANES] = (_rotate(dq, c, s1, s2, -1.0) * HEAD_DIM ** -0.5).astype(BF16)
            o_ref[:, C_B + at:C_B + at + LANES] = _rotate(dk, c, s1, s2, -1.0).astype(BF16)
            o_ref[:, 2 * C_B + at:2 * C_B + at + LANES] = dv.astype(BF16)

    tab = _row_spec(tr, LANES)
    return pl.pallas_call(
        body, name=name, grid=(S // tr,),
        in_specs=[_dilated_spec(tr, d, C_B) for _ in range(3) for d in DILATIONS]
        + [tab, tab, tab, _row_spec(tr, base), _row_spec(tr, C_C)],
        out_specs=_row_spec(tr, width), out_shape=jax.ShapeDtypeStruct((S, width), BF16),
        compiler_params=_params("parallel"),
    )(*dqs, *dks, *dvs, *tables, dconv, du)


def _attn_specs(bq, width, L):
    per = bq // ATTN_HALF
    last = L // ATTN_HALF - 1
    cur = pl.BlockSpec((None, bq, width), lambda r, j: (r, j, 0))
    prev = pl.BlockSpec((None, ATTN_HALF, width), lambda r, j: (r, jnp.maximum(j * per - 1, 0), 0))
    nxt = pl.BlockSpec((None, ATTN_HALF, width), lambda r, j: (r, jnp.minimum((j + 1) * per, last), 0))
    return prev, cur, nxt


def _window(refs, sl):
    return jnp.concatenate([r[:, sl] for r in refs], axis=0)


def _band_mask(j, bq, L, rows_are_window):
    bw = bq + 2 * ATTN_HALF
    if rows_are_window:
        rp = j * bq - ATTN_HALF + lax.broadcasted_iota(jnp.int32, (bw, 1), 0)
        cp = j * bq + lax.broadcasted_iota(jnp.int32, (1, bq), 1)
        return (jnp.abs(rp - cp) <= ATTN_HALF) & (rp >= 0) & (rp < L)
    rp = j * bq + lax.broadcasted_iota(jnp.int32, (bq, 1), 0)
    cp = j * bq - ATTN_HALF + lax.broadcasted_iota(jnp.int32, (1, bw), 1)
    return (jnp.abs(rp - cp) <= ATTN_HALF) & (cp >= 0) & (cp < L)


def _head_col(stats, h):
    lane = lax.broadcasted_iota(jnp.int32, (1, LANES), 1)
    return jnp.sum(jnp.where(lane == h, stats, 0.0), axis=1, keepdims=True)


def _stack_heads(x):
    first = lax.broadcasted_iota(jnp.int32, (1, LANES), 1) < HEAD_DIM
    zero = jnp.zeros_like(x)
    return jnp.concatenate([jnp.where(first, x, zero), jnp.where(first, zero, x)], axis=0)


_NT = (((1,), (1,)), ((), ()))
_TN = (((0,), (0,)), ((), ()))


def attn_fwd_pattern(qd, kd, vd, name):
    d, L, _ = qd.shape
    bq = _rows(L, ATTN_BLOCK)

    def body(q_ref, kp, kc, kn, vp, vc, vn, o_ref, lse_ref):
        j = pl.program_id(1)
        mask = _band_mask(j, bq, L, False)
        mask2 = jnp.concatenate([mask, mask], axis=0)
        lane = lax.broadcasted_iota(jnp.int32, (1, LANES), 1)
        first = lane < HEAD_DIM
        lse = jnp.zeros((bq, LANES), F32)
        for p in range(C_B // LANES):
            sl = slice(p * LANES, (p + 1) * LANES)
            kw = _window((kp, kc, kn), sl)
            vw = _window((vp, vc, vn), sl)
            s = jnp.where(mask2, lax.dot_general(_stack_heads(q_ref[:, sl]), kw, _NT, preferred_element_type=F32), NEG)
            m = jnp.max(s, axis=1, keepdims=True)
            e = jnp.exp(s - m)
            l = jnp.sum(e, axis=1, keepdims=True)
            o = jnp.dot(e.astype(BF16), vw, preferred_element_type=F32) * (1.0 / l)
            stat = m + jnp.log(l)
            lse = jnp.where(lane == 2 * p, stat[:bq], jnp.where(lane == 2 * p + 1, stat[bq:], lse))
            o_ref[:, sl] = jnp.where(first, o[:bq], o[bq:]).astype(o_ref.dtype)
        lse_ref[...] = lse

    kv = _attn_specs(bq, C_B, L)
    return pl.pallas_call(
        body, name=name, grid=(d, L // bq), in_specs=[kv[1], *kv, *kv],
        out_specs=[kv[1], pl.BlockSpec((None, bq, LANES), lambda r, j: (r, j, 0))],
        out_shape=[jax.ShapeDtypeStruct((d, L, C_B), BF16), jax.ShapeDtypeStruct((d, L, LANES), F32)],
        compiler_params=_params("parallel", "parallel"),
    )(qd, kd, kd, kd, vd, vd, vd)


def attn_combine(os_, lses, gm, ya, yc, name):
    S = ya.shape[0]
    tr = _rows(S, ROW_TILE)
    n = len(os_)

    def body(*refs):
        o_refs, l_refs = refs[:n], refs[n:2 * n]
        gm_ref, ya_ref, yc_ref, y_ref, out_ref, lse_ref = refs[2 * n:]
        ls = [_load_dilated(r, d) for r, d in zip(l_refs, DILATIONS)]
        os_tok = [_load_dilated(r, d) for r, d in zip(o_refs, DILATIONS)]
        mx = functools.reduce(jnp.maximum, ls)
        ws = [jnp.exp(l - mx) for l in ls]
        den = sum(ws)
        lse_ref[...] = mx + jnp.log(den)
        wn = [w / den for w in ws]
        lane = lax.broadcasted_iota(jnp.int32, (1, LANES), 1)
        first = lane < HEAD_DIM
        blocks = []
        for p in range(C_B // LANES):
            sl = slice(p * LANES, (p + 1) * LANES)
            acc = jnp.zeros((tr, LANES), F32)
            for w, o in zip(wn, os_tok):
                acc = acc + jnp.where(first, _head_col(w, 2 * p), _head_col(w, 2 * p + 1)) * o[:, sl].astype(F32)
            blocks.append(acc)
        out = jnp.concatenate(blocks, axis=1)
        out_ref[...] = out.astype(out_ref.dtype)
        y_ref[:, :C_A] = ya_ref[...]
        y_ref[:, C_A:C_A + C_B] = (_rms(out) * gm_ref[...]).astype(y_ref.dtype)
        y_ref[:, C_A + C_B:] = yc_ref[...]

    st = _row_spec(tr, LANES)
    mix = C_A + C_B + C_C
    return pl.pallas_call(
        body, name=name, grid=(S // tr,),
        in_specs=[_dilated_spec(tr, d, C_B) for d in DILATIONS] + [_dilated_spec(tr, d, LANES) for d in DILATIONS]
        + [_vec_spec(C_B), _row_spec(tr, C_A), _row_spec(tr, C_C)],
        out_specs=[_row_spec(tr, mix), _row_spec(tr, C_B), st],
        out_shape=[jax.ShapeDtypeStruct((S, mix), BF16), jax.ShapeDtypeStruct((S, C_B), BF16),
                   jax.ShapeDtypeStruct((S, LANES), F32)],
        compiler_params=_params("parallel"),
    )(*os_, *lses, gm, ya, yc)


def attn_out_bwd(out, lse, dy, gm, name):
    S = out.shape[0]
    tr = _rows(S, ROW_TILE)
    nd = len(DILATIONS)

    def body(o_ref, lse_ref, dy1, dy2, g_ref, *outs):
        dg_ref = outs[-1]
        o = o_ref[...].astype(F32)
        dyv = jnp.concatenate([dy1[...], dy2[...]], axis=1).astype(F32)
        do, n = _rms_bwd(o, dyv * g_ref[...])
        dob = do.astype(BF16)
        prod = dob.astype(F32) * o
        lane = lax.broadcasted_iota(jnp.int32, (1, LANES), 1)
        first = lane < HEAD_DIM
        delta = jnp.zeros((tr, LANES), F32)
        for p in range(C_B // LANES):
            blk = prod[:, p * LANES:(p + 1) * LANES]
            delta = jnp.where(lane == 2 * p, jnp.sum(jnp.where(first, blk, 0.0), axis=1, keepdims=True), delta)
            delta = jnp.where(lane == 2 * p + 1, jnp.sum(jnp.where(first, 0.0, blk), axis=1, keepdims=True), delta)
        lse_v = lse_ref[...]
        for k, d in enumerate(DILATIONS):
            _store_dilated(outs[3 * k], dob, d)
            _store_dilated(outs[3 * k + 1], lse_v, d)
            _store_dilated(outs[3 * k + 2], delta, d)

        @pl.when(pl.program_id(0) == 0)
        def _():
            dg_ref[...] = jnp.zeros_like(dg_ref)

        dg_ref[...] += jnp.sum(dyv * n, axis=0, keepdims=True)

    widths = (C_B, LANES, LANES)
    dtypes = (BF16, F32, F32)
    res = pl.pallas_call(
        body, name=name, grid=(S // tr,),
        in_specs=[_row_spec(tr, C_B), _row_spec(tr, LANES), pl.BlockSpec((tr, C_A), lambda i: (i, 1)),
                  pl.BlockSpec((tr, C_A), lambda i: (i, 2)), _vec_spec(C_B)],
        out_specs=[_dilated_spec(tr, d, wd) for d in DILATIONS for wd in widths] + [_vec_spec(C_B)],
        out_shape=[jax.ShapeDtypeStruct((d, S // d, wd), dt) for d in DILATIONS for wd, dt in zip(widths, dtypes)]
        + [jax.ShapeDtypeStruct((1, C_B), F32)],
        compiler_params=_params("arbitrary"),
    )(out, lse, dy, dy, gm)
    return [tuple(res[3 * k:3 * k + 3]) for k in range(nd)], res[-1]


def attn_bwd_pattern(qd, kd, vd, dod, lsed, deltad, name):
    d, L, _ = qd.shape
    bq = _rows(L, ATTN_BLOCK)

    nb = L // bq
    bw = bq + 2 * ATTN_HALF
    lo = bq - ATTN_HALF

    def body(qc, dc_, lc, tc, kp, kc, kn, vp, vc, vn, dq_ref, dk_ref, dv_ref, dk_acc, dv_acc):
        j = pl.program_id(1)

        @pl.when(j == 0)
        def _():
            dk_acc[...] = jnp.zeros_like(dk_acc)
            dv_acc[...] = jnp.zeros_like(dv_acc)

        @pl.when(j > 0)
        def _():
            for acc in (dk_acc, dv_acc):
                acc[0:bq, :] = acc[bq:2 * bq, :]
                acc[bq:2 * bq, :] = acc[2 * bq:, :]
                acc[2 * bq:, :] = jnp.zeros((bq, C_B), F32)

        @pl.when(j < nb)
        def _():
            mask = _band_mask(j, bq, L, False)
            mask2 = jnp.concatenate([mask, mask], axis=0)
            first = lax.broadcasted_iota(jnp.int32, (1, LANES), 1) < HEAD_DIM
            lse_c, delta_c = lc[...], tc[...]
            for p in range(C_B // LANES):
                sl = slice(p * LANES, (p + 1) * LANES)
                qs, dos = _stack_heads(qc[:, sl]), _stack_heads(dc_[:, sl])
                kw, vw = _window((kp, kc, kn), sl), _window((vp, vc, vn), sl)
                lse_s = jnp.concatenate([_head_col(lse_c, 2 * p), _head_col(lse_c, 2 * p + 1)], axis=0)
                delta_s = jnp.concatenate([_head_col(delta_c, 2 * p), _head_col(delta_c, 2 * p + 1)], axis=0)
                s = lax.dot_general(qs, kw, _NT, preferred_element_type=F32)
                pr = jnp.where(mask2, jnp.exp(s - lse_s), 0.0)
                dpr = lax.dot_general(dos, vw, _NT, preferred_element_type=F32)
                ds = (pr * (dpr - delta_s)).astype(BF16)
                dq = jnp.dot(ds, kw, preferred_element_type=F32)
                dq_ref[:, sl] = jnp.where(first, dq[:bq], dq[bq:]).astype(dq_ref.dtype)
                dk_acc[lo:lo + bw, sl] += lax.dot_general(ds, qs, _TN, preferred_element_type=F32)
                dv_acc[lo:lo + bw, sl] += lax.dot_general(pr.astype(BF16), dos, _TN, preferred_element_type=F32)

        dk_ref[...] = dk_acc[0:bq, :].astype(dk_ref.dtype)
        dv_ref[...] = dv_acc[0:bq, :].astype(dv_ref.dtype)

    per = bq // ATTN_HALF
    last = L // ATTN_HALF - 1

    def clamp(j):
        return jnp.minimum(j, nb - 1)

    def specs(width):
        cur = pl.BlockSpec((None, bq, width), lambda r, j: (r, clamp(j), 0))
        prev = pl.BlockSpec((None, ATTN_HALF, width), lambda r, j: (r, jnp.maximum(clamp(j) * per - 1, 0), 0))
        nxt = pl.BlockSpec((None, ATTN_HALF, width), lambda r, j: (r, jnp.minimum((clamp(j) + 1) * per, last), 0))
        return prev, cur, nxt

    wide = specs(C_B)
    stat = specs(LANES)[1]
    lagged = pl.BlockSpec((None, bq, C_B), lambda r, j: (r, jnp.maximum(j - 1, 0), 0))
    shape = jax.ShapeDtypeStruct((d, L, C_B), BF16)
    return pl.pallas_call(
        body, name=name, grid=(d, nb + 1), in_specs=[wide[1], wide[1], stat, stat, *wide, *wide],
        out_specs=[wide[1], lagged, lagged], out_shape=[shape] * 3,
        scratch_shapes=[pltpu.VMEM((3 * bq, C_B), F32), pltpu.VMEM((3 * bq, C_B), F32)],
        compiler_params=_params("arbitrary", "arbitrary"),
    )(qd, dod, lsed, deltad, kd, kd, kd, vd, vd, vd)


def _dilated_spec(tr, d, width):
    return pl.BlockSpec((d, tr // d, width), lambda i: (0, i, 0))


def _perm_matrix(n, d, inverse):
    lb = n // d
    row = lax.broadcasted_iota(jnp.int32, (n, n), 0)
    col = lax.broadcasted_iota(jnp.int32, (n, n), 1)
    source = (row % d) * lb + row // d if inverse else (row % lb) * d + row // lb
    return (col == source).astype(BF16)


def _permute(p, x):
    if x.dtype == BF16:
        return jnp.dot(p, x, preferred_element_type=F32).astype(BF16)
    hi = x.astype(BF16)
    rest = x - hi.astype(F32)
    mid = rest.astype(BF16)
    lo = (rest - mid.astype(F32)).astype(BF16)
    return (jnp.dot(p, hi, preferred_element_type=F32) + jnp.dot(p, mid, preferred_element_type=F32)
            + jnp.dot(p, lo, preferred_element_type=F32))


def _store_dilated(ref, x, d):
    n = x.shape[0]
    if d == 1:
        ref[0] = x
        return
    y = _permute(_perm_matrix(n, d, False), x)
    lb = n // d
    for r in range(d):
        ref[r] = y[r * lb:(r + 1) * lb]


def _load_dilated(ref, d):
    if d == 1:
        return ref[0]
    y = jnp.concatenate([ref[r] for r in range(d)], axis=0)
    return _permute(_perm_matrix(y.shape[0], d, True), y)


def adamw(parts, w, m, v, name):
    n, R, C = parts.shape
    tr = _rows(R, ROW_TILE)

    def body(p_ref, w_ref, m_ref, v_ref, g_ref, d_ref, nm_ref, nv_ref):
        g = p_ref[0].astype(F32)
        for k in range(1, n):
            g = g + p_ref[k].astype(F32)
        mm = ADAM_B1 * m_ref[...] + (1.0 - ADAM_B1) * g
        vv = ADAM_B2 * v_ref[...] + (1.0 - ADAM_B2) * jnp.square(g)
        m_hat = mm / (1.0 - ADAM_B1 ** ADAM_STEP)
        v_hat = vv / (1.0 - ADAM_B2 ** ADAM_STEP)
        g_ref[...] = g
        d_ref[...] = -ADAM_LR * (m_hat / (jnp.sqrt(v_hat) + ADAM_EPS) + ADAM_WD * w_ref[...])
        nm_ref[...] = mm
        nv_ref[...] = vv

    spec = _row_spec(tr, C)
    shape = jax.ShapeDtypeStruct((R, C), F32)
    return pl.pallas_call(
        body, name=name, grid=(R // tr,),
        in_specs=[pl.BlockSpec((n, tr, C), lambda i: (0, i, 0)), spec, spec, spec],
        out_specs=[spec] * 4, out_shape=[shape] * 4, compiler_params=_params("parallel"),
    )(parts, w, m, v)


_ANY = pl.BlockSpec(memory_space=pl.ANY)


def _place():
    return lax.axis_index("x"), lax.axis_index("y"), lax.axis_index("c")


def _index(px, py, pc):
    return 4 * px + 2 * py + pc


class Gather:
    def __init__(self, shards):
        self.ins = list(shards)
        T = self.n = len(shards)
        self.out_shape = [jax.ShapeDtypeStruct((N_DEV, *s.shape), s.dtype) for s in shards]
        self.sems = [pltpu.SemaphoreType.DMA((T, 7)), pltpu.SemaphoreType.DMA((T, 7)), pltpu.SemaphoreType.DMA((T,))]

    def _plan(self, ins, outs, sems):
        send_sems, recv_sems, local_sems = sems
        x, y, c = _place()
        me, sibling = (x, y, c), (x, y, 1 - c)
        chips = [(1 - x, y), (x, 1 - y), (1 - x, 1 - y)]

        def copy(t, k, block, to, src=None):
            rows = outs[t].at[_index(*block)]
            return pltpu.make_async_remote_copy(
                src_ref=rows if src is None else src, dst_ref=rows, send_sem=send_sems.at[t, k],
                recv_sem=recv_sems.at[t, k], device_id=to, device_id_type=MESH)

        mine = [pltpu.make_async_copy(ins[t], outs[t].at[_index(*me)], local_sems.at[t]) for t in range(self.n)]
        first = []
        for t in range(self.n):
            first.append(copy(t, 0, me, sibling, src=ins[t]))
            first += [copy(t, 1 + j, me, (*chip, c), src=ins[t]) for j, chip in enumerate(chips)]
        return copy, mine, first, me, sibling, chips, c

    def start(self, ins, outs, sems):
        _, mine, first, *_ = self._plan(ins, outs, sems)
        for cp in mine + first:
            cp.start()

    def finish(self, ins, outs, sems):
        copy, mine, first, me, sibling, chips, c = self._plan(ins, outs, sems)
        passed = []
        for j, chip in enumerate(chips):
            for t in range(self.n):
                copy(t, 1 + j, (*chip, c), me).wait_recv()
                fwd = copy(t, 4 + j, (*chip, c), sibling)
                fwd.start()
                passed.append(fwd)
        for t in range(self.n):
            copy(t, 0, sibling, me).wait_recv()
            for j, chip in enumerate(chips):
                copy(t, 4 + j, (*chip, 1 - c), me).wait_recv()
        for cp in first + passed:
            cp.wait_send()
        for cp in mine:
            cp.wait()


class Exchange:
    def __init__(self, parts):
        self.ins = list(parts)
        T = self.n = len(parts)
        self.out_shape = [jax.ShapeDtypeStruct(p.shape, p.dtype) for p in parts]
        self.sems = [pltpu.SemaphoreType.DMA((T, 7)), pltpu.SemaphoreType.DMA((T, 7)), pltpu.SemaphoreType.DMA((T,))]

    def _plan(self, ins, outs, sems):
        send_sems, recv_sems, local_sems = sems
        x, y, c = _place()
        me = _index(x, y, c)
        copies = [pltpu.make_async_copy(ins[t].at[me], outs[t].at[me], local_sems.at[t]) for t in range(self.n)]
        for k in range(1, N_DEV):
            peer = ((x + (k >> 2)) % 2, (y + ((k >> 1) & 1)) % 2, (c + (k & 1)) % 2)
            there = _index(*peer)
            for t in range(self.n):
                copies.append(pltpu.make_async_remote_copy(
                    src_ref=ins[t].at[there], dst_ref=outs[t].at[me], send_sem=send_sems.at[t, k - 1],
                    recv_sem=recv_sems.at[t, k - 1], device_id=peer, device_id_type=MESH))
        return copies

    def start(self, ins, outs, sems):
        for cp in self._plan(ins, outs, sems):
            cp.start()

    def finish(self, ins, outs, sems):
        for cp in self._plan(ins, outs, sems):
            cp.wait()


def communicate(comm, name):
    T = comm.n

    def body(*refs):
        ins, outs, sems = refs[:T], refs[T:2 * T], refs[2 * T:]
        comm.start(ins, outs, sems)
        comm.finish(ins, outs, sems)

    return pl.pallas_call(
        body, name=name, in_specs=[_ANY] * T, out_specs=[_ANY] * T, out_shape=comm.out_shape,
        scratch_shapes=comm.sems,
    )(*comm.ins)


def all_gather(shards, name):
    return communicate(Gather(shards), name)


def exchange(parts, name):
    return communicate(Exchange(parts), name)


def _row(v):
    return v.reshape(1, -1)


def _hosted(res):
    return res if isinstance(res, tuple) else (res, None)


def mix_forward(x, w, p, tables, comm=None):
    gm = p["g_mix"]
    h1, h1t = rms_fwd(x, _row(p["g_pre_mix"]), "rms_pre_mix")
    proj, got = _hosted(matmul(h1, w["win"], "nn", BF16, "proj", comm=comm))
    ya = conv_fwd(proj, p["cw"], _row(p["conv_b"]), _row(p["conv_ln_g"]), _row(p["conv_ln_b"]), _row(gm[:C_A]),
                  "conv_fwd")
    dil = rope_fwd(proj, tables, "rope_fwd")
    os_, lses = [], []
    for d, (qd, kd, vd) in zip(DILATIONS, dil):
        o, lse = attn_fwd_pattern(qd, kd, vd, f"attn_fwd_d{d}")
        os_.append(o)
        lses.append(lse)
    yc = pool_fwd(proj, p["pool_w"], _row(p["pool_scale"]), _row(gm[C_A + C_B:]), "pool_fwd")
    y, out, lse = attn_combine(os_, lses, _row(gm[C_A:C_A + C_B]), ya, yc, "attn_combine")
    z = matmul(y, w["wout"], "nn", BF16, "mix_out")
    x2 = norm_residual(z, x, _row(p["g_post_mix"]), "res_mix")
    return x2, dict(x=x, h1t=h1t, proj=proj, dil=dil, out=out, lse=lse, y=y, z=z), got


def ffn_forward(x2, w, p, comm_in=None, comm_out=None):
    h2, h2t = rms_fwd(x2, _row(p["g_pre_ffn"]), "rms_pre_ffn")
    gu, got_in = _hosted(matmul(h2, w["wgu"], "nn", BF16, "ffn_in", comm=comm_in))
    a, at = swiglu_fwd(gu, "swiglu_fwd")
    f, got_out = _hosted(matmul(a, w["wd"], "nn", BF16, "ffn_out", tk=5632, comm=comm_out))
    x3 = norm_residual(f, x2, _row(p["g_post_ffn"]), "res_ffn")
    return x3, dict(x2=x2, h2t=h2t, gu=gu, at=at, f=f), got_in, got_out


def _to_blocks(g, by_columns):
    if by_columns:
        rows = g.shape[0]
        return g.reshape(rows, N_DEV, -1).transpose(1, 0, 2).astype(BF16)
    return g.reshape(N_DEV, -1, g.shape[1]).astype(BF16)


def _from_blocks(b, by_columns):
    if by_columns:
        return b.transpose(1, 0, 2).reshape(b.shape[1], -1)
    return b.reshape(-1, b.shape[2])


def _exchange_of(g, by_columns):
    return Exchange([_to_blocks(g, by_columns)]) if g is not None else None


def ffn_backward(dx3, s, w, p, ride_act=None, ride_down=None):
    df, dg_post_ffn = rms_bwd(s["f"], dx3, _row(p["g_post_ffn"]), None, BF16, "rms_bwd_post_ffn")
    da, got_act = _hosted(matmul(df, w["wd"], "nt", BF16, "d_ffn_act", comm=ride_act))
    dwd, got_down = _hosted(matmul(s["at"], df, "nn", F32, "dw_down", tm=1408, tn=1024, tk=2048, comm=ride_down))
    dgu = swiglu_bwd(s["gu"], da, "swiglu_bwd")
    dh2, got_wd = matmul(dgu, w["wgu"], "nt", BF16, "d_ffn_in", tk=5632, comm=_exchange_of(dwd, False))
    dwgu = matmul(s["h2t"], dgu, "nn", F32, "dw_gate_up", tm=1024, tn=1408, tk=2048)
    dx2, dg_pre_ffn = rms_bwd(s["x2"], dh2, _row(p["g_pre_ffn"]), dx3, F32, "rms_bwd_pre_ffn")
    return dx2, dict(dwgu=dwgu, parts_wd=got_wd[0], got_act=got_act, got_down=got_down,
                     g_pre_ffn=dg_pre_ffn[0], g_post_ffn=dg_post_ffn[0])


def mix_backward(dx2, s, w, p, tables, dwgu):
    gm = p["g_mix"]
    dz, dg_post_mix = rms_bwd(s["z"], dx2, _row(p["g_post_mix"]), None, BF16, "rms_bwd_post_mix")
    dy = matmul(dz, w["wout"], "nt", BF16, "d_mix")
    dwout = matmul(s["y"], dz, "tn", F32, "dw_out", tm=1024, tn=1024, tk=512)
    dconv, dcw, dcb, dlg, dlb, dgm_a = conv_bwd(
        s["proj"], dy, p["cw"], _row(p["conv_b"]), _row(p["conv_ln_g"]), _row(p["conv_ln_b"]), _row(gm[:C_A]),
        "conv_bwd")
    stats, dgm_b = attn_out_bwd(s["out"], s["lse"], dy, _row(gm[C_A:C_A + C_B]), "attn_out_bwd")
    dqs, dks, dvs = [], [], []
    for d, (qd, kd, vd), (dod, lsed, deltad) in zip(DILATIONS, s["dil"], stats):
        dq, dk, dv = attn_bwd_pattern(qd, kd, vd, dod, lsed, deltad, f"attn_bwd_d{d}")
        dqs.append(dq)
        dks.append(dk)
        dvs.append(dv)
    du, dpw, dps, dgm_c = pool_bwd(s["proj"], dy, p["pool_w"], _row(p["pool_scale"]), _row(gm[C_A + C_B:]),
                                   "pool_bwd")
    dproj = rope_bwd(dqs, dks, dvs, tables, dconv, du, "rope_bwd")
    F = dwgu.shape[1] // 2
    dh1, got_gate = matmul(dproj, w["win"], "nt", BF16, "d_proj", tk=4608, comm=_exchange_of(dwgu[:, :F], True))
    dwin, got_up = matmul(s["h1t"], dproj, "nn", F32, "dw_in", tm=1024, tn=1152, tk=2048,
                          comm=_exchange_of(dwgu[:, F:], True))
    dx, dg_pre_mix = rms_bwd(s["x"], dh1, _row(p["g_pre_mix"]), dx2, F32, "rms_bwd_pre_mix")
    return dx, dict(
        dwin=dwin, dwout=dwout, parts_gate=got_gate[0], parts_up=got_up[0],
        conv_w=dcw[:CONV_WIDTH], conv_b=dcb[0], conv_ln_g=dlg[0], conv_ln_b=dlb[0], pool_w=dpw, pool_scale=dps[0],
        g_mix=jnp.concatenate([dgm_a[0], dgm_b[0], dgm_c[0]]), g_pre_mix=dg_pre_mix[0], g_post_mix=dg_post_mix[0])


WEIGHTS = ["w_in", "conv_w", "conv_b", "conv_ln_g", "conv_ln_b", "pool_w", "pool_scale", "g_mix", "w_out", "g_pre_mix",
           "g_post_mix", "g_pre_ffn", "g_post_ffn", "w_gate", "w_up", "w_down"]
BIG = ["w_in", "w_out", "w_gate", "w_up", "w_down"]
REPLICATED = ["conv_b", "conv_ln_g", "conv_ln_b", "pool_w", "pool_scale", "g_mix", "g_pre_mix", "g_post_mix",
              "g_pre_ffn", "g_post_ffn"]
PACK_ROWS = 256


def adamw_layer(parts, w, m, v, layer, prev, name):
    n, R, C = parts.shape
    tr = _rows(R, ROW_TILE)

    def body(p_ref, w_ref, m_ref, v_ref, *rest):
        g_ref, d_ref, nm_ref, nv_ref = rest[-4:]
        g = p_ref[0].astype(F32)
        for k in range(1, n):
            g = g + p_ref[k].astype(F32)
        mm = ADAM_B1 * m_ref[...] + (1.0 - ADAM_B1) * g
        vv = ADAM_B2 * v_ref[...] + (1.0 - ADAM_B2) * jnp.square(g)
        m_hat = mm / (1.0 - ADAM_B1 ** ADAM_STEP)
        v_hat = vv / (1.0 - ADAM_B2 ** ADAM_STEP)
        g_ref[...] = g
        d_ref[...] = -ADAM_LR * (m_hat / (jnp.sqrt(v_hat) + ADAM_EPS) + ADAM_WD * w_ref[...])
        nm_ref[...] = mm
        nv_ref[...] = vv

    spec = pl.BlockSpec((None, tr, C), lambda i: (layer, i, 0))
    shape = jax.ShapeDtypeStruct(w.shape, F32)
    prev = list(prev) if prev is not None else []
    return pl.pallas_call(
        body, name=name, grid=(R // tr,),
        in_specs=[pl.BlockSpec((n, tr, C), lambda i: (0, i, 0)), spec, spec, spec] + [_ANY] * len(prev),
        out_specs=[spec] * 4, out_shape=[shape] * 4,
        input_output_aliases={4 + k: k for k in range(len(prev))}, compiler_params=_params("parallel"),
    )(parts, w, m, v, *prev)


def _pack(arrays):
    flat = jnp.concatenate([a.reshape(-1).astype(F32) for a in arrays])
    unit = PACK_ROWS * LANES
    padded = -(-flat.shape[0] // unit) * unit
    return jnp.pad(flat, (0, padded - flat.shape[0])).reshape(-1, LANES)


def _unpack(packed, like):
    flat = packed.reshape(-1)
    out, at = [], 0
    for a in like:
        out.append(flat[at:at + a.size].reshape(a.shape))
        at += a.size
    return out


def kernel(x, w_in, conv_w, conv_b, conv_ln_g, conv_ln_b, pool_w, pool_scale, g_mix, w_out, g_pre_mix, g_post_mix, g_pre_ffn, g_post_ffn, w_gate, w_up, w_down, loss_target, m_w_in, m_conv_w, m_conv_b, m_conv_ln_g, m_conv_ln_b, m_pool_w, m_pool_scale, m_g_mix, m_w_out, m_g_pre_mix, m_g_post_mix, m_g_pre_ffn, m_g_post_ffn, m_w_gate, m_w_up, m_w_down, v_w_in, v_conv_w, v_conv_b, v_conv_ln_g, v_conv_ln_b, v_pool_w, v_pool_scale, v_g_mix, v_w_out, v_g_pre_mix, v_g_post_mix, v_g_pre_ffn, v_g_post_ffn, v_w_gate, v_w_up, v_w_down):
    w = dict(w_in=w_in, conv_w=conv_w, conv_b=conv_b, conv_ln_g=conv_ln_g, conv_ln_b=conv_ln_b, pool_w=pool_w,
             pool_scale=pool_scale, g_mix=g_mix, w_out=w_out, g_pre_mix=g_pre_mix, g_post_mix=g_post_mix,
             g_pre_ffn=g_pre_ffn, g_post_ffn=g_post_ffn, w_gate=w_gate, w_up=w_up, w_down=w_down)
    m = dict(w_in=m_w_in, conv_w=m_conv_w, conv_b=m_conv_b, conv_ln_g=m_conv_ln_g, conv_ln_b=m_conv_ln_b,
             pool_w=m_pool_w, pool_scale=m_pool_scale, g_mix=m_g_mix, w_out=m_w_out, g_pre_mix=m_g_pre_mix,
             g_post_mix=m_g_post_mix, g_pre_ffn=m_g_pre_ffn, g_post_ffn=m_g_post_ffn, w_gate=m_w_gate, w_up=m_w_up,
             w_down=m_w_down)
    v = dict(w_in=v_w_in, conv_w=v_conv_w, conv_b=v_conv_b, conv_ln_g=v_conv_ln_g, conv_ln_b=v_conv_ln_b,
             pool_w=v_pool_w, pool_scale=v_pool_scale, g_mix=v_g_mix, w_out=v_w_out, g_pre_mix=v_g_pre_mix,
             g_post_mix=v_g_post_mix, g_pre_ffn=v_g_pre_ffn, g_post_ffn=v_g_post_ffn, w_gate=v_w_gate, w_up=v_w_up,
             w_down=v_w_down)
    depth = w_in.shape[0]
    xs, target = x[0], loss_target[0]
    S, D = xs.shape
    tables = rope_tables(S)

    cw_shard = jnp.pad(conv_w, ((0, 0), (0, 1), (0, 0)))
    cw_all = all_gather([cw_shard.reshape(-1, LANES)], "gather_conv_w")[0]
    cw_full = cw_all.reshape(N_DEV, depth, 32, -1).transpose(1, 2, 0, 3).reshape(depth, 32, C_A)
    small = []
    for l in range(depth):
        small.append({n: w[n][l] for n in REPLICATED})
        small[l]["cw"] = cw_full[l]

    def mix_shards(l):
        return Gather([w["w_in"][l].astype(BF16), w["w_out"][l].astype(BF16)])

    def ffn_shards(l):
        return Gather([w[n][l].astype(BF16) for n in ("w_gate", "w_up", "w_down")])

    def mix_weights(blocks):
        return dict(win=_from_blocks(blocks[0], True), wout=_from_blocks(blocks[1], False))

    def ffn_weights(blocks):
        return dict(wgu=jnp.concatenate([_from_blocks(blocks[0], True), _from_blocks(blocks[1], True)], axis=1),
                    wd=_from_blocks(blocks[2], False))

    w_mix = [None] * depth
    w_ffn = [None] * depth
    saved_mix, saved_ffn = [None] * depth, [None] * depth
    w_mix[0] = mix_weights(communicate(mix_shards(0), "gather_mix_weights"))
    h = xs
    for l in range(depth):
        x2, saved_mix[l], got = mix_forward(h, w_mix[l], small[l], tables, ffn_shards(0) if l == 0 else None)
        if l == 0:
            w_ffn[0] = ffn_weights(got)
        more = l + 1 < depth
        h, saved_ffn[l], got_in, got_out = ffn_forward(
            x2, w_ffn[l], small[l], ffn_shards(l + 1) if more else None, mix_shards(l + 1) if more else None)
        if more:
            w_ffn[l + 1] = ffn_weights(got_in)
            w_mix[l + 1] = mix_weights(got_out)

    dh, sq = loss_head(h, target, "loss_head")
    loss = lax.psum(0.5 * jnp.sum(sq) / D, ("x", "y", "c"))

    big_out = {n: None for n in BIG}

    def update(n, l, parts):
        big_out[n] = adamw_layer(parts, w[n], m[n], v[n], l, big_out[n], f"adamw_{n}_layer{l}")

    small_grads = [None] * depth
    carried = None
    for l in reversed(range(depth)):
        ride_act = _exchange_of(carried[0], True) if carried else None
        ride_down = _exchange_of(carried[1], False) if carried else None
        dx2, gf = ffn_backward(dh, saved_ffn[l], w_ffn[l], small[l], ride_act, ride_down)
        if carried:
            update("w_in", l + 1, gf["got_act"][0])
            update("w_out", l + 1, gf["got_down"][0])
        update("w_down", l, gf["parts_wd"])
        dh, gm_ = mix_backward(dx2, saved_mix[l], w_mix[l], small[l], tables, gf["dwgu"])
        update("w_gate", l, gm_["parts_gate"])
        update("w_up", l, gm_["parts_up"])
        carried = (gm_["dwin"], gm_["dwout"])
        small_grads[l] = {**gf, **gm_}
    last = exchange([_to_blocks(carried[0], True), _to_blocks(carried[1], False)], "exchange_last_grads")
    update("w_in", 0, last[0])
    update("w_out", 0, last[1])

    names = REPLICATED + ["conv_w"]
    stacked = [jnp.stack([small_grads[l][n] for l in range(depth)]) for n in names]
    partial = all_gather([_pack(stacked)], "gather_small_grads")[0]
    zeros = jnp.zeros_like(stacked[-1])
    packed = adamw(partial, _pack([w[n] for n in REPLICATED] + [zeros]), _pack([m[n] for n in REPLICATED] + [zeros]),
                   _pack([v[n] for n in REPLICATED] + [zeros + 1.0]), "adamw_replicated")
    small_out = [_unpack(o, stacked) for o in packed]
    out = {n: tuple(o[i] for o in small_out) for i, n in enumerate(REPLICATED)}
    width = conv_w.shape[2]
    g_cw = lax.dynamic_slice_in_dim(small_out[0][-1], _index(*_place()) * width, width, axis=2)
    cw_res = adamw(g_cw.reshape(1, -1, LANES), conv_w.reshape(-1, LANES), m["conv_w"].reshape(-1, LANES),
                   v["conv_w"].reshape(-1, LANES), "adamw_conv_w")
    out["conv_w"] = tuple(o.reshape(conv_w.shape) for o in cw_res)
    for n in BIG:
        out[n] = tuple(big_out[n])
    results = [loss, dh[None]]
    for k in range(4):
        results += [out[n][k] for n in WEIGHTS]
    return tuple(results)
```

```python
import functools
import math

import jax
import jax.numpy as jnp
from jax import lax
from jax.experimental import pallas as pl
from jax.experimental.pallas import tpu as pltpu

F32 = jnp.float32
BF16 = jnp.bfloat16

N_DEV = 8
DEPTH = 4
EPS = 1e-6
NEG = -1e30

C_A = 512
N_HEADS = 16
HEAD_DIM = 64
C_B = N_HEADS * HEAD_DIM
C_C = 512
POOL_WINDOWS = (2, 4, 8, 16)
C_G = C_C // len(POOL_WINDOWS)
CONV_WIDTH = 31
CONV_HALF = CONV_WIDTH // 2
DILATIONS = (1, 4, 16)
ATTN_HALF = 64
ROT_DIM = HEAD_DIM // 4
ROPE_THETA = 500000.0

ADAM_LR = 0.001
ADAM_B1 = 0.9
ADAM_B2 = 0.999
ADAM_EPS = 1e-08
ADAM_WD = 0.01
ADAM_STEP = 10

LANES = 128
SUBLANES = 8
VMEM_LIMIT = 56 * 1024 * 1024
ROW_TILE = 256
SEQ_TILE = 256
ATTN_BLOCK = 128
MESH = pl.DeviceIdType.MESH


def _params(*sem):
    return pltpu.CompilerParams(dimension_semantics=sem, vmem_limit_bytes=VMEM_LIMIT)


def _tile(n, target):
    if n <= target:
        return n
    t = (target // LANES) * LANES
    while t >= LANES:
        if n % t == 0:
            return t
        t -= LANES
    return n


def _rows(n, target):
    t = min(n, target)
    while n % t:
        t //= 2
    return t


def matmul(a, b, mode, out_dtype, name, tm=1024, tn=512, tk=2048, comm=None):
    if mode == "nn":
        (M, K), (_, N) = a.shape, b.shape
    elif mode == "nt":
        (M, K), (N, _) = a.shape, b.shape
    else:
        (K, M), (_, N) = a.shape, b.shape
    tm, tn, tk = _tile(M, tm), _tile(N, tn), _tile(K, tk)
    nm, nn, nk = M // tm, N // tn, K // tk
    dims = {"nn": (((1,), (0,)), ((), ())), "nt": (((1,), (1,)), ((), ())), "tn": (((0,), (0,)), ((), ()))}[mode]
    nc = comm.n if comm is not None else 0

    def body(*refs):
        a_ref, b_ref = refs[:2]
        c_ins, o_ref, c_outs = refs[2:2 + nc], refs[2 + nc], refs[3 + nc:3 + 2 * nc]
        scratch = refs[3 + 2 * nc:]
        acc, sems = (scratch[:1], scratch[1:]) if nk > 1 else ((), scratch)
        i, j, k = pl.program_id(0), pl.program_id(1), pl.program_id(2)
        if comm is not None:
            @pl.when((i == 0) & (j == 0) & (k == 0))
            def _():
                comm.start(c_ins, c_outs, sems)

        p = lax.dot_general(a_ref[...], b_ref[...], dims, preferred_element_type=F32)
        if nk == 1:
            o_ref[...] = p.astype(o_ref.dtype)
        else:
            acc_ref, = acc

            @pl.when(k == 0)
            def _():
                acc_ref[...] = p

            @pl.when(k > 0)
            def _():
                acc_ref[...] += p

            @pl.when(k == nk - 1)
            def _():
                o_ref[...] = acc_ref[...].astype(o_ref.dtype)

        if comm is not None:
            @pl.when((i == nm - 1) & (j == nn - 1) & (k == nk - 1))
            def _():
                comm.finish(c_ins, c_outs, sems)

    if mode == "nn":
        a_spec = pl.BlockSpec((tm, tk), lambda i, j, k: (i, k))
        b_spec = pl.BlockSpec((tk, tn), lambda i, j, k: (k, j))
    elif mode == "nt":
        a_spec = pl.BlockSpec((tm, tk), lambda i, j, k: (i, k))
        b_spec = pl.BlockSpec((tn, tk), lambda i, j, k: (j, k))
    else:
        a_spec = pl.BlockSpec((tk, tm), lambda i, j, k: (k, i))
        b_spec = pl.BlockSpec((tk, tn), lambda i, j, k: (k, j))
    o_spec = pl.BlockSpec((tm, tn), lambda i, j, k: (i, j))
    o_shape = jax.ShapeDtypeStruct((M, N), out_dtype)
    acc_shape = [pltpu.VMEM((tm, tn), F32)] if nk > 1 else []
    if comm is None:
        return pl.pallas_call(
            body, name=name, grid=(nm, nn, nk), in_specs=[a_spec, b_spec], out_specs=o_spec, out_shape=o_shape,
            scratch_shapes=acc_shape, compiler_params=_params("parallel", "parallel", "arbitrary"),
        )(a, b)
    res = pl.pallas_call(
        body, name=name, grid=(nm, nn, nk), in_specs=[a_spec, b_spec] + [_ANY] * nc,
        out_specs=[o_spec] + [_ANY] * nc, out_shape=[o_shape] + comm.out_shape,
        scratch_shapes=acc_shape + comm.sems, compiler_params=_params("arbitrary", "arbitrary", "arbitrary"),
    )(a, b, *comm.ins)
    return res[0], res[1:]


def _rms(t):
    return t * lax.rsqrt(jnp.mean(t * t, axis=-1, keepdims=True) + EPS)


def _rms_bwd(t, dn):
    r = lax.rsqrt(jnp.mean(t * t, axis=-1, keepdims=True) + EPS)
    n = t * r
    return r * (dn - n * jnp.mean(dn * n, axis=-1, keepdims=True)), n


def _row_spec(tr, d):
    return pl.BlockSpec((tr, d), lambda i: (i, 0))


def _vec_spec(d):
    return pl.BlockSpec((1, d), lambda i: (0, 0))


def _col_spec(d, tr):
    return pl.BlockSpec((d, tr), lambda i: (0, i))


def rms_fwd(x, g, name):
    S, D = x.shape
    tr = _rows(S, ROW_TILE)

    def body(x_ref, g_ref, o_ref, ot_ref):
        h = _rms(x_ref[...].astype(F32)) * g_ref[...]
        o_ref[...] = h.astype(o_ref.dtype)
        ot_ref[...] = h.T.astype(ot_ref.dtype)

    return pl.pallas_call(
        body, name=name, grid=(S // tr,), in_specs=[_row_spec(tr, D), _vec_spec(D)],
        out_specs=[_row_spec(tr, D), _col_spec(D, tr)],
        out_shape=[jax.ShapeDtypeStruct((S, D), BF16), jax.ShapeDtypeStruct((D, S), BF16)],
        compiler_params=_params("parallel"),
    )(x, g)


def norm_residual(z, x, g, name):
    S, D = x.shape
    tr = _rows(S, ROW_TILE)

    def body(z_ref, x_ref, g_ref, o_ref):
        o_ref[...] = x_ref[...] + _rms(z_ref[...].astype(F32)) * g_ref[...]

    return pl.pallas_call(
        body, name=name, grid=(S // tr,), in_specs=[_row_spec(tr, D), _row_spec(tr, D), _vec_spec(D)],
        out_specs=_row_spec(tr, D), out_shape=jax.ShapeDtypeStruct((S, D), F32), compiler_params=_params("parallel"),
    )(z, x, g)


def rms_bwd(t, dy, g, res, out_dtype, name):
    S, D = t.shape
    tr = _rows(S, ROW_TILE)
    has_res = res is not None

    def body(t_ref, dy_ref, g_ref, *rest):
        if has_res:
            res_ref, dt_ref, dg_ref = rest
        else:
            dt_ref, dg_ref = rest
        dyv = dy_ref[...].astype(F32)
        dt, n = _rms_bwd(t_ref[...].astype(F32), dyv * g_ref[...])
        if has_res:
            dt = dt + res_ref[...]
        dt_ref[...] = dt.astype(dt_ref.dtype)

        @pl.when(pl.program_id(0) == 0)
        def _():
            dg_ref[...] = jnp.zeros_like(dg_ref)

        dg_ref[...] += jnp.sum(dyv * n, axis=0, keepdims=True)

    ins = [t, dy, g] + ([res] if has_res else [])
    specs = [_row_spec(tr, D), _row_spec(tr, D), _vec_spec(D)] + ([_row_spec(tr, D)] if has_res else [])
    return pl.pallas_call(
        body, name=name, grid=(S // tr,), in_specs=specs, out_specs=[_row_spec(tr, D), _vec_spec(D)],
        out_shape=[jax.ShapeDtypeStruct((S, D), out_dtype), jax.ShapeDtypeStruct((1, D), F32)],
        compiler_params=_params("arbitrary"),
    )(*ins)


def swiglu_fwd(gu, name):
    S, F2 = gu.shape
    F = F2 // 2
    tr = _rows(S, ROW_TILE)

    def body(g_ref, u_ref, o_ref, ot_ref):
        g = g_ref[...].astype(F32)
        a = g * jax.nn.sigmoid(g) * u_ref[...].astype(F32)
        o_ref[...] = a.astype(o_ref.dtype)
        ot_ref[...] = a.T.astype(ot_ref.dtype)

    return pl.pallas_call(
        body, name=name, grid=(S // tr,),
        in_specs=[pl.BlockSpec((tr, F), lambda i: (i, 0)), pl.BlockSpec((tr, F), lambda i: (i, 1))],
        out_specs=[_row_spec(tr, F), _col_spec(F, tr)],
        out_shape=[jax.ShapeDtypeStruct((S, F), BF16), jax.ShapeDtypeStruct((F, S), BF16)],
        compiler_params=_params("parallel"),
    )(gu, gu)


def swiglu_bwd(gu, da, name):
    S, F2 = gu.shape
    F = F2 // 2
    tr = _rows(S, ROW_TILE)

    def body(g_ref, u_ref, da_ref, o_ref):
        g = g_ref[...].astype(F32)
        u = u_ref[...].astype(F32)
        dav = da_ref[...].astype(F32)
        sig = jax.nn.sigmoid(g)
        o_ref[:, :F] = (dav * u * (sig * (1.0 + g * (1.0 - sig)))).astype(o_ref.dtype)
        o_ref[:, F:] = (dav * (g * sig)).astype(o_ref.dtype)

    return pl.pallas_call(
        body, name=name, grid=(S // tr,),
        in_specs=[pl.BlockSpec((tr, F), lambda i: (i, 0)), pl.BlockSpec((tr, F), lambda i: (i, 1)), _row_spec(tr, F)],
        out_specs=_row_spec(tr, F2), out_shape=jax.ShapeDtypeStruct((S, F2), BF16), compiler_params=_params("parallel"),
    )(gu, gu, da)


def loss_head(y, target, name):
    S, D = y.shape
    tr = _rows(S, ROW_TILE)

    def body(y_ref, t_ref, dy_ref, sq_ref):
        e = y_ref[...] - t_ref[...]
        dy_ref[...] = e * (1.0 / D)

        @pl.when(pl.program_id(0) == 0)
        def _():
            sq_ref[...] = jnp.zeros_like(sq_ref)

        sq_ref[...] += jnp.sum(e * e, axis=0, keepdims=True)

    return pl.pallas_call(
        body, name=name, grid=(S // tr,), in_specs=[_row_spec(tr, D), _row_spec(tr, D)],
        out_specs=[_row_spec(tr, D), _vec_spec(D)],
        out_shape=[jax.ShapeDtypeStruct((S, D), F32), jax.ShapeDtypeStruct((1, D), F32)],
        compiler_params=_params("arbitrary"),
    )(y, target)


def _halo_specs(bs, halo, width, col, n_rows):
    per = bs // halo
    last = n_rows // halo - 1
    cur = pl.BlockSpec((bs, width), lambda i: (i, col))
    prev = pl.BlockSpec((halo, width), lambda i: (jnp.maximum(i * per - 1, 0), col))
    nxt = pl.BlockSpec((halo, width), lambda i: (jnp.minimum((i + 1) * per, last), col))
    return prev, cur, nxt


TAP_CHUNK = 32


def _build_phases(ref, phases_ref):
    total = ref.shape[0] - SUBLANES
    for b in range(SUBLANES):
        phases_ref[b, 0:total, :] = ref[pl.ds(b, total), :]


def _tap_rows(phases_ref, off, n):
    b = off % SUBLANES
    return phases_ref[b, off - b:off - b + n, :]


def _conv_taps(phases_ref, offsets, n, weights_ref, init, out_ref):
    for r0 in range(0, n, TAP_CHUNK):
        rows = min(TAP_CHUNK, n - r0)
        acc = jnp.zeros((rows, C_A), F32) + init
        for t, off in offsets.items():
            acc = acc + weights_ref[t:t + 1, :] * _tap_rows(phases_ref, off + r0, rows)
        out_ref[r0:r0 + rows, :] = acc


def _glu(a, g):
    return a.astype(F32) * jax.nn.sigmoid(g.astype(F32))


def _layernorm_silu(c, lg, lb):
    mu = jnp.mean(c, axis=-1, keepdims=True)
    cc = c - mu
    rstd = lax.rsqrt(jnp.mean(cc * cc, axis=-1, keepdims=True) + EPS)
    xh = cc * rstd
    ln = xh * lg + lb
    sig = jax.nn.sigmoid(ln)
    return xh, rstd, ln, sig


def conv_fwd(proj, cw, cb, lg, lb, gm, name):
    S = proj.shape[0]
    bs = _rows(S, SEQ_TILE)
    nb = S // bs
    H = 16

    def body(ap, ac, an, gp, gc, gn, cw_ref, cb_ref, lg_ref, lb_ref, gm_ref, y_ref, win_ref, phases_ref, c_ref):
        i = pl.program_id(0)
        win_ref[0:H, :] = jnp.where(i > 0, _glu(ap[...], gp[...]), 0.0)
        win_ref[H:H + bs, :] = _glu(ac[...], gc[...])
        win_ref[H + bs:2 * H + bs, :] = jnp.where(i < nb - 1, _glu(an[...], gn[...]), 0.0)
        win_ref[bs + 2 * H:, :] = jnp.zeros((SUBLANES, C_A), F32)
        _build_phases(win_ref, phases_ref)
        _conv_taps(phases_ref, {t: H - CONV_HALF + t for t in range(CONV_WIDTH)}, bs, cw_ref, cb_ref[...], c_ref)
        _, _, ln, sig = _layernorm_silu(c_ref[...], lg_ref[...], lb_ref[...])
        y_ref[...] = (_rms(ln * sig) * gm_ref[...]).astype(y_ref.dtype)

    a_specs = _halo_specs(bs, H, C_A, 0, S)
    g_specs = _halo_specs(bs, H, C_A, 1, S)
    vec = _vec_spec(C_A)
    return pl.pallas_call(
        body, name=name, grid=(nb,),
        in_specs=[*a_specs, *g_specs, pl.BlockSpec((32, C_A), lambda i: (0, 0)), vec, vec, vec, vec],
        out_specs=_row_spec(bs, C_A), out_shape=jax.ShapeDtypeStruct((S, C_A), BF16),
        scratch_shapes=[pltpu.VMEM((bs + 2 * H + SUBLANES, C_A), F32), pltpu.VMEM((SUBLANES, bs + 2 * H, C_A), F32),
                        pltpu.VMEM((bs, C_A), F32)],
        compiler_params=_params("parallel"),
    )(proj, proj, proj, proj, proj, proj, cw, cb, lg, lb, gm)


def conv_bwd(proj, dy, cw, cb, lg, lb, gm, name):
    S = proj.shape[0]
    bs = _rows(S, SEQ_TILE)
    nb = S // bs
    H = 32
    HC = 16
    bc = bs + 2 * HC

    def body(ap, ac, an, gp, gc, gn, dp, dc_, dn, cw_ref, cb_ref, lg_ref, lb_ref, gm_ref,
             dproj_ref, dcw_ref, dcb_ref, dlg_ref, dlb_ref, dgm_ref, win_ref, dcs_ref, phases_ref, c_ref):
        i = pl.program_id(0)
        win_ref[0:H, :] = jnp.where(i > 0, _glu(ap[...], gp[...]), 0.0)
        win_ref[H:H + bs, :] = _glu(ac[...], gc[...])
        win_ref[H + bs:2 * H + bs, :] = jnp.where(i < nb - 1, _glu(an[...], gn[...]), 0.0)
        win_ref[bs + 2 * H:, :] = jnp.zeros((SUBLANES, C_A), F32)
        _build_phases(win_ref, phases_ref)
        _conv_taps(phases_ref, {t: H - HC - CONV_HALF + t for t in range(CONV_WIDTH)}, bc, cw_ref, cb_ref[...], c_ref)
        xh, rstd, ln, sig = _layernorm_silu(c_ref[...], lg_ref[...], lb_ref[...])
        ya = ln * sig
        dyv = jnp.concatenate([dp[...], dc_[...], dn[...]], axis=0).astype(F32)
        dya, n = _rms_bwd(ya, dyv * gm_ref[...])
        dln = dya * (sig * (1.0 + ln * (1.0 - sig)))
        dxh = dln * lg_ref[...]
        dcv = rstd * (dxh - jnp.mean(dxh, axis=-1, keepdims=True) - xh * jnp.mean(dxh * xh, axis=-1, keepdims=True))
        pos = i * bs - HC + lax.broadcasted_iota(jnp.int32, (bc, 1), 0)
        dcv = jnp.where((pos >= 0) & (pos < S), dcv, 0.0)
        dcs_ref[0:bc, :] = dcv
        dcs_ref[bc:, :] = jnp.zeros((SUBLANES, C_A), F32)

        @pl.when(i == 0)
        def _():
            for r in (dcw_ref, dcb_ref, dlg_ref, dlb_ref, dgm_ref):
                r[...] = jnp.zeros_like(r)

        mid = slice(HC, HC + bs)
        dcb_ref[...] += jnp.sum(dcv[mid], axis=0, keepdims=True)
        dlg_ref[...] += jnp.sum((dln * xh)[mid], axis=0, keepdims=True)
        dlb_ref[...] += jnp.sum(dln[mid], axis=0, keepdims=True)
        dgm_ref[...] += jnp.sum((dyv * n)[mid], axis=0, keepdims=True)
        dcm = dcv[mid]
        for t in range(CONV_WIDTH):
            dcw_ref[t:t + 1, :] += jnp.sum(dcm * _tap_rows(phases_ref, H - CONV_HALF + t, bs), axis=0, keepdims=True)
        _build_phases(dcs_ref, phases_ref)
        _conv_taps(phases_ref, {t: HC + CONV_HALF - t for t in range(CONV_WIDTH)}, bs, cw_ref, 0.0, c_ref)
        dh = c_ref[0:bs, :]
        a = ac[...].astype(F32)
        sg = jax.nn.sigmoid(gc[...].astype(F32))
        dproj_ref[:, :C_A] = (dh * sg).astype(dproj_ref.dtype)
        dproj_ref[:, C_A:] = (dh * a * sg * (1.0 - sg)).astype(dproj_ref.dtype)

    a_specs = _halo_specs(bs, H, C_A, 0, S)
    g_specs = _halo_specs(bs, H, C_A, 1, S)
    d_specs = _halo_specs(bs, HC, C_A, 0, S)
    vec = _vec_spec(C_A)
    full = pl.BlockSpec((32, C_A), lambda i: (0, 0))
    vshape = jax.ShapeDtypeStruct((1, C_A), F32)
    return pl.pallas_call(
        body, name=name, grid=(nb,),
        in_specs=[*a_specs, *g_specs, *d_specs, full, vec, vec, vec, vec],
        out_specs=[_row_spec(bs, 2 * C_A), full, vec, vec, vec, vec],
        out_shape=[jax.ShapeDtypeStruct((S, 2 * C_A), BF16), jax.ShapeDtypeStruct((32, C_A), F32),
                   vshape, vshape, vshape, vshape],
        scratch_shapes=[pltpu.VMEM((bs + 2 * H + SUBLANES, C_A), F32), pltpu.VMEM((bc + SUBLANES, C_A), F32),
                        pltpu.VMEM((SUBLANES, bs + 2 * H, C_A), F32), pltpu.VMEM((bc, C_A), F32)],
        compiler_params=_params("arbitrary"),
    )(proj, proj, proj, proj, proj, proj, dy, dy, dy, cw, cb, lg, lb, gm)


POOL_HALO = 16


def _shift(x, k):
    n = x.shape[0]
    return pltpu.roll(x, (-k) % n, axis=0)


def _pool_means(u, pos, S):
    w2 = _shift(u, -1) + u
    w4 = _shift(w2, -1) + _shift(w2, 1)
    w8 = _shift(w4, -2) + _shift(w4, 2)
    w16 = _shift(w8, -4) + _shift(w8, 4)
    sums = (w2, w4, w8, w16)
    lane = lax.broadcasted_iota(jnp.int32, (1, C_C), 1)
    total = jnp.zeros_like(u)
    inv = jnp.zeros_like(u)
    for gi, win in enumerate(POOL_WINDOWS):
        cnt = jnp.minimum(pos + (win - win // 2), S) - jnp.maximum(pos - win // 2, 0)
        icnt = 1.0 / jnp.maximum(cnt, 1).astype(F32)
        sel = (lane >= gi * C_G) & (lane < (gi + 1) * C_G)
        total = jnp.where(sel, sums[gi], total)
        inv = jnp.where(sel, icnt, inv)
    return total * inv - u, inv


def _pool_adjoint(e):
    v2 = e + _shift(e, 1)
    v4 = _shift(v2, -1) + _shift(v2, 1)
    v8 = _shift(v4, -2) + _shift(v4, 2)
    v16 = _shift(v8, -4) + _shift(v8, 4)
    sums = (v2, v4, v8, v16)
    lane = lax.broadcasted_iota(jnp.int32, (1, C_C), 1)
    out = jnp.zeros_like(e)
    for gi in range(len(POOL_WINDOWS)):
        sel = (lane >= gi * C_G) & (lane < (gi + 1) * C_G)
        out = jnp.where(sel, sums[gi], out)
    return out


def _pool_window(up, uc, un, i, nb, bs, S):
    H = POOL_HALO
    u = jnp.concatenate([jnp.where(i > 0, up[...].astype(F32), 0.0), uc[...].astype(F32),
                         jnp.where(i < nb - 1, un[...].astype(F32), 0.0)], axis=0)
    pos = i * bs - H + lax.broadcasted_iota(jnp.int32, (bs + 2 * H, 1), 0)
    return u, pos


def _pool_mix(pooled, pw_ref):
    outs = []
    for gi in range(len(POOL_WINDOWS)):
        outs.append(jnp.dot(pooled[:, gi * C_G:(gi + 1) * C_G].astype(BF16), pw_ref[gi].astype(BF16),
                            preferred_element_type=F32))
    return jnp.concatenate(outs, axis=1)


def pool_fwd(proj, pw, ps, gm, name):
    S, width = proj.shape
    col = width // C_C - 1
    bs = _rows(S, SEQ_TILE)
    nb = S // bs
    H = POOL_HALO

    def body(up, uc, un, pw_ref, ps_ref, gm_ref, y_ref):
        i = pl.program_id(0)
        u, pos = _pool_window(up, uc, un, i, nb, bs, S)
        pooled, _ = _pool_means(u, pos, S)
        mixed = _pool_mix(pooled[H:H + bs], pw_ref)
        y_ref[...] = (_rms(mixed * ps_ref[...]) * gm_ref[...]).astype(y_ref.dtype)

    vec = _vec_spec(C_C)
    return pl.pallas_call(
        body, name=name, grid=(nb,),
        in_specs=[*_halo_specs(bs, H, C_C, col, S), pl.BlockSpec((4, C_G, C_G), lambda i: (0, 0, 0)), vec, vec],
        out_specs=_row_spec(bs, C_C), out_shape=jax.ShapeDtypeStruct((S, C_C), BF16),
        compiler_params=_params("parallel"),
    )(proj, proj, proj, pw, ps, gm)


def pool_bwd(proj, dy, pw, ps, gm, name):
    S, width = proj.shape
    col = width // C_C - 1
    dcol = dy.shape[1] // C_C - 1
    bs = _rows(S, SEQ_TILE)
    nb = S // bs
    H = POOL_HALO
    W = bs + 2 * H

    def body(up, uc, un, dp, dc_, dn, pw_ref, ps_ref, gm_ref, du_ref, dpw_ref, dps_ref, dgm_ref):
        i = pl.program_id(0)
        u, pos = _pool_window(up, uc, un, i, nb, bs, S)
        pooled, inv = _pool_means(u, pos, S)
        mixed = _pool_mix(pooled, pw_ref)
        dyv = jnp.concatenate([dp[...], dc_[...], dn[...]], axis=0).astype(F32)
        dyc, n = _rms_bwd(mixed * ps_ref[...], dyv * gm_ref[...])
        dmixed = dyc * ps_ref[...]
        dmb = dmixed.astype(BF16)
        dpooled = jnp.concatenate(
            [lax.dot_general(dmb[:, gi * C_G:(gi + 1) * C_G], pw_ref[gi].astype(BF16), (((1,), (1,)), ((), ())),
                             preferred_element_type=F32) for gi in range(len(POOL_WINDOWS))], axis=1)
        dpooled = jnp.where((pos >= 0) & (pos < S), dpooled, 0.0)
        du = _pool_adjoint(dpooled * inv) - dpooled
        du_ref[...] = du[H:H + bs].astype(du_ref.dtype)

        @pl.when(i == 0)
        def _():
            for r in (dpw_ref, dps_ref, dgm_ref):
                r[...] = jnp.zeros_like(r)

        mid = slice(H, H + bs)
        dps_ref[...] += jnp.sum((dyc * mixed)[mid], axis=0, keepdims=True)
        dgm_ref[...] += jnp.sum((dyv * n)[mid], axis=0, keepdims=True)
        pb = pooled[mid].astype(BF16)
        for gi in range(len(POOL_WINDOWS)):
            sl = slice(gi * C_G, (gi + 1) * C_G)
            dpw_ref[gi] += lax.dot_general(pb[:, sl], dmb[mid][:, sl], (((0,), (0,)), ((), ())),
                                           preferred_element_type=F32)

    vec = _vec_spec(C_C)
    full = pl.BlockSpec((4, C_G, C_G), lambda i: (0, 0, 0))
    vshape = jax.ShapeDtypeStruct((1, C_C), F32)
    return pl.pallas_call(
        body, name=name, grid=(nb,),
        in_specs=[*_halo_specs(bs, H, C_C, col, S), *_halo_specs(bs, H, C_C, dcol, S), full, vec, vec],
        out_specs=[_row_spec(bs, C_C), full, vec, vec],
        out_shape=[jax.ShapeDtypeStruct((S, C_C), BF16), jax.ShapeDtypeStruct((4, C_G, C_G), F32), vshape, vshape],
        compiler_params=_params("arbitrary"),
    )(proj, proj, proj, dy, dy, dy, pw, ps, gm)


def rope_tables(S):
    pos = jnp.arange(S, dtype=F32)
    inv = ROPE_THETA ** (-jnp.arange(0, ROT_DIM, 2, dtype=F32) / ROT_DIM)
    ang = pos[:, None] * inv[None, :]
    half = ROT_DIM // 2
    cos, sin = jnp.cos(ang), jnp.sin(ang)
    zeros = jnp.zeros((S, half), F32)
    rest = jnp.zeros((S, HEAD_DIM - ROT_DIM), F32)
    per_head = (jnp.concatenate([cos, cos, rest + 1.0], axis=1), jnp.concatenate([-sin, zeros, rest], axis=1),
                jnp.concatenate([zeros, sin, rest], axis=1))
    return tuple(jnp.tile(t, (1, LANES // HEAD_DIM)) for t in per_head)


def _rotate(t, c, s1, s2, sign):
    half = ROT_DIM // 2
    return t * c + sign * (pltpu.roll(t, LANES - half, axis=1) * s1 + pltpu.roll(t, half, axis=1) * s2)


def rope_fwd(proj, tables, name):
    S = proj.shape[0]
    tr = _rows(S, ROW_TILE)
    qcol = 2 * C_A // C_B
    nd = len(DILATIONS)

    def body(q_ref, k_ref, v_ref, c_ref, s1_ref, s2_ref, *outs):
        c, s1, s2 = c_ref[...], s1_ref[...], s2_ref[...]
        qs, ks = [], []
        for p in range(C_B // LANES):
            sl = slice(p * LANES, (p + 1) * LANES)
            qs.append((_rotate(q_ref[:, sl].astype(F32), c, s1, s2, 1.0) * HEAD_DIM ** -0.5).astype(BF16))
            ks.append(_rotate(k_ref[:, sl].astype(F32), c, s1, s2, 1.0).astype(BF16))
        tensors = (jnp.concatenate(qs, axis=1), jnp.concatenate(ks, axis=1), v_ref[...])
        for n, d in enumerate(DILATIONS):
            for t, x in enumerate(tensors):
                _store_dilated(outs[3 * n + t], x, d)

    tab = _row_spec(tr, LANES)
    res = pl.pallas_call(
        body, name=name, grid=(S // tr,),
        in_specs=[pl.BlockSpec((tr, C_B), lambda i, col=qcol + n: (i, col)) for n in range(3)] + [tab, tab, tab],
        out_specs=[_dilated_spec(tr, d, C_B) for d in DILATIONS for _ in range(3)],
        out_shape=[jax.ShapeDtypeStruct((d, S // d, C_B), BF16) for d in DILATIONS for _ in range(3)],
        compiler_params=_params("parallel"),
    )(proj, proj, proj, *tables)
    return [tuple(res[3 * n:3 * n + 3]) for n in range(nd)]


def rope_bwd(dqs, dks, dvs, tables, dconv, du, name):
    S = dconv.shape[0]
    tr = _rows(S, ROW_TILE)
    n = len(dqs)
    base = 2 * C_A
    width = base + 3 * C_B + C_C

    def body(*refs):
        dq_refs, dk_refs, dv_refs = refs[:n], refs[n:2 * n], refs[2 * n:3 * n]
        c_ref, s1_ref, s2_ref, dconv_ref, du_ref, o_ref = refs[3 * n:]
        c, s1, s2 = c_ref[...], s1_ref[...], s2_ref[...]
        o_ref[:, :base] = dconv_ref[...]
        o_ref[:, base + 3 * C_B:] = du_ref[...]
        dq_all = sum(_load_dilated(r, d).astype(F32) for r, d in zip(dq_refs, DILATIONS))
        dk_all = sum(_load_dilated(r, d).astype(F32) for r, d in zip(dk_refs, DILATIONS))
        dv_all = sum(_load_dilated(r, d).astype(F32) for r, d in zip(dv_refs, DILATIONS))
        for p in range(C_B // LANES):
            sl = slice(p * LANES, (p + 1) * LANES)
            dq, dk, dv = dq_all[:, sl], dk_all[:, sl], dv_all[:, sl]
            at = base + p * LANES
            o_ref[:, at:at + LANES] = (_rotate(dq, c, s1, s2, -1.0) * HEAD_DIM ** -0.5).astype(BF16)
            o_ref[:, C_B + at:C_B + at + LANES] = _rotate(dk, c, s1, s2, -1.0).astype(BF16)
            o_ref[:, 2 * C_B + at:2 * C_B + at + LANES] = dv.astype(BF16)

    tab = _row_spec(tr, LANES)
    return pl.pallas_call(
        body, name=name, grid=(S // tr,),
        in_specs=[_dilated_spec(tr, d, C_B) for _ in range(3) for d in DILATIONS]
        + [tab, tab, tab, _row_spec(tr, base), _row_spec(tr, C_C)],
        out_specs=_row_spec(tr, width), out_shape=jax.ShapeDtypeStruct((S, width), BF16),
        compiler_params=_params("parallel"),
    )(*dqs, *dks, *dvs, *tables, dconv, du)


def _attn_specs(bq, width, L):
    per = bq // ATTN_HALF
    last = L // ATTN_HALF - 1
    cur = pl.BlockSpec((None, bq, width), lambda r, j: (r, j, 0))
    prev = pl.BlockSpec((None, ATTN_HALF, width), lambda r, j: (r, jnp.maximum(j * per - 1, 0), 0))
    nxt = pl.BlockSpec((None, ATTN_HALF, width), lambda r, j: (r, jnp.minimum((j + 1) * per, last), 0))
    return prev, cur, nxt


def _window(refs, sl):
    return jnp.concatenate([r[:, sl] for r in refs], axis=0)


def _band_mask(j, bq, L, rows_are_window):
    bw = bq + 2 * ATTN_HALF
    if rows_are_window:
        rp = j * bq - ATTN_HALF + lax.broadcasted_iota(jnp.int32, (bw, 1), 0)
        cp = j * bq + lax.broadcasted_iota(jnp.int32, (1, bq), 1)
        return (jnp.abs(rp - cp) <= ATTN_HALF) & (rp >= 0) & (rp < L)
    rp = j * bq + lax.broadcasted_iota(jnp.int32, (bq, 1), 0)
    cp = j * bq - ATTN_HALF + lax.broadcasted_iota(jnp.int32, (1, bw), 1)
    return (jnp.abs(rp - cp) <= ATTN_HALF) & (cp >= 0) & (cp < L)


def _head_col(stats, h):
    lane = lax.broadcasted_iota(jnp.int32, (1, LANES), 1)
    return jnp.sum(jnp.where(lane == h, stats, 0.0), axis=1, keepdims=True)


def _stack_heads(x):
    first = lax.broadcasted_iota(jnp.int32, (1, LANES), 1) < HEAD_DIM
    zero = jnp.zeros_like(x)
    return jnp.concatenate([jnp.where(first, x, zero), jnp.where(first, zero, x)], axis=0)


_NT = (((1,), (1,)), ((), ()))
_TN = (((0,), (0,)), ((), ()))


def attn_fwd_pattern(qd, kd, vd, name):
    d, L, _ = qd.shape
    bq = _rows(L, ATTN_BLOCK)

    def body(q_ref, kp, kc, kn, vp, vc, vn, o_ref, lse_ref):
        j = pl.program_id(1)
        mask = _band_mask(j, bq, L, False)
        mask2 = jnp.concatenate([mask, mask], axis=0)
        lane = lax.broadcasted_iota(jnp.int32, (1, LANES), 1)
        first = lane < HEAD_DIM
        lse = jnp.zeros((bq, LANES), F32)
        for p in range(C_B // LANES):
            sl = slice(p * LANES, (p + 1) * LANES)
            kw = _window((kp, kc, kn), sl)
            vw = _window((vp, vc, vn), sl)
            s = jnp.where(mask2, lax.dot_general(_stack_heads(q_ref[:, sl]), kw, _NT, preferred_element_type=F32), NEG)
            m = jnp.max(s, axis=1, keepdims=True)
            e = jnp.exp(s - m)
            l = jnp.sum(e, axis=1, keepdims=True)
            o = jnp.dot(e.astype(BF16), vw, preferred_element_type=F32) * (1.0 / l)
            stat = m + jnp.log(l)
            lse = jnp.where(lane == 2 * p, stat[:bq], jnp.where(lane == 2 * p + 1, stat[bq:], lse))
            o_ref[:, sl] = jnp.where(first, o[:bq], o[bq:]).astype(o_ref.dtype)
        lse_ref[...] = lse

    kv = _attn_specs(bq, C_B, L)
    return pl.pallas_call(
        body, name=name, grid=(d, L // bq), in_specs=[kv[1], *kv, *kv],
        out_specs=[kv[1], pl.BlockSpec((None, bq, LANES), lambda r, j: (r, j, 0))],
        out_shape=[jax.ShapeDtypeStruct((d, L, C_B), BF16), jax.ShapeDtypeStruct((d, L, LANES), F32)],
        compiler_params=_params("parallel", "parallel"),
    )(qd, kd, kd, kd, vd, vd, vd)


def attn_combine(os_, lses, gm, ya, yc, name):
    S = ya.shape[0]
    tr = _rows(S, ROW_TILE)
    n = len(os_)

    def body(*refs):
        o_refs, l_refs = refs[:n], refs[n:2 * n]
        gm_ref, ya_ref, yc_ref, y_ref, out_ref, lse_ref = refs[2 * n:]
        ls = [_load_dilated(r, d) for r, d in zip(l_refs, DILATIONS)]
        os_tok = [_load_dilated(r, d) for r, d in zip(o_refs, DILATIONS)]
        mx = functools.reduce(jnp.maximum, ls)
        ws = [jnp.exp(l - mx) for l in ls]
        den = sum(ws)
        lse_ref[...] = mx + jnp.log(den)
        wn = [w / den for w in ws]
        lane = lax.broadcasted_iota(jnp.int32, (1, LANES), 1)
        first = lane < HEAD_DIM
        blocks = []
        for p in range(C_B // LANES):
            sl = slice(p * LANES, (p + 1) * LANES)
            acc = jnp.zeros((tr, LANES), F32)
            for w, o in zip(wn, os_tok):
                acc = acc + jnp.where(first, _head_col(w, 2 * p), _head_col(w, 2 * p + 1)) * o[:, sl].astype(F32)
            blocks.append(acc)
        out = jnp.concatenate(blocks, axis=1)
        out_ref[...] = out.astype(out_ref.dtype)
        y_ref[:, :C_A] = ya_ref[...]
        y_ref[:, C_A:C_A + C_B] = (_rms(out) * gm_ref[...]).astype(y_ref.dtype)
        y_ref[:, C_A + C_B:] = yc_ref[...]

    st = _row_spec(tr, LANES)
    mix = C_A + C_B + C_C
    return pl.pallas_call(
        body, name=name, grid=(S // tr,),
        in_specs=[_dilated_spec(tr, d, C_B) for d in DILATIONS] + [_dilated_spec(tr, d, LANES) for d in DILATIONS]
        + [_vec_spec(C_B), _row_spec(tr, C_A), _row_spec(tr, C_C)],
        out_specs=[_row_spec(tr, mix), _row_spec(tr, C_B), st],
        out_shape=[jax.ShapeDtypeStruct((S, mix), BF16), jax.ShapeDtypeStruct((S, C_B), BF16),
                   jax.ShapeDtypeStruct((S, LANES), F32)],
        compiler_params=_params("parallel"),
    )(*os_, *lses, gm, ya, yc)


def attn_out_bwd(out, lse, dy, gm, name):
    S = out.shape[0]
    tr = _rows(S, ROW_TILE)
    nd = len(DILATIONS)

    def body(o_ref, lse_ref, dy1, dy2, g_ref, *outs):
        dg_ref = outs[-1]
        o = o_ref[...].astype(F32)
        dyv = jnp.concatenate([dy1[...], dy2[...]], axis=1).astype(F32)
        do, n = _rms_bwd(o, dyv * g_ref[...])
        dob = do.astype(BF16)
        prod = dob.astype(F32) * o
        lane = lax.broadcasted_iota(jnp.int32, (1, LANES), 1)
        first = lane < HEAD_DIM
        delta = jnp.zeros((tr, LANES), F32)
        for p in range(C_B // LANES):
            blk = prod[:, p * LANES:(p + 1) * LANES]
            delta = jnp.where(lane == 2 * p, jnp.sum(jnp.where(first, blk, 0.0), axis=1, keepdims=True), delta)
            delta = jnp.where(lane == 2 * p + 1, jnp.sum(jnp.where(first, 0.0, blk), axis=1, keepdims=True), delta)
        lse_v = lse_ref[...]
        for k, d in enumerate(DILATIONS):
            _store_dilated(outs[3 * k], dob, d)
            _store_dilated(outs[3 * k + 1], lse_v, d)
            _store_dilated(outs[3 * k + 2], delta, d)

        @pl.when(pl.program_id(0) == 0)
        def _():
            dg_ref[...] = jnp.zeros_like(dg_ref)

        dg_ref[...] += jnp.sum(dyv * n, axis=0, keepdims=True)

    widths = (C_B, LANES, LANES)
    dtypes = (BF16, F32, F32)
    res = pl.pallas_call(
        body, name=name, grid=(S // tr,),
        in_specs=[_row_spec(tr, C_B), _row_spec(tr, LANES), pl.BlockSpec((tr, C_A), lambda i: (i, 1)),
                  pl.BlockSpec((tr, C_A), lambda i: (i, 2)), _vec_spec(C_B)],
        out_specs=[_dilated_spec(tr, d, wd) for d in DILATIONS for wd in widths] + [_vec_spec(C_B)],
        out_shape=[jax.ShapeDtypeStruct((d, S // d, wd), dt) for d in DILATIONS for wd, dt in zip(widths, dtypes)]
        + [jax.ShapeDtypeStruct((1, C_B), F32)],
        compiler_params=_params("arbitrary"),
    )(out, lse, dy, dy, gm)
    return [tuple(res[3 * k:3 * k + 3]) for k in range(nd)], res[-1]


def attn_bwd_pattern(qd, kd, vd, dod, lsed, deltad, name, comm=None):
    d, L, _ = qd.shape
    nc = comm.n if comm is not None else 0
    bq = _rows(L, ATTN_BLOCK)

    nb = L // bq
    bw = bq + 2 * ATTN_HALF
    lo = bq - ATTN_HALF

    def body(qc, dc_, lc, tc, kp, kc, kn, vp, vc, vn, *rest):
        c_ins, (dq_ref, dk_ref, dv_ref), c_outs = rest[:nc], rest[nc:nc + 3], rest[nc + 3:2 * nc + 3]
        dk_acc, dv_acc = rest[2 * nc + 3:2 * nc + 5]
        sems = rest[2 * nc + 5:]
        r, j = pl.program_id(0), pl.program_id(1)
        if comm is not None:
            @pl.when((r == 0) & (j == 0))
            def _():
                comm.start(c_ins, c_outs, sems)

        @pl.when(j == 0)
        def _():
            dk_acc[...] = jnp.zeros_like(dk_acc)
            dv_acc[...] = jnp.zeros_like(dv_acc)

        @pl.when(j > 0)
        def _():
            for acc in (dk_acc, dv_acc):
                acc[0:bq, :] = acc[bq:2 * bq, :]
                acc[bq:2 * bq, :] = acc[2 * bq:, :]
                acc[2 * bq:, :] = jnp.zeros((bq, C_B), F32)

        @pl.when(j < nb)
        def _():
            mask = _band_mask(j, bq, L, False)
            mask2 = jnp.concatenate([mask, mask], axis=0)
            first = lax.broadcasted_iota(jnp.int32, (1, LANES), 1) < HEAD_DIM
            lse_c, delta_c = lc[...], tc[...]
            for p in range(C_B // LANES):
                sl = slice(p * LANES, (p + 1) * LANES)
                qs, dos = _stack_heads(qc[:, sl]), _stack_heads(dc_[:, sl])
                kw, vw = _window((kp, kc, kn), sl), _window((vp, vc, vn), sl)
                lse_s = jnp.concatenate([_head_col(lse_c, 2 * p), _head_col(lse_c, 2 * p + 1)], axis=0)
                delta_s = jnp.concatenate([_head_col(delta_c, 2 * p), _head_col(delta_c, 2 * p + 1)], axis=0)
                s = lax.dot_general(qs, kw, _NT, preferred_element_type=F32)
                pr = jnp.where(mask2, jnp.exp(s - lse_s), 0.0)
                dpr = lax.dot_general(dos, vw, _NT, preferred_element_type=F32)
                ds = (pr * (dpr - delta_s)).astype(BF16)
                dq = jnp.dot(ds, kw, preferred_element_type=F32)
                dq_ref[:, sl] = jnp.where(first, dq[:bq], dq[bq:]).astype(dq_ref.dtype)
                dk_acc[lo:lo + bw, sl] += lax.dot_general(ds, qs, _TN, preferred_element_type=F32)
                dv_acc[lo:lo + bw, sl] += lax.dot_general(pr.astype(BF16), dos, _TN, preferred_element_type=F32)

        dk_ref[...] = dk_acc[0:bq, :].astype(dk_ref.dtype)
        dv_ref[...] = dv_acc[0:bq, :].astype(dv_ref.dtype)
        if comm is not None:
            @pl.when((r == d - 1) & (j == nb))
            def _():
                comm.finish(c_ins, c_outs, sems)

    per = bq // ATTN_HALF
    last = L // ATTN_HALF - 1

    def clamp(j):
        return jnp.minimum(j, nb - 1)

    def specs(width):
        cur = pl.BlockSpec((None, bq, width), lambda r, j: (r, clamp(j), 0))
        prev = pl.BlockSpec((None, ATTN_HALF, width), lambda r, j: (r, jnp.maximum(clamp(j) * per - 1, 0), 0))
        nxt = pl.BlockSpec((None, ATTN_HALF, width), lambda r, j: (r, jnp.minimum((clamp(j) + 1) * per, last), 0))
        return prev, cur, nxt

    wide = specs(C_B)
    stat = specs(LANES)[1]
    lagged = pl.BlockSpec((None, bq, C_B), lambda r, j: (r, jnp.maximum(j - 1, 0), 0))
    shape = jax.ShapeDtypeStruct((d, L, C_B), BF16)
    res = pl.pallas_call(
        body, name=name, grid=(d, nb + 1), in_specs=[wide[1], wide[1], stat, stat, *wide, *wide] + [_ANY] * nc,
        out_specs=[wide[1], lagged, lagged] + [_ANY] * nc,
        out_shape=[shape] * 3 + (comm.out_shape if comm is not None else []),
        scratch_shapes=[pltpu.VMEM((3 * bq, C_B), F32), pltpu.VMEM((3 * bq, C_B), F32)]
        + (comm.sems if comm is not None else []),
        compiler_params=_params("arbitrary", "arbitrary"),
    )(qd, dod, lsed, deltad, kd, kd, kd, vd, vd, vd, *(comm.ins if comm is not None else []))
    return tuple(res[:3]), (res[3:] if comm is not None else None)


def _dilated_spec(tr, d, width):
    return pl.BlockSpec((d, tr // d, width), lambda i: (0, i, 0))


def _perm_matrix(n, d, inverse):
    lb = n // d
    row = lax.broadcasted_iota(jnp.int32, (n, n), 0)
    col = lax.broadcasted_iota(jnp.int32, (n, n), 1)
    source = (row % d) * lb + row // d if inverse else (row % lb) * d + row // lb
    return (col == source).astype(BF16)


def _permute(p, x):
    if x.dtype == BF16:
        return jnp.dot(p, x, preferred_element_type=F32).astype(BF16)
    hi = x.astype(BF16)
    rest = x - hi.astype(F32)
    mid = rest.astype(BF16)
    lo = (rest - mid.astype(F32)).astype(BF16)
    return (jnp.dot(p, hi, preferred_element_type=F32) + jnp.dot(p, mid, preferred_element_type=F32)
            + jnp.dot(p, lo, preferred_element_type=F32))


def _store_dilated(ref, x, d):
    n = x.shape[0]
    if d == 1:
        ref[0] = x
        return
    y = _permute(_perm_matrix(n, d, False), x)
    lb = n // d
    for r in range(d):
        ref[r] = y[r * lb:(r + 1) * lb]


def _load_dilated(ref, d):
    if d == 1:
        return ref[0]
    y = jnp.concatenate([ref[r] for r in range(d)], axis=0)
    return _permute(_perm_matrix(y.shape[0], d, True), y)


def adamw(parts, w, m, v, name):
    n, R, C = parts.shape
    tr = _rows(R, ROW_TILE)

    def body(p_ref, w_ref, m_ref, v_ref, g_ref, d_ref, nm_ref, nv_ref):
        g = p_ref[0].astype(F32)
        for k in range(1, n):
            g = g + p_ref[k].astype(F32)
        mm = ADAM_B1 * m_ref[...] + (1.0 - ADAM_B1) * g
        vv = ADAM_B2 * v_ref[...] + (1.0 - ADAM_B2) * jnp.square(g)
        m_hat = mm / (1.0 - ADAM_B1 ** ADAM_STEP)
        v_hat = vv / (1.0 - ADAM_B2 ** ADAM_STEP)
        g_ref[...] = g
        d_ref[...] = -ADAM_LR * (m_hat / (jnp.sqrt(v_hat) + ADAM_EPS) + ADAM_WD * w_ref[...])
        nm_ref[...] = mm
        nv_ref[...] = vv

    spec = _row_spec(tr, C)
    shape = jax.ShapeDtypeStruct((R, C), F32)
    return pl.pallas_call(
        body, name=name, grid=(R // tr,),
        in_specs=[pl.BlockSpec((n, tr, C), lambda i: (0, i, 0)), spec, spec, spec],
        out_specs=[spec] * 4, out_shape=[shape] * 4, compiler_params=_params("parallel"),
    )(parts, w, m, v)


_ANY = pl.BlockSpec(memory_space=pl.ANY)


def _place():
    return lax.axis_index("x"), lax.axis_index("y"), lax.axis_index("c")


def _index(px, py, pc):
    return 4 * px + 2 * py + pc


class Gather:
    def __init__(self, shards):
        self.ins = list(shards)
        T = self.n = len(shards)
        self.out_shape = [jax.ShapeDtypeStruct((N_DEV, *s.shape), s.dtype) for s in shards]
        self.sems = [pltpu.SemaphoreType.DMA((T, 7)), pltpu.SemaphoreType.DMA((T, 7)), pltpu.SemaphoreType.DMA((T,))]

    def _plan(self, ins, outs, sems):
        send_sems, recv_sems, local_sems = sems
        x, y, c = _place()
        me, sibling = (x, y, c), (x, y, 1 - c)
        chips = [(1 - x, y), (x, 1 - y), (1 - x, 1 - y)]

        def copy(t, k, block, to, src=None):
            rows = outs[t].at[_index(*block)]
            return pltpu.make_async_remote_copy(
                src_ref=rows if src is None else src, dst_ref=rows, send_sem=send_sems.at[t, k],
                recv_sem=recv_sems.at[t, k], device_id=to, device_id_type=MESH)

        mine = [pltpu.make_async_copy(ins[t], outs[t].at[_index(*me)], local_sems.at[t]) for t in range(self.n)]
        first = []
        for t in range(self.n):
            first.append(copy(t, 0, me, sibling, src=ins[t]))
            first += [copy(t, 1 + j, me, (*chip, c), src=ins[t]) for j, chip in enumerate(chips)]
        return copy, mine, first, me, sibling, chips, c

    def start(self, ins, outs, sems):
        _, mine, first, *_ = self._plan(ins, outs, sems)
        for cp in mine + first:
            cp.start()

    def finish(self, ins, outs, sems):
        copy, mine, first, me, sibling, chips, c = self._plan(ins, outs, sems)
        passed = []
        for j, chip in enumerate(chips):
            for t in range(self.n):
                copy(t, 1 + j, (*chip, c), me).wait_recv()
                fwd = copy(t, 4 + j, (*chip, c), sibling)
                fwd.start()
                passed.append(fwd)
        for t in range(self.n):
            copy(t, 0, sibling, me).wait_recv()
            for j, chip in enumerate(chips):
                copy(t, 4 + j, (*chip, 1 - c), me).wait_recv()
        for cp in first + passed:
            cp.wait_send()
        for cp in mine:
            cp.wait()


class Exchange:
    def __init__(self, parts):
        self.ins = list(parts)
        T = self.n = len(parts)
        self.out_shape = [jax.ShapeDtypeStruct(p.shape, p.dtype) for p in parts]
        self.sems = [pltpu.SemaphoreType.DMA((T, 7)), pltpu.SemaphoreType.DMA((T, 7)), pltpu.SemaphoreType.DMA((T,))]

    def _plan(self, ins, outs, sems):
        send_sems, recv_sems, local_sems = sems
        x, y, c = _place()
        me = _index(x, y, c)
        copies = [pltpu.make_async_copy(ins[t].at[me], outs[t].at[me], local_sems.at[t]) for t in range(self.n)]
        for k in range(1, N_DEV):
            peer = ((x + (k >> 2)) % 2, (y + ((k >> 1) & 1)) % 2, (c + (k & 1)) % 2)
            there = _index(*peer)
            for t in range(self.n):
                copies.append(pltpu.make_async_remote_copy(
                    src_ref=ins[t].at[there], dst_ref=outs[t].at[me], send_sem=send_sems.at[t, k - 1],
                    recv_sem=recv_sems.at[t, k - 1], device_id=peer, device_id_type=MESH))
        return copies

    def start(self, ins, outs, sems):
        for cp in self._plan(ins, outs, sems):
            cp.start()

    def finish(self, ins, outs, sems):
        for cp in self._plan(ins, outs, sems):
            cp.wait()


def communicate(comm, name):
    T = comm.n

    def body(*refs):
        ins, outs, sems = refs[:T], refs[T:2 * T], refs[2 * T:]
        comm.start(ins, outs, sems)
        comm.finish(ins, outs, sems)

    return pl.pallas_call(
        body, name=name, in_specs=[_ANY] * T, out_specs=[_ANY] * T, out_shape=comm.out_shape,
        scratch_shapes=comm.sems,
    )(*comm.ins)


def all_gather(shards, name):
    return communicate(Gather(shards), name)


def exchange(parts, name):
    return communicate(Exchange(parts), name)


def _row(v):
    return v.reshape(1, -1)


def _hosted(res):
    return res if isinstance(res, tuple) else (res, None)


def mix_forward(x, w, p, tables, comm=None):
    gm = p["g_mix"]
    h1, h1t = rms_fwd(x, _row(p["g_pre_mix"]), "rms_pre_mix")
    proj, got = _hosted(matmul(h1, w["win_t"], "nt", BF16, "proj", comm=comm))
    ya = conv_fwd(proj, p["cw"], _row(p["conv_b"]), _row(p["conv_ln_g"]), _row(p["conv_ln_b"]), _row(gm[:C_A]),
                  "conv_fwd")
    dil = rope_fwd(proj, tables, "rope_fwd")
    os_, lses = [], []
    for d, (qd, kd, vd) in zip(DILATIONS, dil):
        o, lse = attn_fwd_pattern(qd, kd, vd, f"attn_fwd_d{d}")
        os_.append(o)
        lses.append(lse)
    yc = pool_fwd(proj, p["pool_w"], _row(p["pool_scale"]), _row(gm[C_A + C_B:]), "pool_fwd")
    y, out, lse = attn_combine(os_, lses, _row(gm[C_A:C_A + C_B]), ya, yc, "attn_combine")
    z = matmul(y, w["wout"], "nn", BF16, "mix_out")
    x2 = norm_residual(z, x, _row(p["g_post_mix"]), "res_mix")
    return x2, dict(x=x, h1t=h1t, proj=proj, dil=dil, out=out, lse=lse, y=y, z=z), got


def ffn_forward(x2, w, p, comm_in=None, comm_out=None):
    h2, h2t = rms_fwd(x2, _row(p["g_pre_ffn"]), "rms_pre_ffn")
    gu, got_in = _hosted(matmul(h2, w["wgu_t"], "nt", BF16, "ffn_in", comm=comm_in))
    a, at = swiglu_fwd(gu, "swiglu_fwd")
    f, got_out = _hosted(matmul(a, w["wd"], "nn", BF16, "ffn_out", tk=5632, comm=comm_out))
    x3 = norm_residual(f, x2, _row(p["g_post_ffn"]), "res_ffn")
    return x3, dict(x2=x2, h2t=h2t, gu=gu, at=at, f=f), got_in, got_out


def _to_blocks(g, by_columns):
    if by_columns:
        return g.T.reshape(N_DEV, -1, g.shape[0])
    return g.reshape(N_DEV, -1, g.shape[1])


def _exchange_of(g, by_columns):
    return Exchange([_to_blocks(g, by_columns)]) if g is not None else None


def ffn_backward(dx3, s, w, p, ride_act=None):
    df, dg_post_ffn = rms_bwd(s["f"], dx3, _row(p["g_post_ffn"]), None, BF16, "rms_bwd_post_ffn")
    da, got_act = _hosted(matmul(df, w["wd"], "nt", BF16, "d_ffn_act", comm=ride_act))
    dwd = matmul(s["at"], df, "nn", BF16, "dw_down", tm=1408, tn=1024, tk=2048)
    dgu = swiglu_bwd(s["gu"], da, "swiglu_bwd")
    dh2, got_wd = matmul(dgu, w["wgu_t"], "nn", BF16, "d_ffn_in", tk=5632, comm=_exchange_of(dwd, False))
    dwgu = matmul(s["h2t"], dgu, "nn", BF16, "dw_gate_up", tm=1024, tn=1408, tk=2048)
    dx2, dg_pre_ffn = rms_bwd(s["x2"], dh2, _row(p["g_pre_ffn"]), dx3, F32, "rms_bwd_pre_ffn")
    return dx2, dict(dwgu=dwgu, parts_wd=got_wd[0], got_act=got_act,
                     g_pre_ffn=dg_pre_ffn[0], g_post_ffn=dg_post_ffn[0])


def mix_backward(dx2, s, w, p, tables, dwgu):
    gm = p["g_mix"]
    F = dwgu.shape[1] // 2
    dz, dg_post_mix = rms_bwd(s["z"], dx2, _row(p["g_post_mix"]), None, BF16, "rms_bwd_post_mix")
    dy = matmul(dz, w["wout"], "nt", BF16, "d_mix")
    dwout = matmul(s["y"], dz, "tn", BF16, "dw_out", tm=1024, tn=1024, tk=512)
    dconv, dcw, dcb, dlg, dlb, dgm_a = conv_bwd(
        s["proj"], dy, p["cw"], _row(p["conv_b"]), _row(p["conv_ln_g"]), _row(p["conv_ln_b"]), _row(gm[:C_A]),
        "conv_bwd")
    stats, dgm_b = attn_out_bwd(s["out"], s["lse"], dy, _row(gm[C_A:C_A + C_B]), "attn_out_bwd")
    rides = [_exchange_of(dwgu[:, :F], True), _exchange_of(dwgu[:, F:], True), None]
    dqs, dks, dvs, got = [], [], [], []
    for d, (qd, kd, vd), (dod, lsed, deltad), ride in zip(DILATIONS, s["dil"], stats, rides):
        (dq, dk, dv), parts = attn_bwd_pattern(qd, kd, vd, dod, lsed, deltad, f"attn_bwd_d{d}", comm=ride)
        dqs.append(dq)
        dks.append(dk)
        dvs.append(dv)
        got.append(parts)
    du, dpw, dps, dgm_c = pool_bwd(s["proj"], dy, p["pool_w"], _row(p["pool_scale"]), _row(gm[C_A + C_B:]),
                                   "pool_bwd")
    dproj = rope_bwd(dqs, dks, dvs, tables, dconv, du, "rope_bwd")
    dh1 = matmul(dproj, w["win_t"], "nn", BF16, "d_proj", tk=4608)
    dwin, got_out = matmul(s["h1t"], dproj, "nn", BF16, "dw_in", tm=1024, tn=1152, tk=2048,
                           comm=_exchange_of(dwout, False))
    dx, dg_pre_mix = rms_bwd(s["x"], dh1, _row(p["g_pre_mix"]), dx2, F32, "rms_bwd_pre_mix")
    return dx, dict(
        dwin=dwin, parts_out=got_out[0], parts_gate=got[0][0], parts_up=got[1][0],
        conv_w=dcw[:CONV_WIDTH], conv_b=dcb[0], conv_ln_g=dlg[0], conv_ln_b=dlb[0], pool_w=dpw, pool_scale=dps[0],
        g_mix=jnp.concatenate([dgm_a[0], dgm_b[0], dgm_c[0]]), g_pre_mix=dg_pre_mix[0], g_post_mix=dg_post_mix[0])


WEIGHTS = ["w_in", "conv_w", "conv_b", "conv_ln_g", "conv_ln_b", "pool_w", "pool_scale", "g_mix", "w_out", "g_pre_mix",
           "g_post_mix", "g_pre_ffn", "g_post_ffn", "w_gate", "w_up", "w_down"]
BIG = ["w_in", "w_out", "w_gate", "w_up", "w_down"]
REPLICATED = ["conv_b", "conv_ln_g", "conv_ln_b", "pool_w", "pool_scale", "g_mix", "g_pre_mix", "g_post_mix",
              "g_pre_ffn", "g_post_ffn"]
PACK_ROWS = 256


def adamw_layer(parts, w, m, v, layer, prev, name):
    n, R, C = parts.shape
    tr = _rows(R, ROW_TILE)

    def body(p_ref, w_ref, m_ref, v_ref, *rest):
        g_ref, d_ref, nm_ref, nv_ref = rest[-4:]
        g = p_ref[0].astype(F32)
        for k in range(1, n):
            g = g + p_ref[k].astype(F32)
        mm = ADAM_B1 * m_ref[...] + (1.0 - ADAM_B1) * g
        vv = ADAM_B2 * v_ref[...] + (1.0 - ADAM_B2) * jnp.square(g)
        m_hat = mm / (1.0 - ADAM_B1 ** ADAM_STEP)
        v_hat = vv / (1.0 - ADAM_B2 ** ADAM_STEP)
        g_ref[...] = g
        d_ref[...] = -ADAM_LR * (m_hat / (jnp.sqrt(v_hat) + ADAM_EPS) + ADAM_WD * w_ref[...])
        nm_ref[...] = mm
        nv_ref[...] = vv

    spec = pl.BlockSpec((None, tr, C), lambda i: (layer, i, 0))
    shape = jax.ShapeDtypeStruct(w.shape, F32)
    prev = list(prev) if prev is not None else []
    return pl.pallas_call(
        body, name=name, grid=(R // tr,),
        in_specs=[pl.BlockSpec((n, tr, C), lambda i: (0, i, 0)), spec, spec, spec] + [_ANY] * len(prev),
        out_specs=[spec] * 4, out_shape=[shape] * 4,
        input_output_aliases={4 + k: k for k in range(len(prev))}, compiler_params=_params("parallel"),
    )(parts, w, m, v, *prev)


def _pack(arrays):
    flat = jnp.concatenate([a.reshape(-1).astype(F32) for a in arrays])
    unit = PACK_ROWS * LANES
    padded = -(-flat.shape[0] // unit) * unit
    return jnp.pad(flat, (0, padded - flat.shape[0])).reshape(-1, LANES)


def _unpack(packed, like):
    flat = packed.reshape(-1)
    out, at = [], 0
    for a in like:
        out.append(flat[at:at + a.size].reshape(a.shape))
        at += a.size
    return out


def kernel(x, w_in, conv_w, conv_b, conv_ln_g, conv_ln_b, pool_w, pool_scale, g_mix, w_out, g_pre_mix, g_post_mix, g_pre_ffn, g_post_ffn, w_gate, w_up, w_down, loss_target, m_w_in, m_conv_w, m_conv_b, m_conv_ln_g, m_conv_ln_b, m_pool_w, m_pool_scale, m_g_mix, m_w_out, m_g_pre_mix, m_g_post_mix, m_g_pre_ffn, m_g_post_ffn, m_w_gate, m_w_up, m_w_down, v_w_in, v_conv_w, v_conv_b, v_conv_ln_g, v_conv_ln_b, v_pool_w, v_pool_scale, v_g_mix, v_w_out, v_g_pre_mix, v_g_post_mix, v_g_pre_ffn, v_g_post_ffn, v_w_gate, v_w_up, v_w_down):
    w = dict(w_in=w_in, conv_w=conv_w, conv_b=conv_b, conv_ln_g=conv_ln_g, conv_ln_b=conv_ln_b, pool_w=pool_w,
             pool_scale=pool_scale, g_mix=g_mix, w_out=w_out, g_pre_mix=g_pre_mix, g_post_mix=g_post_mix,
             g_pre_ffn=g_pre_ffn, g_post_ffn=g_post_ffn, w_gate=w_gate, w_up=w_up, w_down=w_down)
    m = dict(w_in=m_w_in, conv_w=m_conv_w, conv_b=m_conv_b, conv_ln_g=m_conv_ln_g, conv_ln_b=m_conv_ln_b,
             pool_w=m_pool_w, pool_scale=m_pool_scale, g_mix=m_g_mix, w_out=m_w_out, g_pre_mix=m_g_pre_mix,
             g_post_mix=m_g_post_mix, g_pre_ffn=m_g_pre_ffn, g_post_ffn=m_g_post_ffn, w_gate=m_w_gate, w_up=m_w_up,
             w_down=m_w_down)
    v = dict(w_in=v_w_in, conv_w=v_conv_w, conv_b=v_conv_b, conv_ln_g=v_conv_ln_g, conv_ln_b=v_conv_ln_b,
             pool_w=v_pool_w, pool_scale=v_pool_scale, g_mix=v_g_mix, w_out=v_w_out, g_pre_mix=v_g_pre_mix,
             g_post_mix=v_g_post_mix, g_pre_ffn=v_g_pre_ffn, g_post_ffn=v_g_post_ffn, w_gate=v_w_gate, w_up=v_w_up,
             w_down=v_w_down)
    depth = w_in.shape[0]
    xs, target = x[0], loss_target[0]
    S, D = xs.shape
    tables = rope_tables(S)

    cw_shard = jnp.pad(conv_w, ((0, 0), (0, 1), (0, 0)))
    cw_all = all_gather([cw_shard.reshape(-1, LANES)], "gather_conv_w")[0]
    cw_full = cw_all.reshape(N_DEV, depth, 32, -1).transpose(1, 2, 0, 3).reshape(depth, 32, C_A)
    small = []
    for l in range(depth):
        small.append({n: w[n][l] for n in REPLICATED})
        small[l]["cw"] = cw_full[l]

    def shard(n, l):
        return (w[n][l].T if n in ("w_in", "w_gate", "w_up") else w[n][l]).astype(BF16)

    def joined(blocks):
        return blocks.reshape(-1, blocks.shape[2])

    def mix_shards(l):
        return Gather([shard("w_in", l), shard("w_out", l)])

    def ffn_shards(l):
        return Gather([shard(n, l) for n in ("w_gate", "w_up", "w_down")])

    def mix_weights(blocks):
        return dict(win_t=joined(blocks[0]), wout=joined(blocks[1]))

    def ffn_weights(blocks):
        return dict(wgu_t=jnp.concatenate([joined(blocks[0]), joined(blocks[1])], axis=0), wd=joined(blocks[2]))

    w_mix = [None] * depth
    w_ffn = [None] * depth
    saved_mix, saved_ffn = [None] * depth, [None] * depth
    w_mix[0] = mix_weights(communicate(mix_shards(0), "gather_mix_weights"))
    h = xs
    for l in range(depth):
        x2, saved_mix[l], got = mix_forward(h, w_mix[l], small[l], tables, ffn_shards(0) if l == 0 else None)
        if l == 0:
            w_ffn[0] = ffn_weights(got)
        more = l + 1 < depth
        h, saved_ffn[l], got_in, got_out = ffn_forward(
            x2, w_ffn[l], small[l], ffn_shards(l + 1) if more else None, mix_shards(l + 1) if more else None)
        if more:
            w_ffn[l + 1] = ffn_weights(got_in)
            w_mix[l + 1] = mix_weights(got_out)

    dh, sq = loss_head(h, target, "loss_head")
    loss = lax.psum(0.5 * jnp.sum(sq) / D, ("x", "y", "c"))

    big_out = {n: None for n in BIG}
    by_columns = ("w_in", "w_gate", "w_up")
    state = {n: tuple(jnp.swapaxes(t[n], 1, 2) if n in by_columns else t[n] for t in (w, m, v)) for n in BIG}

    def update(n, l, parts):
        big_out[n] = adamw_layer(parts, *state[n], l, big_out[n], f"adamw_{n}_layer{l}")

    small_grads = [None] * depth
    carried = None
    for l in reversed(range(depth)):
        dx2, gf = ffn_backward(dh, saved_ffn[l], w_ffn[l], small[l], _exchange_of(carried, True))
        if carried is not None:
            update("w_in", l + 1, gf["got_act"][0])
        update("w_down", l, gf["parts_wd"])
        dh, gm_ = mix_backward(dx2, saved_mix[l], w_mix[l], small[l], tables, gf["dwgu"])
        update("w_gate", l, gm_["parts_gate"])
        update("w_up", l, gm_["parts_up"])
        update("w_out", l, gm_["parts_out"])
        carried = gm_["dwin"]
        small_grads[l] = {**gf, **gm_}
    update("w_in", 0, exchange([_to_blocks(carried, True)], "exchange_last_grads")[0])

    names = REPLICATED + ["conv_w"]
    stacked = [jnp.stack([small_grads[l][n] for l in range(depth)]) for n in names]
    partial = all_gather([_pack(stacked)], "gather_small_grads")[0]
    zeros = jnp.zeros_like(stacked[-1])
    packed = adamw(partial, _pack([w[n] for n in REPLICATED] + [zeros]), _pack([m[n] for n in REPLICATED] + [zeros]),
                   _pack([v[n] for n in REPLICATED] + [zeros + 1.0]), "adamw_replicated")
    small_out = [_unpack(o, stacked) for o in packed]
    out = {n: tuple(o[i] for o in small_out) for i, n in enumerate(REPLICATED)}
    width = conv_w.shape[2]
    g_cw = lax.dynamic_slice_in_dim(small_out[0][-1], _index(*_place()) * width, width, axis=2)
    cw_res = adamw(g_cw.reshape(1, -1, LANES), conv_w.reshape(-1, LANES), m["conv_w"].reshape(-1, LANES),
                   v["conv_w"].reshape(-1, LANES), "adamw_conv_w")
    out["conv_w"] = tuple(o.reshape(conv_w.shape) for o in cw_res)
    for n in BIG:
        out[n] = tuple(jnp.swapaxes(o, 1, 2) if n in by_columns else o for o in big_out[n])
    results = [loss, dh[None]]
    for k in range(4):
        results += [out[n][k] for n in WEIGHTS]
    return tuple(results)
```

```python
import functools
import math

import jax
import jax.numpy as jnp
from jax import lax
from jax.experimental import pallas as pl
from jax.experimental.pallas import tpu as pltpu

F32 = jnp.float32
BF16 = jnp.bfloat16

N_DEV = 8
DEPTH = 4
EPS = 1e-6
NEG = -1e30

C_A = 512
N_HEADS = 16
HEAD_DIM = 64
C_B = N_HEADS * HEAD_DIM
C_C = 512
POOL_WINDOWS = (2, 4, 8, 16)
C_G = C_C // len(POOL_WINDOWS)
CONV_WIDTH = 31
CONV_HALF = CONV_WIDTH // 2
DILATIONS = (1, 4, 16)
ATTN_HALF = 64
ROT_DIM = HEAD_DIM // 4
ROPE_THETA = 500000.0

ADAM_LR = 0.001
ADAM_B1 = 0.9
ADAM_B2 = 0.999
ADAM_EPS = 1e-08
ADAM_WD = 0.01
ADAM_STEP = 10

LANES = 128
SUBLANES = 8
VMEM_LIMIT = 56 * 1024 * 1024
ROW_TILE = 256
SEQ_TILE = 256
ATTN_BLOCK = 128
MESH = pl.DeviceIdType.MESH


def _params(*sem):
    return pltpu.CompilerParams(dimension_semantics=sem, vmem_limit_bytes=VMEM_LIMIT)


def _tile(n, target):
    if n <= target:
        return n
    t = (target // LANES) * LANES
    while t >= LANES:
        if n % t == 0:
            return t
        t -= LANES
    return n


def _rows(n, target):
    t = min(n, target)
    while n % t:
        t //= 2
    return t


def matmul(a, b, mode, out_dtype, name, tm=1024, tn=512, tk=2048, comm=None):
    if mode == "nn":
        (M, K), (_, N) = a.shape, b.shape
    elif mode == "nt":
        (M, K), (N, _) = a.shape, b.shape
    else:
        (K, M), (_, N) = a.shape, b.shape
    tm, tn, tk = _tile(M, tm), _tile(N, tn), _tile(K, tk)
    nm, nn, nk = M // tm, N // tn, K // tk
    dims = {"nn": (((1,), (0,)), ((), ())), "nt": (((1,), (1,)), ((), ())), "tn": (((0,), (0,)), ((), ()))}[mode]
    nc = comm.n if comm is not None else 0

    def body(*refs):
        a_ref, b_ref = refs[:2]
        c_ins, o_ref, c_outs = refs[2:2 + nc], refs[2 + nc], refs[3 + nc:3 + 2 * nc]
        scratch = refs[3 + 2 * nc:]
        acc, sems = (scratch[:1], scratch[1:]) if nk > 1 else ((), scratch)
        i, j, k = pl.program_id(0), pl.program_id(1), pl.program_id(2)
        if comm is not None:
            @pl.when((i == 0) & (j == 0) & (k == 0))
            def _():
                comm.start(c_ins, c_outs, sems)

        p = lax.dot_general(a_ref[...], b_ref[...], dims, preferred_element_type=F32)
        if nk == 1:
            o_ref[...] = p.astype(o_ref.dtype)
        else:
            acc_ref, = acc

            @pl.when(k == 0)
            def _():
                acc_ref[...] = p

            @pl.when(k > 0)
            def _():
                acc_ref[...] += p

            @pl.when(k == nk - 1)
            def _():
                o_ref[...] = acc_ref[...].astype(o_ref.dtype)

        if comm is not None:
            @pl.when((i == nm - 1) & (j == nn - 1) & (k == nk - 1))
            def _():
                comm.finish(c_ins, c_outs, sems)

    if mode == "nn":
        a_spec = pl.BlockSpec((tm, tk), lambda i, j, k: (i, k))
        b_spec = pl.BlockSpec((tk, tn), lambda i, j, k: (k, j))
    elif mode == "nt":
        a_spec = pl.BlockSpec((tm, tk), lambda i, j, k: (i, k))
        b_spec = pl.BlockSpec((tn, tk), lambda i, j, k: (j, k))
    else:
        a_spec = pl.BlockSpec((tk, tm), lambda i, j, k: (k, i))
        b_spec = pl.BlockSpec((tk, tn), lambda i, j, k: (k, j))
    o_spec = pl.BlockSpec((tm, tn), lambda i, j, k: (i, j))
    o_shape = jax.ShapeDtypeStruct((M, N), out_dtype)
    acc_shape = [pltpu.VMEM((tm, tn), F32)] if nk > 1 else []
    if comm is None:
        return pl.pallas_call(
            body, name=name, grid=(nm, nn, nk), in_specs=[a_spec, b_spec], out_specs=o_spec, out_shape=o_shape,
            scratch_shapes=acc_shape, compiler_params=_params("parallel", "parallel", "arbitrary"),
        )(a, b)
    res = pl.pallas_call(
        body, name=name, grid=(nm, nn, nk), in_specs=[a_spec, b_spec] + [_ANY] * nc,
        out_specs=[o_spec] + [_ANY] * nc, out_shape=[o_shape] + comm.out_shape,
        scratch_shapes=acc_shape + comm.sems, compiler_params=_params("arbitrary", "arbitrary", "arbitrary"),
    )(a, b, *comm.ins)
    return res[0], res[1:]


def _rms(t):
    return t * lax.rsqrt(jnp.mean(t * t, axis=-1, keepdims=True) + EPS)


def _rms_bwd(t, dn):
    r = lax.rsqrt(jnp.mean(t * t, axis=-1, keepdims=True) + EPS)
    n = t * r
    return r * (dn - n * jnp.mean(dn * n, axis=-1, keepdims=True)), n


def _row_spec(tr, d):
    return pl.BlockSpec((tr, d), lambda i: (i, 0))


def _vec_spec(d):
    return pl.BlockSpec((1, d), lambda i: (0, 0))


def _col_spec(d, tr):
    return pl.BlockSpec((d, tr), lambda i: (0, i))


def rms_fwd(x, g, name):
    S, D = x.shape
    tr = _rows(S, ROW_TILE)

    def body(x_ref, g_ref, o_ref, ot_ref):
        h = _rms(x_ref[...].astype(F32)) * g_ref[...]
        o_ref[...] = h.astype(o_ref.dtype)
        ot_ref[...] = h.T.astype(ot_ref.dtype)

    return pl.pallas_call(
        body, name=name, grid=(S // tr,), in_specs=[_row_spec(tr, D), _vec_spec(D)],
        out_specs=[_row_spec(tr, D), _col_spec(D, tr)],
        out_shape=[jax.ShapeDtypeStruct((S, D), BF16), jax.ShapeDtypeStruct((D, S), BF16)],
        compiler_params=_params("parallel"),
    )(x, g)


def norm_residual(z, x, g, name):
    S, D = x.shape
    tr = _rows(S, ROW_TILE)

    def body(z_ref, x_ref, g_ref, o_ref):
        o_ref[...] = x_ref[...] + _rms(z_ref[...].astype(F32)) * g_ref[...]

    return pl.pallas_call(
        body, name=name, grid=(S // tr,), in_specs=[_row_spec(tr, D), _row_spec(tr, D), _vec_spec(D)],
        out_specs=_row_spec(tr, D), out_shape=jax.ShapeDtypeStruct((S, D), F32), compiler_params=_params("parallel"),
    )(z, x, g)


def norm_residual_rms(z, x, g, g_next, name):
    S, D = x.shape
    tr = _rows(S, ROW_TILE)

    def body(z_ref, x_ref, g_ref, gn_ref, o_ref, h_ref, ht_ref):
        x_new = x_ref[...] + _rms(z_ref[...].astype(F32)) * g_ref[...]
        o_ref[...] = x_new
        h = _rms(x_new) * gn_ref[...]
        h_ref[...] = h.astype(h_ref.dtype)
        ht_ref[...] = h.T.astype(ht_ref.dtype)

    return pl.pallas_call(
        body, name=name, grid=(S // tr,),
        in_specs=[_row_spec(tr, D), _row_spec(tr, D), _vec_spec(D), _vec_spec(D)],
        out_specs=[_row_spec(tr, D), _row_spec(tr, D), _col_spec(D, tr)],
        out_shape=[jax.ShapeDtypeStruct((S, D), F32), jax.ShapeDtypeStruct((S, D), BF16),
                   jax.ShapeDtypeStruct((D, S), BF16)],
        compiler_params=_params("parallel"),
    )(z, x, g, g_next)


def rms_bwd_chain(t1, dy1, g1, res, t2, g2, name):
    S, D = t1.shape
    tr = _rows(S, ROW_TILE)

    def body(t1_ref, dy1_ref, g1_ref, res_ref, t2_ref, g2_ref, d1_ref, d2_ref, dg1_ref, dg2_ref):
        dy1v = dy1_ref[...].astype(F32)
        dt1, n1 = _rms_bwd(t1_ref[...].astype(F32), dy1v * g1_ref[...])
        d1 = dt1 + res_ref[...]
        d1_ref[...] = d1
        dt2, n2 = _rms_bwd(t2_ref[...].astype(F32), d1 * g2_ref[...])
        d2_ref[...] = dt2.astype(d2_ref.dtype)

        @pl.when(pl.program_id(0) == 0)
        def _():
            dg1_ref[...] = jnp.zeros_like(dg1_ref)
            dg2_ref[...] = jnp.zeros_like(dg2_ref)

        dg1_ref[...] += jnp.sum(dy1v * n1, axis=0, keepdims=True)
        dg2_ref[...] += jnp.sum(d1 * n2, axis=0, keepdims=True)

    row, vec = _row_spec(tr, D), _vec_spec(D)
    return pl.pallas_call(
        body, name=name, grid=(S // tr,), in_specs=[row, row, vec, row, row, vec], out_specs=[row, row, vec, vec],
        out_shape=[jax.ShapeDtypeStruct((S, D), F32), jax.ShapeDtypeStruct((S, D), BF16),
                   jax.ShapeDtypeStruct((1, D), F32), jax.ShapeDtypeStruct((1, D), F32)],
        compiler_params=_params("arbitrary"),
    )(t1, dy1, g1, res, t2, g2)


def rms_bwd(t, dy, g, res, out_dtype, name):
    S, D = t.shape
    tr = _rows(S, ROW_TILE)
    has_res = res is not None

    def body(t_ref, dy_ref, g_ref, *rest):
        if has_res:
            res_ref, dt_ref, dg_ref = rest
        else:
            dt_ref, dg_ref = rest
        dyv = dy_ref[...].astype(F32)
        dt, n = _rms_bwd(t_ref[...].astype(F32), dyv * g_ref[...])
        if has_res:
            dt = dt + res_ref[...]
        dt_ref[...] = dt.astype(dt_ref.dtype)

        @pl.when(pl.program_id(0) == 0)
        def _():
            dg_ref[...] = jnp.zeros_like(dg_ref)

        dg_ref[...] += jnp.sum(dyv * n, axis=0, keepdims=True)

    ins = [t, dy, g] + ([res] if has_res else [])
    specs = [_row_spec(tr, D), _row_spec(tr, D), _vec_spec(D)] + ([_row_spec(tr, D)] if has_res else [])
    return pl.pallas_call(
        body, name=name, grid=(S // tr,), in_specs=specs, out_specs=[_row_spec(tr, D), _vec_spec(D)],
        out_shape=[jax.ShapeDtypeStruct((S, D), out_dtype), jax.ShapeDtypeStruct((1, D), F32)],
        compiler_params=_params("arbitrary"),
    )(*ins)


def swiglu_fwd(gu, name):
    S, F2 = gu.shape
    F = F2 // 2
    tr = _rows(S, ROW_TILE)

    def body(g_ref, u_ref, o_ref, ot_ref):
        g = g_ref[...].astype(F32)
        a = g * jax.nn.sigmoid(g) * u_ref[...].astype(F32)
        o_ref[...] = a.astype(o_ref.dtype)
        ot_ref[...] = a.T.astype(ot_ref.dtype)

    return pl.pallas_call(
        body, name=name, grid=(S // tr,),
        in_specs=[pl.BlockSpec((tr, F), lambda i: (i, 0)), pl.BlockSpec((tr, F), lambda i: (i, 1))],
        out_specs=[_row_spec(tr, F), _col_spec(F, tr)],
        out_shape=[jax.ShapeDtypeStruct((S, F), BF16), jax.ShapeDtypeStruct((F, S), BF16)],
        compiler_params=_params("parallel"),
    )(gu, gu)


def swiglu_bwd(gu, da, name):
    S, F2 = gu.shape
    F = F2 // 2
    tr = _rows(S, ROW_TILE)

    def body(g_ref, u_ref, da_ref, o_ref):
        g = g_ref[...].astype(F32)
        u = u_ref[...].astype(F32)
        dav = da_ref[...].astype(F32)
        sig = jax.nn.sigmoid(g)
        o_ref[:, :F] = (dav * u * (sig * (1.0 + g * (1.0 - sig)))).astype(o_ref.dtype)
        o_ref[:, F:] = (dav * (g * sig)).astype(o_ref.dtype)

    return pl.pallas_call(
        body, name=name, grid=(S // tr,),
        in_specs=[pl.BlockSpec((tr, F), lambda i: (i, 0)), pl.BlockSpec((tr, F), lambda i: (i, 1)), _row_spec(tr, F)],
        out_specs=_row_spec(tr, F2), out_shape=jax.ShapeDtypeStruct((S, F2), BF16), compiler_params=_params("parallel"),
    )(gu, gu, da)


def loss_head(y, target, name):
    S, D = y.shape
    tr = _rows(S, ROW_TILE)

    def body(y_ref, t_ref, dy_ref, sq_ref):
        e = y_ref[...] - t_ref[...]
        dy_ref[...] = e * (1.0 / D)

        @pl.when(pl.program_id(0) == 0)
        def _():
            sq_ref[...] = jnp.zeros_like(sq_ref)

        sq_ref[...] += jnp.sum(e * e, axis=0, keepdims=True)

    return pl.pallas_call(
        body, name=name, grid=(S // tr,), in_specs=[_row_spec(tr, D), _row_spec(tr, D)],
        out_specs=[_row_spec(tr, D), _vec_spec(D)],
        out_shape=[jax.ShapeDtypeStruct((S, D), F32), jax.ShapeDtypeStruct((1, D), F32)],
        compiler_params=_params("arbitrary"),
    )(y, target)


def _halo_specs(bs, halo, width, col, n_rows):
    per = bs // halo
    last = n_rows // halo - 1
    cur = pl.BlockSpec((bs, width), lambda i: (i, col))
    prev = pl.BlockSpec((halo, width), lambda i: (jnp.maximum(i * per - 1, 0), col))
    nxt = pl.BlockSpec((halo, width), lambda i: (jnp.minimum((i + 1) * per, last), col))
    return prev, cur, nxt


TAP_CHUNK = 32


def _build_phases(ref, phases_ref):
    total = ref.shape[0] - SUBLANES
    for b in range(SUBLANES):
        phases_ref[b, 0:total, :] = ref[pl.ds(b, total), :]


def _tap_rows(phases_ref, off, n):
    b = off % SUBLANES
    return phases_ref[b, off - b:off - b + n, :]


def _conv_taps(phases_ref, offsets, n, weights_ref, init, out_ref):
    for r0 in range(0, n, TAP_CHUNK):
        rows = min(TAP_CHUNK, n - r0)
        acc = jnp.zeros((rows, C_A), F32) + init
        for t, off in offsets.items():
            acc = acc + weights_ref[t:t + 1, :] * _tap_rows(phases_ref, off + r0, rows)
        out_ref[r0:r0 + rows, :] = acc


def _glu(a, g):
    return a.astype(F32) * jax.nn.sigmoid(g.astype(F32))


def _layernorm_silu(c, lg, lb):
    mu = jnp.mean(c, axis=-1, keepdims=True)
    cc = c - mu
    rstd = lax.rsqrt(jnp.mean(cc * cc, axis=-1, keepdims=True) + EPS)
    xh = cc * rstd
    ln = xh * lg + lb
    sig = jax.nn.sigmoid(ln)
    return xh, rstd, ln, sig


def conv_fwd(proj, cw, cb, lg, lb, gm, name):
    S = proj.shape[0]
    bs = _rows(S, SEQ_TILE)
    nb = S // bs
    H = 16

    def body(ap, ac, an, gp, gc, gn, cw_ref, cb_ref, lg_ref, lb_ref, gm_ref, y_ref, win_ref, phases_ref, c_ref):
        i = pl.program_id(0)
        win_ref[0:H, :] = jnp.where(i > 0, _glu(ap[...], gp[...]), 0.0)
        win_ref[H:H + bs, :] = _glu(ac[...], gc[...])
        win_ref[H + bs:2 * H + bs, :] = jnp.where(i < nb - 1, _glu(an[...], gn[...]), 0.0)
        win_ref[bs + 2 * H:, :] = jnp.zeros((SUBLANES, C_A), F32)
        _build_phases(win_ref, phases_ref)
        _conv_taps(phases_ref, {t: H - CONV_HALF + t for t in range(CONV_WIDTH)}, bs, cw_ref, cb_ref[...], c_ref)
        _, _, ln, sig = _layernorm_silu(c_ref[...], lg_ref[...], lb_ref[...])
        y_ref[...] = (_rms(ln * sig) * gm_ref[...]).astype(y_ref.dtype)

    a_specs = _halo_specs(bs, H, C_A, 0, S)
    g_specs = _halo_specs(bs, H, C_A, 1, S)
    vec = _vec_spec(C_A)
    return pl.pallas_call(
        body, name=name, grid=(nb,),
        in_specs=[*a_specs, *g_specs, pl.BlockSpec((32, C_A), lambda i: (0, 0)), vec, vec, vec, vec],
        out_specs=_row_spec(bs, C_A), out_shape=jax.ShapeDtypeStruct((S, C_A), BF16),
        scratch_shapes=[pltpu.VMEM((bs + 2 * H + SUBLANES, C_A), F32), pltpu.VMEM((SUBLANES, bs + 2 * H, C_A), F32),
                        pltpu.VMEM((bs, C_A), F32)],
        compiler_params=_params("parallel"),
    )(proj, proj, proj, proj, proj, proj, cw, cb, lg, lb, gm)


def conv_bwd(proj, dy, cw, cb, lg, lb, gm, name):
    S = proj.shape[0]
    bs = _rows(S, SEQ_TILE)
    nb = S // bs
    H = 32
    HC = 16
    bc = bs + 2 * HC

    def body(ap, ac, an, gp, gc, gn, dp, dc_, dn, cw_ref, cb_ref, lg_ref, lb_ref, gm_ref,
             dproj_ref, dcw_ref, dcb_ref, dlg_ref, dlb_ref, dgm_ref, win_ref, dcs_ref, phases_ref, c_ref):
        i = pl.program_id(0)
        win_ref[0:H, :] = jnp.where(i > 0, _glu(ap[...], gp[...]), 0.0)
        win_ref[H:H + bs, :] = _glu(ac[...], gc[...])
        win_ref[H + bs:2 * H + bs, :] = jnp.where(i < nb - 1, _glu(an[...], gn[...]), 0.0)
        win_ref[bs + 2 * H:, :] = jnp.zeros((SUBLANES, C_A), F32)
        _build_phases(win_ref, phases_ref)
        _conv_taps(phases_ref, {t: H - HC - CONV_HALF + t for t in range(CONV_WIDTH)}, bc, cw_ref, cb_ref[...], c_ref)
        xh, rstd, ln, sig = _layernorm_silu(c_ref[...], lg_ref[...], lb_ref[...])
        ya = ln * sig
        dyv = jnp.concatenate([dp[...], dc_[...], dn[...]], axis=0).astype(F32)
        dya, n = _rms_bwd(ya, dyv * gm_ref[...])
        dln = dya * (sig * (1.0 + ln * (1.0 - sig)))
        dxh = dln * lg_ref[...]
        dcv = rstd * (dxh - jnp.mean(dxh, axis=-1, keepdims=True) - xh * jnp.mean(dxh * xh, axis=-1, keepdims=True))
        pos = i * bs - HC + lax.broadcasted_iota(jnp.int32, (bc, 1), 0)
        dcv = jnp.where((pos >= 0) & (pos < S), dcv, 0.0)
        dcs_ref[0:bc, :] = dcv
        dcs_ref[bc:, :] = jnp.zeros((SUBLANES, C_A), F32)

        @pl.when(i == 0)
        def _():
            for r in (dcw_ref, dcb_ref, dlg_ref, dlb_ref, dgm_ref):
                r[...] = jnp.zeros_like(r)

        mid = slice(HC, HC + bs)
        dcb_ref[...] += jnp.sum(dcv[mid], axis=0, keepdims=True)
        dlg_ref[...] += jnp.sum((dln * xh)[mid], axis=0, keepdims=True)
        dlb_ref[...] += jnp.sum(dln[mid], axis=0, keepdims=True)
        dgm_ref[...] += jnp.sum((dyv * n)[mid], axis=0, keepdims=True)
        dcm = dcv[mid]
        for t in range(CONV_WIDTH):
            dcw_ref[t:t + 1, :] += jnp.sum(dcm * _tap_rows(phases_ref, H - CONV_HALF + t, bs), axis=0, keepdims=True)
        _build_phases(dcs_ref, phases_ref)
        _conv_taps(phases_ref, {t: HC + CONV_HALF - t for t in range(CONV_WIDTH)}, bs, cw_ref, 0.0, c_ref)
        dh = c_ref[0:bs, :]
        a = ac[...].astype(F32)
        sg = jax.nn.sigmoid(gc[...].astype(F32))
        dproj_ref[:, :C_A] = (dh * sg).astype(dproj_ref.dtype)
        dproj_ref[:, C_A:] = (dh * a * sg * (1.0 - sg)).astype(dproj_ref.dtype)

    a_specs = _halo_specs(bs, H, C_A, 0, S)
    g_specs = _halo_specs(bs, H, C_A, 1, S)
    d_specs = _halo_specs(bs, HC, C_A, 0, S)
    vec = _vec_spec(C_A)
    full = pl.BlockSpec((32, C_A), lambda i: (0, 0))
    vshape = jax.ShapeDtypeStruct((1, C_A), F32)
    return pl.pallas_call(
        body, name=name, grid=(nb,),
        in_specs=[*a_specs, *g_specs, *d_specs, full, vec, vec, vec, vec],
        out_specs=[_row_spec(bs, 2 * C_A), full, vec, vec, vec, vec],
        out_shape=[jax.ShapeDtypeStruct((S, 2 * C_A), BF16), jax.ShapeDtypeStruct((32, C_A), F32),
                   vshape, vshape, vshape, vshape],
        scratch_shapes=[pltpu.VMEM((bs + 2 * H + SUBLANES, C_A), F32), pltpu.VMEM((bc + SUBLANES, C_A), F32),
                        pltpu.VMEM((SUBLANES, bs + 2 * H, C_A), F32), pltpu.VMEM((bc, C_A), F32)],
        compiler_params=_params("arbitrary"),
    )(proj, proj, proj, proj, proj, proj, dy, dy, dy, cw, cb, lg, lb, gm)


POOL_HALO = 16


def _shift(x, k):
    n = x.shape[0]
    return pltpu.roll(x, (-k) % n, axis=0)


def _pool_means(u, pos, S):
    w2 = _shift(u, -1) + u
    w4 = _shift(w2, -1) + _shift(w2, 1)
    w8 = _shift(w4, -2) + _shift(w4, 2)
    w16 = _shift(w8, -4) + _shift(w8, 4)
    sums = (w2, w4, w8, w16)
    lane = lax.broadcasted_iota(jnp.int32, (1, C_C), 1)
    total = jnp.zeros_like(u)
    inv = jnp.zeros_like(u)
    for gi, win in enumerate(POOL_WINDOWS):
        cnt = jnp.minimum(pos + (win - win // 2), S) - jnp.maximum(pos - win // 2, 0)
        icnt = 1.0 / jnp.maximum(cnt, 1).astype(F32)
        sel = (lane >= gi * C_G) & (lane < (gi + 1) * C_G)
        total = jnp.where(sel, sums[gi], total)
        inv = jnp.where(sel, icnt, inv)
    return total * inv - u, inv


def _pool_adjoint(e):
    v2 = e + _shift(e, 1)
    v4 = _shift(v2, -1) + _shift(v2, 1)
    v8 = _shift(v4, -2) + _shift(v4, 2)
    v16 = _shift(v8, -4) + _shift(v8, 4)
    sums = (v2, v4, v8, v16)
    lane = lax.broadcasted_iota(jnp.int32, (1, C_C), 1)
    out = jnp.zeros_like(e)
    for gi in range(len(POOL_WINDOWS)):
        sel = (lane >= gi * C_G) & (lane < (gi + 1) * C_G)
        out = jnp.where(sel, sums[gi], out)
    return out


def _pool_window(up, uc, un, i, nb, bs, S):
    H = POOL_HALO
    u = jnp.concatenate([jnp.where(i > 0, up[...].astype(F32), 0.0), uc[...].astype(F32),
                         jnp.where(i < nb - 1, un[...].astype(F32), 0.0)], axis=0)
    pos = i * bs - H + lax.broadcasted_iota(jnp.int32, (bs + 2 * H, 1), 0)
    return u, pos


def _pool_mix(pooled, pw_ref):
    outs = []
    for gi in range(len(POOL_WINDOWS)):
        outs.append(jnp.dot(pooled[:, gi * C_G:(gi + 1) * C_G].astype(BF16), pw_ref[gi].astype(BF16),
                            preferred_element_type=F32))
    return jnp.concatenate(outs, axis=1)


def pool_fwd(proj, pw, ps, gm, name):
    S, width = proj.shape
    col = width // C_C - 1
    bs = _rows(S, SEQ_TILE)
    nb = S // bs
    H = POOL_HALO

    def body(up, uc, un, pw_ref, ps_ref, gm_ref, y_ref):
        i = pl.program_id(0)
        u, pos = _pool_window(up, uc, un, i, nb, bs, S)
        pooled, _ = _pool_means(u, pos, S)
        mixed = _pool_mix(pooled[H:H + bs], pw_ref)
        y_ref[...] = (_rms(mixed * ps_ref[...]) * gm_ref[...]).astype(y_ref.dtype)

    vec = _vec_spec(C_C)
    return pl.pallas_call(
        body, name=name, grid=(nb,),
        in_specs=[*_halo_specs(bs, H, C_C, col, S), pl.BlockSpec((4, C_G, C_G), lambda i: (0, 0, 0)), vec, vec],
        out_specs=_row_spec(bs, C_C), out_shape=jax.ShapeDtypeStruct((S, C_C), BF16),
        compiler_params=_params("parallel"),
    )(proj, proj, proj, pw, ps, gm)


def pool_bwd(proj, dy, pw, ps, gm, name):
    S, width = proj.shape
    col = width // C_C - 1
    dcol = dy.shape[1] // C_C - 1
    bs = _rows(S, SEQ_TILE)
    nb = S // bs
    H = POOL_HALO
    W = bs + 2 * H

    def body(up, uc, un, dp, dc_, dn, pw_ref, ps_ref, gm_ref, du_ref, dpw_ref, dps_ref, dgm_ref):
        i = pl.program_id(0)
        u, pos = _pool_window(up, uc, un, i, nb, bs, S)
        pooled, inv = _pool_means(u, pos, S)
        mixed = _pool_mix(pooled, pw_ref)
        dyv = jnp.concatenate([dp[...], dc_[...], dn[...]], axis=0).astype(F32)
        dyc, n = _rms_bwd(mixed * ps_ref[...], dyv * gm_ref[...])
        dmixed = dyc * ps_ref[...]
        dmb = dmixed.astype(BF16)
        dpooled = jnp.concatenate(
            [lax.dot_general(dmb[:, gi * C_G:(gi + 1) * C_G], pw_ref[gi].astype(BF16), (((1,), (1,)), ((), ())),
                             preferred_element_type=F32) for gi in range(len(POOL_WINDOWS))], axis=1)
        dpooled = jnp.where((pos >= 0) & (pos < S), dpooled, 0.0)
        du = _pool_adjoint(dpooled * inv) - dpooled
        du_ref[...] = du[H:H + bs].astype(du_ref.dtype)

        @pl.when(i == 0)
        def _():
            for r in (dpw_ref, dps_ref, dgm_ref):
                r[...] = jnp.zeros_like(r)

        mid = slice(H, H + bs)
        dps_ref[...] += jnp.sum((dyc * mixed)[mid], axis=0, keepdims=True)
        dgm_ref[...] += jnp.sum((dyv * n)[mid], axis=0, keepdims=True)
        pb = pooled[mid].astype(BF16)
        for gi in range(len(POOL_WINDOWS)):
            sl = slice(gi * C_G, (gi + 1) * C_G)
            dpw_ref[gi] += lax.dot_general(pb[:, sl], dmb[mid][:, sl], (((0,), (0,)), ((), ())),
                                           preferred_element_type=F32)

    vec = _vec_spec(C_C)
    full = pl.BlockSpec((4, C_G, C_G), lambda i: (0, 0, 0))
    vshape = jax.ShapeDtypeStruct((1, C_C), F32)
    return pl.pallas_call(
        body, name=name, grid=(nb,),
        in_specs=[*_halo_specs(bs, H, C_C, col, S), *_halo_specs(bs, H, C_C, dcol, S), full, vec, vec],
        out_specs=[_row_spec(bs, C_C), full, vec, vec],
        out_shape=[jax.ShapeDtypeStruct((S, C_C), BF16), jax.ShapeDtypeStruct((4, C_G, C_G), F32), vshape, vshape],
        compiler_params=_params("arbitrary"),
    )(proj, proj, proj, dy, dy, dy, pw, ps, gm)


def rope_tables(S):
    pos = jnp.arange(S, dtype=F32)
    inv = ROPE_THETA ** (-jnp.arange(0, ROT_DIM, 2, dtype=F32) / ROT_DIM)
    ang = pos[:, None] * inv[None, :]
    half = ROT_DIM // 2
    cos, sin = jnp.cos(ang), jnp.sin(ang)
    zeros = jnp.zeros((S, half), F32)
    rest = jnp.zeros((S, HEAD_DIM - ROT_DIM), F32)
    per_head = (jnp.concatenate([cos, cos, rest + 1.0], axis=1), jnp.concatenate([-sin, zeros, rest], axis=1),
                jnp.concatenate([zeros, sin, rest], axis=1))
    return tuple(jnp.tile(t, (1, LANES // HEAD_DIM)) for t in per_head)


def _rotate(t, c, s1, s2, sign):
    half = ROT_DIM // 2
    return t * c + sign * (pltpu.roll(t, LANES - half, axis=1) * s1 + pltpu.roll(t, half, axis=1) * s2)


def rope_fwd(proj, tables, name):
    S = proj.shape[0]
    tr = _rows(S, ROW_TILE)
    qcol = 2 * C_A // C_B
    nd = len(DILATIONS)

    def body(q_ref, k_ref, v_ref, c_ref, s1_ref, s2_ref, *outs):
        c, s1, s2 = c_ref[...], s1_ref[...], s2_ref[...]
        qs, ks = [], []
        for p in range(C_B // LANES):
            sl = slice(p * LANES, (p + 1) * LANES)
            qs.append((_rotate(q_ref[:, sl].astype(F32), c, s1, s2, 1.0) * HEAD_DIM ** -0.5).astype(BF16))
            ks.append(_rotate(k_ref[:, sl].astype(F32), c, s1, s2, 1.0).astype(BF16))
        tensors = (jnp.concatenate(qs, axis=1), jnp.concatenate(ks, axis=1), v_ref[...])
        for n, d in enumerate(DILATIONS):
            for t, x in enumerate(tensors):
                _store_dilated(outs[3 * n + t], x, d)

    tab = _row_spec(tr, LANES)
    res = pl.pallas_call(
        body, name=name, grid=(S // tr,),
        in_specs=[pl.BlockSpec((tr, C_B), lambda i, col=qcol + n: (i, col)) for n in range(3)] + [tab, tab, tab],
        out_specs=[_dilated_spec(tr, d, C_B) for d in DILATIONS for _ in range(3)],
        out_shape=[jax.ShapeDtypeStruct((d, S // d, C_B), BF16) for d in DILATIONS for _ in range(3)],
        compiler_params=_params("parallel"),
    )(proj, proj, proj, *tables)
    return [tuple(res[3 * n:3 * n + 3]) for n in range(nd)]


def rope_bwd(dqs, dks, dvs, tables, dconv, du, name):
    S = dconv.shape[0]
    tr = _rows(S, ROW_TILE)
    n = len(dqs)
    base = 2 * C_A
    width = base + 3 * C_B + C_C

    def body(*refs):
        dq_refs, dk_refs, dv_refs = refs[:n], refs[n:2 * n], refs[2 * n:3 * n]
        c_ref, s1_ref, s2_ref, dconv_ref, du_ref, o_ref = refs[3 * n:]
        c, s1, s2 = c_ref[...], s1_ref[...], s2_ref[...]
        o_ref[:, :base] = dconv_ref[...]
        o_ref[:, base + 3 * C_B:] = du_ref[...]
        dq_all = sum(_load_dilated(r, d).astype(F32) for r, d in zip(dq_refs, DILATIONS))
        dk_all = sum(_load_dilated(r, d).astype(F32) for r, d in zip(dk_refs, DILATIONS))
        dv_all = sum(_load_dilated(r, d).astype(F32) for r, d in zip(dv_refs, DILATIONS))
        for p in range(C_B // LANES):
            sl = slice(p * LANES, (p + 1) * LANES)
            dq, dk, dv = dq_all[:, sl], dk_all[:, sl], dv_all[:, sl]
            at = base + p * LANES
            o_ref[:, at:at + LANES] = (_rotate(dq, c, s1, s2, -1.0) * HEAD_DIM ** -0.5).astype(BF16)
            o_ref[:, C_B + at:C_B + at + LANES] = _rotate(dk, c, s1, s2, -1.0).astype(BF16)
            o_ref[:, 2 * C_B + at:2 * C_B + at + LANES] = dv.astype(BF16)

    tab = _row_spec(tr, LANES)
    return pl.pallas_call(
        body, name=name, grid=(S // tr,),
        in_specs=[_dilated_spec(tr, d, C_B) for _ in range(3) for d in DILATIONS]
        + [tab, tab, tab, _row_spec(tr, base), _row_spec(tr, C_C)],
        out_specs=_row_spec(tr, width), out_shape=jax.ShapeDtypeStruct((S, width), BF16),
        compiler_params=_params("parallel"),
    )(*dqs, *dks, *dvs, *tables, dconv, du)


def _attn_specs(bq, width, L):
    per = bq // ATTN_HALF
    last = L // ATTN_HALF - 1
    cur = pl.BlockSpec((None, bq, width), lambda r, j: (r, j, 0))
    prev = pl.BlockSpec((None, ATTN_HALF, width), lambda r, j: (r, jnp.maximum(j * per - 1, 0), 0))
    nxt = pl.BlockSpec((None, ATTN_HALF, width), lambda r, j: (r, jnp.minimum((j + 1) * per, last), 0))
    return prev, cur, nxt


def _window(refs, sl):
    return jnp.concatenate([r[:, sl] for r in refs], axis=0)


def _band_mask(j, bq, L, rows_are_window):
    bw = bq + 2 * ATTN_HALF
    if rows_are_window:
        rp = j * bq - ATTN_HALF + lax.broadcasted_iota(jnp.int32, (bw, 1), 0)
        cp = j * bq + lax.broadcasted_iota(jnp.int32, (1, bq), 1)
        return (jnp.abs(rp - cp) <= ATTN_HALF) & (rp >= 0) & (rp < L)
    rp = j * bq + lax.broadcasted_iota(jnp.int32, (bq, 1), 0)
    cp = j * bq - ATTN_HALF + lax.broadcasted_iota(jnp.int32, (1, bw), 1)
    return (jnp.abs(rp - cp) <= ATTN_HALF) & (cp >= 0) & (cp < L)


def _head_col(stats, h):
    lane = lax.broadcasted_iota(jnp.int32, (1, LANES), 1)
    return jnp.sum(jnp.where(lane == h, stats, 0.0), axis=1, keepdims=True)


def _stack_heads(x):
    first = lax.broadcasted_iota(jnp.int32, (1, LANES), 1) < HEAD_DIM
    zero = jnp.zeros_like(x)
    return jnp.concatenate([jnp.where(first, x, zero), jnp.where(first, zero, x)], axis=0)


_NT = (((1,), (1,)), ((), ()))
_TN = (((0,), (0,)), ((), ()))


def attn_fwd_pattern(qd, kd, vd, name):
    d, L, _ = qd.shape
    bq = _rows(L, ATTN_BLOCK)

    def body(q_ref, kp, kc, kn, vp, vc, vn, o_ref, lse_ref):
        j = pl.program_id(1)
        mask = _band_mask(j, bq, L, False)
        mask2 = jnp.concatenate([mask, mask], axis=0)
        lane = lax.broadcasted_iota(jnp.int32, (1, LANES), 1)
        first = lane < HEAD_DIM
        lse = jnp.zeros((bq, LANES), F32)
        for p in range(C_B // LANES):
            sl = slice(p * LANES, (p + 1) * LANES)
            kw = _window((kp, kc, kn), sl)
            vw = _window((vp, vc, vn), sl)
            s = jnp.where(mask2, lax.dot_general(_stack_heads(q_ref[:, sl]), kw, _NT, preferred_element_type=F32), NEG)
            m = jnp.max(s, axis=1, keepdims=True)
            e = jnp.exp(s - m)
            l = jnp.sum(e, axis=1, keepdims=True)
            o = jnp.dot(e.astype(BF16), vw, preferred_element_type=F32) * (1.0 / l)
            stat = m + jnp.log(l)
            lse = jnp.where(lane == 2 * p, stat[:bq], jnp.where(lane == 2 * p + 1, stat[bq:], lse))
            o_ref[:, sl] = jnp.where(first, o[:bq], o[bq:]).astype(o_ref.dtype)
        lse_ref[...] = lse

    kv = _attn_specs(bq, C_B, L)
    return pl.pallas_call(
        body, name=name, grid=(d, L // bq), in_specs=[kv[1], *kv, *kv],
        out_specs=[kv[1], pl.BlockSpec((None, bq, LANES), lambda r, j: (r, j, 0))],
        out_shape=[jax.ShapeDtypeStruct((d, L, C_B), BF16), jax.ShapeDtypeStruct((d, L, LANES), F32)],
        compiler_params=_params("parallel", "parallel"),
    )(qd, kd, kd, kd, vd, vd, vd)


def attn_combine(os_, lses, gm, ya, yc, name):
    S = ya.shape[0]
    tr = _rows(S, ROW_TILE)
    n = len(os_)

    def body(*refs):
        o_refs, l_refs = refs[:n], refs[n:2 * n]
        gm_ref, ya_ref, yc_ref, y_ref, yt_ref, out_ref, lse_ref = refs[2 * n:]
        ls = [_load_dilated(r, d) for r, d in zip(l_refs, DILATIONS)]
        os_tok = [_load_dilated(r, d) for r, d in zip(o_refs, DILATIONS)]
        mx = functools.reduce(jnp.maximum, ls)
        ws = [jnp.exp(l - mx) for l in ls]
        den = sum(ws)
        lse_ref[...] = mx + jnp.log(den)
        wn = [w / den for w in ws]
        lane = lax.broadcasted_iota(jnp.int32, (1, LANES), 1)
        first = lane < HEAD_DIM
        blocks = []
        for p in range(C_B // LANES):
            sl = slice(p * LANES, (p + 1) * LANES)
            acc = jnp.zeros((tr, LANES), F32)
            for w, o in zip(wn, os_tok):
                acc = acc + jnp.where(first, _head_col(w, 2 * p), _head_col(w, 2 * p + 1)) * o[:, sl].astype(F32)
            blocks.append(acc)
        out = jnp.concatenate(blocks, axis=1)
        out_ref[...] = out.astype(out_ref.dtype)
        y = jnp.concatenate([ya_ref[...].astype(F32), _rms(out) * gm_ref[...], yc_ref[...].astype(F32)], axis=1)
        y_ref[...] = y.astype(y_ref.dtype)
        yt_ref[...] = y.T.astype(yt_ref.dtype)

    st = _row_spec(tr, LANES)
    mix = C_A + C_B + C_C
    return pl.pallas_call(
        body, name=name, grid=(S // tr,),
        in_specs=[_dilated_spec(tr, d, C_B) for d in DILATIONS] + [_dilated_spec(tr, d, LANES) for d in DILATIONS]
        + [_vec_spec(C_B), _row_spec(tr, C_A), _row_spec(tr, C_C)],
        out_specs=[_row_spec(tr, mix), _col_spec(mix, tr), _row_spec(tr, C_B), st],
        out_shape=[jax.ShapeDtypeStruct((S, mix), BF16), jax.ShapeDtypeStruct((mix, S), BF16),
                   jax.ShapeDtypeStruct((S, C_B), BF16), jax.ShapeDtypeStruct((S, LANES), F32)],
        compiler_params=_params("parallel"),
    )(*os_, *lses, gm, ya, yc)


def attn_out_bwd(out, lse, dy, gm, name):
    S = out.shape[0]
    tr = _rows(S, ROW_TILE)
    nd = len(DILATIONS)

    def body(o_ref, lse_ref, dy1, dy2, g_ref, *outs):
        dg_ref = outs[-1]
        o = o_ref[...].astype(F32)
        dyv = jnp.concatenate([dy1[...], dy2[...]], axis=1).astype(F32)
        do, n = _rms_bwd(o, dyv * g_ref[...])
        dob = do.astype(BF16)
        prod = dob.astype(F32) * o
        lane = lax.broadcasted_iota(jnp.int32, (1, LANES), 1)
        first = lane < HEAD_DIM
        delta = jnp.zeros((tr, LANES), F32)
        for p in range(C_B // LANES):
            blk = prod[:, p * LANES:(p + 1) * LANES]
            delta = jnp.where(lane == 2 * p, jnp.sum(jnp.where(first, blk, 0.0), axis=1, keepdims=True), delta)
            delta = jnp.where(lane == 2 * p + 1, jnp.sum(jnp.where(first, 0.0, blk), axis=1, keepdims=True), delta)
        lse_v = lse_ref[...]
        for k, d in enumerate(DILATIONS):
            _store_dilated(outs[3 * k], dob, d)
            _store_dilated(outs[3 * k + 1], lse_v, d)
            _store_dilated(outs[3 * k + 2], delta, d)

        @pl.when(pl.program_id(0) == 0)
        def _():
            dg_ref[...] = jnp.zeros_like(dg_ref)

        dg_ref[...] += jnp.sum(dyv * n, axis=0, keepdims=True)

    widths = (C_B, LANES, LANES)
    dtypes = (BF16, F32, F32)
    res = pl.pallas_call(
        body, name=name, grid=(S // tr,),
        in_specs=[_row_spec(tr, C_B), _row_spec(tr, LANES), pl.BlockSpec((tr, C_A), lambda i: (i, 1)),
                  pl.BlockSpec((tr, C_A), lambda i: (i, 2)), _vec_spec(C_B)],
        out_specs=[_dilated_spec(tr, d, wd) for d in DILATIONS for wd in widths] + [_vec_spec(C_B)],
        out_shape=[jax.ShapeDtypeStruct((d, S // d, wd), dt) for d in DILATIONS for wd, dt in zip(widths, dtypes)]
        + [jax.ShapeDtypeStruct((1, C_B), F32)],
        compiler_params=_params("arbitrary"),
    )(out, lse, dy, dy, gm)
    return [tuple(res[3 * k:3 * k + 3]) for k in range(nd)], res[-1]


def attn_bwd_pattern(qd, kd, vd, dod, lsed, deltad, name, comm=None):
    d, L, _ = qd.shape
    nc = comm.n if comm is not None else 0
    bq = _rows(L, ATTN_BLOCK)

    nb = L // bq
    bw = bq + 2 * ATTN_HALF
    lo = bq - ATTN_HALF

    def body(qc, dc_, lc, tc, kp, kc, kn, vp, vc, vn, *rest):
        c_ins, (dq_ref, dk_ref, dv_ref), c_outs = rest[:nc], rest[nc:nc + 3], rest[nc + 3:2 * nc + 3]
        dk_acc, dv_acc = rest[2 * nc + 3:2 * nc + 5]
        sems = rest[2 * nc + 5:]
        r, j = pl.program_id(0), pl.program_id(1)
        if comm is not None:
            @pl.when((r == 0) & (j == 0))
            def _():
                comm.start(c_ins, c_outs, sems)

        @pl.when(j == 0)
        def _():
            dk_acc[...] = jnp.zeros_like(dk_acc)
            dv_acc[...] = jnp.zeros_like(dv_acc)

        @pl.when(j > 0)
        def _():
            for acc in (dk_acc, dv_acc):
                acc[0:bq, :] = acc[bq:2 * bq, :]
                acc[bq:2 * bq, :] = acc[2 * bq:, :]
                acc[2 * bq:, :] = jnp.zeros((bq, C_B), F32)

        @pl.when(j < nb)
        def _():
            mask = _band_mask(j, bq, L, False)
            mask2 = jnp.concatenate([mask, mask], axis=0)
            first = lax.broadcasted_iota(jnp.int32, (1, LANES), 1) < HEAD_DIM
            lse_c, delta_c = lc[...], tc[...]
            for p in range(C_B // LANES):
                sl = slice(p * LANES, (p + 1) * LANES)
                qs, dos = _stack_heads(qc[:, sl]), _stack_heads(dc_[:, sl])
                kw, vw = _window((kp, kc, kn), sl), _window((vp, vc, vn), sl)
                lse_s = jnp.concatenate([_head_col(lse_c, 2 * p), _head_col(lse_c, 2 * p + 1)], axis=0)
                delta_s = jnp.concatenate([_head_col(delta_c, 2 * p), _head_col(delta_c, 2 * p + 1)], axis=0)
                s = lax.dot_general(qs, kw, _NT, preferred_element_type=F32)
                pr = jnp.where(mask2, jnp.exp(s - lse_s), 0.0)
                dpr = lax.dot_general(dos, vw, _NT, preferred_element_type=F32)
                ds = (pr * (dpr - delta_s)).astype(BF16)
                dq = jnp.dot(ds, kw, preferred_element_type=F32)
                dq_ref[:, sl] = jnp.where(first, dq[:bq], dq[bq:]).astype(dq_ref.dtype)
                dk_acc[lo:lo + bw, sl] += lax.dot_general(ds, qs, _TN, preferred_element_type=F32)
                dv_acc[lo:lo + bw, sl] += lax.dot_general(pr.astype(BF16), dos, _TN, preferred_element_type=F32)

        dk_ref[...] = dk_acc[0:bq, :].astype(dk_ref.dtype)
        dv_ref[...] = dv_acc[0:bq, :].astype(dv_ref.dtype)
        if comm is not None:
            @pl.when((r == d - 1) & (j == nb))
            def _():
                comm.finish(c_ins, c_outs, sems)

    per = bq // ATTN_HALF
    last = L // ATTN_HALF - 1

    def clamp(j):
        return jnp.minimum(j, nb - 1)

    def specs(width):
        cur = pl.BlockSpec((None, bq, width), lambda r, j: (r, clamp(j), 0))
        prev = pl.BlockSpec((None, ATTN_HALF, width), lambda r, j: (r, jnp.maximum(clamp(j) * per - 1, 0), 0))
        nxt = pl.BlockSpec((None, ATTN_HALF, width), lambda r, j: (r, jnp.minimum((clamp(j) + 1) * per, last), 0))
        return prev, cur, nxt

    wide = specs(C_B)
    stat = specs(LANES)[1]
    lagged = pl.BlockSpec((None, bq, C_B), lambda r, j: (r, jnp.maximum(j - 1, 0), 0))
    shape = jax.ShapeDtypeStruct((d, L, C_B), BF16)
    res = pl.pallas_call(
        body, name=name, grid=(d, nb + 1), in_specs=[wide[1], wide[1], stat, stat, *wide, *wide] + [_ANY] * nc,
        out_specs=[wide[1], lagged, lagged] + [_ANY] * nc,
        out_shape=[shape] * 3 + (comm.out_shape if comm is not None else []),
        scratch_shapes=[pltpu.VMEM((3 * bq, C_B), F32), pltpu.VMEM((3 * bq, C_B), F32)]
        + (comm.sems if comm is not None else []),
        compiler_params=_params("arbitrary", "arbitrary"),
    )(qd, dod, lsed, deltad, kd, kd, kd, vd, vd, vd, *(comm.ins if comm is not None else []))
    return tuple(res[:3]), (res[3:] if comm is not None else None)


def _dilated_spec(tr, d, width):
    return pl.BlockSpec((d, tr // d, width), lambda i: (0, i, 0))


def _perm_matrix(n, d, inverse):
    lb = n // d
    row = lax.broadcasted_iota(jnp.int32, (n, n), 0)
    col = lax.broadcasted_iota(jnp.int32, (n, n), 1)
    source = (row % d) * lb + row // d if inverse else (row % lb) * d + row // lb
    return (col == source).astype(BF16)


def _permute(p, x):
    if x.dtype == BF16:
        return jnp.dot(p, x, preferred_element_type=F32).astype(BF16)
    hi = x.astype(BF16)
    rest = x - hi.astype(F32)
    mid = rest.astype(BF16)
    lo = (rest - mid.astype(F32)).astype(BF16)
    return (jnp.dot(p, hi, preferred_element_type=F32) + jnp.dot(p, mid, preferred_element_type=F32)
            + jnp.dot(p, lo, preferred_element_type=F32))


def _store_dilated(ref, x, d):
    n = x.shape[0]
    if d == 1:
        ref[0] = x
        return
    y = _permute(_perm_matrix(n, d, False), x)
    lb = n // d
    for r in range(d):
        ref[r] = y[r * lb:(r + 1) * lb]


def _load_dilated(ref, d):
    if d == 1:
        return ref[0]
    y = jnp.concatenate([ref[r] for r in range(d)], axis=0)
    return _permute(_perm_matrix(y.shape[0], d, True), y)


def adamw(parts, w, m, v, name):
    n, R, C = parts.shape
    tr = _rows(R, ROW_TILE)

    def body(p_ref, w_ref, m_ref, v_ref, g_ref, d_ref, nm_ref, nv_ref):
        g = p_ref[0].astype(F32)
        for k in range(1, n):
            g = g + p_ref[k].astype(F32)
        mm = ADAM_B1 * m_ref[...] + (1.0 - ADAM_B1) * g
        vv = ADAM_B2 * v_ref[...] + (1.0 - ADAM_B2) * jnp.square(g)
        m_hat = mm / (1.0 - ADAM_B1 ** ADAM_STEP)
        v_hat = vv / (1.0 - ADAM_B2 ** ADAM_STEP)
        g_ref[...] = g
        d_ref[...] = -ADAM_LR * (m_hat / (jnp.sqrt(v_hat) + ADAM_EPS) + ADAM_WD * w_ref[...])
        nm_ref[...] = mm
        nv_ref[...] = vv

    spec = _row_spec(tr, C)
    shape = jax.ShapeDtypeStruct((R, C), F32)
    return pl.pallas_call(
        body, name=name, grid=(R // tr,),
        in_specs=[pl.BlockSpec((n, tr, C), lambda i: (0, i, 0)), spec, spec, spec],
        out_specs=[spec] * 4, out_shape=[shape] * 4, compiler_params=_params("parallel"),
    )(parts, w, m, v)


_ANY = pl.BlockSpec(memory_space=pl.ANY)


def _place():
    return lax.axis_index("x"), lax.axis_index("y"), lax.axis_index("c")


def _index(px, py, pc):
    return 4 * px + 2 * py + pc


class Gather:
    def __init__(self, shards):
        self.ins = list(shards)
        T = self.n = len(shards)
        self.out_shape = [jax.ShapeDtypeStruct((N_DEV, *s.shape), s.dtype) for s in shards]
        self.sems = [pltpu.SemaphoreType.DMA((T, 7)), pltpu.SemaphoreType.DMA((T, 7)), pltpu.SemaphoreType.DMA((T,))]

    def _plan(self, ins, outs, sems):
        send_sems, recv_sems, local_sems = sems
        x, y, c = _place()
        me, sibling = (x, y, c), (x, y, 1 - c)
        chips = [(1 - x, y), (x, 1 - y), (1 - x, 1 - y)]

        def copy(t, k, block, to, src=None):
            rows = outs[t].at[_index(*block)]
            return pltpu.make_async_remote_copy(
                src_ref=rows if src is None else src, dst_ref=rows, send_sem=send_sems.at[t, k],
                recv_sem=recv_sems.at[t, k], device_id=to, device_id_type=MESH)

        mine = [pltpu.make_async_copy(ins[t], outs[t].at[_index(*me)], local_sems.at[t]) for t in range(self.n)]
        first = []
        for t in range(self.n):
            first.append(copy(t, 0, me, sibling, src=ins[t]))
            first += [copy(t, 1 + j, me, (*chip, c), src=ins[t]) for j, chip in enumerate(chips)]
        return copy, mine, first, me, sibling, chips, c

    def start(self, ins, outs, sems):
        _, mine, first, *_ = self._plan(ins, outs, sems)
        for cp in mine + first:
            cp.start()

    def finish(self, ins, outs, sems):
        copy, mine, first, me, sibling, chips, c = self._plan(ins, outs, sems)
        passed = []
        for j, chip in enumerate(chips):
            for t in range(self.n):
                copy(t, 1 + j, (*chip, c), me).wait_recv()
                fwd = copy(t, 4 + j, (*chip, c), sibling)
                fwd.start()
                passed.append(fwd)
        for t in range(self.n):
            copy(t, 0, sibling, me).wait_recv()
            for j, chip in enumerate(chips):
                copy(t, 4 + j, (*chip, 1 - c), me).wait_recv()
        for cp in first + passed:
            cp.wait_send()
        for cp in mine:
            cp.wait()


class Exchange:
    def __init__(self, parts):
        self.ins = list(parts)
        T = self.n = len(parts)
        self.out_shape = [jax.ShapeDtypeStruct(p.shape, p.dtype) for p in parts]
        self.sems = [pltpu.SemaphoreType.DMA((T, 7)), pltpu.SemaphoreType.DMA((T, 7)), pltpu.SemaphoreType.DMA((T,))]

    def _plan(self, ins, outs, sems):
        send_sems, recv_sems, local_sems = sems
        x, y, c = _place()
        me = _index(x, y, c)
        copies = [pltpu.make_async_copy(ins[t].at[me], outs[t].at[me], local_sems.at[t]) for t in range(self.n)]
        for k in range(1, N_DEV):
            peer = ((x + (k >> 2)) % 2, (y + ((k >> 1) & 1)) % 2, (c + (k & 1)) % 2)
            there = _index(*peer)
            for t in range(self.n):
                copies.append(pltpu.make_async_remote_copy(
                    src_ref=ins[t].at[there], dst_ref=outs[t].at[me], send_sem=send_sems.at[t, k - 1],
                    recv_sem=recv_sems.at[t, k - 1], device_id=peer, device_id_type=MESH))
        return copies

    def start(self, ins, outs, sems):
        for cp in self._plan(ins, outs, sems):
            cp.start()

    def finish(self, ins, outs, sems):
        for cp in self._plan(ins, outs, sems):
            cp.wait()


def communicate(comm, name):
    T = comm.n

    def body(*refs):
        ins, outs, sems = refs[:T], refs[T:2 * T], refs[2 * T:]
        comm.start(ins, outs, sems)
        comm.finish(ins, outs, sems)

    return pl.pallas_call(
        body, name=name, in_specs=[_ANY] * T, out_specs=[_ANY] * T, out_shape=comm.out_shape,
        scratch_shapes=comm.sems,
    )(*comm.ins)


def all_gather(shards, name):
    return communicate(Gather(shards), name)


def exchange(parts, name):
    return communicate(Exchange(parts), name)


def _row(v):
    return v.reshape(1, -1)


def _hosted(res):
    return res if isinstance(res, tuple) else (res, None)


def mix_forward(x, h1, h1t, w, p, tables, comm=None):
    gm = p["g_mix"]
    proj, got = _hosted(matmul(h1, w["win_t"], "nt", BF16, "proj", comm=comm))
    ya = conv_fwd(proj, p["cw"], _row(p["conv_b"]), _row(p["conv_ln_g"]), _row(p["conv_ln_b"]), _row(gm[:C_A]),
                  "conv_fwd")
    dil = rope_fwd(proj, tables, "rope_fwd")
    os_, lses = [], []
    for d, (qd, kd, vd) in zip(DILATIONS, dil):
        o, lse = attn_fwd_pattern(qd, kd, vd, f"attn_fwd_d{d}")
        os_.append(o)
        lses.append(lse)
    yc = pool_fwd(proj, p["pool_w"], _row(p["pool_scale"]), _row(gm[C_A + C_B:]), "pool_fwd")
    y, yt, out, lse = attn_combine(os_, lses, _row(gm[C_A:C_A + C_B]), ya, yc, "attn_combine")
    z = matmul(y, w["wout"], "nn", BF16, "mix_out")
    x2, h2, h2t = norm_residual_rms(z, x, _row(p["g_post_mix"]), _row(p["g_pre_ffn"]), "res_mix")
    return x2, h2, h2t, dict(x=x, h1t=h1t, proj=proj, dil=dil, out=out, lse=lse, yt=yt, z=z), got


def ffn_forward(x2, h2, h2t, w, p, g_next=None, comm_in=None, comm_out=None):
    gu, got_in = _hosted(matmul(h2, w["wgu_t"], "nt", BF16, "ffn_in", comm=comm_in))
    a, at = swiglu_fwd(gu, "swiglu_fwd")
    f, got_out = _hosted(matmul(a, w["wd"], "nn", BF16, "ffn_out", tk=5632, comm=comm_out))
    saved = dict(x2=x2, h2t=h2t, gu=gu, at=at, f=f)
    if g_next is None:
        return norm_residual(f, x2, _row(p["g_post_ffn"]), "res_last"), None, saved, got_in, got_out
    x3, h1, h1t = norm_residual_rms(f, x2, _row(p["g_post_ffn"]), _row(g_next), "res_ffn")
    return x3, (h1, h1t), saved, got_in, got_out


def _to_blocks(g, by_columns):
    if by_columns:
        return g.T.reshape(N_DEV, -1, g.shape[0])
    return g.reshape(N_DEV, -1, g.shape[1])


def _exchange_of(g, by_columns):
    return Exchange([_to_blocks(g, by_columns)]) if g is not None else None


def ffn_backward(dx3, df, s, z, w, p, ride_act=None):
    da, got_act = _hosted(matmul(df, w["wd"], "nt", BF16, "d_ffn_act", comm=ride_act))
    dwd = matmul(s["at"], df, "nn", BF16, "dw_down", tm=1408, tn=1024, tk=2048)
    dgu = swiglu_bwd(s["gu"], da, "swiglu_bwd")
    dh2, got_wd = matmul(dgu, w["wgu_t"], "nn", BF16, "d_ffn_in", tk=5632, comm=_exchange_of(dwd, False))
    dwgu = matmul(s["h2t"], dgu, "nn", BF16, "dw_gate_up", tm=1024, tn=1408, tk=2048)
    dx2, dz, dg_pre_ffn, dg_post_mix = rms_bwd_chain(
        s["x2"], dh2, _row(p["g_pre_ffn"]), dx3, z, _row(p["g_post_mix"]), "rms_bwd_ffn_to_mix")
    return dx2, dz, dict(dwgu=dwgu, parts_wd=got_wd[0], got_act=got_act,
                         g_pre_ffn=dg_pre_ffn[0], g_post_mix=dg_post_mix[0])


def mix_backward(dx2, dz, s, w, p, tables, dwgu, below=None):
    gm = p["g_mix"]
    F = dwgu.shape[1] // 2
    dy = matmul(dz, w["wout"], "nt", BF16, "d_mix")
    dwout = matmul(s["yt"], dz, "nn", BF16, "dw_out", tm=1024, tn=1024, tk=2048)
    dconv, dcw, dcb, dlg, dlb, dgm_a = conv_bwd(
        s["proj"], dy, p["cw"], _row(p["conv_b"]), _row(p["conv_ln_g"]), _row(p["conv_ln_b"]), _row(gm[:C_A]),
        "conv_bwd")
    stats, dgm_b = attn_out_bwd(s["out"], s["lse"], dy, _row(gm[C_A:C_A + C_B]), "attn_out_bwd")
    rides = [_exchange_of(dwgu[:, :F], True), _exchange_of(dwgu[:, F:], True), None]
    dqs, dks, dvs, got = [], [], [], []
    for d, (qd, kd, vd), (dod, lsed, deltad), ride in zip(DILATIONS, s["dil"], stats, rides):
        (dq, dk, dv), parts = attn_bwd_pattern(qd, kd, vd, dod, lsed, deltad, f"attn_bwd_d{d}", comm=ride)
        dqs.append(dq)
        dks.append(dk)
        dvs.append(dv)
        got.append(parts)
    du, dpw, dps, dgm_c = pool_bwd(s["proj"], dy, p["pool_w"], _row(p["pool_scale"]), _row(gm[C_A + C_B:]),
                                   "pool_bwd")
    dproj = rope_bwd(dqs, dks, dvs, tables, dconv, du, "rope_bwd")
    dh1 = matmul(dproj, w["win_t"], "nn", BF16, "d_proj", tk=4608)
    dwin, got_out = matmul(s["h1t"], dproj, "nn", BF16, "dw_in", tm=1024, tn=1152, tk=2048,
                           comm=_exchange_of(dwout, False))
    grads = dict(
        dwin=dwin, parts_out=got_out[0], parts_gate=got[0][0], parts_up=got[1][0],
        conv_w=dcw[:CONV_WIDTH], conv_b=dcb[0], conv_ln_g=dlg[0], conv_ln_b=dlb[0], pool_w=dpw, pool_scale=dps[0],
        g_mix=jnp.concatenate([dgm_a[0], dgm_b[0], dgm_c[0]]))
    if below is None:
        dx, dg_pre_mix = rms_bwd(s["x"], dh1, _row(p["g_pre_mix"]), dx2, F32, "rms_bwd_first")
        df = None
    else:
        dx, df, dg_pre_mix, dg_post_ffn = rms_bwd_chain(
            s["x"], dh1, _row(p["g_pre_mix"]), dx2, below[0], _row(below[1]), "rms_bwd_mix_to_ffn")
        grads["g_post_ffn_below"] = dg_post_ffn[0]
    grads["g_pre_mix"] = dg_pre_mix[0]
    return dx, df, grads


WEIGHTS = ["w_in", "conv_w", "conv_b", "conv_ln_g", "conv_ln_b", "pool_w", "pool_scale", "g_mix", "w_out", "g_pre_mix",
           "g_post_mix", "g_pre_ffn", "g_post_ffn", "w_gate", "w_up", "w_down"]
BIG = ["w_in", "w_out", "w_gate", "w_up", "w_down"]
REPLICATED = ["conv_b", "conv_ln_g", "conv_ln_b", "pool_w", "pool_scale", "g_mix", "g_pre_mix", "g_post_mix",
              "g_pre_ffn", "g_post_ffn"]
PACK_ROWS = 256


def adamw_layer(parts, w, m, v, layer, prev, name):
    n, R, C = parts.shape
    tr = _rows(R, ROW_TILE)

    def body(p_ref, w_ref, m_ref, v_ref, *rest):
        g_ref, d_ref, nm_ref, nv_ref = rest[-4:]
        g = p_ref[0].astype(F32)
        for k in range(1, n):
            g = g + p_ref[k].astype(F32)
        mm = ADAM_B1 * m_ref[...] + (1.0 - ADAM_B1) * g
        vv = ADAM_B2 * v_ref[...] + (1.0 - ADAM_B2) * jnp.square(g)
        m_hat = mm / (1.0 - ADAM_B1 ** ADAM_STEP)
        v_hat = vv / (1.0 - ADAM_B2 ** ADAM_STEP)
        g_ref[...] = g
        d_ref[...] = -ADAM_LR * (m_hat / (jnp.sqrt(v_hat) + ADAM_EPS) + ADAM_WD * w_ref[...])
        nm_ref[...] = mm
        nv_ref[...] = vv

    spec = pl.BlockSpec((None, tr, C), lambda i: (layer, i, 0))
    shape = jax.ShapeDtypeStruct(w.shape, F32)
    prev = list(prev) if prev is not None else []
    return pl.pallas_call(
        body, name=name, grid=(R // tr,),
        in_specs=[pl.BlockSpec((n, tr, C), lambda i: (0, i, 0)), spec, spec, spec] + [_ANY] * len(prev),
        out_specs=[spec] * 4, out_shape=[shape] * 4,
        input_output_aliases={4 + k: k for k in range(len(prev))}, compiler_params=_params("parallel"),
    )(parts, w, m, v, *prev)


def _pack(arrays):
    flat = jnp.concatenate([a.reshape(-1).astype(F32) for a in arrays])
    unit = PACK_ROWS * LANES
    padded = -(-flat.shape[0] // unit) * unit
    return jnp.pad(flat, (0, padded - flat.shape[0])).reshape(-1, LANES)


def _unpack(packed, like):
    flat = packed.reshape(-1)
    out, at = [], 0
    for a in like:
        out.append(flat[at:at + a.size].reshape(a.shape))
        at += a.size
    return out


def kernel(x, w_in, conv_w, conv_b, conv_ln_g, conv_ln_b, pool_w, pool_scale, g_mix, w_out, g_pre_mix, g_post_mix, g_pre_ffn, g_post_ffn, w_gate, w_up, w_down, loss_target, m_w_in, m_conv_w, m_conv_b, m_conv_ln_g, m_conv_ln_b, m_pool_w, m_pool_scale, m_g_mix, m_w_out, m_g_pre_mix, m_g_post_mix, m_g_pre_ffn, m_g_post_ffn, m_w_gate, m_w_up, m_w_down, v_w_in, v_conv_w, v_conv_b, v_conv_ln_g, v_conv_ln_b, v_pool_w, v_pool_scale, v_g_mix, v_w_out, v_g_pre_mix, v_g_post_mix, v_g_pre_ffn, v_g_post_ffn, v_w_gate, v_w_up, v_w_down):
    w = dict(w_in=w_in, conv_w=conv_w, conv_b=conv_b, conv_ln_g=conv_ln_g, conv_ln_b=conv_ln_b, pool_w=pool_w,
             pool_scale=pool_scale, g_mix=g_mix, w_out=w_out, g_pre_mix=g_pre_mix, g_post_mix=g_post_mix,
             g_pre_ffn=g_pre_ffn, g_post_ffn=g_post_ffn, w_gate=w_gate, w_up=w_up, w_down=w_down)
    m = dict(w_in=m_w_in, conv_w=m_conv_w, conv_b=m_conv_b, conv_ln_g=m_conv_ln_g, conv_ln_b=m_conv_ln_b,
             pool_w=m_pool_w, pool_scale=m_pool_scale, g_mix=m_g_mix, w_out=m_w_out, g_pre_mix=m_g_pre_mix,
             g_post_mix=m_g_post_mix, g_pre_ffn=m_g_pre_ffn, g_post_ffn=m_g_post_ffn, w_gate=m_w_gate, w_up=m_w_up,
             w_down=m_w_down)
    v = dict(w_in=v_w_in, conv_w=v_conv_w, conv_b=v_conv_b, conv_ln_g=v_conv_ln_g, conv_ln_b=v_conv_ln_b,
             pool_w=v_pool_w, pool_scale=v_pool_scale, g_mix=v_g_mix, w_out=v_w_out, g_pre_mix=v_g_pre_mix,
             g_post_mix=v_g_post_mix, g_pre_ffn=v_g_pre_ffn, g_post_ffn=v_g_post_ffn, w_gate=v_w_gate, w_up=v_w_up,
             w_down=v_w_down)
    depth = w_in.shape[0]
    xs, target = x[0], loss_target[0]
    S, D = xs.shape
    tables = rope_tables(S)

    cw_shard = jnp.pad(conv_w, ((0, 0), (0, 1), (0, 0)))
    cw_all = all_gather([cw_shard.reshape(-1, LANES)], "gather_conv_w")[0]
    cw_full = cw_all.reshape(N_DEV, depth, 32, -1).transpose(1, 2, 0, 3).reshape(depth, 32, C_A)
    small = []
    for l in range(depth):
        small.append({n: w[n][l] for n in REPLICATED})
        small[l]["cw"] = cw_full[l]

    def shard(n, l):
        return (w[n][l].T if n in ("w_in", "w_gate", "w_up") else w[n][l]).astype(BF16)

    def joined(blocks):
        return blocks.reshape(-1, blocks.shape[2])

    def mix_shards(l):
        return Gather([shard("w_in", l), shard("w_out", l)])

    def ffn_shards(l):
        return Gather([shard(n, l) for n in ("w_gate", "w_up", "w_down")])

    def mix_weights(blocks):
        return dict(win_t=joined(blocks[0]), wout=joined(blocks[1]))

    def ffn_weights(blocks):
        return dict(wgu_t=jnp.concatenate([joined(blocks[0]), joined(blocks[1])], axis=0), wd=joined(blocks[2]))

    w_mix = [None] * depth
    w_ffn = [None] * depth
    saved_mix, saved_ffn = [None] * depth, [None] * depth
    w_mix[0] = mix_weights(communicate(mix_shards(0), "gather_mix_weights"))
    h = xs
    normed = rms_fwd(xs, _row(small[0]["g_pre_mix"]), "rms_first")
    for l in range(depth):
        x2, h2, h2t, saved_mix[l], got = mix_forward(h, *normed, w_mix[l], small[l], tables,
                                                     ffn_shards(0) if l == 0 else None)
        if l == 0:
            w_ffn[0] = ffn_weights(got)
        more = l + 1 < depth
        h, normed, saved_ffn[l], got_in, got_out = ffn_forward(
            x2, h2, h2t, w_ffn[l], small[l], small[l + 1]["g_pre_mix"] if more else None,
            ffn_shards(l + 1) if more else None, mix_shards(l + 1) if more else None)
        if more:
            w_ffn[l + 1] = ffn_weights(got_in)
            w_mix[l + 1] = mix_weights(got_out)

    dh, sq = loss_head(h, target, "loss_head")
    loss = lax.psum(0.5 * jnp.sum(sq) / D, ("x", "y", "c"))

    big_out = {n: None for n in BIG}
    by_columns = ("w_in", "w_gate", "w_up")
    state = {n: tuple(jnp.swapaxes(t[n], 1, 2) if n in by_columns else t[n] for t in (w, m, v)) for n in BIG}

    def update(n, l, parts):
        big_out[n] = adamw_layer(parts, *state[n], l, big_out[n], f"adamw_{n}_layer{l}")

    small_grads = [None] * depth
    carried = None
    top = depth - 1
    df, dg_post_ffn = rms_bwd(saved_ffn[top]["f"], dh, _row(small[top]["g_post_ffn"]), None, BF16, "rms_bwd_last")
    post_ffn = {top: dg_post_ffn[0]}
    for l in reversed(range(depth)):
        dx2, dz, gf = ffn_backward(dh, df, saved_ffn[l], saved_mix[l]["z"], w_ffn[l], small[l],
                                   _exchange_of(carried, True))
        if carried is not None:
            update("w_in", l + 1, gf["got_act"][0])
        update("w_down", l, gf["parts_wd"])
        below = (saved_ffn[l - 1]["f"], small[l - 1]["g_post_ffn"]) if l > 0 else None
        dh, df, gm_ = mix_backward(dx2, dz, saved_mix[l], w_mix[l], small[l], tables, gf["dwgu"], below)
        if l > 0:
            post_ffn[l - 1] = gm_["g_post_ffn_below"]
        update("w_gate", l, gm_["parts_gate"])
        update("w_up", l, gm_["parts_up"])
        update("w_out", l, gm_["parts_out"])
        carried = gm_["dwin"]
        small_grads[l] = {**gf, **gm_, "g_post_ffn": post_ffn[l]}
    update("w_in", 0, exchange([_to_blocks(carried, True)], "exchange_last_grads")[0])

    names = REPLICATED + ["conv_w"]
    stacked = [jnp.stack([small_grads[l][n] for l in range(depth)]) for n in names]
    partial = all_gather([_pack(stacked)], "gather_small_grads")[0]
    zeros = jnp.zeros_like(stacked[-1])
    packed = adamw(partial, _pack([w[n] for n in REPLICATED] + [zeros]), _pack([m[n] for n in REPLICATED] + [zeros]),
                   _pack([v[n] for n in REPLICATED] + [zeros + 1.0]), "adamw_replicated")
    small_out = [_unpack(o, stacked) for o in packed]
    out = {n: tuple(o[i] for o in small_out) for i, n in enumerate(REPLICATED)}
    width = conv_w.shape[2]
    g_cw = lax.dynamic_slice_in_dim(small_out[0][-1], _index(*_place()) * width, width, axis=2)
    cw_res = adamw(g_cw.reshape(1, -1, LANES), conv_w.reshape(-1, LANES), m["conv_w"].reshape(-1, LANES),
                   v["conv_w"].reshape(-1, LANES), "adamw_conv_w")
    out["conv_w"] = tuple(o.reshape(conv_w.shape) for o in cw_res)
    for n in BIG:
        out[n] = tuple(jnp.swapaxes(o, 1, 2) if n in by_columns else o for o in big_out[n])
    results = [loss, dh[None]]
    for k in range(4):
        results += [out[n][k] for n in WEIGHTS]
    return tuple(results)
```

```python
import functools
import math

import jax
import jax.numpy as jnp
from jax import lax
from jax.experimental import pallas as pl
from jax.experimental.pallas import tpu as pltpu

F32 = jnp.float32
BF16 = jnp.bfloat16

N_DEV = 8
DEPTH = 4
EPS = 1e-6
NEG = -1e30

C_A = 512
N_HEADS = 16
HEAD_DIM = 64
C_B = N_HEADS * HEAD_DIM
C_C = 512
POOL_WINDOWS = (2, 4, 8, 16)
C_G = C_C // len(POOL_WINDOWS)
CONV_WIDTH = 31
CONV_HALF = CONV_WIDTH // 2
DILATIONS = (1, 4, 16)
ATTN_HALF = 64
ROT_DIM = HEAD_DIM // 4
ROPE_THETA = 500000.0

ADAM_LR = 0.001
ADAM_B1 = 0.9
ADAM_B2 = 0.999
ADAM_EPS = 1e-08
ADAM_WD = 0.01
ADAM_STEP = 10

LANES = 128
SUBLANES = 8
VMEM_LIMIT = 56 * 1024 * 1024
ROW_TILE = 256
SEQ_TILE = 256
ATTN_BLOCK = 128
MESH = pl.DeviceIdType.MESH


def _params(*sem):
    return pltpu.CompilerParams(dimension_semantics=sem, vmem_limit_bytes=VMEM_LIMIT)


def _tile(n, target):
    if n <= target:
        return n
    t = (target // LANES) * LANES
    while t >= LANES:
        if n % t == 0:
            return t
        t -= LANES
    return n


def _rows(n, target):
    t = min(n, target)
    while n % t:
        t //= 2
    return t


def matmul(a, b, mode, out_dtype, name, tm=1024, tn=512, tk=2048, comm=None):
    if mode == "nn":
        (M, K), (_, N) = a.shape, b.shape
    elif mode == "nt":
        (M, K), (N, _) = a.shape, b.shape
    else:
        (K, M), (_, N) = a.shape, b.shape
    tm, tn, tk = _tile(M, tm), _tile(N, tn), _tile(K, tk)
    nm, nn, nk = M // tm, N // tn, K // tk
    dims = {"nn": (((1,), (0,)), ((), ())), "nt": (((1,), (1,)), ((), ())), "tn": (((0,), (0,)), ((), ()))}[mode]
    nc = comm.n if comm is not None else 0

    def body(*refs):
        a_ref, b_ref = refs[:2]
        c_ins, o_ref, c_outs = refs[2:2 + nc], refs[2 + nc], refs[3 + nc:3 + 2 * nc]
        scratch = refs[3 + 2 * nc:]
        acc, sems = (scratch[:1], scratch[1:]) if nk > 1 else ((), scratch)
        i, j, k = pl.program_id(0), pl.program_id(1), pl.program_id(2)
        if comm is not None:
            @pl.when((i == 0) & (j == 0) & (k == 0))
            def _():
                comm.start(c_ins, c_outs, sems)

        p = lax.dot_general(a_ref[...], b_ref[...], dims, preferred_element_type=F32)
        if nk == 1:
            o_ref[...] = p.astype(o_ref.dtype)
        else:
            acc_ref, = acc

            @pl.when(k == 0)
            def _():
                acc_ref[...] = p

            @pl.when(k > 0)
            def _():
                acc_ref[...] += p

            @pl.when(k == nk - 1)
            def _():
                o_ref[...] = acc_ref[...].astype(o_ref.dtype)

        if comm is not None:
            @pl.when((i == nm - 1) & (j == nn - 1) & (k == nk - 1))
            def _():
                comm.finish(c_ins, c_outs, sems)

    if mode == "nn":
        a_spec = pl.BlockSpec((tm, tk), lambda i, j, k: (i, k))
        b_spec = pl.BlockSpec((tk, tn), lambda i, j, k: (k, j))
    elif mode == "nt":
        a_spec = pl.BlockSpec((tm, tk), lambda i, j, k: (i, k))
        b_spec = pl.BlockSpec((tn, tk), lambda i, j, k: (j, k))
    else:
        a_spec = pl.BlockSpec((tk, tm), lambda i, j, k: (k, i))
        b_spec = pl.BlockSpec((tk, tn), lambda i, j, k: (k, j))
    o_spec = pl.BlockSpec((tm, tn), lambda i, j, k: (i, j))
    o_shape = jax.ShapeDtypeStruct((M, N), out_dtype)
    acc_shape = [pltpu.VMEM((tm, tn), F32)] if nk > 1 else []
    if comm is None:
        return pl.pallas_call(
            body, name=name, grid=(nm, nn, nk), in_specs=[a_spec, b_spec], out_specs=o_spec, out_shape=o_shape,
            scratch_shapes=acc_shape, compiler_params=_params("parallel", "parallel", "arbitrary"),
        )(a, b)
    res = pl.pallas_call(
        body, name=name, grid=(nm, nn, nk), in_specs=[a_spec, b_spec] + [_ANY] * nc,
        out_specs=[o_spec] + [_ANY] * nc, out_shape=[o_shape] + comm.out_shape,
        scratch_shapes=acc_shape + comm.sems, compiler_params=_params("arbitrary", "arbitrary", "arbitrary"),
    )(a, b, *comm.ins)
    return res[0], res[1:]


def _rms(t):
    return t * lax.rsqrt(jnp.mean(t * t, axis=-1, keepdims=True) + EPS)


def _rms_bwd(t, dn):
    r = lax.rsqrt(jnp.mean(t * t, axis=-1, keepdims=True) + EPS)
    n = t * r
    return r * (dn - n * jnp.mean(dn * n, axis=-1, keepdims=True)), n


def _row_spec(tr, d):
    return pl.BlockSpec((tr, d), lambda i: (i, 0))


def _vec_spec(d):
    return pl.BlockSpec((1, d), lambda i: (0, 0))


def _col_spec(d, tr):
    return pl.BlockSpec((d, tr), lambda i: (0, i))


def rms_fwd(x, g, name):
    S, D = x.shape
    tr = _rows(S, ROW_TILE)

    def body(x_ref, g_ref, o_ref, ot_ref):
        h = _rms(x_ref[...].astype(F32)) * g_ref[...]
        o_ref[...] = h.astype(o_ref.dtype)
        ot_ref[...] = h.T.astype(ot_ref.dtype)

    return pl.pallas_call(
        body, name=name, grid=(S // tr,), in_specs=[_row_spec(tr, D), _vec_spec(D)],
        out_specs=[_row_spec(tr, D), _col_spec(D, tr)],
        out_shape=[jax.ShapeDtypeStruct((S, D), BF16), jax.ShapeDtypeStruct((D, S), BF16)],
        compiler_params=_params("parallel"),
    )(x, g)


def norm_residual(z, x, g, name):
    S, D = x.shape
    tr = _rows(S, ROW_TILE)

    def body(z_ref, x_ref, g_ref, o_ref):
        o_ref[...] = x_ref[...] + _rms(z_ref[...].astype(F32)) * g_ref[...]

    return pl.pallas_call(
        body, name=name, grid=(S // tr,), in_specs=[_row_spec(tr, D), _row_spec(tr, D), _vec_spec(D)],
        out_specs=_row_spec(tr, D), out_shape=jax.ShapeDtypeStruct((S, D), F32), compiler_params=_params("parallel"),
    )(z, x, g)


def norm_residual_rms(z, x, g, g_next, name):
    S, D = x.shape
    tr = _rows(S, ROW_TILE)

    def body(z_ref, x_ref, g_ref, gn_ref, o_ref, h_ref, ht_ref):
        x_new = x_ref[...] + _rms(z_ref[...].astype(F32)) * g_ref[...]
        o_ref[...] = x_new
        h = _rms(x_new) * gn_ref[...]
        h_ref[...] = h.astype(h_ref.dtype)
        ht_ref[...] = h.T.astype(ht_ref.dtype)

    return pl.pallas_call(
        body, name=name, grid=(S // tr,),
        in_specs=[_row_spec(tr, D), _row_spec(tr, D), _vec_spec(D), _vec_spec(D)],
        out_specs=[_row_spec(tr, D), _row_spec(tr, D), _col_spec(D, tr)],
        out_shape=[jax.ShapeDtypeStruct((S, D), F32), jax.ShapeDtypeStruct((S, D), BF16),
                   jax.ShapeDtypeStruct((D, S), BF16)],
        compiler_params=_params("parallel"),
    )(z, x, g, g_next)


def rms_bwd_chain(t1, dy1, g1, res, t2, g2, name):
    S, D = t1.shape
    tr = _rows(S, ROW_TILE)

    def body(t1_ref, dy1_ref, g1_ref, res_ref, t2_ref, g2_ref, d1_ref, d2_ref, dg1_ref, dg2_ref):
        dy1v = dy1_ref[...].astype(F32)
        dt1, n1 = _rms_bwd(t1_ref[...].astype(F32), dy1v * g1_ref[...])
        d1 = dt1 + res_ref[...]
        d1_ref[...] = d1
        dt2, n2 = _rms_bwd(t2_ref[...].astype(F32), d1 * g2_ref[...])
        d2_ref[...] = dt2.astype(d2_ref.dtype)

        @pl.when(pl.program_id(0) == 0)
        def _():
            dg1_ref[...] = jnp.zeros_like(dg1_ref)
            dg2_ref[...] = jnp.zeros_like(dg2_ref)

        dg1_ref[...] += jnp.sum(dy1v * n1, axis=0, keepdims=True)
        dg2_ref[...] += jnp.sum(d1 * n2, axis=0, keepdims=True)

    row, vec = _row_spec(tr, D), _vec_spec(D)
    return pl.pallas_call(
        body, name=name, grid=(S // tr,), in_specs=[row, row, vec, row, row, vec], out_specs=[row, row, vec, vec],
        out_shape=[jax.ShapeDtypeStruct((S, D), F32), jax.ShapeDtypeStruct((S, D), BF16),
                   jax.ShapeDtypeStruct((1, D), F32), jax.ShapeDtypeStruct((1, D), F32)],
        compiler_params=_params("arbitrary"),
    )(t1, dy1, g1, res, t2, g2)


def rms_bwd(t, dy, g, res, out_dtype, name):
    S, D = t.shape
    tr = _rows(S, ROW_TILE)
    has_res = res is not None

    def body(t_ref, dy_ref, g_ref, *rest):
        if has_res:
            res_ref, dt_ref, dg_ref = rest
        else:
            dt_ref, dg_ref = rest
        dyv = dy_ref[...].astype(F32)
        dt, n = _rms_bwd(t_ref[...].astype(F32), dyv * g_ref[...])
        if has_res:
            dt = dt + res_ref[...]
        dt_ref[...] = dt.astype(dt_ref.dtype)

        @pl.when(pl.program_id(0) == 0)
        def _():
            dg_ref[...] = jnp.zeros_like(dg_ref)

        dg_ref[...] += jnp.sum(dyv * n, axis=0, keepdims=True)

    ins = [t, dy, g] + ([res] if has_res else [])
    specs = [_row_spec(tr, D), _row_spec(tr, D), _vec_spec(D)] + ([_row_spec(tr, D)] if has_res else [])
    return pl.pallas_call(
        body, name=name, grid=(S // tr,), in_specs=specs, out_specs=[_row_spec(tr, D), _vec_spec(D)],
        out_shape=[jax.ShapeDtypeStruct((S, D), out_dtype), jax.ShapeDtypeStruct((1, D), F32)],
        compiler_params=_params("arbitrary"),
    )(*ins)


def swiglu_fwd(gu, name):
    S, F2 = gu.shape
    F = F2 // 2
    tr = _rows(S, ROW_TILE)

    def body(g_ref, u_ref, o_ref, ot_ref):
        g = g_ref[...].astype(F32)
        a = g * jax.nn.sigmoid(g) * u_ref[...].astype(F32)
        o_ref[...] = a.astype(o_ref.dtype)
        ot_ref[...] = a.T.astype(ot_ref.dtype)

    return pl.pallas_call(
        body, name=name, grid=(S // tr,),
        in_specs=[pl.BlockSpec((tr, F), lambda i: (i, 0)), pl.BlockSpec((tr, F), lambda i: (i, 1))],
        out_specs=[_row_spec(tr, F), _col_spec(F, tr)],
        out_shape=[jax.ShapeDtypeStruct((S, F), BF16), jax.ShapeDtypeStruct((F, S), BF16)],
        compiler_params=_params("parallel"),
    )(gu, gu)


def swiglu_bwd(gu, da, name):
    S, F2 = gu.shape
    F = F2 // 2
    tr = _rows(S, ROW_TILE)

    def body(g_ref, u_ref, da_ref, o_ref):
        g = g_ref[...].astype(F32)
        u = u_ref[...].astype(F32)
        dav = da_ref[...].astype(F32)
        sig = jax.nn.sigmoid(g)
        o_ref[:, :F] = (dav * u * (sig * (1.0 + g * (1.0 - sig)))).astype(o_ref.dtype)
        o_ref[:, F:] = (dav * (g * sig)).astype(o_ref.dtype)

    return pl.pallas_call(
        body, name=name, grid=(S // tr,),
        in_specs=[pl.BlockSpec((tr, F), lambda i: (i, 0)), pl.BlockSpec((tr, F), lambda i: (i, 1)), _row_spec(tr, F)],
        out_specs=_row_spec(tr, F2), out_shape=jax.ShapeDtypeStruct((S, F2), BF16), compiler_params=_params("parallel"),
    )(gu, gu, da)


def loss_head(y, target, name):
    S, D = y.shape
    tr = _rows(S, ROW_TILE)

    def body(y_ref, t_ref, dy_ref, sq_ref):
        e = y_ref[...] - t_ref[...]
        dy_ref[...] = e * (1.0 / D)

        @pl.when(pl.program_id(0) == 0)
        def _():
            sq_ref[...] = jnp.zeros_like(sq_ref)

        sq_ref[...] += jnp.sum(e * e, axis=0, keepdims=True)

    return pl.pallas_call(
        body, name=name, grid=(S // tr,), in_specs=[_row_spec(tr, D), _row_spec(tr, D)],
        out_specs=[_row_spec(tr, D), _vec_spec(D)],
        out_shape=[jax.ShapeDtypeStruct((S, D), F32), jax.ShapeDtypeStruct((1, D), F32)],
        compiler_params=_params("arbitrary"),
    )(y, target)


def _halo_specs(bs, halo, width, col, n_rows):
    per = bs // halo
    last = n_rows // halo - 1
    cur = pl.BlockSpec((bs, width), lambda i: (i, col))
    prev = pl.BlockSpec((halo, width), lambda i: (jnp.maximum(i * per - 1, 0), col))
    nxt = pl.BlockSpec((halo, width), lambda i: (jnp.minimum((i + 1) * per, last), col))
    return prev, cur, nxt


TAP_CHUNK = 32


def _build_phases(ref, phases_ref):
    total = ref.shape[0] - SUBLANES
    for b in range(SUBLANES):
        phases_ref[b, 0:total, :] = ref[pl.ds(b, total), :]


def _tap_rows(phases_ref, off, n):
    b = off % SUBLANES
    return phases_ref[b, off - b:off - b + n, :]


def _conv_taps(phases_ref, offsets, n, weights_ref, init, out_ref):
    for r0 in range(0, n, TAP_CHUNK):
        rows = min(TAP_CHUNK, n - r0)
        acc = jnp.zeros((rows, C_A), F32) + init
        for t, off in offsets.items():
            acc = acc + weights_ref[t:t + 1, :] * _tap_rows(phases_ref, off + r0, rows)
        out_ref[r0:r0 + rows, :] = acc


def _glu(a, g):
    return a.astype(F32) * jax.nn.sigmoid(g.astype(F32))


def _layernorm_silu(c, lg, lb):
    mu = jnp.mean(c, axis=-1, keepdims=True)
    cc = c - mu
    rstd = lax.rsqrt(jnp.mean(cc * cc, axis=-1, keepdims=True) + EPS)
    xh = cc * rstd
    ln = xh * lg + lb
    sig = jax.nn.sigmoid(ln)
    return xh, rstd, ln, sig


def conv_fwd(proj, cw, cb, lg, lb, gm, name):
    S = proj.shape[0]
    bs = _rows(S, SEQ_TILE)
    nb = S // bs
    H = 16

    def body(ap, ac, an, gp, gc, gn, cw_ref, cb_ref, lg_ref, lb_ref, gm_ref, y_ref, win_ref, phases_ref, c_ref):
        i = pl.program_id(0)
        win_ref[0:H, :] = jnp.where(i > 0, _glu(ap[...], gp[...]), 0.0)
        win_ref[H:H + bs, :] = _glu(ac[...], gc[...])
        win_ref[H + bs:2 * H + bs, :] = jnp.where(i < nb - 1, _glu(an[...], gn[...]), 0.0)
        win_ref[bs + 2 * H:, :] = jnp.zeros((SUBLANES, C_A), F32)
        _build_phases(win_ref, phases_ref)
        _conv_taps(phases_ref, {t: H - CONV_HALF + t for t in range(CONV_WIDTH)}, bs, cw_ref, cb_ref[...], c_ref)
        _, _, ln, sig = _layernorm_silu(c_ref[...], lg_ref[...], lb_ref[...])
        y_ref[...] = (_rms(ln * sig) * gm_ref[...]).astype(y_ref.dtype)

    a_specs = _halo_specs(bs, H, C_A, 0, S)
    g_specs = _halo_specs(bs, H, C_A, 1, S)
    vec = _vec_spec(C_A)
    return pl.pallas_call(
        body, name=name, grid=(nb,),
        in_specs=[*a_specs, *g_specs, pl.BlockSpec((32, C_A), lambda i: (0, 0)), vec, vec, vec, vec],
        out_specs=_row_spec(bs, C_A), out_shape=jax.ShapeDtypeStruct((S, C_A), BF16),
        scratch_shapes=[pltpu.VMEM((bs + 2 * H + SUBLANES, C_A), F32), pltpu.VMEM((SUBLANES, bs + 2 * H, C_A), F32),
                        pltpu.VMEM((bs, C_A), F32)],
        compiler_params=_params("parallel"),
    )(proj, proj, proj, proj, proj, proj, cw, cb, lg, lb, gm)


def conv_bwd(proj, dy, cw, cb, lg, lb, gm, name):
    S = proj.shape[0]
    bs = _rows(S, SEQ_TILE)
    nb = S // bs
    H = 32
    HC = 16
    bc = bs + 2 * HC

    def body(ap, ac, an, gp, gc, gn, dp, dc_, dn, cw_ref, cb_ref, lg_ref, lb_ref, gm_ref,
             dproj_ref, dcw_ref, dcb_ref, dlg_ref, dlb_ref, dgm_ref, win_ref, dcs_ref, phases_ref, c_ref):
        i = pl.program_id(0)
        win_ref[0:H, :] = jnp.where(i > 0, _glu(ap[...], gp[...]), 0.0)
        win_ref[H:H + bs, :] = _glu(ac[...], gc[...])
        win_ref[H + bs:2 * H + bs, :] = jnp.where(i < nb - 1, _glu(an[...], gn[...]), 0.0)
        win_ref[bs + 2 * H:, :] = jnp.zeros((SUBLANES, C_A), F32)
        _build_phases(win_ref, phases_ref)
        _conv_taps(phases_ref, {t: H - HC - CONV_HALF + t for t in range(CONV_WIDTH)}, bc, cw_ref, cb_ref[...], c_ref)
        xh, rstd, ln, sig = _layernorm_silu(c_ref[...], lg_ref[...], lb_ref[...])
        ya = ln * sig
        dyv = jnp.concatenate([dp[...], dc_[...], dn[...]], axis=0).astype(F32)
        dya, n = _rms_bwd(ya, dyv * gm_ref[...])
        dln = dya * (sig * (1.0 + ln * (1.0 - sig)))
        dxh = dln * lg_ref[...]
        dcv = rstd * (dxh - jnp.mean(dxh, axis=-1, keepdims=True) - xh * jnp.mean(dxh * xh, axis=-1, keepdims=True))
        pos = i * bs - HC + lax.broadcasted_iota(jnp.int32, (bc, 1), 0)
        dcv = jnp.where((pos >= 0) & (pos < S), dcv, 0.0)
        dcs_ref[0:bc, :] = dcv
        dcs_ref[bc:, :] = jnp.zeros((SUBLANES, C_A), F32)

        @pl.when(i == 0)
        def _():
            for r in (dcw_ref, dcb_ref, dlg_ref, dlb_ref, dgm_ref):
                r[...] = jnp.zeros_like(r)

        mid = slice(HC, HC + bs)
        dcb_ref[...] += jnp.sum(dcv[mid], axis=0, keepdims=True)
        dlg_ref[...] += jnp.sum((dln * xh)[mid], axis=0, keepdims=True)
        dlb_ref[...] += jnp.sum(dln[mid], axis=0, keepdims=True)
        dgm_ref[...] += jnp.sum((dyv * n)[mid], axis=0, keepdims=True)
        dcm = dcv[mid]
        for t in range(CONV_WIDTH):
            dcw_ref[t:t + 1, :] += jnp.sum(dcm * _tap_rows(phases_ref, H - CONV_HALF + t, bs), axis=0, keepdims=True)
        _build_phases(dcs_ref, phases_ref)
        _conv_taps(phases_ref, {t: HC + CONV_HALF - t for t in range(CONV_WIDTH)}, bs, cw_ref, 0.0, c_ref)
        dh = c_ref[0:bs, :]
        a = ac[...].astype(F32)
        sg = jax.nn.sigmoid(gc[...].astype(F32))
        dproj_ref[:, :C_A] = (dh * sg).astype(dproj_ref.dtype)
        dproj_ref[:, C_A:] = (dh * a * sg * (1.0 - sg)).astype(dproj_ref.dtype)

    a_specs = _halo_specs(bs, H, C_A, 0, S)
    g_specs = _halo_specs(bs, H, C_A, 1, S)
    d_specs = _halo_specs(bs, HC, C_A, 0, S)
    vec = _vec_spec(C_A)
    full = pl.BlockSpec((32, C_A), lambda i: (0, 0))
    vshape = jax.ShapeDtypeStruct((1, C_A), F32)
    return pl.pallas_call(
        body, name=name, grid=(nb,),
        in_specs=[*a_specs, *g_specs, *d_specs, full, vec, vec, vec, vec],
        out_specs=[_row_spec(bs, 2 * C_A), full, vec, vec, vec, vec],
        out_shape=[jax.ShapeDtypeStruct((S, 2 * C_A), BF16), jax.ShapeDtypeStruct((32, C_A), F32),
                   vshape, vshape, vshape, vshape],
        scratch_shapes=[pltpu.VMEM((bs + 2 * H + SUBLANES, C_A), F32), pltpu.VMEM((bc + SUBLANES, C_A), F32),
                        pltpu.VMEM((SUBLANES, bs + 2 * H, C_A), F32), pltpu.VMEM((bc, C_A), F32)],
        compiler_params=_params("arbitrary"),
    )(proj, proj, proj, proj, proj, proj, dy, dy, dy, cw, cb, lg, lb, gm)


POOL_HALO = 16


def _shift(x, k):
    n = x.shape[0]
    return pltpu.roll(x, (-k) % n, axis=0)


def _pool_means(u, pos, S):
    w2 = _shift(u, -1) + u
    w4 = _shift(w2, -1) + _shift(w2, 1)
    w8 = _shift(w4, -2) + _shift(w4, 2)
    w16 = _shift(w8, -4) + _shift(w8, 4)
    sums = (w2, w4, w8, w16)
    lane = lax.broadcasted_iota(jnp.int32, (1, C_C), 1)
    total = jnp.zeros_like(u)
    inv = jnp.zeros_like(u)
    for gi, win in enumerate(POOL_WINDOWS):
        cnt = jnp.minimum(pos + (win - win // 2), S) - jnp.maximum(pos - win // 2, 0)
        icnt = 1.0 / jnp.maximum(cnt, 1).astype(F32)
        sel = (lane >= gi * C_G) & (lane < (gi + 1) * C_G)
        total = jnp.where(sel, sums[gi], total)
        inv = jnp.where(sel, icnt, inv)
    return total * inv - u, inv


def _pool_adjoint(e):
    v2 = e + _shift(e, 1)
    v4 = _shift(v2, -1) + _shift(v2, 1)
    v8 = _shift(v4, -2) + _shift(v4, 2)
    v16 = _shift(v8, -4) + _shift(v8, 4)
    sums = (v2, v4, v8, v16)
    lane = lax.broadcasted_iota(jnp.int32, (1, C_C), 1)
    out = jnp.zeros_like(e)
    for gi in range(len(POOL_WINDOWS)):
        sel = (lane >= gi * C_G) & (lane < (gi + 1) * C_G)
        out = jnp.where(sel, sums[gi], out)
    return out


def _pool_window(up, uc, un, i, nb, bs, S):
    H = POOL_HALO
    u = jnp.concatenate([jnp.where(i > 0, up[...].astype(F32), 0.0), uc[...].astype(F32),
                         jnp.where(i < nb - 1, un[...].astype(F32), 0.0)], axis=0)
    pos = i * bs - H + lax.broadcasted_iota(jnp.int32, (bs + 2 * H, 1), 0)
    return u, pos


def _pool_mix(pooled, pw_ref):
    outs = []
    for gi in range(len(POOL_WINDOWS)):
        outs.append(jnp.dot(pooled[:, gi * C_G:(gi + 1) * C_G].astype(BF16), pw_ref[gi].astype(BF16),
                            preferred_element_type=F32))
    return jnp.concatenate(outs, axis=1)


def pool_fwd(proj, pw, ps, gm, name):
    S, width = proj.shape
    col = width // C_C - 1
    bs = _rows(S, SEQ_TILE)
    nb = S // bs
    H = POOL_HALO

    def body(up, uc, un, pw_ref, ps_ref, gm_ref, y_ref):
        i = pl.program_id(0)
        u, pos = _pool_window(up, uc, un, i, nb, bs, S)
        pooled, _ = _pool_means(u, pos, S)
        mixed = _pool_mix(pooled[H:H + bs], pw_ref)
        y_ref[...] = (_rms(mixed * ps_ref[...]) * gm_ref[...]).astype(y_ref.dtype)

    vec = _vec_spec(C_C)
    return pl.pallas_call(
        body, name=name, grid=(nb,),
        in_specs=[*_halo_specs(bs, H, C_C, col, S), pl.BlockSpec((4, C_G, C_G), lambda i: (0, 0, 0)), vec, vec],
        out_specs=_row_spec(bs, C_C), out_shape=jax.ShapeDtypeStruct((S, C_C), BF16),
        compiler_params=_params("parallel"),
    )(proj, proj, proj, pw, ps, gm)


def pool_bwd(proj, dy, pw, ps, gm, name):
    S, width = proj.shape
    col = width // C_C - 1
    dcol = dy.shape[1] // C_C - 1
    bs = _rows(S, SEQ_TILE)
    nb = S // bs
    H = POOL_HALO
    W = bs + 2 * H

    def body(up, uc, un, dp, dc_, dn, pw_ref, ps_ref, gm_ref, du_ref, dpw_ref, dps_ref, dgm_ref):
        i = pl.program_id(0)
        u, pos = _pool_window(up, uc, un, i, nb, bs, S)
        pooled, inv = _pool_means(u, pos, S)
        mixed = _pool_mix(pooled, pw_ref)
        dyv = jnp.concatenate([dp[...], dc_[...], dn[...]], axis=0).astype(F32)
        dyc, n = _rms_bwd(mixed * ps_ref[...], dyv * gm_ref[...])
        dmixed = dyc * ps_ref[...]
        dmb = dmixed.astype(BF16)
        dpooled = jnp.concatenate(
            [lax.dot_general(dmb[:, gi * C_G:(gi + 1) * C_G], pw_ref[gi].astype(BF16), (((1,), (1,)), ((), ())),
                             preferred_element_type=F32) for gi in range(len(POOL_WINDOWS))], axis=1)
        dpooled = jnp.where((pos >= 0) & (pos < S), dpooled, 0.0)
        du = _pool_adjoint(dpooled * inv) - dpooled
        du_ref[...] = du[H:H + bs].astype(du_ref.dtype)

        @pl.when(i == 0)
        def _():
            for r in (dpw_ref, dps_ref, dgm_ref):
                r[...] = jnp.zeros_like(r)

        mid = slice(H, H + bs)
        dps_ref[...] += jnp.sum((dyc * mixed)[mid], axis=0, keepdims=True)
        dgm_ref[...] += jnp.sum((dyv * n)[mid], axis=0, keepdims=True)
        pb = pooled[mid].astype(BF16)
        for gi in range(len(POOL_WINDOWS)):
            sl = slice(gi * C_G, (gi + 1) * C_G)
            dpw_ref[gi] += lax.dot_general(pb[:, sl], dmb[mid][:, sl], (((0,), (0,)), ((), ())),
                                           preferred_element_type=F32)

    vec = _vec_spec(C_C)
    full = pl.BlockSpec((4, C_G, C_G), lambda i: (0, 0, 0))
    vshape = jax.ShapeDtypeStruct((1, C_C), F32)
    return pl.pallas_call(
        body, name=name, grid=(nb,),
        in_specs=[*_halo_specs(bs, H, C_C, col, S), *_halo_specs(bs, H, C_C, dcol, S), full, vec, vec],
        out_specs=[_row_spec(bs, C_C), full, vec, vec],
        out_shape=[jax.ShapeDtypeStruct((S, C_C), BF16), jax.ShapeDtypeStruct((4, C_G, C_G), F32), vshape, vshape],
        compiler_params=_params("arbitrary"),
    )(proj, proj, proj, dy, dy, dy, pw, ps, gm)


def rope_tables(S):
    pos = jnp.arange(S, dtype=F32)
    inv = ROPE_THETA ** (-jnp.arange(0, ROT_DIM, 2, dtype=F32) / ROT_DIM)
    ang = pos[:, None] * inv[None, :]
    half = ROT_DIM // 2
    cos, sin = jnp.cos(ang), jnp.sin(ang)
    zeros = jnp.zeros((S, half), F32)
    rest = jnp.zeros((S, HEAD_DIM - ROT_DIM), F32)
    per_head = (jnp.concatenate([cos, cos, rest + 1.0], axis=1), jnp.concatenate([-sin, zeros, rest], axis=1),
                jnp.concatenate([zeros, sin, rest], axis=1))
    return tuple(jnp.tile(t, (1, LANES // HEAD_DIM)) for t in per_head)


def _rotate(t, c, s1, s2, sign):
    half = ROT_DIM // 2
    return t * c + sign * (pltpu.roll(t, LANES - half, axis=1) * s1 + pltpu.roll(t, half, axis=1) * s2)


def rope_fwd(proj, tables, name):
    S = proj.shape[0]
    tr = _rows(S, ROW_TILE)
    qcol = 2 * C_A // C_B
    nd = len(DILATIONS)

    def body(q_ref, k_ref, v_ref, c_ref, s1_ref, s2_ref, *outs):
        c, s1, s2 = c_ref[...], s1_ref[...], s2_ref[...]
        qs, ks = [], []
        for p in range(C_B // LANES):
            sl = slice(p * LANES, (p + 1) * LANES)
            qs.append((_rotate(q_ref[:, sl].astype(F32), c, s1, s2, 1.0) * HEAD_DIM ** -0.5).astype(BF16))
            ks.append(_rotate(k_ref[:, sl].astype(F32), c, s1, s2, 1.0).astype(BF16))
        tensors = (jnp.concatenate(qs, axis=1), jnp.concatenate(ks, axis=1), v_ref[...])
        for n, d in enumerate(DILATIONS):
            for t, x in enumerate(tensors):
                _store_dilated(outs[3 * n + t], x, d)

    tab = _row_spec(tr, LANES)
    res = pl.pallas_call(
        body, name=name, grid=(S // tr,),
        in_specs=[pl.BlockSpec((tr, C_B), lambda i, col=qcol + n: (i, col)) for n in range(3)] + [tab, tab, tab],
        out_specs=[_dilated_spec(tr, d, C_B) for d in DILATIONS for _ in range(3)],
        out_shape=[jax.ShapeDtypeStruct((d, S // d, C_B), BF16) for d in DILATIONS for _ in range(3)],
        compiler_params=_params("parallel"),
    )(proj, proj, proj, *tables)
    return [tuple(res[3 * n:3 * n + 3]) for n in range(nd)]


def rope_bwd(dqs, dks, dvs, tables, dconv, du, name):
    S = dconv.shape[0]
    tr = _rows(S, ROW_TILE)
    n = len(dqs)
    base = 2 * C_A
    width = base + 3 * C_B + C_C

    def body(*refs):
        dq_refs, dk_refs, dv_refs = refs[:n], refs[n:2 * n], refs[2 * n:3 * n]
        c_ref, s1_ref, s2_ref, dconv_ref, du_ref, o_ref = refs[3 * n:]
        c, s1, s2 = c_ref[...], s1_ref[...], s2_ref[...]
        o_ref[:, :base] = dconv_ref[...]
        o_ref[:, base + 3 * C_B:] = du_ref[...]
        dq_all = sum(_load_dilated(r, d).astype(F32) for r, d in zip(dq_refs, DILATIONS))
        dk_all = sum(_load_dilated(r, d).astype(F32) for r, d in zip(dk_refs, DILATIONS))
        dv_all = sum(_load_dilated(r, d).astype(F32) for r, d in zip(dv_refs, DILATIONS))
        for p in range(C_B // LANES):
            sl = slice(p * LANES, (p + 1) * LANES)
            dq, dk, dv = dq_all[:, sl], dk_all[:, sl], dv_all[:, sl]
            at = base + p * LANES
            o_ref[:, at:at + LANES] = (_rotate(dq, c, s1, s2, -1.0) * HEAD_DIM ** -0.5).astype(BF16)
            o_ref[:, C_B + at:C_B + at + LANES] = _rotate(dk, c, s1, s2, -1.0).astype(BF16)
            o_ref[:, 2 * C_B + at:2 * C_B + at + LANES] = dv.astype(BF16)

    tab = _row_spec(tr, LANES)
    return pl.pallas_call(
        body, name=name, grid=(S // tr,),
        in_specs=[_dilated_spec(tr, d, C_B) for _ in range(3) for d in DILATIONS]
        + [tab, tab, tab, _row_spec(tr, base), _row_spec(tr, C_C)],
        out_specs=_row_spec(tr, width), out_shape=jax.ShapeDtypeStruct((S, width), BF16),
        compiler_params=_params("parallel"),
    )(*dqs, *dks, *dvs, *tables, dconv, du)


def _attn_specs(bq, width, L):
    per = bq // ATTN_HALF
    last = L // ATTN_HALF - 1
    cur = pl.BlockSpec((None, bq, width), lambda r, j: (r, j, 0))
    prev = pl.BlockSpec((None, ATTN_HALF, width), lambda r, j: (r, jnp.maximum(j * per - 1, 0), 0))
    nxt = pl.BlockSpec((None, ATTN_HALF, width), lambda r, j: (r, jnp.minimum((j + 1) * per, last), 0))
    return prev, cur, nxt


def _window(refs, sl):
    return jnp.concatenate([r[:, sl] for r in refs], axis=0)


def _band_mask(j, bq, L, rows_are_window):
    bw = bq + 2 * ATTN_HALF
    if rows_are_window:
        rp = j * bq - ATTN_HALF + lax.broadcasted_iota(jnp.int32, (bw, 1), 0)
        cp = j * bq + lax.broadcasted_iota(jnp.int32, (1, bq), 1)
        return (jnp.abs(rp - cp) <= ATTN_HALF) & (rp >= 0) & (rp < L)
    rp = j * bq + lax.broadcasted_iota(jnp.int32, (bq, 1), 0)
    cp = j * bq - ATTN_HALF + lax.broadcasted_iota(jnp.int32, (1, bw), 1)
    return (jnp.abs(rp - cp) <= ATTN_HALF) & (cp >= 0) & (cp < L)


def _head_col(stats, h):
    lane = lax.broadcasted_iota(jnp.int32, (1, LANES), 1)
    return jnp.sum(jnp.where(lane == h, stats, 0.0), axis=1, keepdims=True)


def _stack_heads(x):
    first = lax.broadcasted_iota(jnp.int32, (1, LANES), 1) < HEAD_DIM
    zero = jnp.zeros_like(x)
    return jnp.concatenate([jnp.where(first, x, zero), jnp.where(first, zero, x)], axis=0)


_NT = (((1,), (1,)), ((), ()))
_TN = (((0,), (0,)), ((), ()))


def attn_fwd_pattern(qd, kd, vd, name):
    d, L, _ = qd.shape
    bq = _rows(L, ATTN_BLOCK)

    def body(q_ref, kp, kc, kn, vp, vc, vn, o_ref, lse_ref):
        j = pl.program_id(1)
        mask = _band_mask(j, bq, L, False)
        mask2 = jnp.concatenate([mask, mask], axis=0)
        lane = lax.broadcasted_iota(jnp.int32, (1, LANES), 1)
        first = lane < HEAD_DIM
        lse = jnp.zeros((bq, LANES), F32)
        for p in range(C_B // LANES):
            sl = slice(p * LANES, (p + 1) * LANES)
            kw = _window((kp, kc, kn), sl)
            vw = _window((vp, vc, vn), sl)
            s = jnp.where(mask2, lax.dot_general(_stack_heads(q_ref[:, sl]), kw, _NT, preferred_element_type=F32), NEG)
            m = jnp.max(s, axis=1, keepdims=True)
            e = jnp.exp(s - m)
            l = jnp.sum(e, axis=1, keepdims=True)
            o = jnp.dot(e.astype(BF16), vw, preferred_element_type=F32) * (1.0 / l)
            stat = m + jnp.log(l)
            lse = jnp.where(lane == 2 * p, stat[:bq], jnp.where(lane == 2 * p + 1, stat[bq:], lse))
            o_ref[:, sl] = jnp.where(first, o[:bq], o[bq:]).astype(o_ref.dtype)
        lse_ref[...] = lse

    kv = _attn_specs(bq, C_B, L)
    return pl.pallas_call(
        body, name=name, grid=(d, L // bq), in_specs=[kv[1], *kv, *kv],
        out_specs=[kv[1], pl.BlockSpec((None, bq, LANES), lambda r, j: (r, j, 0))],
        out_shape=[jax.ShapeDtypeStruct((d, L, C_B), BF16), jax.ShapeDtypeStruct((d, L, LANES), F32)],
        compiler_params=_params("parallel", "parallel"),
    )(qd, kd, kd, kd, vd, vd, vd)


def attn_combine(os_, lses, gm, ya, yc, name):
    S = ya.shape[0]
    tr = _rows(S, ROW_TILE)
    n = len(os_)

    def body(*refs):
        o_refs, l_refs = refs[:n], refs[n:2 * n]
        gm_ref, ya_ref, yc_ref, y_ref, yt_ref, out_ref, lse_ref = refs[2 * n:]
        ls = [_load_dilated(r, d) for r, d in zip(l_refs, DILATIONS)]
        os_tok = [_load_dilated(r, d) for r, d in zip(o_refs, DILATIONS)]
        mx = functools.reduce(jnp.maximum, ls)
        ws = [jnp.exp(l - mx) for l in ls]
        den = sum(ws)
        lse_ref[...] = mx + jnp.log(den)
        wn = [w / den for w in ws]
        lane = lax.broadcasted_iota(jnp.int32, (1, LANES), 1)
        first = lane < HEAD_DIM
        blocks = []
        for p in range(C_B // LANES):
            sl = slice(p * LANES, (p + 1) * LANES)
            acc = jnp.zeros((tr, LANES), F32)
            for w, o in zip(wn, os_tok):
                acc = acc + jnp.where(first, _head_col(w, 2 * p), _head_col(w, 2 * p + 1)) * o[:, sl].astype(F32)
            blocks.append(acc)
        out = jnp.concatenate(blocks, axis=1)
        out_ref[...] = out.astype(out_ref.dtype)
        y = jnp.concatenate([ya_ref[...].astype(F32), _rms(out) * gm_ref[...], yc_ref[...].astype(F32)], axis=1)
        y_ref[...] = y.astype(y_ref.dtype)
        yt_ref[...] = y.T.astype(yt_ref.dtype)

    st = _row_spec(tr, LANES)
    mix = C_A + C_B + C_C
    return pl.pallas_call(
        body, name=name, grid=(S // tr,),
        in_specs=[_dilated_spec(tr, d, C_B) for d in DILATIONS] + [_dilated_spec(tr, d, LANES) for d in DILATIONS]
        + [_vec_spec(C_B), _row_spec(tr, C_A), _row_spec(tr, C_C)],
        out_specs=[_row_spec(tr, mix), _col_spec(mix, tr), _row_spec(tr, C_B), st],
        out_shape=[jax.ShapeDtypeStruct((S, mix), BF16), jax.ShapeDtypeStruct((mix, S), BF16),
                   jax.ShapeDtypeStruct((S, C_B), BF16), jax.ShapeDtypeStruct((S, LANES), F32)],
        compiler_params=_params("parallel"),
    )(*os_, *lses, gm, ya, yc)


def attn_out_bwd(out, lse, dy, gm, name):
    S = out.shape[0]
    tr = _rows(S, ROW_TILE)
    nd = len(DILATIONS)

    def body(o_ref, lse_ref, dy1, dy2, g_ref, *outs):
        dg_ref = outs[-1]
        o = o_ref[...].astype(F32)
        dyv = jnp.concatenate([dy1[...], dy2[...]], axis=1).astype(F32)
        do, n = _rms_bwd(o, dyv * g_ref[...])
        dob = do.astype(BF16)
        prod = dob.astype(F32) * o
        lane = lax.broadcasted_iota(jnp.int32, (1, LANES), 1)
        first = lane < HEAD_DIM
        delta = jnp.zeros((tr, LANES), F32)
        for p in range(C_B // LANES):
            blk = prod[:, p * LANES:(p + 1) * LANES]
            delta = jnp.where(lane == 2 * p, jnp.sum(jnp.where(first, blk, 0.0), axis=1, keepdims=True), delta)
            delta = jnp.where(lane == 2 * p + 1, jnp.sum(jnp.where(first, 0.0, blk), axis=1, keepdims=True), delta)
        lse_v = lse_ref[...]
        for k, d in enumerate(DILATIONS):
            _store_dilated(outs[3 * k], dob, d)
            _store_dilated(outs[3 * k + 1], lse_v, d)
            _store_dilated(outs[3 * k + 2], delta, d)

        @pl.when(pl.program_id(0) == 0)
        def _():
            dg_ref[...] = jnp.zeros_like(dg_ref)

        dg_ref[...] += jnp.sum(dyv * n, axis=0, keepdims=True)

    widths = (C_B, LANES, LANES)
    dtypes = (BF16, F32, F32)
    res = pl.pallas_call(
        body, name=name, grid=(S // tr,),
        in_specs=[_row_spec(tr, C_B), _row_spec(tr, LANES), pl.BlockSpec((tr, C_A), lambda i: (i, 1)),
                  pl.BlockSpec((tr, C_A), lambda i: (i, 2)), _vec_spec(C_B)],
        out_specs=[_dilated_spec(tr, d, wd) for d in DILATIONS for wd in widths] + [_vec_spec(C_B)],
        out_shape=[jax.ShapeDtypeStruct((d, S // d, wd), dt) for d in DILATIONS for wd, dt in zip(widths, dtypes)]
        + [jax.ShapeDtypeStruct((1, C_B), F32)],
        compiler_params=_params("arbitrary"),
    )(out, lse, dy, dy, gm)
    return [tuple(res[3 * k:3 * k + 3]) for k in range(nd)], res[-1]


def attn_bwd_pattern(qd, kd, vd, dod, lsed, deltad, name, comm=None):
    d, L, _ = qd.shape
    nc = comm.n if comm is not None else 0
    bq = _rows(L, ATTN_BLOCK)

    nb = L // bq
    bw = bq + 2 * ATTN_HALF
    lo = bq - ATTN_HALF

    def body(qc, dc_, lc, tc, kp, kc, kn, vp, vc, vn, *rest):
        c_ins, (dq_ref, dk_ref, dv_ref), c_outs = rest[:nc], rest[nc:nc + 3], rest[nc + 3:2 * nc + 3]
        dk_acc, dv_acc = rest[2 * nc + 3:2 * nc + 5]
        sems = rest[2 * nc + 5:]
        r, j = pl.program_id(0), pl.program_id(1)
        if comm is not None:
            @pl.when((r == 0) & (j == 0))
            def _():
                comm.start(c_ins, c_outs, sems)

        @pl.when(j == 0)
        def _():
            dk_acc[...] = jnp.zeros_like(dk_acc)
            dv_acc[...] = jnp.zeros_like(dv_acc)

        @pl.when(j > 0)
        def _():
            for acc in (dk_acc, dv_acc):
                acc[0:bq, :] = acc[bq:2 * bq, :]
                acc[bq:2 * bq, :] = acc[2 * bq:, :]
                acc[2 * bq:, :] = jnp.zeros((bq, C_B), F32)

        @pl.when(j < nb)
        def _():
            mask = _band_mask(j, bq, L, False)
            mask2 = jnp.concatenate([mask, mask], axis=0)
            first = lax.broadcasted_iota(jnp.int32, (1, LANES), 1) < HEAD_DIM
            lse_c, delta_c = lc[...], tc[...]
            for p in range(C_B // LANES):
                sl = slice(p * LANES, (p + 1) * LANES)
                qs, dos = _stack_heads(qc[:, sl]), _stack_heads(dc_[:, sl])
                kw, vw = _window((kp, kc, kn), sl), _window((vp, vc, vn), sl)
                lse_s = jnp.concatenate([_head_col(lse_c, 2 * p), _head_col(lse_c, 2 * p + 1)], axis=0)
                delta_s = jnp.concatenate([_head_col(delta_c, 2 * p), _head_col(delta_c, 2 * p + 1)], axis=0)
                s = lax.dot_general(qs, kw, _NT, preferred_element_type=F32)
                pr = jnp.where(mask2, jnp.exp(s - lse_s), 0.0)
                dpr = lax.dot_general(dos, vw, _NT, preferred_element_type=F32)
                ds = (pr * (dpr - delta_s)).astype(BF16)
                dq = jnp.dot(ds, kw, preferred_element_type=F32)
                dq_ref[:, sl] = jnp.where(first, dq[:bq], dq[bq:]).astype(dq_ref.dtype)
                dk_acc[lo:lo + bw, sl] += lax.dot_general(ds, qs, _TN, preferred_element_type=F32)
                dv_acc[lo:lo + bw, sl] += lax.dot_general(pr.astype(BF16), dos, _TN, preferred_element_type=F32)

        dk_ref[...] = dk_acc[0:bq, :].astype(dk_ref.dtype)
        dv_ref[...] = dv_acc[0:bq, :].astype(dv_ref.dtype)
        if comm is not None:
            @pl.when((r == d - 1) & (j == nb))
            def _():
                comm.finish(c_ins, c_outs, sems)

    per = bq // ATTN_HALF
    last = L // ATTN_HALF - 1

    def clamp(j):
        return jnp.minimum(j, nb - 1)

    def specs(width):
        cur = pl.BlockSpec((None, bq, width), lambda r, j: (r, clamp(j), 0))
        prev = pl.BlockSpec((None, ATTN_HALF, width), lambda r, j: (r, jnp.maximum(clamp(j) * per - 1, 0), 0))
        nxt = pl.BlockSpec((None, ATTN_HALF, width), lambda r, j: (r, jnp.minimum((clamp(j) + 1) * per, last), 0))
        return prev, cur, nxt

    wide = specs(C_B)
    stat = specs(LANES)[1]
    lagged = pl.BlockSpec((None, bq, C_B), lambda r, j: (r, jnp.maximum(j - 1, 0), 0))
    shape = jax.ShapeDtypeStruct((d, L, C_B), BF16)
    res = pl.pallas_call(
        body, name=name, grid=(d, nb + 1), in_specs=[wide[1], wide[1], stat, stat, *wide, *wide] + [_ANY] * nc,
        out_specs=[wide[1], lagged, lagged] + [_ANY] * nc,
        out_shape=[shape] * 3 + (comm.out_shape if comm is not None else []),
        scratch_shapes=[pltpu.VMEM((3 * bq, C_B), F32), pltpu.VMEM((3 * bq, C_B), F32)]
        + (comm.sems if comm is not None else []),
        compiler_params=_params("arbitrary", "arbitrary"),
    )(qd, dod, lsed, deltad, kd, kd, kd, vd, vd, vd, *(comm.ins if comm is not None else []))
    return tuple(res[:3]), (res[3:] if comm is not None else None)


def _dilated_spec(tr, d, width):
    return pl.BlockSpec((d, tr // d, width), lambda i: (0, i, 0))


def _perm_matrix(n, d, inverse):
    lb = n // d
    row = lax.broadcasted_iota(jnp.int32, (n, n), 0)
    col = lax.broadcasted_iota(jnp.int32, (n, n), 1)
    source = (row % d) * lb + row // d if inverse else (row % lb) * d + row // lb
    return (col == source).astype(BF16)


def _permute(p, x):
    if x.dtype == BF16:
        return jnp.dot(p, x, preferred_element_type=F32).astype(BF16)
    hi = x.astype(BF16)
    rest = x - hi.astype(F32)
    mid = rest.astype(BF16)
    lo = (rest - mid.astype(F32)).astype(BF16)
    return (jnp.dot(p, hi, preferred_element_type=F32) + jnp.dot(p, mid, preferred_element_type=F32)
            + jnp.dot(p, lo, preferred_element_type=F32))


def _store_dilated(ref, x, d):
    n = x.shape[0]
    if d == 1:
        ref[0] = x
        return
    y = _permute(_perm_matrix(n, d, False), x)
    lb = n // d
    for r in range(d):
        ref[r] = y[r * lb:(r + 1) * lb]


def _load_dilated(ref, d):
    if d == 1:
        return ref[0]
    y = jnp.concatenate([ref[r] for r in range(d)], axis=0)
    return _permute(_perm_matrix(y.shape[0], d, True), y)


def adamw(parts, w, m, v, name):
    n, R, C = parts.shape
    tr = _rows(R, ROW_TILE)

    def body(p_ref, w_ref, m_ref, v_ref, g_ref, d_ref, nm_ref, nv_ref):
        g = p_ref[0].astype(F32)
        for k in range(1, n):
            g = g + p_ref[k].astype(F32)
        mm = ADAM_B1 * m_ref[...] + (1.0 - ADAM_B1) * g
        vv = ADAM_B2 * v_ref[...] + (1.0 - ADAM_B2) * jnp.square(g)
        m_hat = mm / (1.0 - ADAM_B1 ** ADAM_STEP)
        v_hat = vv / (1.0 - ADAM_B2 ** ADAM_STEP)
        g_ref[...] = g
        d_ref[...] = -ADAM_LR * (m_hat / (jnp.sqrt(v_hat) + ADAM_EPS) + ADAM_WD * w_ref[...])
        nm_ref[...] = mm
        nv_ref[...] = vv

    spec = _row_spec(tr, C)
    shape = jax.ShapeDtypeStruct((R, C), F32)
    return pl.pallas_call(
        body, name=name, grid=(R // tr,),
        in_specs=[pl.BlockSpec((n, tr, C), lambda i: (0, i, 0)), spec, spec, spec],
        out_specs=[spec] * 4, out_shape=[shape] * 4, compiler_params=_params("parallel"),
    )(parts, w, m, v)


_ANY = pl.BlockSpec(memory_space=pl.ANY)


def _place():
    return lax.axis_index("x"), lax.axis_index("y"), lax.axis_index("c")


def _index(px, py, pc):
    return 4 * px + 2 * py + pc


class Gather:
    def __init__(self, shards):
        self.ins = list(shards)
        T = self.n = len(shards)
        self.out_shape = [jax.ShapeDtypeStruct((N_DEV, *s.shape), s.dtype) for s in shards]
        self.sems = [pltpu.SemaphoreType.DMA((T, 7)), pltpu.SemaphoreType.DMA((T, 7)), pltpu.SemaphoreType.DMA((T,))]

    def _plan(self, ins, outs, sems):
        send_sems, recv_sems, local_sems = sems
        x, y, c = _place()
        me, sibling = (x, y, c), (x, y, 1 - c)
        chips = [(1 - x, y), (x, 1 - y), (1 - x, 1 - y)]

        def copy(t, k, block, to, src=None):
            rows = outs[t].at[_index(*block)]
            return pltpu.make_async_remote_copy(
                src_ref=rows if src is None else src, dst_ref=rows, send_sem=send_sems.at[t, k],
                recv_sem=recv_sems.at[t, k], device_id=to, device_id_type=MESH)

        mine = [pltpu.make_async_copy(ins[t], outs[t].at[_index(*me)], local_sems.at[t]) for t in range(self.n)]
        first = []
        for t in range(self.n):
            first.append(copy(t, 0, me, sibling, src=ins[t]))
            first += [copy(t, 1 + j, me, (*chip, c), src=ins[t]) for j, chip in enumerate(chips)]
        return copy, mine, first, me, sibling, chips, c

    def start(self, ins, outs, sems):
        _, mine, first, *_ = self._plan(ins, outs, sems)
        for cp in mine + first:
            cp.start()

    def finish(self, ins, outs, sems):
        copy, mine, first, me, sibling, chips, c = self._plan(ins, outs, sems)
        passed = []
        for j, chip in enumerate(chips):
            for t in range(self.n):
                copy(t, 1 + j, (*chip, c), me).wait_recv()
                fwd = copy(t, 4 + j, (*chip, c), sibling)
                fwd.start()
                passed.append(fwd)
        for t in range(self.n):
            copy(t, 0, sibling, me).wait_recv()
            for j, chip in enumerate(chips):
                copy(t, 4 + j, (*chip, 1 - c), me).wait_recv()
        for cp in first + passed:
            cp.wait_send()
        for cp in mine:
            cp.wait()


class Exchange:
    def __init__(self, parts):
        self.ins = list(parts)
        T = self.n = len(parts)
        self.out_shape = [jax.ShapeDtypeStruct(p.shape, p.dtype) for p in parts]
        self.sems = [pltpu.SemaphoreType.DMA((T, 7)), pltpu.SemaphoreType.DMA((T, 7)), pltpu.SemaphoreType.DMA((T,))]

    def _plan(self, ins, outs, sems):
        send_sems, recv_sems, local_sems = sems
        x, y, c = _place()
        me = _index(x, y, c)
        copies = [pltpu.make_async_copy(ins[t].at[me], outs[t].at[me], local_sems.at[t]) for t in range(self.n)]
        for k in range(1, N_DEV):
            peer = ((x + (k >> 2)) % 2, (y + ((k >> 1) & 1)) % 2, (c + (k & 1)) % 2)
            there = _index(*peer)
            for t in range(self.n):
                copies.append(pltpu.make_async_remote_copy(
                    src_ref=ins[t].at[there], dst_ref=outs[t].at[me], send_sem=send_sems.at[t, k - 1],
                    recv_sem=recv_sems.at[t, k - 1], device_id=peer, device_id_type=MESH))
        return copies

    def start(self, ins, outs, sems):
        for cp in self._plan(ins, outs, sems):
            cp.start()

    def finish(self, ins, outs, sems):
        for cp in self._plan(ins, outs, sems):
            cp.wait()


def communicate(comm, name):
    T = comm.n

    def body(*refs):
        ins, outs, sems = refs[:T], refs[T:2 * T], refs[2 * T:]
        comm.start(ins, outs, sems)
        comm.finish(ins, outs, sems)

    return pl.pallas_call(
        body, name=name, in_specs=[_ANY] * T, out_specs=[_ANY] * T, out_shape=comm.out_shape,
        scratch_shapes=comm.sems,
    )(*comm.ins)


def all_gather(shards, name):
    return communicate(Gather(shards), name)


def exchange(parts, name):
    return communicate(Exchange(parts), name)


def _row(v):
    return v.reshape(1, -1)


def _hosted(res):
    return res if isinstance(res, tuple) else (res, None)


def mix_forward(x, h1, h1t, w, p, tables, comm=None, comm_out=None, after_proj=None):
    gm = p["g_mix"]
    proj, got = _hosted(matmul(h1, w["win_t"], "nt", BF16, "proj", comm=comm))
    if after_proj is not None:
        after_proj(got)
    ya = conv_fwd(proj, p["cw"], _row(p["conv_b"]), _row(p["conv_ln_g"]), _row(p["conv_ln_b"]), _row(gm[:C_A]),
                  "conv_fwd")
    dil = rope_fwd(proj, tables, "rope_fwd")
    os_, lses = [], []
    for d, (qd, kd, vd) in zip(DILATIONS, dil):
        o, lse = attn_fwd_pattern(qd, kd, vd, f"attn_fwd_d{d}")
        os_.append(o)
        lses.append(lse)
    yc = pool_fwd(proj, p["pool_w"], _row(p["pool_scale"]), _row(gm[C_A + C_B:]), "pool_fwd")
    y, yt, out, lse = attn_combine(os_, lses, _row(gm[C_A:C_A + C_B]), ya, yc, "attn_combine")
    z, got_out = _hosted(matmul(y, w["wout"], "nn", BF16, "mix_out", comm=comm_out))
    x2, h2, h2t = norm_residual_rms(z, x, _row(p["g_post_mix"]), _row(p["g_pre_ffn"]), "res_mix")
    return x2, h2, h2t, dict(x=x, h1t=h1t, proj=proj, dil=dil, out=out, lse=lse, yt=yt, z=z), got_out


def ffn_forward(x2, h2, h2t, w, p, g_next=None, comm_in=None, comm_out=None, after_in=None):
    gu, got_in = _hosted(matmul(h2, w["wgu_t"], "nt", BF16, "ffn_in", comm=comm_in))
    if after_in is not None:
        after_in(got_in)
    a, at = swiglu_fwd(gu, "swiglu_fwd")
    f, got_out = _hosted(matmul(a, w["wd"], "nn", BF16, "ffn_out", tk=5632, comm=comm_out))
    saved = dict(x2=x2, h2t=h2t, gu=gu, at=at, f=f)
    if g_next is None:
        return norm_residual(f, x2, _row(p["g_post_ffn"]), "res_last"), None, saved, got_in, got_out
    x3, h1, h1t = norm_residual_rms(f, x2, _row(p["g_post_ffn"]), _row(g_next), "res_ffn")
    return x3, (h1, h1t), saved, got_in, got_out


def _to_blocks(g, by_columns):
    if by_columns:
        return g.T.reshape(N_DEV, -1, g.shape[0])
    return g.reshape(N_DEV, -1, g.shape[1])


def _exchange_of(g, by_columns):
    return Exchange([_to_blocks(g, by_columns)]) if g is not None else None


def ffn_backward(dx3, df, s, z, w, p, ride_act=None):
    da, got_act = _hosted(matmul(df, w["wd"], "nt", BF16, "d_ffn_act", comm=ride_act))
    dwd = matmul(s["at"], df, "nn", BF16, "dw_down", tm=1408, tn=1024, tk=2048)
    dgu = swiglu_bwd(s["gu"], da, "swiglu_bwd")
    dh2, got_wd = matmul(dgu, w["wgu_t"], "nn", BF16, "d_ffn_in", tk=5632, comm=_exchange_of(dwd, False))
    dwgu = matmul(s["h2t"], dgu, "nn", BF16, "dw_gate_up", tm=1024, tn=1408, tk=2048)
    dx2, dz, dg_pre_ffn, dg_post_mix = rms_bwd_chain(
        s["x2"], dh2, _row(p["g_pre_ffn"]), dx3, z, _row(p["g_post_mix"]), "rms_bwd_ffn_to_mix")
    return dx2, dz, dict(dwgu=dwgu, parts_wd=got_wd[0], got_act=got_act,
                         g_pre_ffn=dg_pre_ffn[0], g_post_mix=dg_post_mix[0])


def mix_backward(dx2, dz, s, w, p, tables, dwgu, below=None):
    gm = p["g_mix"]
    F = dwgu.shape[1] // 2
    dy = matmul(dz, w["wout"], "nt", BF16, "d_mix")
    dwout = matmul(s["yt"], dz, "nn", BF16, "dw_out", tm=1024, tn=1024, tk=2048)
    dconv, dcw, dcb, dlg, dlb, dgm_a = conv_bwd(
        s["proj"], dy, p["cw"], _row(p["conv_b"]), _row(p["conv_ln_g"]), _row(p["conv_ln_b"]), _row(gm[:C_A]),
        "conv_bwd")
    stats, dgm_b = attn_out_bwd(s["out"], s["lse"], dy, _row(gm[C_A:C_A + C_B]), "attn_out_bwd")
    rides = [_exchange_of(dwgu[:, :F], True), _exchange_of(dwgu[:, F:], True), None]
    dqs, dks, dvs, got = [], [], [], []
    for d, (qd, kd, vd), (dod, lsed, deltad), ride in zip(DILATIONS, s["dil"], stats, rides):
        (dq, dk, dv), parts = attn_bwd_pattern(qd, kd, vd, dod, lsed, deltad, f"attn_bwd_d{d}", comm=ride)
        dqs.append(dq)
        dks.append(dk)
        dvs.append(dv)
        got.append(parts)
    du, dpw, dps, dgm_c = pool_bwd(s["proj"], dy, p["pool_w"], _row(p["pool_scale"]), _row(gm[C_A + C_B:]),
                                   "pool_bwd")
    dproj = rope_bwd(dqs, dks, dvs, tables, dconv, du, "rope_bwd")
    dh1 = matmul(dproj, w["win_t"], "nn", BF16, "d_proj", tk=4608)
    dwin, got_out = matmul(s["h1t"], dproj, "nn", BF16, "dw_in", tm=1024, tn=1152, tk=2048,
                           comm=_exchange_of(dwout, False))
    grads = dict(
        dwin=dwin, parts_out=got_out[0], parts_gate=got[0][0], parts_up=got[1][0],
        conv_w=dcw[:CONV_WIDTH], conv_b=dcb[0], conv_ln_g=dlg[0], conv_ln_b=dlb[0], pool_w=dpw, pool_scale=dps[0],
        g_mix=jnp.concatenate([dgm_a[0], dgm_b[0], dgm_c[0]]))
    if below is None:
        dx, dg_pre_mix = rms_bwd(s["x"], dh1, _row(p["g_pre_mix"]), dx2, F32, "rms_bwd_first")
        df = None
    else:
        dx, df, dg_pre_mix, dg_post_ffn = rms_bwd_chain(
            s["x"], dh1, _row(p["g_pre_mix"]), dx2, below[0], _row(below[1]), "rms_bwd_mix_to_ffn")
        grads["g_post_ffn_below"] = dg_post_ffn[0]
    grads["g_pre_mix"] = dg_pre_mix[0]
    return dx, df, grads


WEIGHTS = ["w_in", "conv_w", "conv_b", "conv_ln_g", "conv_ln_b", "pool_w", "pool_scale", "g_mix", "w_out", "g_pre_mix",
           "g_post_mix", "g_pre_ffn", "g_post_ffn", "w_gate", "w_up", "w_down"]
BIG = ["w_in", "w_out", "w_gate", "w_up", "w_down"]
REPLICATED = ["conv_b", "conv_ln_g", "conv_ln_b", "pool_w", "pool_scale", "g_mix", "g_pre_mix", "g_post_mix",
              "g_pre_ffn", "g_post_ffn"]
PACK_ROWS = 256


def adamw_layer(parts, w, m, v, layer, prev, name):
    n, R, C = parts.shape
    tr = _rows(R, ROW_TILE)

    def body(p_ref, w_ref, m_ref, v_ref, *rest):
        g_ref, d_ref, nm_ref, nv_ref = rest[-4:]
        g = p_ref[0].astype(F32)
        for k in range(1, n):
            g = g + p_ref[k].astype(F32)
        mm = ADAM_B1 * m_ref[...] + (1.0 - ADAM_B1) * g
        vv = ADAM_B2 * v_ref[...] + (1.0 - ADAM_B2) * jnp.square(g)
        m_hat = mm / (1.0 - ADAM_B1 ** ADAM_STEP)
        v_hat = vv / (1.0 - ADAM_B2 ** ADAM_STEP)
        g_ref[...] = g
        d_ref[...] = -ADAM_LR * (m_hat / (jnp.sqrt(v_hat) + ADAM_EPS) + ADAM_WD * w_ref[...])
        nm_ref[...] = mm
        nv_ref[...] = vv

    spec = pl.BlockSpec((None, tr, C), lambda i: (layer, i, 0))
    shape = jax.ShapeDtypeStruct(w.shape, F32)
    prev = list(prev) if prev is not None else []
    return pl.pallas_call(
        body, name=name, grid=(R // tr,),
        in_specs=[pl.BlockSpec((n, tr, C), lambda i: (0, i, 0)), spec, spec, spec] + [_ANY] * len(prev),
        out_specs=[spec] * 4, out_shape=[shape] * 4,
        input_output_aliases={4 + k: k for k in range(len(prev))}, compiler_params=_params("parallel"),
    )(parts, w, m, v, *prev)


def _pack(arrays):
    flat = jnp.concatenate([a.reshape(-1).astype(F32) for a in arrays])
    unit = PACK_ROWS * LANES
    padded = -(-flat.shape[0] // unit) * unit
    return jnp.pad(flat, (0, padded - flat.shape[0])).reshape(-1, LANES)


def _unpack(packed, like):
    flat = packed.reshape(-1)
    out, at = [], 0
    for a in like:
        out.append(flat[at:at + a.size].reshape(a.shape))
        at += a.size
    return out


def kernel(x, w_in, conv_w, conv_b, conv_ln_g, conv_ln_b, pool_w, pool_scale, g_mix, w_out, g_pre_mix, g_post_mix, g_pre_ffn, g_post_ffn, w_gate, w_up, w_down, loss_target, m_w_in, m_conv_w, m_conv_b, m_conv_ln_g, m_conv_ln_b, m_pool_w, m_pool_scale, m_g_mix, m_w_out, m_g_pre_mix, m_g_post_mix, m_g_pre_ffn, m_g_post_ffn, m_w_gate, m_w_up, m_w_down, v_w_in, v_conv_w, v_conv_b, v_conv_ln_g, v_conv_ln_b, v_pool_w, v_pool_scale, v_g_mix, v_w_out, v_g_pre_mix, v_g_post_mix, v_g_pre_ffn, v_g_post_ffn, v_w_gate, v_w_up, v_w_down):
    w = dict(w_in=w_in, conv_w=conv_w, conv_b=conv_b, conv_ln_g=conv_ln_g, conv_ln_b=conv_ln_b, pool_w=pool_w,
             pool_scale=pool_scale, g_mix=g_mix, w_out=w_out, g_pre_mix=g_pre_mix, g_post_mix=g_post_mix,
             g_pre_ffn=g_pre_ffn, g_post_ffn=g_post_ffn, w_gate=w_gate, w_up=w_up, w_down=w_down)
    m = dict(w_in=m_w_in, conv_w=m_conv_w, conv_b=m_conv_b, conv_ln_g=m_conv_ln_g, conv_ln_b=m_conv_ln_b,
             pool_w=m_pool_w, pool_scale=m_pool_scale, g_mix=m_g_mix, w_out=m_w_out, g_pre_mix=m_g_pre_mix,
             g_post_mix=m_g_post_mix, g_pre_ffn=m_g_pre_ffn, g_post_ffn=m_g_post_ffn, w_gate=m_w_gate, w_up=m_w_up,
             w_down=m_w_down)
    v = dict(w_in=v_w_in, conv_w=v_conv_w, conv_b=v_conv_b, conv_ln_g=v_conv_ln_g, conv_ln_b=v_conv_ln_b,
             pool_w=v_pool_w, pool_scale=v_pool_scale, g_mix=v_g_mix, w_out=v_w_out, g_pre_mix=v_g_pre_mix,
             g_post_mix=v_g_post_mix, g_pre_ffn=v_g_pre_ffn, g_post_ffn=v_g_post_ffn, w_gate=v_w_gate, w_up=v_w_up,
             w_down=v_w_down)
    depth = w_in.shape[0]
    xs, target = x[0], loss_target[0]
    S, D = xs.shape
    tables = rope_tables(S)

    cw_shard = jnp.pad(conv_w, ((0, 0), (0, 1), (0, 0)))
    cw_all = all_gather([cw_shard.reshape(-1, LANES)], "gather_conv_w")[0]
    cw_full = cw_all.reshape(N_DEV, depth, 32, -1).transpose(1, 2, 0, 3).reshape(depth, 32, C_A)
    small = []
    for l in range(depth):
        small.append({n: w[n][l] for n in REPLICATED})
        small[l]["cw"] = cw_full[l]

    def shard(n, l):
        return (w[n][l].T if n in ("w_in", "w_gate", "w_up") else w[n][l]).astype(BF16)

    def joined(blocks):
        return blocks.reshape(-1, blocks.shape[2])

    def gather_of(*names_layers):
        return Gather([shard(n, l) for n, l in names_layers]) if names_layers else None

    def gate_up(gate_blocks, up_blocks):
        return jnp.concatenate([joined(gate_blocks), joined(up_blocks)], axis=0)

    w_mix = [dict() for _ in range(depth)]
    w_ffn = [dict() for _ in range(depth)]
    saved_mix, saved_ffn = [None] * depth, [None] * depth
    w_mix[0]["win_t"] = joined(communicate(gather_of(("w_in", 0)), "gather_first_weights")[0])
    held = {}
    h = xs
    normed = rms_fwd(xs, _row(small[0]["g_pre_mix"]), "rms_first")
    for l in range(depth):
        more = l + 1 < depth
        first = l == 0

        def after_proj(got, l=l, first=first):
            w_mix[l]["wout"] = joined(got[0])
            if first:
                held["gate"] = got[1]

        x2, h2, h2t, saved_mix[l], got_mix = mix_forward(
            h, *normed, w_mix[l], small[l], tables,
            gather_of(("w_out", l), ("w_gate", l)) if first else gather_of(("w_out", l)),
            gather_of(("w_up", l)) if first else None, after_proj)
        if first:
            w_ffn[0]["wgu_t"] = gate_up(held["gate"], got_mix[0])

        def after_in(got, first=first):
            if first:
                w_ffn[0]["wd"] = joined(got[0])

        rides = ([("w_down", 0)] if first else []) + ([("w_gate", l + 1), ("w_up", l + 1)] if more else [])
        h, normed, saved_ffn[l], got_in, got_out = ffn_forward(
            x2, h2, h2t, w_ffn[l], small[l], small[l + 1]["g_pre_mix"] if more else None,
            gather_of(*rides), gather_of(("w_down", l + 1), ("w_in", l + 1)) if more else None, after_in)
        if more:
            w_ffn[l + 1]["wgu_t"] = gate_up(*got_in[-2:])
            w_ffn[l + 1]["wd"] = joined(got_out[0])
            w_mix[l + 1]["win_t"] = joined(got_out[1])

    dh, sq = loss_head(h, target, "loss_head")
    loss = lax.psum(0.5 * jnp.sum(sq) / D, ("x", "y", "c"))

    big_out = {n: None for n in BIG}
    by_columns = ("w_in", "w_gate", "w_up")
    state = {n: tuple(jnp.swapaxes(t[n], 1, 2) if n in by_columns else t[n] for t in (w, m, v)) for n in BIG}

    def update(n, l, parts):
        big_out[n] = adamw_layer(parts, *state[n], l, big_out[n], f"adamw_{n}_layer{l}")

    small_grads = [None] * depth
    carried = None
    top = depth - 1
    df, dg_post_ffn = rms_bwd(saved_ffn[top]["f"], dh, _row(small[top]["g_post_ffn"]), None, BF16, "rms_bwd_last")
    post_ffn = {top: dg_post_ffn[0]}
    for l in reversed(range(depth)):
        dx2, dz, gf = ffn_backward(dh, df, saved_ffn[l], saved_mix[l]["z"], w_ffn[l], small[l],
                                   _exchange_of(carried, True))
        if carried is not None:
            update("w_in", l + 1, gf["got_act"][0])
        update("w_down", l, gf["parts_wd"])
        below = (saved_ffn[l - 1]["f"], small[l - 1]["g_post_ffn"]) if l > 0 else None
        dh, df, gm_ = mix_backward(dx2, dz, saved_mix[l], w_mix[l], small[l], tables, gf["dwgu"], below)
        if l > 0:
            post_ffn[l - 1] = gm_["g_post_ffn_below"]
        update("w_gate", l, gm_["parts_gate"])
        update("w_up", l, gm_["parts_up"])
        update("w_out", l, gm_["parts_out"])
        carried = gm_["dwin"]
        small_grads[l] = {**gf, **gm_, "g_post_ffn": post_ffn[l]}
    update("w_in", 0, exchange([_to_blocks(carried, True)], "exchange_last_grads")[0])

    names = REPLICATED + ["conv_w"]
    stacked = [jnp.stack([small_grads[l][n] for l in range(depth)]) for n in names]
    partial = all_gather([_pack(stacked)], "gather_small_grads")[0]
    zeros = jnp.zeros_like(stacked[-1])
    packed = adamw(partial, _pack([w[n] for n in REPLICATED] + [zeros]), _pack([m[n] for n in REPLICATED] + [zeros]),
                   _pack([v[n] for n in REPLICATED] + [zeros + 1.0]), "adamw_replicated")
    small_out = [_unpack(o, stacked) for o in packed]
    out = {n: tuple(o[i] for o in small_out) for i, n in enumerate(REPLICATED)}
    width = conv_w.shape[2]
    g_cw = lax.dynamic_slice_in_dim(small_out[0][-1], _index(*_place()) * width, width, axis=2)
    cw_res = adamw(g_cw.reshape(1, -1, LANES), conv_w.reshape(-1, LANES), m["conv_w"].reshape(-1, LANES),
                   v["conv_w"].reshape(-1, LANES), "adamw_conv_w")
    out["conv_w"] = tuple(o.reshape(conv_w.shape) for o in cw_res)
    for n in BIG:
        out[n] = tuple(jnp.swapaxes(o, 1, 2) if n in by_columns else o for o in big_out[n])
    results = [loss, dh[None]]
    for k in range(4):
        results += [out[n][k] for n in WEIGHTS]
    return tuple(results)
```

```python
import functools
import math

import jax
import jax.numpy as jnp
from jax import lax
from jax.experimental import pallas as pl
from jax.experimental.pallas import tpu as pltpu

F32 = jnp.float32
BF16 = jnp.bfloat16

N_DEV = 8
DEPTH = 4
EPS = 1e-6
NEG = -1e30

C_A = 512
N_HEADS = 16
HEAD_DIM = 64
C_B = N_HEADS * HEAD_DIM
C_C = 512
POOL_WINDOWS = (2, 4, 8, 16)
C_G = C_C // len(POOL_WINDOWS)
CONV_WIDTH = 31
CONV_HALF = CONV_WIDTH // 2
DILATIONS = (1, 4, 16)
ATTN_HALF = 64
ROT_DIM = HEAD_DIM // 4
ROPE_THETA = 500000.0

ADAM_LR = 0.001
ADAM_B1 = 0.9
ADAM_B2 = 0.999
ADAM_EPS = 1e-08
ADAM_WD = 0.01
ADAM_STEP = 10

LANES = 128
SUBLANES = 8
VMEM_LIMIT = 56 * 1024 * 1024
ROW_TILE = 256
SEQ_TILE = 256
ATTN_BLOCK = 128
MESH = pl.DeviceIdType.MESH


def _params(*sem):
    return pltpu.CompilerParams(dimension_semantics=sem, vmem_limit_bytes=VMEM_LIMIT)


def _tile(n, target):
    if n <= target:
        return n
    t = (target // LANES) * LANES
    while t >= LANES:
        if n % t == 0:
            return t
        t -= LANES
    return n


def _rows(n, target):
    t = min(n, target)
    while n % t:
        t //= 2
    return t


def matmul(a, b, mode, out_dtype, name, tm=1024, tn=512, tk=2048, comm=None):
    if mode == "nn":
        (M, K), (_, N) = a.shape, b.shape
    elif mode == "nt":
        (M, K), (N, _) = a.shape, b.shape
    else:
        (K, M), (_, N) = a.shape, b.shape
    tm, tn, tk = _tile(M, tm), _tile(N, tn), _tile(K, tk)
    nm, nn, nk = M // tm, N // tn, K // tk
    dims = {"nn": (((1,), (0,)), ((), ())), "nt": (((1,), (1,)), ((), ())), "tn": (((0,), (0,)), ((), ()))}[mode]
    nc = comm.n if comm is not None else 0

    def body(*refs):
        a_ref, b_ref = refs[:2]
        c_ins, o_ref, c_outs = refs[2:2 + nc], refs[2 + nc], refs[3 + nc:3 + 2 * nc]
        scratch = refs[3 + 2 * nc:]
        acc, sems = (scratch[:1], scratch[1:]) if nk > 1 else ((), scratch)
        i, j, k = pl.program_id(0), pl.program_id(1), pl.program_id(2)
        if comm is not None:
            @pl.when((i == 0) & (j == 0) & (k == 0))
            def _():
                comm.start(c_ins, c_outs, sems)

        p = lax.dot_general(a_ref[...], b_ref[...], dims, preferred_element_type=F32)
        if nk == 1:
            o_ref[...] = p.astype(o_ref.dtype)
        else:
            acc_ref, = acc

            @pl.when(k == 0)
            def _():
                acc_ref[...] = p

            @pl.when(k > 0)
            def _():
                acc_ref[...] += p

            @pl.when(k == nk - 1)
            def _():
                o_ref[...] = acc_ref[...].astype(o_ref.dtype)

        if comm is not None:
            @pl.when((i == nm - 1) & (j == nn - 1) & (k == nk - 1))
            def _():
                comm.finish(c_ins, c_outs, sems)

    if mode == "nn":
        a_spec = pl.BlockSpec((tm, tk), lambda i, j, k: (i, k))
        b_spec = pl.BlockSpec((tk, tn), lambda i, j, k: (k, j))
    elif mode == "nt":
        a_spec = pl.BlockSpec((tm, tk), lambda i, j, k: (i, k))
        b_spec = pl.BlockSpec((tn, tk), lambda i, j, k: (j, k))
    else:
        a_spec = pl.BlockSpec((tk, tm), lambda i, j, k: (k, i))
        b_spec = pl.BlockSpec((tk, tn), lambda i, j, k: (k, j))
    o_spec = pl.BlockSpec((tm, tn), lambda i, j, k: (i, j))
    o_shape = jax.ShapeDtypeStruct((M, N), out_dtype)
    acc_shape = [pltpu.VMEM((tm, tn), F32)] if nk > 1 else []
    if comm is None:
        return pl.pallas_call(
            body, name=name, grid=(nm, nn, nk), in_specs=[a_spec, b_spec], out_specs=o_spec, out_shape=o_shape,
            scratch_shapes=acc_shape, compiler_params=_params("parallel", "parallel", "arbitrary"),
        )(a, b)
    res = pl.pallas_call(
        body, name=name, grid=(nm, nn, nk), in_specs=[a_spec, b_spec] + [_ANY] * nc,
        out_specs=[o_spec] + [_ANY] * nc, out_shape=[o_shape] + comm.out_shape,
        scratch_shapes=acc_shape + comm.sems, compiler_params=_params("arbitrary", "arbitrary", "arbitrary"),
    )(a, b, *comm.ins)
    return res[0], res[1:]


def _rms(t):
    return t * lax.rsqrt(jnp.mean(t * t, axis=-1, keepdims=True) + EPS)


def _rms_bwd(t, dn):
    r = lax.rsqrt(jnp.mean(t * t, axis=-1, keepdims=True) + EPS)
    n = t * r
    return r * (dn - n * jnp.mean(dn * n, axis=-1, keepdims=True)), n


def _row_spec(tr, d):
    return pl.BlockSpec((tr, d), lambda i: (i, 0))


def _vec_spec(d):
    return pl.BlockSpec((1, d), lambda i: (0, 0))


def _col_spec(d, tr):
    return pl.BlockSpec((d, tr), lambda i: (0, i))


def rms_fwd(x, g, name):
    S, D = x.shape
    tr = _rows(S, ROW_TILE)

    def body(x_ref, g_ref, o_ref, ot_ref):
        h = _rms(x_ref[...].astype(F32)) * g_ref[...]
        o_ref[...] = h.astype(o_ref.dtype)
        ot_ref[...] = h.T.astype(ot_ref.dtype)

    return pl.pallas_call(
        body, name=name, grid=(S // tr,), in_specs=[_row_spec(tr, D), _vec_spec(D)],
        out_specs=[_row_spec(tr, D), _col_spec(D, tr)],
        out_shape=[jax.ShapeDtypeStruct((S, D), BF16), jax.ShapeDtypeStruct((D, S), BF16)],
        compiler_params=_params("parallel"),
    )(x, g)


def norm_residual(z, x, g, name):
    S, D = x.shape
    tr = _rows(S, ROW_TILE)

    def body(z_ref, x_ref, g_ref, o_ref):
        o_ref[...] = x_ref[...] + _rms(z_ref[...].astype(F32)) * g_ref[...]

    return pl.pallas_call(
        body, name=name, grid=(S // tr,), in_specs=[_row_spec(tr, D), _row_spec(tr, D), _vec_spec(D)],
        out_specs=_row_spec(tr, D), out_shape=jax.ShapeDtypeStruct((S, D), F32), compiler_params=_params("parallel"),
    )(z, x, g)


def norm_residual_rms(z, x, g, g_next, name):
    S, D = x.shape
    tr = _rows(S, ROW_TILE)

    def body(z_ref, x_ref, g_ref, gn_ref, o_ref, h_ref, ht_ref):
        x_new = x_ref[...] + _rms(z_ref[...].astype(F32)) * g_ref[...]
        o_ref[...] = x_new
        h = _rms(x_new) * gn_ref[...]
        h_ref[...] = h.astype(h_ref.dtype)
        ht_ref[...] = h.T.astype(ht_ref.dtype)

    return pl.pallas_call(
        body, name=name, grid=(S // tr,),
        in_specs=[_row_spec(tr, D), _row_spec(tr, D), _vec_spec(D), _vec_spec(D)],
        out_specs=[_row_spec(tr, D), _row_spec(tr, D), _col_spec(D, tr)],
        out_shape=[jax.ShapeDtypeStruct((S, D), F32), jax.ShapeDtypeStruct((S, D), BF16),
                   jax.ShapeDtypeStruct((D, S), BF16)],
        compiler_params=_params("parallel"),
    )(z, x, g, g_next)


def rms_bwd_chain(t1, dy1, g1, res, t2, g2, name):
    S, D = t1.shape
    tr = _rows(S, ROW_TILE)

    def body(t1_ref, dy1_ref, g1_ref, res_ref, t2_ref, g2_ref, d1_ref, d2_ref, dg1_ref, dg2_ref):
        dy1v = dy1_ref[...].astype(F32)
        dt1, n1 = _rms_bwd(t1_ref[...].astype(F32), dy1v * g1_ref[...])
        d1 = dt1 + res_ref[...]
        d1_ref[...] = d1
        dt2, n2 = _rms_bwd(t2_ref[...].astype(F32), d1 * g2_ref[...])
        d2_ref[...] = dt2.astype(d2_ref.dtype)

        @pl.when(pl.program_id(0) == 0)
        def _():
            dg1_ref[...] = jnp.zeros_like(dg1_ref)
            dg2_ref[...] = jnp.zeros_like(dg2_ref)

        dg1_ref[...] += jnp.sum(dy1v * n1, axis=0, keepdims=True)
        dg2_ref[...] += jnp.sum(d1 * n2, axis=0, keepdims=True)

    row, vec = _row_spec(tr, D), _vec_spec(D)
    return pl.pallas_call(
        body, name=name, grid=(S // tr,), in_specs=[row, row, vec, row, row, vec], out_specs=[row, row, vec, vec],
        out_shape=[jax.ShapeDtypeStruct((S, D), F32), jax.ShapeDtypeStruct((S, D), BF16),
                   jax.ShapeDtypeStruct((1, D), F32), jax.ShapeDtypeStruct((1, D), F32)],
        compiler_params=_params("arbitrary"),
    )(t1, dy1, g1, res, t2, g2)


def rms_bwd(t, dy, g, res, out_dtype, name):
    S, D = t.shape
    tr = _rows(S, ROW_TILE)
    has_res = res is not None

    def body(t_ref, dy_ref, g_ref, *rest):
        if has_res:
            res_ref, dt_ref, dg_ref = rest
        else:
            dt_ref, dg_ref = rest
        dyv = dy_ref[...].astype(F32)
        dt, n = _rms_bwd(t_ref[...].astype(F32), dyv * g_ref[...])
        if has_res:
            dt = dt + res_ref[...]
        dt_ref[...] = dt.astype(dt_ref.dtype)

        @pl.when(pl.program_id(0) == 0)
        def _():
            dg_ref[...] = jnp.zeros_like(dg_ref)

        dg_ref[...] += jnp.sum(dyv * n, axis=0, keepdims=True)

    ins = [t, dy, g] + ([res] if has_res else [])
    specs = [_row_spec(tr, D), _row_spec(tr, D), _vec_spec(D)] + ([_row_spec(tr, D)] if has_res else [])
    return pl.pallas_call(
        body, name=name, grid=(S // tr,), in_specs=specs, out_specs=[_row_spec(tr, D), _vec_spec(D)],
        out_shape=[jax.ShapeDtypeStruct((S, D), out_dtype), jax.ShapeDtypeStruct((1, D), F32)],
        compiler_params=_params("arbitrary"),
    )(*ins)


def swiglu_fwd(gu, name):
    S, F2 = gu.shape
    F = F2 // 2
    tr = _rows(S, ROW_TILE)

    def body(g_ref, u_ref, o_ref, ot_ref):
        g = g_ref[...].astype(F32)
        a = g * jax.nn.sigmoid(g) * u_ref[...].astype(F32)
        o_ref[...] = a.astype(o_ref.dtype)
        ot_ref[...] = a.T.astype(ot_ref.dtype)

    return pl.pallas_call(
        body, name=name, grid=(S // tr,),
        in_specs=[pl.BlockSpec((tr, F), lambda i: (i, 0)), pl.BlockSpec((tr, F), lambda i: (i, 1))],
        out_specs=[_row_spec(tr, F), _col_spec(F, tr)],
        out_shape=[jax.ShapeDtypeStruct((S, F), BF16), jax.ShapeDtypeStruct((F, S), BF16)],
        compiler_params=_params("parallel"),
    )(gu, gu)


def swiglu_bwd(gu, da, name):
    S, F2 = gu.shape
    F = F2 // 2
    tr = _rows(S, ROW_TILE)

    def body(g_ref, u_ref, da_ref, o_ref):
        g = g_ref[...].astype(F32)
        u = u_ref[...].astype(F32)
        dav = da_ref[...].astype(F32)
        sig = jax.nn.sigmoid(g)
        o_ref[:, :F] = (dav * u * (sig * (1.0 + g * (1.0 - sig)))).astype(o_ref.dtype)
        o_ref[:, F:] = (dav * (g * sig)).astype(o_ref.dtype)

    return pl.pallas_call(
        body, name=name, grid=(S // tr,),
        in_specs=[pl.BlockSpec((tr, F), lambda i: (i, 0)), pl.BlockSpec((tr, F), lambda i: (i, 1)), _row_spec(tr, F)],
        out_specs=_row_spec(tr, F2), out_shape=jax.ShapeDtypeStruct((S, F2), BF16), compiler_params=_params("parallel"),
    )(gu, gu, da)


def loss_head(y, target, name):
    S, D = y.shape
    tr = _rows(S, ROW_TILE)

    def body(y_ref, t_ref, dy_ref, sq_ref):
        e = y_ref[...] - t_ref[...]
        dy_ref[...] = e * (1.0 / D)

        @pl.when(pl.program_id(0) == 0)
        def _():
            sq_ref[...] = jnp.zeros_like(sq_ref)

        sq_ref[...] += jnp.sum(e * e, axis=0, keepdims=True)

    return pl.pallas_call(
        body, name=name, grid=(S // tr,), in_specs=[_row_spec(tr, D), _row_spec(tr, D)],
        out_specs=[_row_spec(tr, D), _vec_spec(D)],
        out_shape=[jax.ShapeDtypeStruct((S, D), F32), jax.ShapeDtypeStruct((1, D), F32)],
        compiler_params=_params("arbitrary"),
    )(y, target)


def _halo_specs(bs, halo, width, col, n_rows):
    per = bs // halo
    last = n_rows // halo - 1
    cur = pl.BlockSpec((bs, width), lambda i: (i, col))
    prev = pl.BlockSpec((halo, width), lambda i: (jnp.maximum(i * per - 1, 0), col))
    nxt = pl.BlockSpec((halo, width), lambda i: (jnp.minimum((i + 1) * per, last), col))
    return prev, cur, nxt


TAP_CHUNK = 32


def _build_phases(ref, phases_ref):
    total = ref.shape[0] - SUBLANES
    for b in range(SUBLANES):
        phases_ref[b, 0:total, :] = ref[pl.ds(b, total), :]


def _tap_rows(phases_ref, off, n):
    b = off % SUBLANES
    return phases_ref[b, off - b:off - b + n, :]


def _conv_taps(phases_ref, offsets, n, weights_ref, init, out_ref):
    for r0 in range(0, n, TAP_CHUNK):
        rows = min(TAP_CHUNK, n - r0)
        acc = jnp.zeros((rows, C_A), F32) + init
        for t, off in offsets.items():
            acc = acc + weights_ref[t:t + 1, :] * _tap_rows(phases_ref, off + r0, rows)
        out_ref[r0:r0 + rows, :] = acc


def _glu(a, g):
    return a.astype(F32) * jax.nn.sigmoid(g.astype(F32))


def _layernorm_silu(c, lg, lb):
    mu = jnp.mean(c, axis=-1, keepdims=True)
    cc = c - mu
    rstd = lax.rsqrt(jnp.mean(cc * cc, axis=-1, keepdims=True) + EPS)
    xh = cc * rstd
    ln = xh * lg + lb
    sig = jax.nn.sigmoid(ln)
    return xh, rstd, ln, sig


def conv_fwd(proj, cw, cb, lg, lb, gm, name):
    S = proj.shape[0]
    bs = _rows(S, SEQ_TILE)
    nb = S // bs
    H = 16

    def body(ap, ac, an, gp, gc, gn, cw_ref, cb_ref, lg_ref, lb_ref, gm_ref, y_ref, win_ref, phases_ref, c_ref):
        i = pl.program_id(0)
        win_ref[0:H, :] = jnp.where(i > 0, _glu(ap[...], gp[...]), 0.0)
        win_ref[H:H + bs, :] = _glu(ac[...], gc[...])
        win_ref[H + bs:2 * H + bs, :] = jnp.where(i < nb - 1, _glu(an[...], gn[...]), 0.0)
        win_ref[bs + 2 * H:, :] = jnp.zeros((SUBLANES, C_A), F32)
        _build_phases(win_ref, phases_ref)
        _conv_taps(phases_ref, {t: H - CONV_HALF + t for t in range(CONV_WIDTH)}, bs, cw_ref, cb_ref[...], c_ref)
        _, _, ln, sig = _layernorm_silu(c_ref[...], lg_ref[...], lb_ref[...])
        y_ref[...] = (_rms(ln * sig) * gm_ref[...]).astype(y_ref.dtype)

    a_specs = _halo_specs(bs, H, C_A, 0, S)
    g_specs = _halo_specs(bs, H, C_A, 1, S)
    vec = _vec_spec(C_A)
    return pl.pallas_call(
        body, name=name, grid=(nb,),
        in_specs=[*a_specs, *g_specs, pl.BlockSpec((32, C_A), lambda i: (0, 0)), vec, vec, vec, vec],
        out_specs=_row_spec(bs, C_A), out_shape=jax.ShapeDtypeStruct((S, C_A), BF16),
        scratch_shapes=[pltpu.VMEM((bs + 2 * H + SUBLANES, C_A), F32), pltpu.VMEM((SUBLANES, bs + 2 * H, C_A), F32),
                        pltpu.VMEM((bs, C_A), F32)],
        compiler_params=_params("parallel"),
    )(proj, proj, proj, proj, proj, proj, cw, cb, lg, lb, gm)


def conv_bwd(proj, dy, cw, cb, lg, lb, gm, name):
    S = proj.shape[0]
    bs = _rows(S, SEQ_TILE)
    nb = S // bs
    H = 32
    HC = 16
    bc = bs + 2 * HC

    def body(ap, ac, an, gp, gc, gn, dp, dc_, dn, cw_ref, cb_ref, lg_ref, lb_ref, gm_ref,
             dproj_ref, dcw_ref, dcb_ref, dlg_ref, dlb_ref, dgm_ref, win_ref, dcs_ref, phases_ref, c_ref):
        i = pl.program_id(0)
        win_ref[0:H, :] = jnp.where(i > 0, _glu(ap[...], gp[...]), 0.0)
        win_ref[H:H + bs, :] = _glu(ac[...], gc[...])
        win_ref[H + bs:2 * H + bs, :] = jnp.where(i < nb - 1, _glu(an[...], gn[...]), 0.0)
        win_ref[bs + 2 * H:, :] = jnp.zeros((SUBLANES, C_A), F32)
        _build_phases(win_ref, phases_ref)
        _conv_taps(phases_ref, {t: H - HC - CONV_HALF + t for t in range(CONV_WIDTH)}, bc, cw_ref, cb_ref[...], c_ref)
        xh, rstd, ln, sig = _layernorm_silu(c_ref[...], lg_ref[...], lb_ref[...])
        ya = ln * sig
        dyv = jnp.concatenate([dp[...], dc_[...], dn[...]], axis=0).astype(F32)
        dya, n = _rms_bwd(ya, dyv * gm_ref[...])
        dln = dya * (sig * (1.0 + ln * (1.0 - sig)))
        dxh = dln * lg_ref[...]
        dcv = rstd * (dxh - jnp.mean(dxh, axis=-1, keepdims=True) - xh * jnp.mean(dxh * xh, axis=-1, keepdims=True))
        pos = i * bs - HC + lax.broadcasted_iota(jnp.int32, (bc, 1), 0)
        dcv = jnp.where((pos >= 0) & (pos < S), dcv, 0.0)
        dcs_ref[0:bc, :] = dcv
        dcs_ref[bc:, :] = jnp.zeros((SUBLANES, C_A), F32)

        @pl.when(i == 0)
        def _():
            for r in (dcw_ref, dcb_ref, dlg_ref, dlb_ref, dgm_ref):
                r[...] = jnp.zeros_like(r)

        mid = slice(HC, HC + bs)
        dcb_ref[...] += jnp.sum(dcv[mid], axis=0, keepdims=True)
        dlg_ref[...] += jnp.sum((dln * xh)[mid], axis=0, keepdims=True)
        dlb_ref[...] += jnp.sum(dln[mid], axis=0, keepdims=True)
        dgm_ref[...] += jnp.sum((dyv * n)[mid], axis=0, keepdims=True)
        dcm = dcv[mid]
        for t in range(CONV_WIDTH):
            dcw_ref[t:t + 1, :] += jnp.sum(dcm * _tap_rows(phases_ref, H - CONV_HALF + t, bs), axis=0, keepdims=True)
        _build_phases(dcs_ref, phases_ref)
        _conv_taps(phases_ref, {t: HC + CONV_HALF - t for t in range(CONV_WIDTH)}, bs, cw_ref, 0.0, c_ref)
        dh = c_ref[0:bs, :]
        a = ac[...].astype(F32)
        sg = jax.nn.sigmoid(gc[...].astype(F32))
        dproj_ref[:, :C_A] = (dh * sg).astype(dproj_ref.dtype)
        dproj_ref[:, C_A:] = (dh * a * sg * (1.0 - sg)).astype(dproj_ref.dtype)

    a_specs = _halo_specs(bs, H, C_A, 0, S)
    g_specs = _halo_specs(bs, H, C_A, 1, S)
    d_specs = _halo_specs(bs, HC, C_A, 0, S)
    vec = _vec_spec(C_A)
    full = pl.BlockSpec((32, C_A), lambda i: (0, 0))
    vshape = jax.ShapeDtypeStruct((1, C_A), F32)
    return pl.pallas_call(
        body, name=name, grid=(nb,),
        in_specs=[*a_specs, *g_specs, *d_specs, full, vec, vec, vec, vec],
        out_specs=[_row_spec(bs, 2 * C_A), full, vec, vec, vec, vec],
        out_shape=[jax.ShapeDtypeStruct((S, 2 * C_A), BF16), jax.ShapeDtypeStruct((32, C_A), F32),
                   vshape, vshape, vshape, vshape],
        scratch_shapes=[pltpu.VMEM((bs + 2 * H + SUBLANES, C_A), F32), pltpu.VMEM((bc + SUBLANES, C_A), F32),
                        pltpu.VMEM((SUBLANES, bs + 2 * H, C_A), F32), pltpu.VMEM((bc, C_A), F32)],
        compiler_params=_params("arbitrary"),
    )(proj, proj, proj, proj, proj, proj, dy, dy, dy, cw, cb, lg, lb, gm)


POOL_HALO = 16


def _shift(x, k):
    n = x.shape[0]
    return pltpu.roll(x, (-k) % n, axis=0)


def _pool_means(u, pos, S):
    w2 = _shift(u, -1) + u
    w4 = _shift(w2, -1) + _shift(w2, 1)
    w8 = _shift(w4, -2) + _shift(w4, 2)
    w16 = _shift(w8, -4) + _shift(w8, 4)
    sums = (w2, w4, w8, w16)
    lane = lax.broadcasted_iota(jnp.int32, (1, C_C), 1)
    total = jnp.zeros_like(u)
    inv = jnp.zeros_like(u)
    for gi, win in enumerate(POOL_WINDOWS):
        cnt = jnp.minimum(pos + (win - win // 2), S) - jnp.maximum(pos - win // 2, 0)
        icnt = 1.0 / jnp.maximum(cnt, 1).astype(F32)
        sel = (lane >= gi * C_G) & (lane < (gi + 1) * C_G)
        total = jnp.where(sel, sums[gi], total)
        inv = jnp.where(sel, icnt, inv)
    return total * inv - u, inv


def _pool_adjoint(e):
    v2 = e + _shift(e, 1)
    v4 = _shift(v2, -1) + _shift(v2, 1)
    v8 = _shift(v4, -2) + _shift(v4, 2)
    v16 = _shift(v8, -4) + _shift(v8, 4)
    sums = (v2, v4, v8, v16)
    lane = lax.broadcasted_iota(jnp.int32, (1, C_C), 1)
    out = jnp.zeros_like(e)
    for gi in range(len(POOL_WINDOWS)):
        sel = (lane >= gi * C_G) & (lane < (gi + 1) * C_G)
        out = jnp.where(sel, sums[gi], out)
    return out


def _pool_window(up, uc, un, i, nb, bs, S):
    H = POOL_HALO
    u = jnp.concatenate([jnp.where(i > 0, up[...].astype(F32), 0.0), uc[...].astype(F32),
                         jnp.where(i < nb - 1, un[...].astype(F32), 0.0)], axis=0)
    pos = i * bs - H + lax.broadcasted_iota(jnp.int32, (bs + 2 * H, 1), 0)
    return u, pos


def _pool_mix(pooled, pw_ref):
    outs = []
    for gi in range(len(POOL_WINDOWS)):
        outs.append(jnp.dot(pooled[:, gi * C_G:(gi + 1) * C_G].astype(BF16), pw_ref[gi].astype(BF16),
                            preferred_element_type=F32))
    return jnp.concatenate(outs, axis=1)


def pool_fwd(proj, pw, ps, gm, name):
    S, width = proj.shape
    col = width // C_C - 1
    bs = _rows(S, SEQ_TILE)
    nb = S // bs
    H = POOL_HALO

    def body(up, uc, un, pw_ref, ps_ref, gm_ref, y_ref):
        i = pl.program_id(0)
        u, pos = _pool_window(up, uc, un, i, nb, bs, S)
        pooled, _ = _pool_means(u, pos, S)
        mixed = _pool_mix(pooled[H:H + bs], pw_ref)
        y_ref[...] = (_rms(mixed * ps_ref[...]) * gm_ref[...]).astype(y_ref.dtype)

    vec = _vec_spec(C_C)
    return pl.pallas_call(
        body, name=name, grid=(nb,),
        in_specs=[*_halo_specs(bs, H, C_C, col, S), pl.BlockSpec((4, C_G, C_G), lambda i: (0, 0, 0)), vec, vec],
        out_specs=_row_spec(bs, C_C), out_shape=jax.ShapeDtypeStruct((S, C_C), BF16),
        compiler_params=_params("parallel"),
    )(proj, proj, proj, pw, ps, gm)


def pool_bwd(proj, dy, pw, ps, gm, name):
    S, width = proj.shape
    col = width // C_C - 1
    dcol = dy.shape[1] // C_C - 1
    bs = _rows(S, SEQ_TILE)
    nb = S // bs
    H = POOL_HALO
    W = bs + 2 * H

    def body(up, uc, un, dp, dc_, dn, pw_ref, ps_ref, gm_ref, du_ref, dpw_ref, dps_ref, dgm_ref):
        i = pl.program_id(0)
        u, pos = _pool_window(up, uc, un, i, nb, bs, S)
        pooled, inv = _pool_means(u, pos, S)
        mixed = _pool_mix(pooled, pw_ref)
        dyv = jnp.concatenate([dp[...], dc_[...], dn[...]], axis=0).astype(F32)
        dyc, n = _rms_bwd(mixed * ps_ref[...], dyv * gm_ref[...])
        dmixed = dyc * ps_ref[...]
        dmb = dmixed.astype(BF16)
        dpooled = jnp.concatenate(
            [lax.dot_general(dmb[:, gi * C_G:(gi + 1) * C_G], pw_ref[gi].astype(BF16), (((1,), (1,)), ((), ())),
                             preferred_element_type=F32) for gi in range(len(POOL_WINDOWS))], axis=1)
        dpooled = jnp.where((pos >= 0) & (pos < S), dpooled, 0.0)
        du = _pool_adjoint(dpooled * inv) - dpooled
        du_ref[...] = du[H:H + bs].astype(du_ref.dtype)

        @pl.when(i == 0)
        def _():
            for r in (dpw_ref, dps_ref, dgm_ref):
                r[...] = jnp.zeros_like(r)

        mid = slice(H, H + bs)
        dps_ref[...] += jnp.sum((dyc * mixed)[mid], axis=0, keepdims=True)
        dgm_ref[...] += jnp.sum((dyv * n)[mid], axis=0, keepdims=True)
        pb = pooled[mid].astype(BF16)
        for gi in range(len(POOL_WINDOWS)):
            sl = slice(gi * C_G, (gi + 1) * C_G)
            dpw_ref[gi] += lax.dot_general(pb[:, sl], dmb[mid][:, sl], (((0,), (0,)), ((), ())),
                                           preferred_element_type=F32)

    vec = _vec_spec(C_C)
    full = pl.BlockSpec((4, C_G, C_G), lambda i: (0, 0, 0))
    vshape = jax.ShapeDtypeStruct((1, C_C), F32)
    return pl.pallas_call(
        body, name=name, grid=(nb,),
        in_specs=[*_halo_specs(bs, H, C_C, col, S), *_halo_specs(bs, H, C_C, dcol, S), full, vec, vec],
        out_specs=[_row_spec(bs, C_C), full, vec, vec],
        out_shape=[jax.ShapeDtypeStruct((S, C_C), BF16), jax.ShapeDtypeStruct((4, C_G, C_G), F32), vshape, vshape],
        compiler_params=_params("arbitrary"),
    )(proj, proj, proj, dy, dy, dy, pw, ps, gm)


def rope_tables(S):
    pos = jnp.arange(S, dtype=F32)
    inv = ROPE_THETA ** (-jnp.arange(0, ROT_DIM, 2, dtype=F32) / ROT_DIM)
    ang = pos[:, None] * inv[None, :]
    half = ROT_DIM // 2
    cos, sin = jnp.cos(ang), jnp.sin(ang)
    zeros = jnp.zeros((S, half), F32)
    rest = jnp.zeros((S, HEAD_DIM - ROT_DIM), F32)
    per_head = (jnp.concatenate([cos, cos, rest + 1.0], axis=1), jnp.concatenate([-sin, zeros, rest], axis=1),
                jnp.concatenate([zeros, sin, rest], axis=1))
    return tuple(jnp.tile(t, (1, LANES // HEAD_DIM)) for t in per_head)


def _rotate(t, c, s1, s2, sign):
    half = ROT_DIM // 2
    return t * c + sign * (pltpu.roll(t, LANES - half, axis=1) * s1 + pltpu.roll(t, half, axis=1) * s2)


def rope_fwd(proj, tables, name):
    S = proj.shape[0]
    tr = _rows(S, ROW_TILE)
    qcol = 2 * C_A // C_B
    nd = len(DILATIONS)

    def body(q_ref, k_ref, v_ref, c_ref, s1_ref, s2_ref, *outs):
        c, s1, s2 = c_ref[...], s1_ref[...], s2_ref[...]
        qs, ks = [], []
        for p in range(C_B // LANES):
            sl = slice(p * LANES, (p + 1) * LANES)
            qs.append((_rotate(q_ref[:, sl].astype(F32), c, s1, s2, 1.0) * HEAD_DIM ** -0.5).astype(BF16))
            ks.append(_rotate(k_ref[:, sl].astype(F32), c, s1, s2, 1.0).astype(BF16))
        tensors = (jnp.concatenate(qs, axis=1), jnp.concatenate(ks, axis=1), v_ref[...])
        for n, d in enumerate(DILATIONS):
            for t, x in enumerate(tensors):
                _store_dilated(outs[3 * n + t], x, d)

    tab = _row_spec(tr, LANES)
    res = pl.pallas_call(
        body, name=name, grid=(S // tr,),
        in_specs=[pl.BlockSpec((tr, C_B), lambda i, col=qcol + n: (i, col)) for n in range(3)] + [tab, tab, tab],
        out_specs=[_dilated_spec(tr, d, C_B) for d in DILATIONS for _ in range(3)],
        out_shape=[jax.ShapeDtypeStruct((d, S // d, C_B), BF16) for d in DILATIONS for _ in range(3)],
        compiler_params=_params("parallel"),
    )(proj, proj, proj, *tables)
    return [tuple(res[3 * n:3 * n + 3]) for n in range(nd)]


def rope_bwd(dqs, dks, dvs, tables, dconv, du, name):
    S = dconv.shape[0]
    tr = _rows(S, ROW_TILE)
    n = len(dqs)
    base = 2 * C_A
    width = base + 3 * C_B + C_C

    def body(*refs):
        dq_refs, dk_refs, dv_refs = refs[:n], refs[n:2 * n], refs[2 * n:3 * n]
        c_ref, s1_ref, s2_ref, dconv_ref, du_ref, o_ref = refs[3 * n:]
        c, s1, s2 = c_ref[...], s1_ref[...], s2_ref[...]
        o_ref[:, :base] = dconv_ref[...]
        o_ref[:, base + 3 * C_B:] = du_ref[...]
        dq_all = sum(_load_dilated(r, d).astype(F32) for r, d in zip(dq_refs, DILATIONS))
        dk_all = sum(_load_dilated(r, d).astype(F32) for r, d in zip(dk_refs, DILATIONS))
        dv_all = sum(_load_dilated(r, d).astype(F32) for r, d in zip(dv_refs, DILATIONS))
        for p in range(C_B // LANES):
            sl = slice(p * LANES, (p + 1) * LANES)
            dq, dk, dv = dq_all[:, sl], dk_all[:, sl], dv_all[:, sl]
            at = base + p * LANES
            o_ref[:, at:at + LANES] = (_rotate(dq, c, s1, s2, -1.0) * HEAD_DIM ** -0.5).astype(BF16)
            o_ref[:, C_B + at:C_B + at + LANES] = _rotate(dk, c, s1, s2, -1.0).astype(BF16)
            o_ref[:, 2 * C_B + at:2 * C_B + at + LANES] = dv.astype(BF16)

    tab = _row_spec(tr, LANES)
    return pl.pallas_call(
        body, name=name, grid=(S // tr,),
        in_specs=[_dilated_spec(tr, d, C_B) for _ in range(3) for d in DILATIONS]
        + [tab, tab, tab, _row_spec(tr, base), _row_spec(tr, C_C)],
        out_specs=_row_spec(tr, width), out_shape=jax.ShapeDtypeStruct((S, width), BF16),
        compiler_params=_params("parallel"),
    )(*dqs, *dks, *dvs, *tables, dconv, du)


def _attn_specs(bq, width, L):
    per = bq // ATTN_HALF
    last = L // ATTN_HALF - 1
    cur = pl.BlockSpec((None, bq, width), lambda r, j: (r, j, 0))
    prev = pl.BlockSpec((None, ATTN_HALF, width), lambda r, j: (r, jnp.maximum(j * per - 1, 0), 0))
    nxt = pl.BlockSpec((None, ATTN_HALF, width), lambda r, j: (r, jnp.minimum((j + 1) * per, last), 0))
    return prev, cur, nxt


def _window(refs, sl):
    return jnp.concatenate([r[:, sl] for r in refs], axis=0)


def _band_mask(j, bq, L, rows_are_window):
    bw = bq + 2 * ATTN_HALF
    if rows_are_window:
        rp = j * bq - ATTN_HALF + lax.broadcasted_iota(jnp.int32, (bw, 1), 0)
        cp = j * bq + lax.broadcasted_iota(jnp.int32, (1, bq), 1)
        return (jnp.abs(rp - cp) <= ATTN_HALF) & (rp >= 0) & (rp < L)
    rp = j * bq + lax.broadcasted_iota(jnp.int32, (bq, 1), 0)
    cp = j * bq - ATTN_HALF + lax.broadcasted_iota(jnp.int32, (1, bw), 1)
    return (jnp.abs(rp - cp) <= ATTN_HALF) & (cp >= 0) & (cp < L)


def _head_col(stats, h):
    lane = lax.broadcasted_iota(jnp.int32, (1, LANES), 1)
    return jnp.sum(jnp.where(lane == h, stats, 0.0), axis=1, keepdims=True)


def _stack_heads(x):
    first = lax.broadcasted_iota(jnp.int32, (1, LANES), 1) < HEAD_DIM
    zero = jnp.zeros_like(x)
    return jnp.concatenate([jnp.where(first, x, zero), jnp.where(first, zero, x)], axis=0)


_NT = (((1,), (1,)), ((), ()))
_TN = (((0,), (0,)), ((), ()))


def attn_fwd_pattern(qd, kd, vd, name):
    d, L, _ = qd.shape
    bq = _rows(L, ATTN_BLOCK)

    def body(q_ref, kp, kc, kn, vp, vc, vn, o_ref, lse_ref):
        j = pl.program_id(1)
        mask = _band_mask(j, bq, L, False)
        mask2 = jnp.concatenate([mask, mask], axis=0)
        lane = lax.broadcasted_iota(jnp.int32, (1, LANES), 1)
        first = lane < HEAD_DIM
        lse = jnp.zeros((bq, LANES), F32)
        for p in range(C_B // LANES):
            sl = slice(p * LANES, (p + 1) * LANES)
            kw = _window((kp, kc, kn), sl)
            vw = _window((vp, vc, vn), sl)
            s = jnp.where(mask2, lax.dot_general(_stack_heads(q_ref[:, sl]), kw, _NT, preferred_element_type=F32), NEG)
            m = jnp.max(s, axis=1, keepdims=True)
            e = jnp.exp(s - m)
            l = jnp.sum(e, axis=1, keepdims=True)
            o = jnp.dot(e.astype(BF16), vw, preferred_element_type=F32) * (1.0 / l)
            stat = m + jnp.log(l)
            lse = jnp.where(lane == 2 * p, stat[:bq], jnp.where(lane == 2 * p + 1, stat[bq:], lse))
            o_ref[:, sl] = jnp.where(first, o[:bq], o[bq:]).astype(o_ref.dtype)
        lse_ref[...] = lse

    kv = _attn_specs(bq, C_B, L)
    return pl.pallas_call(
        body, name=name, grid=(d, L // bq), in_specs=[kv[1], *kv, *kv],
        out_specs=[kv[1], pl.BlockSpec((None, bq, LANES), lambda r, j: (r, j, 0))],
        out_shape=[jax.ShapeDtypeStruct((d, L, C_B), BF16), jax.ShapeDtypeStruct((d, L, LANES), F32)],
        compiler_params=_params("parallel", "parallel"),
    )(qd, kd, kd, kd, vd, vd, vd)


def attn_combine(os_, lses, gm, ya, yc, name):
    S = ya.shape[0]
    tr = _rows(S, ROW_TILE)
    n = len(os_)

    def body(*refs):
        o_refs, l_refs = refs[:n], refs[n:2 * n]
        gm_ref, ya_ref, yc_ref, y_ref, yt_ref, out_ref, lse_ref = refs[2 * n:]
        ls = [_load_dilated(r, d) for r, d in zip(l_refs, DILATIONS)]
        os_tok = [_load_dilated(r, d) for r, d in zip(o_refs, DILATIONS)]
        mx = functools.reduce(jnp.maximum, ls)
        ws = [jnp.exp(l - mx) for l in ls]
        den = sum(ws)
        lse_ref[...] = mx + jnp.log(den)
        wn = [w / den for w in ws]
        lane = lax.broadcasted_iota(jnp.int32, (1, LANES), 1)
        first = lane < HEAD_DIM
        blocks = []
        for p in range(C_B // LANES):
            sl = slice(p * LANES, (p + 1) * LANES)
            acc = jnp.zeros((tr, LANES), F32)
            for w, o in zip(wn, os_tok):
                acc = acc + jnp.where(first, _head_col(w, 2 * p), _head_col(w, 2 * p + 1)) * o[:, sl].astype(F32)
            blocks.append(acc)
        out = jnp.concatenate(blocks, axis=1)
        out_ref[...] = out.astype(out_ref.dtype)
        y = jnp.concatenate([ya_ref[...].astype(F32), _rms(out) * gm_ref[...], yc_ref[...].astype(F32)], axis=1)
        y_ref[...] = y.astype(y_ref.dtype)
        yt_ref[...] = y.T.astype(yt_ref.dtype)

    st = _row_spec(tr, LANES)
    mix = C_A + C_B + C_C
    return pl.pallas_call(
        body, name=name, grid=(S // tr,),
        in_specs=[_dilated_spec(tr, d, C_B) for d in DILATIONS] + [_dilated_spec(tr, d, LANES) for d in DILATIONS]
        + [_vec_spec(C_B), _row_spec(tr, C_A), _row_spec(tr, C_C)],
        out_specs=[_row_spec(tr, mix), _col_spec(mix, tr), _row_spec(tr, C_B), st],
        out_shape=[jax.ShapeDtypeStruct((S, mix), BF16), jax.ShapeDtypeStruct((mix, S), BF16),
                   jax.ShapeDtypeStruct((S, C_B), BF16), jax.ShapeDtypeStruct((S, LANES), F32)],
        compiler_params=_params("parallel"),
    )(*os_, *lses, gm, ya, yc)


def attn_out_bwd(out, lse, dy, gm, name):
    S = out.shape[0]
    tr = _rows(S, ROW_TILE)
    nd = len(DILATIONS)

    def body(o_ref, lse_ref, dy1, dy2, g_ref, *outs):
        dg_ref = outs[-1]
        o = o_ref[...].astype(F32)
        dyv = jnp.concatenate([dy1[...], dy2[...]], axis=1).astype(F32)
        do, n = _rms_bwd(o, dyv * g_ref[...])
        dob = do.astype(BF16)
        prod = dob.astype(F32) * o
        lane = lax.broadcasted_iota(jnp.int32, (1, LANES), 1)
        first = lane < HEAD_DIM
        delta = jnp.zeros((tr, LANES), F32)
        for p in range(C_B // LANES):
            blk = prod[:, p * LANES:(p + 1) * LANES]
            delta = jnp.where(lane == 2 * p, jnp.sum(jnp.where(first, blk, 0.0), axis=1, keepdims=True), delta)
            delta = jnp.where(lane == 2 * p + 1, jnp.sum(jnp.where(first, 0.0, blk), axis=1, keepdims=True), delta)
        lse_v = lse_ref[...]
        for k, d in enumerate(DILATIONS):
            _store_dilated(outs[3 * k], dob, d)
            _store_dilated(outs[3 * k + 1], lse_v, d)
            _store_dilated(outs[3 * k + 2], delta, d)

        @pl.when(pl.program_id(0) == 0)
        def _():
            dg_ref[...] = jnp.zeros_like(dg_ref)

        dg_ref[...] += jnp.sum(dyv * n, axis=0, keepdims=True)

    widths = (C_B, LANES, LANES)
    dtypes = (BF16, F32, F32)
    res = pl.pallas_call(
        body, name=name, grid=(S // tr,),
        in_specs=[_row_spec(tr, C_B), _row_spec(tr, LANES), pl.BlockSpec((tr, C_A), lambda i: (i, 1)),
                  pl.BlockSpec((tr, C_A), lambda i: (i, 2)), _vec_spec(C_B)],
        out_specs=[_dilated_spec(tr, d, wd) for d in DILATIONS for wd in widths] + [_vec_spec(C_B)],
        out_shape=[jax.ShapeDtypeStruct((d, S // d, wd), dt) for d in DILATIONS for wd, dt in zip(widths, dtypes)]
        + [jax.ShapeDtypeStruct((1, C_B), F32)],
        compiler_params=_params("arbitrary"),
    )(out, lse, dy, dy, gm)
    return [tuple(res[3 * k:3 * k + 3]) for k in range(nd)], res[-1]


def attn_bwd_pattern(qd, kd, vd, dod, lsed, deltad, name, comm=None):
    d, L, _ = qd.shape
    nc = comm.n if comm is not None else 0
    bq = _rows(L, ATTN_BLOCK)

    nb = L // bq
    bw = bq + 2 * ATTN_HALF
    lo = bq - ATTN_HALF

    def body(qc, dc_, lc, tc, kp, kc, kn, vp, vc, vn, *rest):
        c_ins, (dq_ref, dk_ref, dv_ref), c_outs = rest[:nc], rest[nc:nc + 3], rest[nc + 3:2 * nc + 3]
        dk_acc, dv_acc = rest[2 * nc + 3:2 * nc + 5]
        sems = rest[2 * nc + 5:]
        r, j = pl.program_id(0), pl.program_id(1)
        if comm is not None:
            @pl.when((r == 0) & (j == 0))
            def _():
                comm.start(c_ins, c_outs, sems)

        @pl.when(j == 0)
        def _():
            dk_acc[...] = jnp.zeros_like(dk_acc)
            dv_acc[...] = jnp.zeros_like(dv_acc)

        @pl.when(j > 0)
        def _():
            for acc in (dk_acc, dv_acc):
                acc[0:bq, :] = acc[bq:2 * bq, :]
                acc[bq:2 * bq, :] = acc[2 * bq:, :]
                acc[2 * bq:, :] = jnp.zeros((bq, C_B), F32)

        @pl.when(j < nb)
        def _():
            mask = _band_mask(j, bq, L, False)
            mask2 = jnp.concatenate([mask, mask], axis=0)
            first = lax.broadcasted_iota(jnp.int32, (1, LANES), 1) < HEAD_DIM
            lse_c, delta_c = lc[...], tc[...]
            for p in range(C_B // LANES):
                sl = slice(p * LANES, (p + 1) * LANES)
                qs, dos = _stack_heads(qc[:, sl]), _stack_heads(dc_[:, sl])
                kw, vw = _window((kp, kc, kn), sl), _window((vp, vc, vn), sl)
                lse_s = jnp.concatenate([_head_col(lse_c, 2 * p), _head_col(lse_c, 2 * p + 1)], axis=0)
                delta_s = jnp.concatenate([_head_col(delta_c, 2 * p), _head_col(delta_c, 2 * p + 1)], axis=0)
                s = lax.dot_general(qs, kw, _NT, preferred_element_type=F32)
                pr = jnp.where(mask2, jnp.exp(s - lse_s), 0.0)
                dpr = lax.dot_general(dos, vw, _NT, preferred_element_type=F32)
                ds = (pr * (dpr - delta_s)).astype(BF16)
                dq = jnp.dot(ds, kw, preferred_element_type=F32)
                dq_ref[:, sl] = jnp.where(first, dq[:bq], dq[bq:]).astype(dq_ref.dtype)
                dk_acc[lo:lo + bw, sl] += lax.dot_general(ds, qs, _TN, preferred_element_type=F32)
                dv_acc[lo:lo + bw, sl] += lax.dot_general(pr.astype(BF16), dos, _TN, preferred_element_type=F32)

        dk_ref[...] = dk_acc[0:bq, :].astype(dk_ref.dtype)
        dv_ref[...] = dv_acc[0:bq, :].astype(dv_ref.dtype)
        if comm is not None:
            @pl.when((r == d - 1) & (j == nb))
            def _():
                comm.finish(c_ins, c_outs, sems)

    per = bq // ATTN_HALF
    last = L // ATTN_HALF - 1

    def clamp(j):
        return jnp.minimum(j, nb - 1)

    def specs(width):
        cur = pl.BlockSpec((None, bq, width), lambda r, j: (r, clamp(j), 0))
        prev = pl.BlockSpec((None, ATTN_HALF, width), lambda r, j: (r, jnp.maximum(clamp(j) * per - 1, 0), 0))
        nxt = pl.BlockSpec((None, ATTN_HALF, width), lambda r, j: (r, jnp.minimum((clamp(j) + 1) * per, last), 0))
        return prev, cur, nxt

    wide = specs(C_B)
    stat = specs(LANES)[1]
    lagged = pl.BlockSpec((None, bq, C_B), lambda r, j: (r, jnp.maximum(j - 1, 0), 0))
    shape = jax.ShapeDtypeStruct((d, L, C_B), BF16)
    res = pl.pallas_call(
        body, name=name, grid=(d, nb + 1), in_specs=[wide[1], wide[1], stat, stat, *wide, *wide] + [_ANY] * nc,
        out_specs=[wide[1], lagged, lagged] + [_ANY] * nc,
        out_shape=[shape] * 3 + (comm.out_shape if comm is not None else []),
        scratch_shapes=[pltpu.VMEM((3 * bq, C_B), F32), pltpu.VMEM((3 * bq, C_B), F32)]
        + (comm.sems if comm is not None else []),
        compiler_params=_params("arbitrary", "arbitrary"),
    )(qd, dod, lsed, deltad, kd, kd, kd, vd, vd, vd, *(comm.ins if comm is not None else []))
    return tuple(res[:3]), (res[3:] if comm is not None else None)


def _dilated_spec(tr, d, width):
    return pl.BlockSpec((d, tr // d, width), lambda i: (0, i, 0))


def _perm_matrix(n, d, inverse):
    lb = n // d
    row = lax.broadcasted_iota(jnp.int32, (n, n), 0)
    col = lax.broadcasted_iota(jnp.int32, (n, n), 1)
    source = (row % d) * lb + row // d if inverse else (row % lb) * d + row // lb
    return (col == source).astype(BF16)


def _permute(p, x):
    if x.dtype == BF16:
        return jnp.dot(p, x, preferred_element_type=F32).astype(BF16)
    hi = x.astype(BF16)
    rest = x - hi.astype(F32)
    mid = rest.astype(BF16)
    lo = (rest - mid.astype(F32)).astype(BF16)
    return (jnp.dot(p, hi, preferred_element_type=F32) + jnp.dot(p, mid, preferred_element_type=F32)
            + jnp.dot(p, lo, preferred_element_type=F32))


def _store_dilated(ref, x, d):
    n = x.shape[0]
    if d == 1:
        ref[0] = x
        return
    y = _permute(_perm_matrix(n, d, False), x)
    lb = n // d
    for r in range(d):
        ref[r] = y[r * lb:(r + 1) * lb]


def _load_dilated(ref, d):
    if d == 1:
        return ref[0]
    y = jnp.concatenate([ref[r] for r in range(d)], axis=0)
    return _permute(_perm_matrix(y.shape[0], d, True), y)


def adamw(parts, w, m, v, name):
    n, R, C = parts.shape
    tr = _rows(R, ROW_TILE)

    def body(p_ref, w_ref, m_ref, v_ref, g_ref, d_ref, nm_ref, nv_ref):
        g = p_ref[0].astype(F32)
        for k in range(1, n):
            g = g + p_ref[k].astype(F32)
        mm = ADAM_B1 * m_ref[...] + (1.0 - ADAM_B1) * g
        vv = ADAM_B2 * v_ref[...] + (1.0 - ADAM_B2) * jnp.square(g)
        m_hat = mm / (1.0 - ADAM_B1 ** ADAM_STEP)
        v_hat = vv / (1.0 - ADAM_B2 ** ADAM_STEP)
        g_ref[...] = g
        d_ref[...] = -ADAM_LR * (m_hat / (jnp.sqrt(v_hat) + ADAM_EPS) + ADAM_WD * w_ref[...])
        nm_ref[...] = mm
        nv_ref[...] = vv

    spec = _row_spec(tr, C)
    shape = jax.ShapeDtypeStruct((R, C), F32)
    return pl.pallas_call(
        body, name=name, grid=(R // tr,),
        in_specs=[pl.BlockSpec((n, tr, C), lambda i: (0, i, 0)), spec, spec, spec],
        out_specs=[spec] * 4, out_shape=[shape] * 4, compiler_params=_params("parallel"),
    )(parts, w, m, v)


_ANY = pl.BlockSpec(memory_space=pl.ANY)


def _place():
    return lax.axis_index("x"), lax.axis_index("y"), lax.axis_index("c")


def _index(px, py, pc):
    return 4 * px + 2 * py + pc


class Gather:
    def __init__(self, shards):
        self.ins = list(shards)
        T = self.n = len(shards)
        self.out_shape = [jax.ShapeDtypeStruct((N_DEV, *s.shape), s.dtype) for s in shards]
        self.sems = [pltpu.SemaphoreType.DMA((T, 7)), pltpu.SemaphoreType.DMA((T, 7)), pltpu.SemaphoreType.DMA((T,))]

    def _plan(self, ins, outs, sems):
        send_sems, recv_sems, local_sems = sems
        x, y, c = _place()
        me, sibling = (x, y, c), (x, y, 1 - c)
        chips = [(1 - x, y), (x, 1 - y), (1 - x, 1 - y)]

        def copy(t, k, block, to, src=None):
            rows = outs[t].at[_index(*block)]
            return pltpu.make_async_remote_copy(
                src_ref=rows if src is None else src, dst_ref=rows, send_sem=send_sems.at[t, k],
                recv_sem=recv_sems.at[t, k], device_id=to, device_id_type=MESH)

        mine = [pltpu.make_async_copy(ins[t], outs[t].at[_index(*me)], local_sems.at[t]) for t in range(self.n)]
        first = []
        for t in range(self.n):
            first.append(copy(t, 0, me, sibling, src=ins[t]))
            first += [copy(t, 1 + j, me, (*chip, c), src=ins[t]) for j, chip in enumerate(chips)]
        return copy, mine, first, me, sibling, chips, c

    def start(self, ins, outs, sems):
        _, mine, first, *_ = self._plan(ins, outs, sems)
        for cp in mine + first:
            cp.start()

    def finish(self, ins, outs, sems):
        copy, mine, first, me, sibling, chips, c = self._plan(ins, outs, sems)
        passed = []
        for j, chip in enumerate(chips):
            for t in range(self.n):
                copy(t, 1 + j, (*chip, c), me).wait_recv()
                fwd = copy(t, 4 + j, (*chip, c), sibling)
                fwd.start()
                passed.append(fwd)
        for t in range(self.n):
            copy(t, 0, sibling, me).wait_recv()
            for j, chip in enumerate(chips):
                copy(t, 4 + j, (*chip, 1 - c), me).wait_recv()
        for cp in first + passed:
            cp.wait_send()
        for cp in mine:
            cp.wait()


class Exchange:
    def __init__(self, parts):
        self.ins = list(parts)
        T = self.n = len(parts)
        self.out_shape = [jax.ShapeDtypeStruct(p.shape, p.dtype) for p in parts]
        self.sems = [pltpu.SemaphoreType.DMA((T, 7)), pltpu.SemaphoreType.DMA((T, 7)), pltpu.SemaphoreType.DMA((T,))]

    def _plan(self, ins, outs, sems):
        send_sems, recv_sems, local_sems = sems
        x, y, c = _place()
        me = _index(x, y, c)
        copies = [pltpu.make_async_copy(ins[t].at[me], outs[t].at[me], local_sems.at[t]) for t in range(self.n)]
        for k in range(1, N_DEV):
            peer = ((x + (k >> 2)) % 2, (y + ((k >> 1) & 1)) % 2, (c + (k & 1)) % 2)
            there = _index(*peer)
            for t in range(self.n):
                copies.append(pltpu.make_async_remote_copy(
                    src_ref=ins[t].at[there], dst_ref=outs[t].at[me], send_sem=send_sems.at[t, k - 1],
                    recv_sem=recv_sems.at[t, k - 1], device_id=peer, device_id_type=MESH))
        return copies

    def start(self, ins, outs, sems):
        for cp in self._plan(ins, outs, sems):
            cp.start()

    def finish(self, ins, outs, sems):
        for cp in self._plan(ins, outs, sems):
            cp.wait()


def communicate(comm, name):
    T = comm.n

    def body(*refs):
        ins, outs, sems = refs[:T], refs[T:2 * T], refs[2 * T:]
        comm.start(ins, outs, sems)
        comm.finish(ins, outs, sems)

    return pl.pallas_call(
        body, name=name, in_specs=[_ANY] * T, out_specs=[_ANY] * T, out_shape=comm.out_shape,
        scratch_shapes=comm.sems,
    )(*comm.ins)


def all_gather(shards, name):
    return communicate(Gather(shards), name)


def exchange(parts, name):
    return communicate(Exchange(parts), name)


def _row(v):
    return v.reshape(1, -1)


def _hosted(res):
    return res if isinstance(res, tuple) else (res, None)


def mix_forward(x, h1, h1t, w, p, tables, comm=None, comm_out=None, after_proj=None):
    gm = p["g_mix"]
    proj, got = _hosted(matmul(h1, w["win_t"], "nt", BF16, "proj", tn=1152, comm=comm))
    if after_proj is not None:
        after_proj(got)
    ya = conv_fwd(proj, p["cw"], _row(p["conv_b"]), _row(p["conv_ln_g"]), _row(p["conv_ln_b"]), _row(gm[:C_A]),
                  "conv_fwd")
    dil = rope_fwd(proj, tables, "rope_fwd")
    os_, lses = [], []
    for d, (qd, kd, vd) in zip(DILATIONS, dil):
        o, lse = attn_fwd_pattern(qd, kd, vd, f"attn_fwd_d{d}")
        os_.append(o)
        lses.append(lse)
    yc = pool_fwd(proj, p["pool_w"], _row(p["pool_scale"]), _row(gm[C_A + C_B:]), "pool_fwd")
    y, yt, out, lse = attn_combine(os_, lses, _row(gm[C_A:C_A + C_B]), ya, yc, "attn_combine")
    z, got_out = _hosted(matmul(y, w["wout"], "nn", BF16, "mix_out", tn=1024, comm=comm_out))
    x2, h2, h2t = norm_residual_rms(z, x, _row(p["g_post_mix"]), _row(p["g_pre_ffn"]), "res_mix")
    return x2, h2, h2t, dict(x=x, h1t=h1t, proj=proj, dil=dil, out=out, lse=lse, yt=yt, z=z), got_out


def ffn_forward(x2, h2, h2t, w, p, g_next=None, comm_in=None, comm_out=None, after_in=None):
    gu, got_in = _hosted(matmul(h2, w["wgu_t"], "nt", BF16, "ffn_in", tn=1024, comm=comm_in))
    if after_in is not None:
        after_in(got_in)
    a, at = swiglu_fwd(gu, "swiglu_fwd")
    f, got_out = _hosted(matmul(a, w["wd"], "nn", BF16, "ffn_out", tk=5632, comm=comm_out))
    saved = dict(x2=x2, h2t=h2t, gu=gu, at=at, f=f)
    if g_next is None:
        return norm_residual(f, x2, _row(p["g_post_ffn"]), "res_last"), None, saved, got_in, got_out
    x3, h1, h1t = norm_residual_rms(f, x2, _row(p["g_post_ffn"]), _row(g_next), "res_ffn")
    return x3, (h1, h1t), saved, got_in, got_out


def _to_blocks(g, by_columns):
    if by_columns:
        return g.T.reshape(N_DEV, -1, g.shape[0])
    return g.reshape(N_DEV, -1, g.shape[1])


def _exchange_of(g, by_columns):
    return Exchange([_to_blocks(g, by_columns)]) if g is not None else None


def ffn_backward(dx3, df, s, z, w, p, ride_act=None):
    da, got_act = _hosted(matmul(df, w["wd"], "nt", BF16, "d_ffn_act", tn=1408, comm=ride_act))
    dwd = matmul(s["at"], df, "nn", BF16, "dw_down", tm=1408, tn=1024, tk=2048)
    dgu = swiglu_bwd(s["gu"], da, "swiglu_bwd")
    dh2, got_wd = matmul(dgu, w["wgu_t"], "nn", BF16, "d_ffn_in", tn=1024, tk=2816, comm=_exchange_of(dwd, False))
    dwgu = matmul(s["h2t"], dgu, "nn", BF16, "dw_gate_up", tm=1024, tn=1408, tk=2048)
    dx2, dz, dg_pre_ffn, dg_post_mix = rms_bwd_chain(
        s["x2"], dh2, _row(p["g_pre_ffn"]), dx3, z, _row(p["g_post_mix"]), "rms_bwd_ffn_to_mix")
    return dx2, dz, dict(dwgu=dwgu, parts_wd=got_wd[0], got_act=got_act,
                         g_pre_ffn=dg_pre_ffn[0], g_post_mix=dg_post_mix[0])


def mix_backward(dx2, dz, s, w, p, tables, dwgu, below=None):
    gm = p["g_mix"]
    F = dwgu.shape[1] // 2
    dy = matmul(dz, w["wout"], "nt", BF16, "d_mix", tn=1024)
    dwout = matmul(s["yt"], dz, "nn", BF16, "dw_out", tm=1024, tn=1024, tk=2048)
    dconv, dcw, dcb, dlg, dlb, dgm_a = conv_bwd(
        s["proj"], dy, p["cw"], _row(p["conv_b"]), _row(p["conv_ln_g"]), _row(p["conv_ln_b"]), _row(gm[:C_A]),
        "conv_bwd")
    stats, dgm_b = attn_out_bwd(s["out"], s["lse"], dy, _row(gm[C_A:C_A + C_B]), "attn_out_bwd")
    rides = [_exchange_of(dwgu[:, :F], True), _exchange_of(dwgu[:, F:], True), None]
    dqs, dks, dvs, got = [], [], [], []
    for d, (qd, kd, vd), (dod, lsed, deltad), ride in zip(DILATIONS, s["dil"], stats, rides):
        (dq, dk, dv), parts = attn_bwd_pattern(qd, kd, vd, dod, lsed, deltad, f"attn_bwd_d{d}", comm=ride)
        dqs.append(dq)
        dks.append(dk)
        dvs.append(dv)
        got.append(parts)
    du, dpw, dps, dgm_c = pool_bwd(s["proj"], dy, p["pool_w"], _row(p["pool_scale"]), _row(gm[C_A + C_B:]),
                                   "pool_bwd")
    dproj = rope_bwd(dqs, dks, dvs, tables, dconv, du, "rope_bwd")
    dh1 = matmul(dproj, w["win_t"], "nn", BF16, "d_proj", tn=1024, tk=4608)
    dwin, got_out = matmul(s["h1t"], dproj, "nn", BF16, "dw_in", tm=1024, tn=1152, tk=2048,
                           comm=_exchange_of(dwout, False))
    grads = dict(
        dwin=dwin, parts_out=got_out[0], parts_gate=got[0][0], parts_up=got[1][0],
        conv_w=dcw[:CONV_WIDTH], conv_b=dcb[0], conv_ln_g=dlg[0], conv_ln_b=dlb[0], pool_w=dpw, pool_scale=dps[0],
        g_mix=jnp.concatenate([dgm_a[0], dgm_b[0], dgm_c[0]]))
    if below is None:
        dx, dg_pre_mix = rms_bwd(s["x"], dh1, _row(p["g_pre_mix"]), dx2, F32, "rms_bwd_first")
        df = None
    else:
        dx, df, dg_pre_mix, dg_post_ffn = rms_bwd_chain(
            s["x"], dh1, _row(p["g_pre_mix"]), dx2, below[0], _row(below[1]), "rms_bwd_mix_to_ffn")
        grads["g_post_ffn_below"] = dg_post_ffn[0]
    grads["g_pre_mix"] = dg_pre_mix[0]
    return dx, df, grads


WEIGHTS = ["w_in", "conv_w", "conv_b", "conv_ln_g", "conv_ln_b", "pool_w", "pool_scale", "g_mix", "w_out", "g_pre_mix",
           "g_post_mix", "g_pre_ffn", "g_post_ffn", "w_gate", "w_up", "w_down"]
BIG = ["w_in", "w_out", "w_gate", "w_up", "w_down"]
REPLICATED = ["conv_b", "conv_ln_g", "conv_ln_b", "pool_w", "pool_scale", "g_mix", "g_pre_mix", "g_post_mix",
              "g_pre_ffn", "g_post_ffn"]
PACK_ROWS = 256


def adamw_layer(parts, w, m, v, layer, prev, name):
    n, R, C = parts.shape
    tr = _rows(R, ROW_TILE)

    def body(p_ref, w_ref, m_ref, v_ref, *rest):
        g_ref, d_ref, nm_ref, nv_ref = rest[-4:]
        g = p_ref[0].astype(F32)
        for k in range(1, n):
            g = g + p_ref[k].astype(F32)
        mm = ADAM_B1 * m_ref[...] + (1.0 - ADAM_B1) * g
        vv = ADAM_B2 * v_ref[...] + (1.0 - ADAM_B2) * jnp.square(g)
        m_hat = mm / (1.0 - ADAM_B1 ** ADAM_STEP)
        v_hat = vv / (1.0 - ADAM_B2 ** ADAM_STEP)
        g_ref[...] = g
        d_ref[...] = -ADAM_LR * (m_hat / (jnp.sqrt(v_hat) + ADAM_EPS) + ADAM_WD * w_ref[...])
        nm_ref[...] = mm
        nv_ref[...] = vv

    spec = pl.BlockSpec((None, tr, C), lambda i: (layer, i, 0))
    shape = jax.ShapeDtypeStruct(w.shape, F32)
    prev = list(prev) if prev is not None else []
    return pl.pallas_call(
        body, name=name, grid=(R // tr,),
        in_specs=[pl.BlockSpec((n, tr, C), lambda i: (0, i, 0)), spec, spec, spec] + [_ANY] * len(prev),
        out_specs=[spec] * 4, out_shape=[shape] * 4,
        input_output_aliases={4 + k: k for k in range(len(prev))}, compiler_params=_params("parallel"),
    )(parts, w, m, v, *prev)


def _pack(arrays):
    flat = jnp.concatenate([a.reshape(-1).astype(F32) for a in arrays])
    unit = PACK_ROWS * LANES
    padded = -(-flat.shape[0] // unit) * unit
    return jnp.pad(flat, (0, padded - flat.shape[0])).reshape(-1, LANES)


def _unpack(packed, like):
    flat = packed.reshape(-1)
    out, at = [], 0
    for a in like:
        out.append(flat[at:at + a.size].reshape(a.shape))
        at += a.size
    return out


def kernel(x, w_in, conv_w, conv_b, conv_ln_g, conv_ln_b, pool_w, pool_scale, g_mix, w_out, g_pre_mix, g_post_mix, g_pre_ffn, g_post_ffn, w_gate, w_up, w_down, loss_target, m_w_in, m_conv_w, m_conv_b, m_conv_ln_g, m_conv_ln_b, m_pool_w, m_pool_scale, m_g_mix, m_w_out, m_g_pre_mix, m_g_post_mix, m_g_pre_ffn, m_g_post_ffn, m_w_gate, m_w_up, m_w_down, v_w_in, v_conv_w, v_conv_b, v_conv_ln_g, v_conv_ln_b, v_pool_w, v_pool_scale, v_g_mix, v_w_out, v_g_pre_mix, v_g_post_mix, v_g_pre_ffn, v_g_post_ffn, v_w_gate, v_w_up, v_w_down):
    w = dict(w_in=w_in, conv_w=conv_w, conv_b=conv_b, conv_ln_g=conv_ln_g, conv_ln_b=conv_ln_b, pool_w=pool_w,
             pool_scale=pool_scale, g_mix=g_mix, w_out=w_out, g_pre_mix=g_pre_mix, g_post_mix=g_post_mix,
             g_pre_ffn=g_pre_ffn, g_post_ffn=g_post_ffn, w_gate=w_gate, w_up=w_up, w_down=w_down)
    m = dict(w_in=m_w_in, conv_w=m_conv_w, conv_b=m_conv_b, conv_ln_g=m_conv_ln_g, conv_ln_b=m_conv_ln_b,
             pool_w=m_pool_w, pool_scale=m_pool_scale, g_mix=m_g_mix, w_out=m_w_out, g_pre_mix=m_g_pre_mix,
             g_post_mix=m_g_post_mix, g_pre_ffn=m_g_pre_ffn, g_post_ffn=m_g_post_ffn, w_gate=m_w_gate, w_up=m_w_up,
             w_down=m_w_down)
    v = dict(w_in=v_w_in, conv_w=v_conv_w, conv_b=v_conv_b, conv_ln_g=v_conv_ln_g, conv_ln_b=v_conv_ln_b,
             pool_w=v_pool_w, pool_scale=v_pool_scale, g_mix=v_g_mix, w_out=v_w_out, g_pre_mix=v_g_pre_mix,
             g_post_mix=v_g_post_mix, g_pre_ffn=v_g_pre_ffn, g_post_ffn=v_g_post_ffn, w_gate=v_w_gate, w_up=v_w_up,
             w_down=v_w_down)
    depth = w_in.shape[0]
    xs, target = x[0], loss_target[0]
    S, D = xs.shape
    tables = rope_tables(S)

    cw_shard = jnp.pad(conv_w, ((0, 0), (0, 1), (0, 0)))
    cw_all = all_gather([cw_shard.reshape(-1, LANES)], "gather_conv_w")[0]
    cw_full = cw_all.reshape(N_DEV, depth, 32, -1).transpose(1, 2, 0, 3).reshape(depth, 32, C_A)
    small = []
    for l in range(depth):
        small.append({n: w[n][l] for n in REPLICATED})
        small[l]["cw"] = cw_full[l]

    def shard(n, l):
        return (w[n][l].T if n in ("w_in", "w_gate", "w_up") else w[n][l]).astype(BF16)

    def joined(blocks):
        return blocks.reshape(-1, blocks.shape[2])

    def gather_of(*names_layers):
        return Gather([shard(n, l) for n, l in names_layers]) if names_layers else None

    def gate_up(gate_blocks, up_blocks):
        return jnp.concatenate([joined(gate_blocks), joined(up_blocks)], axis=0)

    w_mix = [dict() for _ in range(depth)]
    w_ffn = [dict() for _ in range(depth)]
    saved_mix, saved_ffn = [None] * depth, [None] * depth
    w_mix[0]["win_t"] = joined(communicate(gather_of(("w_in", 0)), "gather_first_weights")[0])
    held = {}
    h = xs
    normed = rms_fwd(xs, _row(small[0]["g_pre_mix"]), "rms_first")
    for l in range(depth):
        more = l + 1 < depth
        first = l == 0

        def after_proj(got, l=l, first=first):
            w_mix[l]["wout"] = joined(got[0])
            if first:
                held["gate"] = got[1]

        x2, h2, h2t, saved_mix[l], got_mix = mix_forward(
            h, *normed, w_mix[l], small[l], tables,
            gather_of(("w_out", l), ("w_gate", l)) if first else gather_of(("w_out", l)),
            gather_of(("w_up", l)) if first else None, after_proj)
        if first:
            w_ffn[0]["wgu_t"] = gate_up(held["gate"], got_mix[0])

        def after_in(got, first=first):
            if first:
                w_ffn[0]["wd"] = joined(got[0])

        rides = ([("w_down", 0)] if first else []) + ([("w_gate", l + 1), ("w_up", l + 1)] if more else [])
        h, normed, saved_ffn[l], got_in, got_out = ffn_forward(
            x2, h2, h2t, w_ffn[l], small[l], small[l + 1]["g_pre_mix"] if more else None,
            gather_of(*rides), gather_of(("w_down", l + 1), ("w_in", l + 1)) if more else None, after_in)
        if more:
            w_ffn[l + 1]["wgu_t"] = gate_up(*got_in[-2:])
            w_ffn[l + 1]["wd"] = joined(got_out[0])
            w_mix[l + 1]["win_t"] = joined(got_out[1])

    dh, sq = loss_head(h, target, "loss_head")
    loss = lax.psum(0.5 * jnp.sum(sq) / D, ("x", "y", "c"))

    big_out = {n: None for n in BIG}
    by_columns = ("w_in", "w_gate", "w_up")
    state = {n: tuple(jnp.swapaxes(t[n], 1, 2) if n in by_columns else t[n] for t in (w, m, v)) for n in BIG}

    def update(n, l, parts):
        big_out[n] = adamw_layer(parts, *state[n], l, big_out[n], f"adamw_{n}_layer{l}")

    small_grads = [None] * depth
    carried = None
    top = depth - 1
    df, dg_post_ffn = rms_bwd(saved_ffn[top]["f"], dh, _row(small[top]["g_post_ffn"]), None, BF16, "rms_bwd_last")
    post_ffn = {top: dg_post_ffn[0]}
    for l in reversed(range(depth)):
        dx2, dz, gf = ffn_backward(dh, df, saved_ffn[l], saved_mix[l]["z"], w_ffn[l], small[l],
                                   _exchange_of(carried, True))
        if carried is not None:
            update("w_in", l + 1, gf["got_act"][0])
        update("w_down", l, gf["parts_wd"])
        below = (saved_ffn[l - 1]["f"], small[l - 1]["g_post_ffn"]) if l > 0 else None
        dh, df, gm_ = mix_backward(dx2, dz, saved_mix[l], w_mix[l], small[l], tables, gf["dwgu"], below)
        if l > 0:
            post_ffn[l - 1] = gm_["g_post_ffn_below"]
        update("w_gate", l, gm_["parts_gate"])
        update("w_up", l, gm_["parts_up"])
        update("w_out", l, gm_["parts_out"])
        carried = gm_["dwin"]
        small_grads[l] = {**gf, **gm_, "g_post_ffn": post_ffn[l]}
    update("w_in", 0, exchange([_to_blocks(carried, True)], "exchange_last_grads")[0])

    names = REPLICATED + ["conv_w"]
    stacked = [jnp.stack([small_grads[l][n] for l in range(depth)]) for n in names]
    partial = all_gather([_pack(stacked)], "gather_small_grads")[0]
    zeros = jnp.zeros_like(stacked[-1])
    packed = adamw(partial, _pack([w[n] for n in REPLICATED] + [zeros]), _pack([m[n] for n in REPLICATED] + [zeros]),
                   _pack([v[n] for n in REPLICATED] + [zeros + 1.0]), "adamw_replicated")
    small_out = [_unpack(o, stacked) for o in packed]
    out = {n: tuple(o[i] for o in small_out) for i, n in enumerate(REPLICATED)}
    width = conv_w.shape[2]
    g_cw = lax.dynamic_slice_in_dim(small_out[0][-1], _index(*_place()) * width, width, axis=2)
    cw_res = adamw(g_cw.reshape(1, -1, LANES), conv_w.reshape(-1, LANES), m["conv_w"].reshape(-1, LANES),
                   v["conv_w"].reshape(-1, LANES), "adamw_conv_w")
    out["conv_w"] = tuple(o.reshape(conv_w.shape) for o in cw_res)
    for n in BIG:
        out[n] = tuple(jnp.swapaxes(o, 1, 2) if n in by_columns else o for o in big_out[n])
    results = [loss, dh[None]]
    for k in range(4):
        results += [out[n][k] for n in WEIGHTS]
    return tuple(results)
```

```python
import functools
import math

import jax
import jax.numpy as jnp
from jax import lax
from jax.experimental import pallas as pl
from jax.experimental.pallas import tpu as pltpu

F32 = jnp.float32
BF16 = jnp.bfloat16

N_DEV = 8
DEPTH = 4
EPS = 1e-6
NEG = -1e30

C_A = 512
N_HEADS = 16
HEAD_DIM = 64
C_B = N_HEADS * HEAD_DIM
C_C = 512
POOL_WINDOWS = (2, 4, 8, 16)
C_G = C_C // len(POOL_WINDOWS)
CONV_WIDTH = 31
CONV_HALF = CONV_WIDTH // 2
DILATIONS = (1, 4, 16)
ATTN_HALF = 64
ROT_DIM = HEAD_DIM // 4
ROPE_THETA = 500000.0

ADAM_LR = 0.001
ADAM_B1 = 0.9
ADAM_B2 = 0.999
ADAM_EPS = 1e-08
ADAM_WD = 0.01
ADAM_STEP = 10

LANES = 128
SUBLANES = 8
VMEM_LIMIT = 56 * 1024 * 1024
ROW_TILE = 256
SEQ_TILE = 256
ATTN_BLOCK = 128
MESH = pl.DeviceIdType.MESH


def _params(*sem):
    return pltpu.CompilerParams(dimension_semantics=sem, vmem_limit_bytes=VMEM_LIMIT)


def _tile(n, target):
    if n <= target:
        return n
    t = (target // LANES) * LANES
    while t >= LANES:
        if n % t == 0:
            return t
        t -= LANES
    return n


def _rows(n, target):
    t = min(n, target)
    while n % t:
        t //= 2
    return t


def matmul(a, b, mode, out_dtype, name, tm=1024, tn=512, tk=2048, comm=None):
    if mode == "nn":
        (M, K), (_, N) = a.shape, b.shape
    elif mode == "nt":
        (M, K), (N, _) = a.shape, b.shape
    else:
        (K, M), (_, N) = a.shape, b.shape
    tm, tn, tk = _tile(M, tm), _tile(N, tn), _tile(K, tk)
    nm, nn, nk = M // tm, N // tn, K // tk
    dims = {"nn": (((1,), (0,)), ((), ())), "nt": (((1,), (1,)), ((), ())), "tn": (((0,), (0,)), ((), ()))}[mode]
    nc = comm.n if comm is not None else 0

    def body(*refs):
        a_ref, b_ref = refs[:2]
        c_ins, o_ref, c_outs = refs[2:2 + nc], refs[2 + nc], refs[3 + nc:3 + 2 * nc]
        scratch = refs[3 + 2 * nc:]
        acc, sems = (scratch[:1], scratch[1:]) if nk > 1 else ((), scratch)
        i, j, k = pl.program_id(0), pl.program_id(1), pl.program_id(2)
        if comm is not None:
            @pl.when((i == 0) & (j == 0) & (k == 0))
            def _():
                comm.start(c_ins, c_outs, sems)

        p = lax.dot_general(a_ref[...], b_ref[...], dims, preferred_element_type=F32)
        if nk == 1:
            o_ref[...] = p.astype(o_ref.dtype)
        else:
            acc_ref, = acc

            @pl.when(k == 0)
            def _():
                acc_ref[...] = p

            @pl.when(k > 0)
            def _():
                acc_ref[...] += p

            @pl.when(k == nk - 1)
            def _():
                o_ref[...] = acc_ref[...].astype(o_ref.dtype)

        if comm is not None:
            @pl.when((i == nm - 1) & (j == nn - 1) & (k == nk - 1))
            def _():
                comm.finish(c_ins, c_outs, sems)

    if mode == "nn":
        a_spec = pl.BlockSpec((tm, tk), lambda i, j, k: (i, k))
        b_spec = pl.BlockSpec((tk, tn), lambda i, j, k: (k, j))
    elif mode == "nt":
        a_spec = pl.BlockSpec((tm, tk), lambda i, j, k: (i, k))
        b_spec = pl.BlockSpec((tn, tk), lambda i, j, k: (j, k))
    else:
        a_spec = pl.BlockSpec((tk, tm), lambda i, j, k: (k, i))
        b_spec = pl.BlockSpec((tk, tn), lambda i, j, k: (k, j))
    o_spec = pl.BlockSpec((tm, tn), lambda i, j, k: (i, j))
    o_shape = jax.ShapeDtypeStruct((M, N), out_dtype)
    acc_shape = [pltpu.VMEM((tm, tn), F32)] if nk > 1 else []
    if comm is None:
        return pl.pallas_call(
            body, name=name, grid=(nm, nn, nk), in_specs=[a_spec, b_spec], out_specs=o_spec, out_shape=o_shape,
            scratch_shapes=acc_shape, compiler_params=_params("parallel", "parallel", "arbitrary"),
        )(a, b)
    res = pl.pallas_call(
        body, name=name, grid=(nm, nn, nk), in_specs=[a_spec, b_spec] + [_ANY] * nc,
        out_specs=[o_spec] + [_ANY] * nc, out_shape=[o_shape] + comm.out_shape,
        scratch_shapes=acc_shape + comm.sems, compiler_params=_params("arbitrary", "arbitrary", "arbitrary"),
    )(a, b, *comm.ins)
    return res[0], res[1:]


def _rms(t):
    return t * lax.rsqrt(jnp.mean(t * t, axis=-1, keepdims=True) + EPS)


def _rms_bwd(t, dn):
    r = lax.rsqrt(jnp.mean(t * t, axis=-1, keepdims=True) + EPS)
    n = t * r
    return r * (dn - n * jnp.mean(dn * n, axis=-1, keepdims=True)), n


def _row_spec(tr, d):
    return pl.BlockSpec((tr, d), lambda i: (i, 0))


def _vec_spec(d):
    return pl.BlockSpec((1, d), lambda i: (0, 0))


def _col_spec(d, tr):
    return pl.BlockSpec((d, tr), lambda i: (0, i))


def rms_fwd(x, g, name):
    S, D = x.shape
    tr = _rows(S, ROW_TILE)

    def body(x_ref, g_ref, o_ref, ot_ref):
        h = _rms(x_ref[...].astype(F32)) * g_ref[...]
        o_ref[...] = h.astype(o_ref.dtype)
        ot_ref[...] = h.T.astype(ot_ref.dtype)

    return pl.pallas_call(
        body, name=name, grid=(S // tr,), in_specs=[_row_spec(tr, D), _vec_spec(D)],
        out_specs=[_row_spec(tr, D), _col_spec(D, tr)],
        out_shape=[jax.ShapeDtypeStruct((S, D), BF16), jax.ShapeDtypeStruct((D, S), BF16)],
        compiler_params=_params("parallel"),
    )(x, g)


def norm_residual(z, x, g, name):
    S, D = x.shape
    tr = _rows(S, ROW_TILE)

    def body(z_ref, x_ref, g_ref, o_ref):
        o_ref[...] = x_ref[...] + _rms(z_ref[...].astype(F32)) * g_ref[...]

    return pl.pallas_call(
        body, name=name, grid=(S // tr,), in_specs=[_row_spec(tr, D), _row_spec(tr, D), _vec_spec(D)],
        out_specs=_row_spec(tr, D), out_shape=jax.ShapeDtypeStruct((S, D), F32), compiler_params=_params("parallel"),
    )(z, x, g)


def norm_residual_rms(z, x, g, g_next, name):
    S, D = x.shape
    tr = _rows(S, ROW_TILE)

    def body(z_ref, x_ref, g_ref, gn_ref, o_ref, h_ref, ht_ref):
        x_new = x_ref[...] + _rms(z_ref[...].astype(F32)) * g_ref[...]
        o_ref[...] = x_new
        h = _rms(x_new) * gn_ref[...]
        h_ref[...] = h.astype(h_ref.dtype)
        ht_ref[...] = h.T.astype(ht_ref.dtype)

    return pl.pallas_call(
        body, name=name, grid=(S // tr,),
        in_specs=[_row_spec(tr, D), _row_spec(tr, D), _vec_spec(D), _vec_spec(D)],
        out_specs=[_row_spec(tr, D), _row_spec(tr, D), _col_spec(D, tr)],
        out_shape=[jax.ShapeDtypeStruct((S, D), F32), jax.ShapeDtypeStruct((S, D), BF16),
                   jax.ShapeDtypeStruct((D, S), BF16)],
        compiler_params=_params("parallel"),
    )(z, x, g, g_next)


def rms_bwd_chain(t1, dy1, g1, res, t2, g2, name):
    S, D = t1.shape
    tr = _rows(S, ROW_TILE)

    def body(t1_ref, dy1_ref, g1_ref, res_ref, t2_ref, g2_ref, d1_ref, d2_ref, dg1_ref, dg2_ref):
        dy1v = dy1_ref[...].astype(F32)
        dt1, n1 = _rms_bwd(t1_ref[...].astype(F32), dy1v * g1_ref[...])
        d1 = dt1 + res_ref[...]
        d1_ref[...] = d1
        dt2, n2 = _rms_bwd(t2_ref[...].astype(F32), d1 * g2_ref[...])
        d2_ref[...] = dt2.astype(d2_ref.dtype)

        @pl.when(pl.program_id(0) == 0)
        def _():
            dg1_ref[...] = jnp.zeros_like(dg1_ref)
            dg2_ref[...] = jnp.zeros_like(dg2_ref)

        dg1_ref[...] += jnp.sum(dy1v * n1, axis=0, keepdims=True)
        dg2_ref[...] += jnp.sum(d1 * n2, axis=0, keepdims=True)

    row, vec = _row_spec(tr, D), _vec_spec(D)
    return pl.pallas_call(
        body, name=name, grid=(S // tr,), in_specs=[row, row, vec, row, row, vec], out_specs=[row, row, vec, vec],
        out_shape=[jax.ShapeDtypeStruct((S, D), F32), jax.ShapeDtypeStruct((S, D), BF16),
                   jax.ShapeDtypeStruct((1, D), F32), jax.ShapeDtypeStruct((1, D), F32)],
        compiler_params=_params("arbitrary"),
    )(t1, dy1, g1, res, t2, g2)


def rms_bwd(t, dy, g, res, out_dtype, name):
    S, D = t.shape
    tr = _rows(S, ROW_TILE)
    has_res = res is not None

    def body(t_ref, dy_ref, g_ref, *rest):
        if has_res:
            res_ref, dt_ref, dg_ref = rest
        else:
            dt_ref, dg_ref = rest
        dyv = dy_ref[...].astype(F32)
        dt, n = _rms_bwd(t_ref[...].astype(F32), dyv * g_ref[...])
        if has_res:
            dt = dt + res_ref[...]
        dt_ref[...] = dt.astype(dt_ref.dtype)

        @pl.when(pl.program_id(0) == 0)
        def _():
            dg_ref[...] = jnp.zeros_like(dg_ref)

        dg_ref[...] += jnp.sum(dyv * n, axis=0, keepdims=True)

    ins = [t, dy, g] + ([res] if has_res else [])
    specs = [_row_spec(tr, D), _row_spec(tr, D), _vec_spec(D)] + ([_row_spec(tr, D)] if has_res else [])
    return pl.pallas_call(
        body, name=name, grid=(S // tr,), in_specs=specs, out_specs=[_row_spec(tr, D), _vec_spec(D)],
        out_shape=[jax.ShapeDtypeStruct((S, D), out_dtype), jax.ShapeDtypeStruct((1, D), F32)],
        compiler_params=_params("arbitrary"),
    )(*ins)


def swiglu_fwd(gu, name):
    S, F2 = gu.shape
    F = F2 // 2
    tr = _rows(S, ROW_TILE)

    def body(g_ref, u_ref, o_ref, ot_ref):
        g = g_ref[...].astype(F32)
        a = g * jax.nn.sigmoid(g) * u_ref[...].astype(F32)
        o_ref[...] = a.astype(o_ref.dtype)
        ot_ref[...] = a.T.astype(ot_ref.dtype)

    return pl.pallas_call(
        body, name=name, grid=(S // tr,),
        in_specs=[pl.BlockSpec((tr, F), lambda i: (i, 0)), pl.BlockSpec((tr, F), lambda i: (i, 1))],
        out_specs=[_row_spec(tr, F), _col_spec(F, tr)],
        out_shape=[jax.ShapeDtypeStruct((S, F), BF16), jax.ShapeDtypeStruct((F, S), BF16)],
        compiler_params=_params("parallel"),
    )(gu, gu)


def swiglu_bwd(gu, da, name):
    S, F2 = gu.shape
    F = F2 // 2
    tr = _rows(S, ROW_TILE)

    def body(g_ref, u_ref, da_ref, o_ref):
        g = g_ref[...].astype(F32)
        u = u_ref[...].astype(F32)
        dav = da_ref[...].astype(F32)
        sig = jax.nn.sigmoid(g)
        o_ref[:, :F] = (dav * u * (sig * (1.0 + g * (1.0 - sig)))).astype(o_ref.dtype)
        o_ref[:, F:] = (dav * (g * sig)).astype(o_ref.dtype)

    return pl.pallas_call(
        body, name=name, grid=(S // tr,),
        in_specs=[pl.BlockSpec((tr, F), lambda i: (i, 0)), pl.BlockSpec((tr, F), lambda i: (i, 1)), _row_spec(tr, F)],
        out_specs=_row_spec(tr, F2), out_shape=jax.ShapeDtypeStruct((S, F2), BF16), compiler_params=_params("parallel"),
    )(gu, gu, da)


def loss_head(y, target, name):
    S, D = y.shape
    tr = _rows(S, ROW_TILE)

    def body(y_ref, t_ref, dy_ref, sq_ref):
        e = y_ref[...] - t_ref[...]
        dy_ref[...] = e * (1.0 / D)

        @pl.when(pl.program_id(0) == 0)
        def _():
            sq_ref[...] = jnp.zeros_like(sq_ref)

        sq_ref[...] += jnp.sum(e * e, axis=0, keepdims=True)

    return pl.pallas_call(
        body, name=name, grid=(S // tr,), in_specs=[_row_spec(tr, D), _row_spec(tr, D)],
        out_specs=[_row_spec(tr, D), _vec_spec(D)],
        out_shape=[jax.ShapeDtypeStruct((S, D), F32), jax.ShapeDtypeStruct((1, D), F32)],
        compiler_params=_params("arbitrary"),
    )(y, target)


def _halo_specs(bs, halo, width, col, n_rows):
    per = bs // halo
    last = n_rows // halo - 1
    cur = pl.BlockSpec((bs, width), lambda i: (i, col))
    prev = pl.BlockSpec((halo, width), lambda i: (jnp.maximum(i * per - 1, 0), col))
    nxt = pl.BlockSpec((halo, width), lambda i: (jnp.minimum((i + 1) * per, last), col))
    return prev, cur, nxt


TAP_CHUNK = 32


def _build_phases(ref, phases_ref):
    total = ref.shape[0] - SUBLANES
    for b in range(SUBLANES):
        phases_ref[b, 0:total, :] = ref[pl.ds(b, total), :]


def _tap_rows(phases_ref, off, n):
    b = off % SUBLANES
    return phases_ref[b, off - b:off - b + n, :]


def _conv_taps(phases_ref, offsets, n, weights_ref, init, out_ref):
    for r0 in range(0, n, TAP_CHUNK):
        rows = min(TAP_CHUNK, n - r0)
        acc = jnp.zeros((rows, C_A), F32) + init
        for t, off in offsets.items():
            acc = acc + weights_ref[t:t + 1, :] * _tap_rows(phases_ref, off + r0, rows)
        out_ref[r0:r0 + rows, :] = acc


def _glu(a, g):
    return a.astype(F32) * jax.nn.sigmoid(g.astype(F32))


def _layernorm_silu(c, lg, lb):
    mu = jnp.mean(c, axis=-1, keepdims=True)
    cc = c - mu
    rstd = lax.rsqrt(jnp.mean(cc * cc, axis=-1, keepdims=True) + EPS)
    xh = cc * rstd
    ln = xh * lg + lb
    sig = jax.nn.sigmoid(ln)
    return xh, rstd, ln, sig


def conv_fwd(proj, cw, cb, lg, lb, gm, name):
    S = proj.shape[0]
    bs = _rows(S, SEQ_TILE)
    nb = S // bs
    H = 16

    def body(ap, ac, an, gp, gc, gn, cw_ref, cb_ref, lg_ref, lb_ref, gm_ref, y_ref, win_ref, phases_ref, c_ref):
        i = pl.program_id(0)
        win_ref[0:H, :] = jnp.where(i > 0, _glu(ap[...], gp[...]), 0.0)
        win_ref[H:H + bs, :] = _glu(ac[...], gc[...])
        win_ref[H + bs:2 * H + bs, :] = jnp.where(i < nb - 1, _glu(an[...], gn[...]), 0.0)
        win_ref[bs + 2 * H:, :] = jnp.zeros((SUBLANES, C_A), F32)
        _build_phases(win_ref, phases_ref)
        _conv_taps(phases_ref, {t: H - CONV_HALF + t for t in range(CONV_WIDTH)}, bs, cw_ref, cb_ref[...], c_ref)
        _, _, ln, sig = _layernorm_silu(c_ref[...], lg_ref[...], lb_ref[...])
        y_ref[...] = (_rms(ln * sig) * gm_ref[...]).astype(y_ref.dtype)

    a_specs = _halo_specs(bs, H, C_A, 0, S)
    g_specs = _halo_specs(bs, H, C_A, 1, S)
    vec = _vec_spec(C_A)
    return pl.pallas_call(
        body, name=name, grid=(nb,),
        in_specs=[*a_specs, *g_specs, pl.BlockSpec((32, C_A), lambda i: (0, 0)), vec, vec, vec, vec],
        out_specs=_row_spec(bs, C_A), out_shape=jax.ShapeDtypeStruct((S, C_A), BF16),
        scratch_shapes=[pltpu.VMEM((bs + 2 * H + SUBLANES, C_A), F32), pltpu.VMEM((SUBLANES, bs + 2 * H, C_A), F32),
                        pltpu.VMEM((bs, C_A), F32)],
        compiler_params=_params("parallel"),
    )(proj, proj, proj, proj, proj, proj, cw, cb, lg, lb, gm)


def conv_bwd(proj, dy, cw, cb, lg, lb, gm, name):
    S = proj.shape[0]
    bs = _rows(S, SEQ_TILE)
    nb = S // bs
    H = 32
    HC = 16
    bc = bs + 2 * HC

    def body(ap, ac, an, gp, gc, gn, dp, dc_, dn, cw_ref, cb_ref, lg_ref, lb_ref, gm_ref,
             dproj_ref, dcw_ref, dcb_ref, dlg_ref, dlb_ref, dgm_ref, win_ref, dcs_ref, phases_ref, c_ref):
        i = pl.program_id(0)
        win_ref[0:H, :] = jnp.where(i > 0, _glu(ap[...], gp[...]), 0.0)
        win_ref[H:H + bs, :] = _glu(ac[...], gc[...])
        win_ref[H + bs:2 * H + bs, :] = jnp.where(i < nb - 1, _glu(an[...], gn[...]), 0.0)
        win_ref[bs + 2 * H:, :] = jnp.zeros((SUBLANES, C_A), F32)
        _build_phases(win_ref, phases_ref)
        _conv_taps(phases_ref, {t: H - HC - CONV_HALF + t for t in range(CONV_WIDTH)}, bc, cw_ref, cb_ref[...], c_ref)
        xh, rstd, ln, sig = _layernorm_silu(c_ref[...], lg_ref[...], lb_ref[...])
        ya = ln * sig
        dyv = jnp.concatenate([dp[...], dc_[...], dn[...]], axis=0).astype(F32)
        dya, n = _rms_bwd(ya, dyv * gm_ref[...])
        dln = dya * (sig * (1.0 + ln * (1.0 - sig)))
        dxh = dln * lg_ref[...]
        dcv = rstd * (dxh - jnp.mean(dxh, axis=-1, keepdims=True) - xh * jnp.mean(dxh * xh, axis=-1, keepdims=True))
        pos = i * bs - HC + lax.broadcasted_iota(jnp.int32, (bc, 1), 0)
        dcv = jnp.where((pos >= 0) & (pos < S), dcv, 0.0)
        dcs_ref[0:bc, :] = dcv
        dcs_ref[bc:, :] = jnp.zeros((SUBLANES, C_A), F32)

        @pl.when(i == 0)
        def _():
            for r in (dcw_ref, dcb_ref, dlg_ref, dlb_ref, dgm_ref):
                r[...] = jnp.zeros_like(r)

        mid = slice(HC, HC + bs)
        dcb_ref[...] += jnp.sum(dcv[mid], axis=0, keepdims=True)
        dlg_ref[...] += jnp.sum((dln * xh)[mid], axis=0, keepdims=True)
        dlb_ref[...] += jnp.sum(dln[mid], axis=0, keepdims=True)
        dgm_ref[...] += jnp.sum((dyv * n)[mid], axis=0, keepdims=True)
        dcm = dcv[mid]
        for t in range(CONV_WIDTH):
            dcw_ref[t:t + 1, :] += jnp.sum(dcm * _tap_rows(phases_ref, H - CONV_HALF + t, bs), axis=0, keepdims=True)
        _build_phases(dcs_ref, phases_ref)
        _conv_taps(phases_ref, {t: HC + CONV_HALF - t for t in range(CONV_WIDTH)}, bs, cw_ref, 0.0, c_ref)
        dh = c_ref[0:bs, :]
        a = ac[...].astype(F32)
        sg = jax.nn.sigmoid(gc[...].astype(F32))
        dproj_ref[:, :C_A] = (dh * sg).astype(dproj_ref.dtype)
        dproj_ref[:, C_A:] = (dh * a * sg * (1.0 - sg)).astype(dproj_ref.dtype)

    a_specs = _halo_specs(bs, H, C_A, 0, S)
    g_specs = _halo_specs(bs, H, C_A, 1, S)
    d_specs = _halo_specs(bs, HC, C_A, 0, S)
    vec = _vec_spec(C_A)
    full = pl.BlockSpec((32, C_A), lambda i: (0, 0))
    vshape = jax.ShapeDtypeStruct((1, C_A), F32)
    return pl.pallas_call(
        body, name=name, grid=(nb,),
        in_specs=[*a_specs, *g_specs, *d_specs, full, vec, vec, vec, vec],
        out_specs=[_row_spec(bs, 2 * C_A), full, vec, vec, vec, vec],
        out_shape=[jax.ShapeDtypeStruct((S, 2 * C_A), BF16), jax.ShapeDtypeStruct((32, C_A), F32),
                   vshape, vshape, vshape, vshape],
        scratch_shapes=[pltpu.VMEM((bs + 2 * H + SUBLANES, C_A), F32), pltpu.VMEM((bc + SUBLANES, C_A), F32),
                        pltpu.VMEM((SUBLANES, bs + 2 * H, C_A), F32), pltpu.VMEM((bc, C_A), F32)],
        compiler_params=_params("arbitrary"),
    )(proj, proj, proj, proj, proj, proj, dy, dy, dy, cw, cb, lg, lb, gm)


POOL_HALO = 16


def _shift(x, k):
    n = x.shape[0]
    return pltpu.roll(x, (-k) % n, axis=0)


def _pool_means(u, pos, S):
    w2 = _shift(u, -1) + u
    w4 = _shift(w2, -1) + _shift(w2, 1)
    w8 = _shift(w4, -2) + _shift(w4, 2)
    w16 = _shift(w8, -4) + _shift(w8, 4)
    sums = (w2, w4, w8, w16)
    lane = lax.broadcasted_iota(jnp.int32, (1, C_C), 1)
    total = jnp.zeros_like(u)
    inv = jnp.zeros_like(u)
    for gi, win in enumerate(POOL_WINDOWS):
        cnt = jnp.minimum(pos + (win - win // 2), S) - jnp.maximum(pos - win // 2, 0)
        icnt = 1.0 / jnp.maximum(cnt, 1).astype(F32)
        sel = (lane >= gi * C_G) & (lane < (gi + 1) * C_G)
        total = jnp.where(sel, sums[gi], total)
        inv = jnp.where(sel, icnt, inv)
    return total * inv - u, inv


def _pool_adjoint(e):
    v2 = e + _shift(e, 1)
    v4 = _shift(v2, -1) + _shift(v2, 1)
    v8 = _shift(v4, -2) + _shift(v4, 2)
    v16 = _shift(v8, -4) + _shift(v8, 4)
    sums = (v2, v4, v8, v16)
    lane = lax.broadcasted_iota(jnp.int32, (1, C_C), 1)
    out = jnp.zeros_like(e)
    for gi in range(len(POOL_WINDOWS)):
        sel = (lane >= gi * C_G) & (lane < (gi + 1) * C_G)
        out = jnp.where(sel, sums[gi], out)
    return out


def _pool_window(up, uc, un, i, nb, bs, S):
    H = POOL_HALO
    u = jnp.concatenate([jnp.where(i > 0, up[...].astype(F32), 0.0), uc[...].astype(F32),
                         jnp.where(i < nb - 1, un[...].astype(F32), 0.0)], axis=0)
    pos = i * bs - H + lax.broadcasted_iota(jnp.int32, (bs + 2 * H, 1), 0)
    return u, pos


def _pool_mix(pooled, pw_ref):
    outs = []
    for gi in range(len(POOL_WINDOWS)):
        outs.append(jnp.dot(pooled[:, gi * C_G:(gi + 1) * C_G].astype(BF16), pw_ref[gi].astype(BF16),
                            preferred_element_type=F32))
    return jnp.concatenate(outs, axis=1)


def pool_fwd(proj, pw, ps, gm, name):
    S, width = proj.shape
    col = width // C_C - 1
    bs = _rows(S, SEQ_TILE)
    nb = S // bs
    H = POOL_HALO

    def body(up, uc, un, pw_ref, ps_ref, gm_ref, y_ref):
        i = pl.program_id(0)
        u, pos = _pool_window(up, uc, un, i, nb, bs, S)
        pooled, _ = _pool_means(u, pos, S)
        mixed = _pool_mix(pooled[H:H + bs], pw_ref)
        y_ref[...] = (_rms(mixed * ps_ref[...]) * gm_ref[...]).astype(y_ref.dtype)

    vec = _vec_spec(C_C)
    return pl.pallas_call(
        body, name=name, grid=(nb,),
        in_specs=[*_halo_specs(bs, H, C_C, col, S), pl.BlockSpec((4, C_G, C_G), lambda i: (0, 0, 0)), vec, vec],
        out_specs=_row_spec(bs, C_C), out_shape=jax.ShapeDtypeStruct((S, C_C), BF16),
        compiler_params=_params("parallel"),
    )(proj, proj, proj, pw, ps, gm)


def pool_bwd(proj, dy, pw, ps, gm, name):
    S, width = proj.shape
    col = width // C_C - 1
    dcol = dy.shape[1] // C_C - 1
    bs = _rows(S, SEQ_TILE)
    nb = S // bs
    H = POOL_HALO
    W = bs + 2 * H

    def body(up, uc, un, dp, dc_, dn, pw_ref, ps_ref, gm_ref, du_ref, dpw_ref, dps_ref, dgm_ref):
        i = pl.program_id(0)
        u, pos = _pool_window(up, uc, un, i, nb, bs, S)
        pooled, inv = _pool_means(u, pos, S)
        mixed = _pool_mix(pooled, pw_ref)
        dyv = jnp.concatenate([dp[...], dc_[...], dn[...]], axis=0).astype(F32)
        dyc, n = _rms_bwd(mixed * ps_ref[...], dyv * gm_ref[...])
        dmixed = dyc * ps_ref[...]
        dmb = dmixed.astype(BF16)
        dpooled = jnp.concatenate(
            [lax.dot_general(dmb[:, gi * C_G:(gi + 1) * C_G], pw_ref[gi].astype(BF16), (((1,), (1,)), ((), ())),
                             preferred_element_type=F32) for gi in range(len(POOL_WINDOWS))], axis=1)
        dpooled = jnp.where((pos >= 0) & (pos < S), dpooled, 0.0)
        du = _pool_adjoint(dpooled * inv) - dpooled
        du_ref[...] = du[H:H + bs].astype(du_ref.dtype)

        @pl.when(i == 0)
        def _():
            for r in (dpw_ref, dps_ref, dgm_ref):
                r[...] = jnp.zeros_like(r)

        mid = slice(H, H + bs)
        dps_ref[...] += jnp.sum((dyc * mixed)[mid], axis=0, keepdims=True)
        dgm_ref[...] += jnp.sum((dyv * n)[mid], axis=0, keepdims=True)
        pb = pooled[mid].astype(BF16)
        for gi in range(len(POOL_WINDOWS)):
            sl = slice(gi * C_G, (gi + 1) * C_G)
            dpw_ref[gi] += lax.dot_general(pb[:, sl], dmb[mid][:, sl], (((0,), (0,)), ((), ())),
                                           preferred_element_type=F32)

    vec = _vec_spec(C_C)
    full = pl.BlockSpec((4, C_G, C_G), lambda i: (0, 0, 0))
    vshape = jax.ShapeDtypeStruct((1, C_C), F32)
    return pl.pallas_call(
        body, name=name, grid=(nb,),
        in_specs=[*_halo_specs(bs, H, C_C, col, S), *_halo_specs(bs, H, C_C, dcol, S), full, vec, vec],
        out_specs=[_row_spec(bs, C_C), full, vec, vec],
        out_shape=[jax.ShapeDtypeStruct((S, C_C), BF16), jax.ShapeDtypeStruct((4, C_G, C_G), F32), vshape, vshape],
        compiler_params=_params("arbitrary"),
    )(proj, proj, proj, dy, dy, dy, pw, ps, gm)


def rope_tables(S):
    pos = jnp.arange(S, dtype=F32)
    inv = ROPE_THETA ** (-jnp.arange(0, ROT_DIM, 2, dtype=F32) / ROT_DIM)
    ang = pos[:, None] * inv[None, :]
    half = ROT_DIM // 2
    cos, sin = jnp.cos(ang), jnp.sin(ang)
    zeros = jnp.zeros((S, half), F32)
    rest = jnp.zeros((S, HEAD_DIM - ROT_DIM), F32)
    per_head = (jnp.concatenate([cos, cos, rest + 1.0], axis=1), jnp.concatenate([-sin, zeros, rest], axis=1),
                jnp.concatenate([zeros, sin, rest], axis=1))
    return tuple(jnp.tile(t, (1, LANES // HEAD_DIM)) for t in per_head)


def _rotate(t, c, s1, s2, sign):
    half = ROT_DIM // 2
    return t * c + sign * (pltpu.roll(t, LANES - half, axis=1) * s1 + pltpu.roll(t, half, axis=1) * s2)


def rope_fwd(proj, tables, name):
    S = proj.shape[0]
    tr = _rows(S, ROW_TILE)
    qcol = 2 * C_A // C_B
    nd = len(DILATIONS)

    def body(q_ref, k_ref, v_ref, c_ref, s1_ref, s2_ref, *outs):
        c, s1, s2 = c_ref[...], s1_ref[...], s2_ref[...]
        qs, ks = [], []
        for p in range(C_B // LANES):
            sl = slice(p * LANES, (p + 1) * LANES)
            qs.append((_rotate(q_ref[:, sl].astype(F32), c, s1, s2, 1.0) * HEAD_DIM ** -0.5).astype(BF16))
            ks.append(_rotate(k_ref[:, sl].astype(F32), c, s1, s2, 1.0).astype(BF16))
        tensors = (jnp.concatenate(qs, axis=1), jnp.concatenate(ks, axis=1), v_ref[...])
        for n, d in enumerate(DILATIONS):
            for t, x in enumerate(tensors):
                _store_dilated(outs[3 * n + t], x, d)

    tab = _row_spec(tr, LANES)
    res = pl.pallas_call(
        body, name=name, grid=(S // tr,),
        in_specs=[pl.BlockSpec((tr, C_B), lambda i, col=qcol + n: (i, col)) for n in range(3)] + [tab, tab, tab],
        out_specs=[_dilated_spec(tr, d, C_B) for d in DILATIONS for _ in range(3)],
        out_shape=[jax.ShapeDtypeStruct((d, S // d, C_B), BF16) for d in DILATIONS for _ in range(3)],
        compiler_params=_params("parallel"),
    )(proj, proj, proj, *tables)
    return [tuple(res[3 * n:3 * n + 3]) for n in range(nd)]


def rope_bwd(dqs, dks, dvs, tables, dconv, du, name):
    S = dconv.shape[0]
    tr = _rows(S, ROW_TILE)
    n = len(dqs)
    base = 2 * C_A
    width = base + 3 * C_B + C_C

    def body(*refs):
        dq_refs, dk_refs, dv_refs = refs[:n], refs[n:2 * n], refs[2 * n:3 * n]
        c_ref, s1_ref, s2_ref, dconv_ref, du_ref, o_ref = refs[3 * n:]
        c, s1, s2 = c_ref[...], s1_ref[...], s2_ref[...]
        o_ref[:, :base] = dconv_ref[...]
        o_ref[:, base + 3 * C_B:] = du_ref[...]
        dq_all = sum(_load_dilated(r, d).astype(F32) for r, d in zip(dq_refs, DILATIONS))
        dk_all = sum(_load_dilated(r, d).astype(F32) for r, d in zip(dk_refs, DILATIONS))
        dv_all = sum(_load_dilated(r, d).astype(F32) for r, d in zip(dv_refs, DILATIONS))
        for p in range(C_B // LANES):
            sl = slice(p * LANES, (p + 1) * LANES)
            dq, dk, dv = dq_all[:, sl], dk_all[:, sl], dv_all[:, sl]
            at = base + p * LANES
            o_ref[:, at:at + LANES] = (_rotate(dq, c, s1, s2, -1.0) * HEAD_DIM ** -0.5).astype(BF16)
            o_ref[:, C_B + at:C_B + at + LANES] = _rotate(dk, c, s1, s2, -1.0).astype(BF16)
            o_ref[:, 2 * C_B + at:2 * C_B + at + LANES] = dv.astype(BF16)

    tab = _row_spec(tr, LANES)
    return pl.pallas_call(
        body, name=name, grid=(S // tr,),
        in_specs=[_dilated_spec(tr, d, C_B) for _ in range(3) for d in DILATIONS]
        + [tab, tab, tab, _row_spec(tr, base), _row_spec(tr, C_C)],
        out_specs=_row_spec(tr, width), out_shape=jax.ShapeDtypeStruct((S, width), BF16),
        compiler_params=_params("parallel"),
    )(*dqs, *dks, *dvs, *tables, dconv, du)


def _attn_specs(bq, width, L):
    per = bq // ATTN_HALF
    last = L // ATTN_HALF - 1
    cur = pl.BlockSpec((None, bq, width), lambda r, j: (r, j, 0))
    prev = pl.BlockSpec((None, ATTN_HALF, width), lambda r, j: (r, jnp.maximum(j * per - 1, 0), 0))
    nxt = pl.BlockSpec((None, ATTN_HALF, width), lambda r, j: (r, jnp.minimum((j + 1) * per, last), 0))
    return prev, cur, nxt


def _window(refs, sl):
    return jnp.concatenate([r[:, sl] for r in refs], axis=0)


def _band_mask(j, bq, L, rows_are_window):
    bw = bq + 2 * ATTN_HALF
    if rows_are_window:
        rp = j * bq - ATTN_HALF + lax.broadcasted_iota(jnp.int32, (bw, 1), 0)
        cp = j * bq + lax.broadcasted_iota(jnp.int32, (1, bq), 1)
        return (jnp.abs(rp - cp) <= ATTN_HALF) & (rp >= 0) & (rp < L)
    rp = j * bq + lax.broadcasted_iota(jnp.int32, (bq, 1), 0)
    cp = j * bq - ATTN_HALF + lax.broadcasted_iota(jnp.int32, (1, bw), 1)
    return (jnp.abs(rp - cp) <= ATTN_HALF) & (cp >= 0) & (cp < L)


def _head_col(stats, h):
    lane = lax.broadcasted_iota(jnp.int32, (1, LANES), 1)
    return jnp.sum(jnp.where(lane == h, stats, 0.0), axis=1, keepdims=True)


def _stack_heads(x):
    first = lax.broadcasted_iota(jnp.int32, (1, LANES), 1) < HEAD_DIM
    zero = jnp.zeros_like(x)
    return jnp.concatenate([jnp.where(first, x, zero), jnp.where(first, zero, x)], axis=0)


_NT = (((1,), (1,)), ((), ()))
_TN = (((0,), (0,)), ((), ()))


def attn_fwd_pattern(qd, kd, vd, name):
    d, L, _ = qd.shape
    bq = _rows(L, ATTN_BLOCK)

    def body(q_ref, kp, kc, kn, vp, vc, vn, o_ref, lse_ref):
        j = pl.program_id(1)
        mask = _band_mask(j, bq, L, False)
        mask2 = jnp.concatenate([mask, mask], axis=0)
        lane = lax.broadcasted_iota(jnp.int32, (1, LANES), 1)
        first = lane < HEAD_DIM
        lse = jnp.zeros((bq, LANES), F32)
        for p in range(C_B // LANES):
            sl = slice(p * LANES, (p + 1) * LANES)
            kw = _window((kp, kc, kn), sl)
            vw = _window((vp, vc, vn), sl)
            s = jnp.where(mask2, lax.dot_general(_stack_heads(q_ref[:, sl]), kw, _NT, preferred_element_type=F32), NEG)
            m = jnp.max(s, axis=1, keepdims=True)
            e = jnp.exp(s - m)
            l = jnp.sum(e, axis=1, keepdims=True)
            o = jnp.dot(e.astype(BF16), vw, preferred_element_type=F32) * (1.0 / l)
            stat = m + jnp.log(l)
            lse = jnp.where(lane == 2 * p, stat[:bq], jnp.where(lane == 2 * p + 1, stat[bq:], lse))
            o_ref[:, sl] = jnp.where(first, o[:bq], o[bq:]).astype(o_ref.dtype)
        lse_ref[...] = lse

    kv = _attn_specs(bq, C_B, L)
    return pl.pallas_call(
        body, name=name, grid=(d, L // bq), in_specs=[kv[1], *kv, *kv],
        out_specs=[kv[1], pl.BlockSpec((None, bq, LANES), lambda r, j: (r, j, 0))],
        out_shape=[jax.ShapeDtypeStruct((d, L, C_B), BF16), jax.ShapeDtypeStruct((d, L, LANES), F32)],
        compiler_params=_params("parallel", "parallel"),
    )(qd, kd, kd, kd, vd, vd, vd)


def attn_combine(os_, lses, gm, ya, yc, name):
    S = ya.shape[0]
    tr = _rows(S, ROW_TILE)
    n = len(os_)

    def body(*refs):
        o_refs, l_refs = refs[:n], refs[n:2 * n]
        gm_ref, ya_ref, yc_ref, y_ref, yt_ref, out_ref, lse_ref = refs[2 * n:]
        ls = [_load_dilated(r, d) for r, d in zip(l_refs, DILATIONS)]
        os_tok = [_load_dilated(r, d) for r, d in zip(o_refs, DILATIONS)]
        mx = functools.reduce(jnp.maximum, ls)
        ws = [jnp.exp(l - mx) for l in ls]
        den = sum(ws)
        lse_ref[...] = mx + jnp.log(den)
        wn = [w / den for w in ws]
        lane = lax.broadcasted_iota(jnp.int32, (1, LANES), 1)
        first = lane < HEAD_DIM
        blocks = []
        for p in range(C_B // LANES):
            sl = slice(p * LANES, (p + 1) * LANES)
            acc = jnp.zeros((tr, LANES), F32)
            for w, o in zip(wn, os_tok):
                acc = acc + jnp.where(first, _head_col(w, 2 * p), _head_col(w, 2 * p + 1)) * o[:, sl].astype(F32)
            blocks.append(acc)
        out = jnp.concatenate(blocks, axis=1)
        out_ref[...] = out.astype(out_ref.dtype)
        y = jnp.concatenate([ya_ref[...].astype(F32), _rms(out) * gm_ref[...], yc_ref[...].astype(F32)], axis=1)
        y_ref[...] = y.astype(y_ref.dtype)
        yt_ref[...] = y.T.astype(yt_ref.dtype)

    st = _row_spec(tr, LANES)
    mix = C_A + C_B + C_C
    return pl.pallas_call(
        body, name=name, grid=(S // tr,),
        in_specs=[_dilated_spec(tr, d, C_B) for d in DILATIONS] + [_dilated_spec(tr, d, LANES) for d in DILATIONS]
        + [_vec_spec(C_B), _row_spec(tr, C_A), _row_spec(tr, C_C)],
        out_specs=[_row_spec(tr, mix), _col_spec(mix, tr), _row_spec(tr, C_B), st],
        out_shape=[jax.ShapeDtypeStruct((S, mix), BF16), jax.ShapeDtypeStruct((mix, S), BF16),
                   jax.ShapeDtypeStruct((S, C_B), BF16), jax.ShapeDtypeStruct((S, LANES), F32)],
        compiler_params=_params("parallel"),
    )(*os_, *lses, gm, ya, yc)


def attn_out_bwd(out, lse, dy, gm, name):
    S = out.shape[0]
    tr = _rows(S, ROW_TILE)
    nd = len(DILATIONS)

    def body(o_ref, lse_ref, dy1, dy2, g_ref, *outs):
        dg_ref = outs[-1]
        o = o_ref[...].astype(F32)
        dyv = jnp.concatenate([dy1[...], dy2[...]], axis=1).astype(F32)
        do, n = _rms_bwd(o, dyv * g_ref[...])
        dob = do.astype(BF16)
        prod = dob.astype(F32) * o
        lane = lax.broadcasted_iota(jnp.int32, (1, LANES), 1)
        first = lane < HEAD_DIM
        delta = jnp.zeros((tr, LANES), F32)
        for p in range(C_B // LANES):
            blk = prod[:, p * LANES:(p + 1) * LANES]
            delta = jnp.where(lane == 2 * p, jnp.sum(jnp.where(first, blk, 0.0), axis=1, keepdims=True), delta)
            delta = jnp.where(lane == 2 * p + 1, jnp.sum(jnp.where(first, 0.0, blk), axis=1, keepdims=True), delta)
        lse_v = lse_ref[...]
        for k, d in enumerate(DILATIONS):
            _store_dilated(outs[3 * k], dob, d)
            _store_dilated(outs[3 * k + 1], lse_v, d)
            _store_dilated(outs[3 * k + 2], delta, d)

        @pl.when(pl.program_id(0) == 0)
        def _():
            dg_ref[...] = jnp.zeros_like(dg_ref)

        dg_ref[...] += jnp.sum(dyv * n, axis=0, keepdims=True)

    widths = (C_B, LANES, LANES)
    dtypes = (BF16, F32, F32)
    res = pl.pallas_call(
        body, name=name, grid=(S // tr,),
        in_specs=[_row_spec(tr, C_B), _row_spec(tr, LANES), pl.BlockSpec((tr, C_A), lambda i: (i, 1)),
                  pl.BlockSpec((tr, C_A), lambda i: (i, 2)), _vec_spec(C_B)],
        out_specs=[_dilated_spec(tr, d, wd) for d in DILATIONS for wd in widths] + [_vec_spec(C_B)],
        out_shape=[jax.ShapeDtypeStruct((d, S // d, wd), dt) for d in DILATIONS for wd, dt in zip(widths, dtypes)]
        + [jax.ShapeDtypeStruct((1, C_B), F32)],
        compiler_params=_params("arbitrary"),
    )(out, lse, dy, dy, gm)
    return [tuple(res[3 * k:3 * k + 3]) for k in range(nd)], res[-1]


def attn_bwd_pattern(qd, kd, vd, dod, lsed, deltad, name, comm=None):
    d, L, _ = qd.shape
    nc = comm.n if comm is not None else 0
    bq = _rows(L, ATTN_BLOCK)

    nb = L // bq
    bw = bq + 2 * ATTN_HALF
    lo = bq - ATTN_HALF

    def body(qc, dc_, lc, tc, kp, kc, kn, vp, vc, vn, *rest):
        c_ins, (dq_ref, dk_ref, dv_ref), c_outs = rest[:nc], rest[nc:nc + 3], rest[nc + 3:2 * nc + 3]
        dk_acc, dv_acc = rest[2 * nc + 3:2 * nc + 5]
        sems = rest[2 * nc + 5:]
        r, j = pl.program_id(0), pl.program_id(1)
        if comm is not None:
            @pl.when((r == 0) & (j == 0))
            def _():
                comm.start(c_ins, c_outs, sems)

        @pl.when(j == 0)
        def _():
            dk_acc[...] = jnp.zeros_like(dk_acc)
            dv_acc[...] = jnp.zeros_like(dv_acc)

        @pl.when(j > 0)
        def _():
            for acc in (dk_acc, dv_acc):
                acc[0:bq, :] = acc[bq:2 * bq, :]
                acc[bq:2 * bq, :] = acc[2 * bq:, :]
                acc[2 * bq:, :] = jnp.zeros((bq, C_B), F32)

        @pl.when(j < nb)
        def _():
            mask = _band_mask(j, bq, L, False)
            mask2 = jnp.concatenate([mask, mask], axis=0)
            first = lax.broadcasted_iota(jnp.int32, (1, LANES), 1) < HEAD_DIM
            lse_c, delta_c = lc[...], tc[...]
            for p in range(C_B // LANES):
                sl = slice(p * LANES, (p + 1) * LANES)
                qs, dos = _stack_heads(qc[:, sl]), _stack_heads(dc_[:, sl])
                kw, vw = _window((kp, kc, kn), sl), _window((vp, vc, vn), sl)
                lse_s = jnp.concatenate([_head_col(lse_c, 2 * p), _head_col(lse_c, 2 * p + 1)], axis=0)
                delta_s = jnp.concatenate([_head_col(delta_c, 2 * p), _head_col(delta_c, 2 * p + 1)], axis=0)
                s = lax.dot_general(qs, kw, _NT, preferred_element_type=F32)
                pr = jnp.where(mask2, jnp.exp(s - lse_s), 0.0)
                dpr = lax.dot_general(dos, vw, _NT, preferred_element_type=F32)
                ds = (pr * (dpr - delta_s)).astype(BF16)
                dq = jnp.dot(ds, kw, preferred_element_type=F32)
                dq_ref[:, sl] = jnp.where(first, dq[:bq], dq[bq:]).astype(dq_ref.dtype)
                dk_acc[lo:lo + bw, sl] += lax.dot_general(ds, qs, _TN, preferred_element_type=F32)
                dv_acc[lo:lo + bw, sl] += lax.dot_general(pr.astype(BF16), dos, _TN, preferred_element_type=F32)

        dk_ref[...] = dk_acc[0:bq, :].astype(dk_ref.dtype)
        dv_ref[...] = dv_acc[0:bq, :].astype(dv_ref.dtype)
        if comm is not None:
            @pl.when((r == d - 1) & (j == nb))
            def _():
                comm.finish(c_ins, c_outs, sems)

    per = bq // ATTN_HALF
    last = L // ATTN_HALF - 1

    def clamp(j):
        return jnp.minimum(j, nb - 1)

    def specs(width):
        cur = pl.BlockSpec((None, bq, width), lambda r, j: (r, clamp(j), 0))
        prev = pl.BlockSpec((None, ATTN_HALF, width), lambda r, j: (r, jnp.maximum(clamp(j) * per - 1, 0), 0))
        nxt = pl.BlockSpec((None, ATTN_HALF, width), lambda r, j: (r, jnp.minimum((clamp(j) + 1) * per, last), 0))
        return prev, cur, nxt

    wide = specs(C_B)
    stat = specs(LANES)[1]
    lagged = pl.BlockSpec((None, bq, C_B), lambda r, j: (r, jnp.maximum(j - 1, 0), 0))
    shape = jax.ShapeDtypeStruct((d, L, C_B), BF16)
    res = pl.pallas_call(
        body, name=name, grid=(d, nb + 1), in_specs=[wide[1], wide[1], stat, stat, *wide, *wide] + [_ANY] * nc,
        out_specs=[wide[1], lagged, lagged] + [_ANY] * nc,
        out_shape=[shape] * 3 + (comm.out_shape if comm is not None else []),
        scratch_shapes=[pltpu.VMEM((3 * bq, C_B), F32), pltpu.VMEM((3 * bq, C_B), F32)]
        + (comm.sems if comm is not None else []),
        compiler_params=_params("arbitrary", "arbitrary"),
    )(qd, dod, lsed, deltad, kd, kd, kd, vd, vd, vd, *(comm.ins if comm is not None else []))
    return tuple(res[:3]), (res[3:] if comm is not None else None)


def _dilated_spec(tr, d, width):
    return pl.BlockSpec((d, tr // d, width), lambda i: (0, i, 0))


def _perm_matrix(n, d, inverse):
    lb = n // d
    row = lax.broadcasted_iota(jnp.int32, (n, n), 0)
    col = lax.broadcasted_iota(jnp.int32, (n, n), 1)
    source = (row % d) * lb + row // d if inverse else (row % lb) * d + row // lb
    return (col == source).astype(BF16)


def _permute(p, x):
    if x.dtype == BF16:
        return jnp.dot(p, x, preferred_element_type=F32).astype(BF16)
    hi = x.astype(BF16)
    rest = x - hi.astype(F32)
    mid = rest.astype(BF16)
    lo = (rest - mid.astype(F32)).astype(BF16)
    return (jnp.dot(p, hi, preferred_element_type=F32) + jnp.dot(p, mid, preferred_element_type=F32)
            + jnp.dot(p, lo, preferred_element_type=F32))


def _store_dilated(ref, x, d):
    n = x.shape[0]
    if d == 1:
        ref[0] = x
        return
    y = _permute(_perm_matrix(n, d, False), x)
    lb = n // d
    for r in range(d):
        ref[r] = y[r * lb:(r + 1) * lb]


def _load_dilated(ref, d):
    if d == 1:
        return ref[0]
    y = jnp.concatenate([ref[r] for r in range(d)], axis=0)
    return _permute(_perm_matrix(y.shape[0], d, True), y)


def adamw(parts, w, m, v, name):
    n, R, C = parts.shape
    tr = _rows(R, ROW_TILE)

    def body(p_ref, w_ref, m_ref, v_ref, g_ref, d_ref, nm_ref, nv_ref):
        g = p_ref[0].astype(F32)
        for k in range(1, n):
            g = g + p_ref[k].astype(F32)
        mm = ADAM_B1 * m_ref[...] + (1.0 - ADAM_B1) * g
        vv = ADAM_B2 * v_ref[...] + (1.0 - ADAM_B2) * jnp.square(g)
        m_hat = mm / (1.0 - ADAM_B1 ** ADAM_STEP)
        v_hat = vv / (1.0 - ADAM_B2 ** ADAM_STEP)
        g_ref[...] = g
        d_ref[...] = -ADAM_LR * (m_hat / (jnp.sqrt(v_hat) + ADAM_EPS) + ADAM_WD * w_ref[...])
        nm_ref[...] = mm
        nv_ref[...] = vv

    spec = _row_spec(tr, C)
    shape = jax.ShapeDtypeStruct((R, C), F32)
    return pl.pallas_call(
        body, name=name, grid=(R // tr,),
        in_specs=[pl.BlockSpec((n, tr, C), lambda i: (0, i, 0)), spec, spec, spec],
        out_specs=[spec] * 4, out_shape=[shape] * 4, compiler_params=_params("parallel"),
    )(parts, w, m, v)


_ANY = pl.BlockSpec(memory_space=pl.ANY)


def _place():
    return lax.axis_index("x"), lax.axis_index("y"), lax.axis_index("c")


def _index(px, py, pc):
    return 4 * px + 2 * py + pc


class Gather:
    def __init__(self, shards):
        self.ins = list(shards)
        T = self.n = len(shards)
        self.out_shape = [jax.ShapeDtypeStruct((N_DEV, *s.shape), s.dtype) for s in shards]
        self.sems = [pltpu.SemaphoreType.DMA((T, 7)), pltpu.SemaphoreType.DMA((T, 7)), pltpu.SemaphoreType.DMA((T,))]

    def _plan(self, ins, outs, sems):
        send_sems, recv_sems, local_sems = sems
        x, y, c = _place()
        me, sibling = (x, y, c), (x, y, 1 - c)
        chips = [(1 - x, y), (x, 1 - y), (1 - x, 1 - y)]

        def copy(t, k, block, to, src=None):
            rows = outs[t].at[_index(*block)]
            return pltpu.make_async_remote_copy(
                src_ref=rows if src is None else src, dst_ref=rows, send_sem=send_sems.at[t, k],
                recv_sem=recv_sems.at[t, k], device_id=to, device_id_type=MESH)

        mine = [pltpu.make_async_copy(ins[t], outs[t].at[_index(*me)], local_sems.at[t]) for t in range(self.n)]
        first = []
        for t in range(self.n):
            first.append(copy(t, 0, me, sibling, src=ins[t]))
            first += [copy(t, 1 + j, me, (*chip, c), src=ins[t]) for j, chip in enumerate(chips)]
        return copy, mine, first, me, sibling, chips, c

    def start(self, ins, outs, sems):
        _, mine, first, *_ = self._plan(ins, outs, sems)
        for cp in mine + first:
            cp.start()

    def finish(self, ins, outs, sems):
        copy, mine, first, me, sibling, chips, c = self._plan(ins, outs, sems)
        passed = []
        for j, chip in enumerate(chips):
            for t in range(self.n):
                copy(t, 1 + j, (*chip, c), me).wait_recv()
                fwd = copy(t, 4 + j, (*chip, c), sibling)
                fwd.start()
                passed.append(fwd)
        for t in range(self.n):
            copy(t, 0, sibling, me).wait_recv()
            for j, chip in enumerate(chips):
                copy(t, 4 + j, (*chip, 1 - c), me).wait_recv()
        for cp in first + passed:
            cp.wait_send()
        for cp in mine:
            cp.wait()


class Exchange:
    def __init__(self, parts):
        self.ins = list(parts)
        T = self.n = len(parts)
        self.out_shape = [jax.ShapeDtypeStruct(p.shape, p.dtype) for p in parts]
        self.sems = [pltpu.SemaphoreType.DMA((T, 7)), pltpu.SemaphoreType.DMA((T, 7)), pltpu.SemaphoreType.DMA((T,))]

    def _plan(self, ins, outs, sems):
        send_sems, recv_sems, local_sems = sems
        x, y, c = _place()
        me = _index(x, y, c)
        copies = [pltpu.make_async_copy(ins[t].at[me], outs[t].at[me], local_sems.at[t]) for t in range(self.n)]
        for k in range(1, N_DEV):
            peer = ((x + (k >> 2)) % 2, (y + ((k >> 1) & 1)) % 2, (c + (k & 1)) % 2)
            there = _index(*peer)
            for t in range(self.n):
                copies.append(pltpu.make_async_remote_copy(
                    src_ref=ins[t].at[there], dst_ref=outs[t].at[me], send_sem=send_sems.at[t, k - 1],
                    recv_sem=recv_sems.at[t, k - 1], device_id=peer, device_id_type=MESH))
        return copies

    def start(self, ins, outs, sems):
        for cp in self._plan(ins, outs, sems):
            cp.start()

    def finish(self, ins, outs, sems):
        for cp in self._plan(ins, outs, sems):
            cp.wait()


def communicate(comm, name):
    T = comm.n

    def body(*refs):
        ins, outs, sems = refs[:T], refs[T:2 * T], refs[2 * T:]
        comm.start(ins, outs, sems)
        comm.finish(ins, outs, sems)

    return pl.pallas_call(
        body, name=name, in_specs=[_ANY] * T, out_specs=[_ANY] * T, out_shape=comm.out_shape,
        scratch_shapes=comm.sems,
    )(*comm.ins)


def all_gather(shards, name):
    return communicate(Gather(shards), name)


def exchange(parts, name):
    return communicate(Exchange(parts), name)


def _row(v):
    return v.reshape(1, -1)


def _hosted(res):
    return res if isinstance(res, tuple) else (res, None)


def mix_forward(x, h1, h1t, w, p, tables, comm=None, comm_out=None, after_proj=None):
    gm = p["g_mix"]
    proj, got = _hosted(matmul(h1, w["win_t"], "nt", BF16, "proj", tn=1152, comm=comm))
    if after_proj is not None:
        after_proj(got)
    ya = conv_fwd(proj, p["cw"], _row(p["conv_b"]), _row(p["conv_ln_g"]), _row(p["conv_ln_b"]), _row(gm[:C_A]),
                  "conv_fwd")
    dil = rope_fwd(proj, tables, "rope_fwd")
    os_, lses = [], []
    for d, (qd, kd, vd) in zip(DILATIONS, dil):
        o, lse = attn_fwd_pattern(qd, kd, vd, f"attn_fwd_d{d}")
        os_.append(o)
        lses.append(lse)
    yc = pool_fwd(proj, p["pool_w"], _row(p["pool_scale"]), _row(gm[C_A + C_B:]), "pool_fwd")
    y, yt, out, lse = attn_combine(os_, lses, _row(gm[C_A:C_A + C_B]), ya, yc, "attn_combine")
    z, got_out = _hosted(matmul(y, w["wout"], "nn", BF16, "mix_out", tn=1024, comm=comm_out))
    x2, h2, h2t = norm_residual_rms(z, x, _row(p["g_post_mix"]), _row(p["g_pre_ffn"]), "res_mix")
    return x2, h2, h2t, dict(x=x, h1t=h1t, proj=proj, dil=dil, out=out, lse=lse, yt=yt, z=z), got_out


def ffn_forward(x2, h2, h2t, w, p, g_next=None, comm_in=None, comm_out=None, after_in=None):
    gu, got_in = _hosted(matmul(h2, w["wgu_t"], "nt", BF16, "ffn_in", tn=1024, comm=comm_in))
    if after_in is not None:
        after_in(got_in)
    a, at = swiglu_fwd(gu, "swiglu_fwd")
    f, got_out = _hosted(matmul(a, w["wd"], "nn", BF16, "ffn_out", tk=5632, comm=comm_out))
    saved = dict(x2=x2, h2t=h2t, gu=gu, at=at, f=f)
    if g_next is None:
        return norm_residual(f, x2, _row(p["g_post_ffn"]), "res_last"), None, saved, got_in, got_out
    x3, h1, h1t = norm_residual_rms(f, x2, _row(p["g_post_ffn"]), _row(g_next), "res_ffn")
    return x3, (h1, h1t), saved, got_in, got_out


def _to_blocks(g, by_columns):
    if by_columns:
        return g.T.reshape(N_DEV, -1, g.shape[0])
    return g.reshape(N_DEV, -1, g.shape[1])


def _exchange_of(g, by_columns):
    return Exchange([_to_blocks(g, by_columns)]) if g is not None else None


def ffn_backward(dx3, df, s, z, w, p, ride_act=None):
    da, got_act = _hosted(matmul(df, w["wd"], "nt", BF16, "d_ffn_act", tn=1408, comm=ride_act))
    dwd = matmul(s["at"], df, "nn", BF16, "dw_down", tm=1408, tn=1024, tk=2048)
    dgu = swiglu_bwd(s["gu"], da, "swiglu_bwd")
    dh2, got_wd = matmul(dgu, w["wgu_t"], "nn", BF16, "d_ffn_in", tn=1024, tk=2816, comm=_exchange_of(dwd, False))
    dwgu = matmul(s["h2t"], dgu, "nn", BF16, "dw_gate_up", tm=1024, tn=1408, tk=2048)
    dx2, dz, dg_pre_ffn, dg_post_mix = rms_bwd_chain(
        s["x2"], dh2, _row(p["g_pre_ffn"]), dx3, z, _row(p["g_post_mix"]), "rms_bwd_ffn_to_mix")
    return dx2, dz, dict(dwgu=dwgu, parts_wd=got_wd[0], got_act=got_act,
                         g_pre_ffn=dg_pre_ffn[0], g_post_mix=dg_post_mix[0])


def mix_backward(dx2, dz, s, w, p, tables, dwgu, below=None):
    gm = p["g_mix"]
    F = dwgu.shape[1] // 2
    dy = matmul(dz, w["wout"], "nt", BF16, "d_mix", tn=1024)
    dwout = matmul(s["yt"], dz, "nn", BF16, "dw_out", tm=1024, tn=1024, tk=2048)
    dconv, dcw, dcb, dlg, dlb, dgm_a = conv_bwd(
        s["proj"], dy, p["cw"], _row(p["conv_b"]), _row(p["conv_ln_g"]), _row(p["conv_ln_b"]), _row(gm[:C_A]),
        "conv_bwd")
    stats, dgm_b = attn_out_bwd(s["out"], s["lse"], dy, _row(gm[C_A:C_A + C_B]), "attn_out_bwd")
    rides = [_exchange_of(dwgu[:, :F], True), _exchange_of(dwgu[:, F:], True), None]
    dqs, dks, dvs, got = [], [], [], []
    for d, (qd, kd, vd), (dod, lsed, deltad), ride in zip(DILATIONS, s["dil"], stats, rides):
        (dq, dk, dv), parts = attn_bwd_pattern(qd, kd, vd, dod, lsed, deltad, f"attn_bwd_d{d}", comm=ride)
        dqs.append(dq)
        dks.append(dk)
        dvs.append(dv)
        got.append(parts)
    du, dpw, dps, dgm_c = pool_bwd(s["proj"], dy, p["pool_w"], _row(p["pool_scale"]), _row(gm[C_A + C_B:]),
                                   "pool_bwd")
    dproj = rope_bwd(dqs, dks, dvs, tables, dconv, du, "rope_bwd")
    bottom = below is None
    if not bottom:
        dh1 = matmul(dproj, w["win_t"], "nn", BF16, "d_proj", tn=1024, tk=4608)
    dwin, got_out = matmul(s["h1t"], dproj, "nn", BF16, "dw_in", tm=1024, tn=1152, tk=2048,
                           comm=_exchange_of(dwout, False))
    if bottom:
        dh1, got_in = matmul(dproj, w["win_t"], "nn", BF16, "d_proj", tn=1024, tk=4608,
                             comm=_exchange_of(dwin, True))
    grads = dict(
        dwin=dwin, parts_out=got_out[0], parts_gate=got[0][0], parts_up=got[1][0],
        conv_w=dcw[:CONV_WIDTH], conv_b=dcb[0], conv_ln_g=dlg[0], conv_ln_b=dlb[0], pool_w=dpw, pool_scale=dps[0],
        g_mix=jnp.concatenate([dgm_a[0], dgm_b[0], dgm_c[0]]))
    if bottom:
        grads["parts_in"] = got_in[0]
        dx, dg_pre_mix = rms_bwd(s["x"], dh1, _row(p["g_pre_mix"]), dx2, F32, "rms_bwd_first")
        df = None
    else:
        dx, df, dg_pre_mix, dg_post_ffn = rms_bwd_chain(
            s["x"], dh1, _row(p["g_pre_mix"]), dx2, below[0], _row(below[1]), "rms_bwd_mix_to_ffn")
        grads["g_post_ffn_below"] = dg_post_ffn[0]
    grads["g_pre_mix"] = dg_pre_mix[0]
    return dx, df, grads


WEIGHTS = ["w_in", "conv_w", "conv_b", "conv_ln_g", "conv_ln_b", "pool_w", "pool_scale", "g_mix", "w_out", "g_pre_mix",
           "g_post_mix", "g_pre_ffn", "g_post_ffn", "w_gate", "w_up", "w_down"]
BIG = ["w_in", "w_out", "w_gate", "w_up", "w_down"]
REPLICATED = ["conv_b", "conv_ln_g", "conv_ln_b", "pool_w", "pool_scale", "g_mix", "g_pre_mix", "g_post_mix",
              "g_pre_ffn", "g_post_ffn"]
PACK_ROWS = 256


def adamw_layer(parts, w, m, v, layer, prev, name):
    n, R, C = parts.shape
    tr = _rows(R, ROW_TILE)

    def body(p_ref, w_ref, m_ref, v_ref, *rest):
        g_ref, d_ref, nm_ref, nv_ref = rest[-4:]
        g = p_ref[0].astype(F32)
        for k in range(1, n):
            g = g + p_ref[k].astype(F32)
        mm = ADAM_B1 * m_ref[...] + (1.0 - ADAM_B1) * g
        vv = ADAM_B2 * v_ref[...] + (1.0 - ADAM_B2) * jnp.square(g)
        m_hat = mm / (1.0 - ADAM_B1 ** ADAM_STEP)
        v_hat = vv / (1.0 - ADAM_B2 ** ADAM_STEP)
        g_ref[...] = g
        d_ref[...] = -ADAM_LR * (m_hat / (jnp.sqrt(v_hat) + ADAM_EPS) + ADAM_WD * w_ref[...])
        nm_ref[...] = mm
        nv_ref[...] = vv

    spec = pl.BlockSpec((None, tr, C), lambda i: (layer, i, 0))
    shape = jax.ShapeDtypeStruct(w.shape, F32)
    prev = list(prev) if prev is not None else []
    return pl.pallas_call(
        body, name=name, grid=(R // tr,),
        in_specs=[pl.BlockSpec((n, tr, C), lambda i: (0, i, 0)), spec, spec, spec] + [_ANY] * len(prev),
        out_specs=[spec] * 4, out_shape=[shape] * 4,
        input_output_aliases={4 + k: k for k in range(len(prev))}, compiler_params=_params("parallel"),
    )(parts, w, m, v, *prev)


def _pack(arrays):
    flat = jnp.concatenate([a.reshape(-1).astype(F32) for a in arrays])
    unit = PACK_ROWS * LANES
    padded = -(-flat.shape[0] // unit) * unit
    return jnp.pad(flat, (0, padded - flat.shape[0])).reshape(-1, LANES)


def _unpack(packed, like):
    flat = packed.reshape(-1)
    out, at = [], 0
    for a in like:
        out.append(flat[at:at + a.size].reshape(a.shape))
        at += a.size
    return out


def kernel(x, w_in, conv_w, conv_b, conv_ln_g, conv_ln_b, pool_w, pool_scale, g_mix, w_out, g_pre_mix, g_post_mix, g_pre_ffn, g_post_ffn, w_gate, w_up, w_down, loss_target, m_w_in, m_conv_w, m_conv_b, m_conv_ln_g, m_conv_ln_b, m_pool_w, m_pool_scale, m_g_mix, m_w_out, m_g_pre_mix, m_g_post_mix, m_g_pre_ffn, m_g_post_ffn, m_w_gate, m_w_up, m_w_down, v_w_in, v_conv_w, v_conv_b, v_conv_ln_g, v_conv_ln_b, v_pool_w, v_pool_scale, v_g_mix, v_w_out, v_g_pre_mix, v_g_post_mix, v_g_pre_ffn, v_g_post_ffn, v_w_gate, v_w_up, v_w_down):
    w = dict(w_in=w_in, conv_w=conv_w, conv_b=conv_b, conv_ln_g=conv_ln_g, conv_ln_b=conv_ln_b, pool_w=pool_w,
             pool_scale=pool_scale, g_mix=g_mix, w_out=w_out, g_pre_mix=g_pre_mix, g_post_mix=g_post_mix,
             g_pre_ffn=g_pre_ffn, g_post_ffn=g_post_ffn, w_gate=w_gate, w_up=w_up, w_down=w_down)
    m = dict(w_in=m_w_in, conv_w=m_conv_w, conv_b=m_conv_b, conv_ln_g=m_conv_ln_g, conv_ln_b=m_conv_ln_b,
             pool_w=m_pool_w, pool_scale=m_pool_scale, g_mix=m_g_mix, w_out=m_w_out, g_pre_mix=m_g_pre_mix,
             g_post_mix=m_g_post_mix, g_pre_ffn=m_g_pre_ffn, g_post_ffn=m_g_post_ffn, w_gate=m_w_gate, w_up=m_w_up,
             w_down=m_w_down)
    v = dict(w_in=v_w_in, conv_w=v_conv_w, conv_b=v_conv_b, conv_ln_g=v_conv_ln_g, conv_ln_b=v_conv_ln_b,
             pool_w=v_pool_w, pool_scale=v_pool_scale, g_mix=v_g_mix, w_out=v_w_out, g_pre_mix=v_g_pre_mix,
             g_post_mix=v_g_post_mix, g_pre_ffn=v_g_pre_ffn, g_post_ffn=v_g_post_ffn, w_gate=v_w_gate, w_up=v_w_up,
             w_down=v_w_down)
    depth = w_in.shape[0]
    xs, target = x[0], loss_target[0]
    S, D = xs.shape
    tables = rope_tables(S)

    cw_shard = jnp.pad(conv_w, ((0, 0), (0, 1), (0, 0)))
    cw_all = all_gather([cw_shard.reshape(-1, LANES)], "gather_conv_w")[0]
    cw_full = cw_all.reshape(N_DEV, depth, 32, -1).transpose(1, 2, 0, 3).reshape(depth, 32, C_A)
    small = []
    for l in range(depth):
        small.append({n: w[n][l] for n in REPLICATED})
        small[l]["cw"] = cw_full[l]

    def shard(n, l):
        return (w[n][l].T if n in ("w_in", "w_gate", "w_up") else w[n][l]).astype(BF16)

    def joined(blocks):
        return blocks.reshape(-1, blocks.shape[2])

    def gather_of(*names_layers):
        return Gather([shard(n, l) for n, l in names_layers]) if names_layers else None

    def gate_up(gate_blocks, up_blocks):
        return jnp.concatenate([joined(gate_blocks), joined(up_blocks)], axis=0)

    w_mix = [dict() for _ in range(depth)]
    w_ffn = [dict() for _ in range(depth)]
    saved_mix, saved_ffn = [None] * depth, [None] * depth
    w_mix[0]["win_t"] = joined(communicate(gather_of(("w_in", 0)), "gather_first_weights")[0])
    held = {}
    h = xs
    normed = rms_fwd(xs, _row(small[0]["g_pre_mix"]), "rms_first")
    for l in range(depth):
        more = l + 1 < depth
        first = l == 0

        def after_proj(got, l=l, first=first):
            w_mix[l]["wout"] = joined(got[0])
            if first:
                held["gate"] = got[1]

        x2, h2, h2t, saved_mix[l], got_mix = mix_forward(
            h, *normed, w_mix[l], small[l], tables,
            gather_of(("w_out", l), ("w_gate", l)) if first else gather_of(("w_out", l)),
            gather_of(("w_up", l)) if first else None, after_proj)
        if first:
            w_ffn[0]["wgu_t"] = gate_up(held["gate"], got_mix[0])

        def after_in(got, first=first):
            if first:
                w_ffn[0]["wd"] = joined(got[0])

        rides = ([("w_down", 0)] if first else []) + ([("w_gate", l + 1), ("w_up", l + 1)] if more else [])
        h, normed, saved_ffn[l], got_in, got_out = ffn_forward(
            x2, h2, h2t, w_ffn[l], small[l], small[l + 1]["g_pre_mix"] if more else None,
            gather_of(*rides), gather_of(("w_down", l + 1), ("w_in", l + 1)) if more else None, after_in)
        if more:
            w_ffn[l + 1]["wgu_t"] = gate_up(*got_in[-2:])
            w_ffn[l + 1]["wd"] = joined(got_out[0])
            w_mix[l + 1]["win_t"] = joined(got_out[1])

    dh, sq = loss_head(h, target, "loss_head")
    loss = lax.psum(0.5 * jnp.sum(sq) / D, ("x", "y", "c"))

    big_out = {n: None for n in BIG}
    by_columns = ("w_in", "w_gate", "w_up")
    state = {n: tuple(jnp.swapaxes(t[n], 1, 2) if n in by_columns else t[n] for t in (w, m, v)) for n in BIG}

    def update(n, l, parts):
        big_out[n] = adamw_layer(parts, *state[n], l, big_out[n], f"adamw_{n}_layer{l}")

    small_grads = [None] * depth
    carried = None
    top = depth - 1
    df, dg_post_ffn = rms_bwd(saved_ffn[top]["f"], dh, _row(small[top]["g_post_ffn"]), None, BF16, "rms_bwd_last")
    post_ffn = {top: dg_post_ffn[0]}
    for l in reversed(range(depth)):
        dx2, dz, gf = ffn_backward(dh, df, saved_ffn[l], saved_mix[l]["z"], w_ffn[l], small[l],
                                   _exchange_of(carried, True))
        if carried is not None:
            update("w_in", l + 1, gf["got_act"][0])
        update("w_down", l, gf["parts_wd"])
        below = (saved_ffn[l - 1]["f"], small[l - 1]["g_post_ffn"]) if l > 0 else None
        dh, df, gm_ = mix_backward(dx2, dz, saved_mix[l], w_mix[l], small[l], tables, gf["dwgu"], below)
        if l > 0:
            post_ffn[l - 1] = gm_["g_post_ffn_below"]
        update("w_gate", l, gm_["parts_gate"])
        update("w_up", l, gm_["parts_up"])
        update("w_out", l, gm_["parts_out"])
        carried = gm_["dwin"]
        small_grads[l] = {**gf, **gm_, "g_post_ffn": post_ffn[l]}
    update("w_in", 0, small_grads[0]["parts_in"])

    names = REPLICATED + ["conv_w"]
    stacked = [jnp.stack([small_grads[l][n] for l in range(depth)]) for n in names]
    partial = all_gather([_pack(stacked)], "gather_small_grads")[0]
    zeros = jnp.zeros_like(stacked[-1])
    packed = adamw(partial, _pack([w[n] for n in REPLICATED] + [zeros]), _pack([m[n] for n in REPLICATED] + [zeros]),
                   _pack([v[n] for n in REPLICATED] + [zeros + 1.0]), "adamw_replicated")
    small_out = [_unpack(o, stacked) for o in packed]
    out = {n: tuple(o[i] for o in small_out) for i, n in enumerate(REPLICATED)}
    width = conv_w.shape[2]
    g_cw = lax.dynamic_slice_in_dim(small_out[0][-1], _index(*_place()) * width, width, axis=2)
    cw_res = adamw(g_cw.reshape(1, -1, LANES), conv_w.reshape(-1, LANES), m["conv_w"].reshape(-1, LANES),
                   v["conv_w"].reshape(-1, LANES), "adamw_conv_w")
    out["conv_w"] = tuple(o.reshape(conv_w.shape) for o in cw_res)
    for n in BIG:
        out[n] = tuple(jnp.swapaxes(o, 1, 2) if n in by_columns else o for o in big_out[n])
    results = [loss, dh[None]]
    for k in range(4):
        results += [out[n][k] for n in WEIGHTS]
    return tuple(results)
```
